```python
import math
import jax, jax.numpy as jnp
from jax import lax
import numpy as np

D_MODEL = 1024
BATCH = 8
SEQ = 4096
DEPTH = 2

N_EVEN = (DEPTH + 1) // 2
N_ODD = DEPTH // 2
EPS = 1e-6
CONV_WIDTH = 4

LRU_WIDTH = D_MODEL
LRU_HEADS = 8
LRU_HEAD_DIM = LRU_WIDTH // LRU_HEADS
LRU_C = 8.0
MLSTM_WIDTH = D_MODEL
MLSTM_HEADS = 8
MLSTM_HEAD_DIM = MLSTM_WIDTH // MLSTM_HEADS
MLSTM_QKV_BLOCK = 4
MLSTM_CHUNK = 128
EVEN_IN = 2 * LRU_WIDTH + 2 * MLSTM_WIDTH
EVEN_MIX = LRU_WIDTH + MLSTM_WIDTH

SSD_INNER = 2 * D_MODEL
SSD_HEAD_DIM = 64
SSD_HEADS = SSD_INNER // SSD_HEAD_DIM
SSD_GROUPS = 8
SSD_HPG = SSD_HEADS // SSD_GROUPS
SSD_STATE = 128
SSD_CHUNK = 128
SSD_CONV_CH = SSD_INNER + 2 * SSD_GROUPS * SSD_STATE
SSD_IN = SSD_INNER + SSD_CONV_CH + SSD_HEADS

N_EXPERTS = 32
TOP_K = 4
D_FF = D_MODEL
SWIGLU_ALPHA = 1.702
SWIGLU_LIMIT = 7.0
MOE_BLOCK = 256

kernel_name = 'hybrid_lru_mlstm_ssd_moe_adaln'


def rmsnorm(x, g):
    xf = x.astype(jnp.float32)
    y = xf * lax.rsqrt(jnp.mean(xf * xf, axis=-1, keepdims=True) + EPS)
    return y * g.astype(jnp.float32)


def modulate(h, shift, scale):
    return h * (1.0 + scale[:, None, :]) + shift[:, None, :]


def causal_dwconv(x, w, b):
    seq = x.shape[1]
    xp = jnp.pad(x, ((0, 0), (CONV_WIDTH - 1, 0), (0, 0)))
    out = b + xp[:, 0:seq] * w[0]
    for j in range(1, CONV_WIDTH):
        out = out + xp[:, j:j + seq] * w[j]
    return out


def rg_lru(x, w_r, b_r, w_i, b_i, lam):
    bsz, seq, width = x.shape
    x = x.astype(jnp.float32)
    xh = x.reshape(bsz, seq, LRU_HEADS, LRU_HEAD_DIM)
    r = jax.nn.sigmoid(jnp.einsum('bshd,hde->bshe', xh, w_r) + b_r).reshape(bsz, seq, width)
    i = jax.nn.sigmoid(jnp.einsum('bshd,hde->bshe', xh, w_i) + b_i).reshape(bsz, seq, width)
    log_a = LRU_C * r * jax.nn.log_sigmoid(lam.astype(jnp.float32))
    a = jnp.exp(log_a)
    u = jnp.sqrt(-jnp.expm1(2.0 * log_a)) * (i * x)

    def combine(left, right):
        a_l, u_l = left
        a_r, u_r = right
        return a_l * a_r, a_r * u_l + u_r

    _, h = lax.associative_scan(combine, (a, u), axis=1)
    return h


def mlstm_chunkwise(q, k, v, ig, lf):
    bsz, nh, seq, dh = q.shape
    L = MLSTM_CHUNK
    nc = seq // L
    k = k * (dh ** -0.5)

    def to_chunks(t):
        return jnp.moveaxis(t.reshape(bsz, nh, nc, L, *t.shape[3:]), 2, 0)

    causal = jnp.tril(jnp.ones((L, L), dtype=bool))

    def step(carry, inp):
        C, n, m = carry
        q_, k_, v_, i_, f_ = inp
        g = jnp.cumsum(f_, axis=-1)
        dmat = g[..., :, None] - g[..., None, :] + i_[..., None, :]
        dmat = jnp.where(causal, dmat, -jnp.inf)
        m_inter = m[..., None] + g
        m_t = jnp.maximum(m_inter, jnp.max(dmat, axis=-1))
        w_intra = jnp.exp(dmat - m_t[..., None])
        w_inter = jnp.exp(m_inter - m_t)
        qk = jnp.einsum('bhtd,bhsd->bhts', q_, k_) * w_intra
        num = jnp.einsum('bhts,bhse->bhte', qk, v_) + w_inter[..., None] * jnp.einsum('bhtd,bhde->bhte', q_, C)
        den = jnp.sum(qk, axis=-1) + w_inter * jnp.einsum('bhtd,bhd->bht', q_, n)
        h = num / jnp.maximum(jnp.abs(den), jnp.exp(-m_t))[..., None]
        G = g[..., -1]
        dec_s = G[..., None] - g + i_
        m_new = jnp.maximum(m + G, jnp.max(dec_s, axis=-1))
        ws = jnp.exp(dec_s - m_new[..., None])
        wc = jnp.exp(m + G - m_new)
        C_new = wc[..., None, None] * C + jnp.einsum('bhs,bhsd,bhse->bhde', ws, k_, v_)
        n_new = wc[..., None] * n + jnp.einsum('bhs,bhsd->bhd', ws, k_)
        return (C_new, n_new, m_new), h

    init = (jnp.zeros((bsz, nh, dh, dh), jnp.float32),
            jnp.zeros((bsz, nh, dh), jnp.float32),
            jnp.full((bsz, nh), -jnp.inf, jnp.float32))
    _, hs = lax.scan(step, init, (to_chunks(q), to_chunks(k), to_chunks(v), to_chunks(ig), to_chunks(lf)))
    return jnp.moveaxis(hs, 0, 2).reshape(bsz, nh, seq, dh)


def mlstm_branch(xb, zb, conv_w, conv_b, w_q, w_k, w_v, w_ig, b_ig, w_fg, b_fg, norm_w, skip):
    bsz, seq, width = xb.shape
    xb = xb.astype(jnp.float32)
    xc = jax.nn.silu(causal_dwconv(xb, conv_w, conv_b))

    def headwise(t, w):
        tb = t.reshape(bsz, seq, width // MLSTM_QKV_BLOCK, MLSTM_QKV_BLOCK)
        return jnp.einsum('bsnd,nde->bsne', tb, w).reshape(bsz, seq, width)

    q = headwise(xc, w_q)
    k = headwise(xc, w_k)
    v = headwise(xb, w_v)
    qkv = jnp.concatenate([q, k, v], axis=-1)
    ig = jnp.transpose(qkv @ w_ig + b_ig, (0, 2, 1))
    lf = jnp.transpose(jax.nn.log_sigmoid(qkv @ w_fg + b_fg), (0, 2, 1))

    def heads(t):
        return jnp.transpose(t.reshape(bsz, seq, MLSTM_HEADS, MLSTM_HEAD_DIM), (0, 2, 1, 3))

    h = mlstm_chunkwise(heads(q), heads(k), heads(v), ig, lf)
    h = jnp.transpose(h, (0, 2, 1, 3))
    mu = jnp.mean(h, axis=-1, keepdims=True)
    var = jnp.mean(jnp.square(h - mu), axis=-1, keepdims=True)
    hn = ((h - mu) * lax.rsqrt(var + EPS)).reshape(bsz, seq, width) * norm_w
    return (hn + skip * xc) * jax.nn.silu(zb)


def even_mixer(u, w_in, lru_conv_w, lru_conv_b, lru_w_r, lru_b_r, lru_w_i, lru_b_i, lru_lambda,
               ml_conv_w, ml_conv_b, ml_w_q, ml_w_k, ml_w_v, ml_w_ig, ml_b_ig, ml_w_fg, ml_b_fg,
               ml_norm, ml_skip, w_out):
    proj = u @ w_in
    xa, ga, xb, zb = jnp.split(proj, [LRU_WIDTH, 2 * LRU_WIDTH, 2 * LRU_WIDTH + MLSTM_WIDTH], axis=-1)
    ya = rg_lru(causal_dwconv(xa, lru_conv_w, lru_conv_b), lru_w_r, lru_b_r, lru_w_i, lru_b_i, lru_lambda)
    ya = ya * jax.nn.gelu(ga)
    yb = mlstm_branch(xb, zb, ml_conv_w, ml_conv_b, ml_w_q, ml_w_k, ml_w_v, ml_w_ig, ml_b_ig,
                      ml_w_fg, ml_b_fg, ml_norm, ml_skip)
    return jnp.concatenate([ya, yb], axis=-1) @ w_out


def ssd_chunked(x, dt, A, Bm, Cm):
    bsz, seq = x.shape[:2]
    L = SSD_CHUNK
    nc = seq // L

    def to_chunks(t):
        return jnp.moveaxis(t.reshape(bsz, nc, L, *t.shape[2:]), 1, 0)

    causal = jnp.tril(jnp.ones((L, L), dtype=bool))[None, :, :, None, None]

    def step(state, inp):
        x_, dt_, B_, C_ = inp
        a = jnp.cumsum(dt_ * A, axis=1)
        seg = a[:, :, None] - a[:, None, :]
        decay = jnp.exp(jnp.where(causal, seg, -jnp.inf))
        cb = jnp.einsum('btgn,bsgn->btsg', C_, B_)
        w = cb[..., None] * decay * dt_[:, None]
        y_intra = jnp.einsum('btsgh,bsghp->btghp', w, x_)
        y_inter = jnp.einsum('btgn,bghpn->btghp', C_, state) * jnp.exp(a)[..., None]
        a_last = a[:, -1]
        ws = jnp.exp(a_last[:, None] - a) * dt_
        new_state = jnp.exp(a_last)[..., None, None] * state + jnp.einsum('bsgh,bsghp,bsgn->bghpn', ws, x_, B_)
        return new_state, y_intra + y_inter

    init = jnp.zeros((bsz, SSD_GROUPS, SSD_HPG, SSD_HEAD_DIM, SSD_STATE), jnp.float32)
    _, ys = lax.scan(step, init, (to_chunks(x), to_chunks(dt), to_chunks(Bm), to_chunks(Cm)))
    return jnp.moveaxis(ys, 0, 1).reshape(x.shape)


def ssd_mixer(u, w_in, conv_w, conv_b, dt_bias, a_log, d_skip, norm_w, w_out):
    bsz, seq, _ = u.shape
    proj = u @ w_in
    z = proj[..., :SSD_INNER]
    xbc = jax.nn.silu(causal_dwconv(proj[..., SSD_INNER:SSD_INNER + SSD_CONV_CH], conv_w, conv_b)).astype(jnp.float32)
    dt = jax.nn.softplus(proj[..., SSD_INNER + SSD_CONV_CH:] + dt_bias).astype(jnp.float32)
    gn = SSD_GROUPS * SSD_STATE
    xs = xbc[..., :SSD_INNER].reshape(bsz, seq, SSD_GROUPS, SSD_HPG, SSD_HEAD_DIM)
    Bm = xbc[..., SSD_INNER:SSD_INNER + gn].reshape(bsz, seq, SSD_GROUPS, SSD_STATE)
    Cm = xbc[..., SSD_INNER + gn:].reshape(bsz, seq, SSD_GROUPS, SSD_STATE)
    dt = dt.reshape(bsz, seq, SSD_GROUPS, SSD_HPG)
    A = -jnp.exp(a_log.astype(jnp.float32)).reshape(SSD_GROUPS, SSD_HPG)
    y = ssd_chunked(xs, dt, A, Bm, Cm)
    y = y + d_skip.reshape(SSD_GROUPS, SSD_HPG)[:, :, None] * xs
    y = (y.reshape(bsz, seq, SSD_INNER) * jax.nn.silu(z)).reshape(bsz, seq, SSD_GROUPS, SSD_INNER // SSD_GROUPS)
    y = y * lax.rsqrt(jnp.mean(y * y, axis=-1, keepdims=True) + EPS)
    y = y.reshape(bsz, seq, SSD_INNER) * norm_w
    return y @ w_out


def moe_ffn(u, router_w, router_b, w_gu, b_gu, w_down, b_down):
    bsz, seq, dm = u.shape
    xf = u.reshape(-1, dm)
    nt = xf.shape[0]
    logits = (xf @ router_w + router_b).astype(jnp.float32)
    top_v, top_i = lax.top_k(logits, TOP_K)
    gates = jax.nn.softmax(top_v, axis=-1)
    n_assign = nt * TOP_K
    flat_e = top_i.reshape(-1)
    flat_t = jnp.arange(n_assign, dtype=jnp.int32) // TOP_K
    flat_g = gates.reshape(-1)
    order = jnp.argsort(flat_e)
    e_sorted = flat_e[order]
    counts = jnp.bincount(flat_e, length=N_EXPERTS)
    padded = (counts + MOE_BLOCK - 1) // MOE_BLOCK * MOE_BLOCK
    pad_end = jnp.cumsum(padded)
    pad_start = pad_end - padded
    start = jnp.cumsum(counts) - counts
    dest = pad_start[e_sorted] + jnp.arange(n_assign, dtype=jnp.int32) - start[e_sorted]
    n_blocks = -(-n_assign // MOE_BLOCK) + N_EXPERTS
    n_rows = n_blocks * MOE_BLOCK
    row_tok = jnp.zeros((n_rows,), jnp.int32).at[dest].set(flat_t[order])
    row_gate = jnp.zeros((n_rows,), gates.dtype).at[dest].set(flat_g[order])
    block_e = jnp.minimum(jnp.searchsorted(pad_end, jnp.arange(n_blocks) * MOE_BLOCK, side='right'), N_EXPERTS - 1)

    def expert_block(args):
        e, tok, g = args
        hb = xf[tok] @ w_gu[e] + b_gu[e]
        h_glu = jnp.minimum(hb[:, :D_FF], SWIGLU_LIMIT)
        h_lin = jnp.clip(hb[:, D_FF:], -SWIGLU_LIMIT, SWIGLU_LIMIT)
        act = h_glu * jax.nn.sigmoid(SWIGLU_ALPHA * h_glu) * (h_lin + 1.0)
        return ((act @ w_down[e] + b_down[e]) * g[:, None]).astype(xf.dtype)

    y = lax.map(expert_block, (block_e, row_tok.reshape(n_blocks, MOE_BLOCK), row_gate.reshape(n_blocks, MOE_BLOCK)))
    out = jnp.zeros_like(xf).at[row_tok].add(y.reshape(n_rows, dm))
    return out.reshape(bsz, seq, dm)


def setup_inputs(seed: int = 0) -> dict:
    key = jax.random.key(seed)
    ks = iter(jax.random.split(key, 64))

    def nrm(shape, s):
        return jax.random.normal(next(ks), shape, jnp.float32) * s

    def gain(shape):
        return 1.0 + nrm(shape, 0.02)

    D = D_MODEL
    NE = N_EVEN
    NO = N_ODD
    x = nrm((BATCH, SEQ, D), 1.0)
    c = nrm((BATCH, D), 1.0)
    mod_w = nrm((DEPTH, D, 6 * D), D ** -0.5)
    mod_b = nrm((DEPTH, 6 * D), 0.01)
    norm_mix = gain((DEPTH, D))
    norm_ffn = gain((DEPTH, D))
    ev_w_in = nrm((NE, D, EVEN_IN), D ** -0.5)
    ev_lru_conv_w = nrm((NE, CONV_WIDTH, LRU_WIDTH), CONV_WIDTH ** -0.5)
    ev_lru_conv_b = nrm((NE, LRU_WIDTH), 0.01)
    ev_lru_w_r = nrm((NE, LRU_HEADS, LRU_HEAD_DIM, LRU_HEAD_DIM), LRU_HEAD_DIM ** -0.5)
    ev_lru_b_r = nrm((NE, LRU_HEADS, LRU_HEAD_DIM), 0.1)
    ev_lru_w_i = nrm((NE, LRU_HEADS, LRU_HEAD_DIM, LRU_HEAD_DIM), LRU_HEAD_DIM ** -0.5)
    ev_lru_b_i = nrm((NE, LRU_HEADS, LRU_HEAD_DIM), 0.1)
    a0 = jax.random.uniform(next(ks), (NE, LRU_WIDTH), jnp.float32, 0.9, 0.999) ** (1.0 / LRU_C)
    ev_lru_lambda = jnp.log(a0) - jnp.log1p(-a0)
    ev_ml_conv_w = nrm((NE, CONV_WIDTH, MLSTM_WIDTH), CONV_WIDTH ** -0.5)
    ev_ml_conv_b = nrm((NE, MLSTM_WIDTH), 0.01)
    nb = MLSTM_WIDTH // MLSTM_QKV_BLOCK
    ev_ml_w_q = nrm((NE, nb, MLSTM_QKV_BLOCK, MLSTM_QKV_BLOCK), MLSTM_QKV_BLOCK ** -0.5)
    ev_ml_w_k = nrm((NE, nb, MLSTM_QKV_BLOCK, MLSTM_QKV_BLOCK), MLSTM_QKV_BLOCK ** -0.5)
    ev_ml_w_v = nrm((NE, nb, MLSTM_QKV_BLOCK, MLSTM_QKV_BLOCK), MLSTM_QKV_BLOCK ** -0.5)
    ev_ml_w_ig = nrm((NE, 3 * MLSTM_WIDTH, MLSTM_HEADS), (3 * MLSTM_WIDTH) ** -0.5)
    ev_ml_b_ig = nrm((NE, MLSTM_HEADS), 0.1)
    ev_ml_w_fg = nrm((NE, 3 * MLSTM_WIDTH, MLSTM_HEADS), (3 * MLSTM_WIDTH) ** -0.5)
    ev_ml_b_fg = jnp.linspace(3.0, 6.0, MLSTM_HEADS, dtype=jnp.float32)[None, :] + nrm((NE, MLSTM_HEADS), 0.01)
    ev_ml_norm = gain((NE, MLSTM_WIDTH))
    ev_ml_skip = gain((NE, MLSTM_WIDTH))
    ev_w_out = nrm((NE, EVEN_MIX, D), EVEN_MIX ** -0.5)
    od_w_in = nrm((NO, D, SSD_IN), D ** -0.5)
    od_conv_w = nrm((NO, CONV_WIDTH, SSD_CONV_CH), CONV_WIDTH ** -0.5)
    od_conv_b = nrm((NO, SSD_CONV_CH), 0.01)
    dt0 = jnp.exp(jax.random.uniform(next(ks), (NO, SSD_HEADS), jnp.float32, math.log(1e-3), math.log(1e-1)))
    od_dt_bias = dt0 + jnp.log(-jnp.expm1(-dt0))
    od_a_log = jnp.log(jax.random.uniform(next(ks), (NO, SSD_HEADS), jnp.float32, 1.0, 16.0))
    od_d = 1.0 + nrm((NO, SSD_HEADS), 0.1)
    od_norm = gain((NO, SSD_INNER))
    od_w_out = nrm((NO, SSD_INNER, D), SSD_INNER ** -0.5)
    moe_router_w = nrm((DEPTH, D, N_EXPERTS), D ** -0.5)
    moe_router_b = nrm((DEPTH, N_EXPERTS), 0.01)
    moe_w_gu = nrm((DEPTH, N_EXPERTS, D, 2 * D_FF), D ** -0.5)
    moe_b_gu = nrm((DEPTH, N_EXPERTS, 2 * D_FF), 0.01)
    moe_w_down = nrm((DEPTH, N_EXPERTS, D_FF, D), D_FF ** -0.5)
    moe_b_down = nrm((DEPTH, N_EXPERTS, D), 0.01)
    final_norm = gain((D,))
    return {'x': x, 'c': c, 'mod_w': mod_w, 'mod_b': mod_b, 'norm_mix': norm_mix, 'norm_ffn': norm_ffn,
            'ev_w_in': ev_w_in, 'ev_lru_conv_w': ev_lru_conv_w, 'ev_lru_conv_b': ev_lru_conv_b,
            'ev_lru_w_r': ev_lru_w_r, 'ev_lru_b_r': ev_lru_b_r, 'ev_lru_w_i': ev_lru_w_i, 'ev_lru_b_i': ev_lru_b_i,
            'ev_lru_lambda': ev_lru_lambda, 'ev_ml_conv_w': ev_ml_conv_w, 'ev_ml_conv_b': ev_ml_conv_b,
            'ev_ml_w_q': ev_ml_w_q, 'ev_ml_w_k': ev_ml_w_k, 'ev_ml_w_v': ev_ml_w_v,
            'ev_ml_w_ig': ev_ml_w_ig, 'ev_ml_b_ig': ev_ml_b_ig, 'ev_ml_w_fg': ev_ml_w_fg, 'ev_ml_b_fg': ev_ml_b_fg,
            'ev_ml_norm': ev_ml_norm, 'ev_ml_skip': ev_ml_skip, 'ev_w_out': ev_w_out,
            'od_w_in': od_w_in, 'od_conv_w': od_conv_w, 'od_conv_b': od_conv_b, 'od_dt_bias': od_dt_bias,
            'od_a_log': od_a_log, 'od_d': od_d, 'od_norm': od_norm, 'od_w_out': od_w_out,
            'moe_router_w': moe_router_w, 'moe_router_b': moe_router_b, 'moe_w_gu': moe_w_gu,
            'moe_b_gu': moe_b_gu, 'moe_w_down': moe_w_down, 'moe_b_down': moe_b_down,
            'final_norm': final_norm}


def reference(x, c, mod_w, mod_b, norm_mix, norm_ffn,
              ev_w_in, ev_lru_conv_w, ev_lru_conv_b, ev_lru_w_r, ev_lru_b_r, ev_lru_w_i, ev_lru_b_i,
              ev_lru_lambda, ev_ml_conv_w, ev_ml_conv_b, ev_ml_w_q, ev_ml_w_k, ev_ml_w_v,
              ev_ml_w_ig, ev_ml_b_ig, ev_ml_w_fg, ev_ml_b_fg, ev_ml_norm, ev_ml_skip, ev_w_out,
              od_w_in, od_conv_w, od_conv_b, od_dt_bias, od_a_log, od_d, od_norm, od_w_out,
              moe_router_w, moe_router_b, moe_w_gu, moe_b_gu, moe_w_down, moe_b_down,
              final_norm):
    cond = jax.nn.silu(c.astype(jnp.float32))
    h = x
    for layer in range(DEPTH):
        mod = cond @ mod_w[layer] + mod_b[layer]
        sh_m, sc_m, g_m, sh_f, sc_f, g_f = jnp.split(mod, 6, axis=-1)
        u = modulate(rmsnorm(h, norm_mix[layer]), sh_m, sc_m)
        j = layer // 2
        if layer % 2 == 0:
            y = even_mixer(u, ev_w_in[j], ev_lru_conv_w[j], ev_lru_conv_b[j], ev_lru_w_r[j], ev_lru_b_r[j],
                           ev_lru_w_i[j], ev_lru_b_i[j], ev_lru_lambda[j], ev_ml_conv_w[j], ev_ml_conv_b[j],
                           ev_ml_w_q[j], ev_ml_w_k[j], ev_ml_w_v[j], ev_ml_w_ig[j], ev_ml_b_ig[j],
                           ev_ml_w_fg[j], ev_ml_b_fg[j], ev_ml_norm[j], ev_ml_skip[j], ev_w_out[j])
        else:
            y = ssd_mixer(u, od_w_in[j], od_conv_w[j], od_conv_b[j], od_dt_bias[j], od_a_log[j],
                          od_d[j], od_norm[j], od_w_out[j])
        h = h + g_m[:, None, :] * y
        u = modulate(rmsnorm(h, norm_ffn[layer]), sh_f, sc_f)
        h = h + g_f[:, None, :] * moe_ffn(u, moe_router_w[layer], moe_router_b[layer], moe_w_gu[layer],
                                          moe_b_gu[layer], moe_w_down[layer], moe_b_down[layer])
    return rmsnorm(h, final_norm)
```

```python
import functools

import jax
import jax.numpy as jnp
from jax import lax
from jax.experimental import pallas as pl
from jax.experimental.pallas import tpu as pltpu

F32 = jnp.float32
BF16 = jnp.bfloat16
I32 = jnp.int32
HIGHEST = lax.Precision.HIGHEST

EPS = 1e-6
CONV_WIDTH = 4
LANES = 128
SUBLANES = 8
LRU_HEADS = 8
LRU_C = 8.0
ML_HEADS = 8
ML_QKV_BLOCK = 4
CHUNK = 128
SSD_HEAD_DIM = 64
SSD_GROUPS = 8
SSD_STATE = 128
N_EXPERTS = 32
TOP_K = 4
SWIGLU_ALPHA = 1.702
SWIGLU_LIMIT = 7.0
EXPERT_BLOCK = 256
VMEM_LIMIT = 56 * 1024 * 1024


def _cparams(sem, **kw):
    return pltpu.CompilerParams(dimension_semantics=sem, vmem_limit_bytes=VMEM_LIMIT, **kw)


def _silu(x):
    return x * jax.nn.sigmoid(x)


def _log_sigmoid(x):
    return jnp.minimum(x, 0.0) - jnp.log1p(jnp.exp(-jnp.abs(x)))


def _softplus(x):
    return jnp.maximum(x, 0.0) + jnp.log1p(jnp.exp(-jnp.abs(x)))


def _dot(a, b, **kw):
    return jnp.dot(a, b, preferred_element_type=F32, **kw)


def _dot_nt(a, b):
    return lax.dot_general(a, b, (((1,), (1,)), ((), ())), preferred_element_type=F32)


def _pack_pairs(x):
    w = x.shape[1] // 2
    lo = lax.bitcast_convert_type(x[:, :w].astype(BF16).astype(F32), I32)
    hi = lax.bitcast_convert_type(x[:, w:].astype(BF16).astype(F32), I32)
    return lax.shift_right_logical(lo, 16) | (hi & jnp.int32(-65536))


def _unpack_pairs(p):
    lo = lax.bitcast_convert_type(lax.shift_left(p, 16), F32)
    hi = lax.bitcast_convert_type(p & jnp.int32(-65536), F32)
    return jnp.concatenate([lo, hi], axis=1)


def _norm_mod(h, g, shift, scale):
    y = h * lax.rsqrt(jnp.mean(h * h, axis=-1, keepdims=True) + EPS)
    return (y * g) * (1.0 + scale) + shift


def _causal_conv(x, tail_ref, w_ref, b_ref, sl):
    t = x.shape[0]
    tail = tail_ref[:, sl]
    row8 = lax.broadcasted_iota(I32, tail.shape, 0)
    out = b_ref[:, sl] + x * w_ref[CONV_WIDTH - 1:CONV_WIDTH, sl]
    for k in range(1, CONV_WIDTH):
        xs = pltpu.roll(x, k, axis=0)
        first = jnp.where(row8 < k, pltpu.roll(tail, k, axis=0), xs[:SUBLANES])
        xs = jnp.concatenate([first, xs[SUBLANES:]], axis=0)
        out = out + xs * w_ref[CONV_WIDTH - 1 - k:CONV_WIDTH - k, sl]
    tail_ref[:, sl] = x[t - SUBLANES:]
    return out


def _mod_kernel(c_ref, w_ref, b_ref, o_ref):
    cond = _silu(c_ref[...])
    o_ref[0, 0] = _dot(cond, w_ref[0], precision=HIGHEST) + b_ref[0, 0]


def _modulation(c, mod_w, mod_b):
    depth, d, _ = mod_w.shape
    bsz = c.shape[0]
    out = pl.pallas_call(
        _mod_kernel,
        out_shape=jax.ShapeDtypeStruct((depth, 6, bsz, d), F32),
        grid=(depth, 6),
        in_specs=[pl.BlockSpec((bsz, d), lambda l, j: (0, 0)),
                  pl.BlockSpec((1, d, d), lambda l, j: (l, 0, j)),
                  pl.BlockSpec((1, 1, 1, d), lambda l, j: (l, j, 0, 0))],
        out_specs=pl.BlockSpec((1, 1, bsz, d), lambda l, j: (l, j, 0, 0)),
        compiler_params=_cparams(("parallel", "parallel")),
    )(c.astype(F32), mod_w, mod_b.reshape(depth, 6, 1, d))
    return out.reshape(depth, 6, bsz, 1, d)


def _inproj_kernel(h_ref, g_ref, sh_ref, sc_ref, w_ref, *rest, n_chunk, with_dt):
    if with_dt:
        wdt_ref, *o_refs, odt_ref = rest
    else:
        o_refs = rest
    u = _norm_mod(h_ref[...], g_ref[...], sh_ref[0], sc_ref[0]).astype(BF16)
    off = 0
    for o_ref in o_refs:
        for n0 in range(0, o_ref.shape[1], n_chunk):
            o_ref[:, n0:n0 + n_chunk] = _dot(u, w_ref[:, off + n0:off + n0 + n_chunk]).astype(o_ref.dtype)
        off += o_ref.shape[1]
    if with_dt:
        odt_ref[...] = _dot(u, wdt_ref[...])


def _inproj(h, g, shift, scale, w, wdt, splits, seq, tm):
    m, d = h.shape
    n = w.shape[1]
    assert sum(splits) == n
    tiles_per_seq = seq // tm
    bmap = lambda i: (i // tiles_per_seq, 0, 0)
    in_specs = [pl.BlockSpec((tm, d), lambda i: (i, 0)),
                pl.BlockSpec((1, d), lambda i: (0, 0)),
                pl.BlockSpec((1, 1, d), bmap),
                pl.BlockSpec((1, 1, d), bmap),
                pl.BlockSpec((d, n), lambda i: (0, 0), pipeline_mode=pl.Buffered(1))]
    out_shape = [jax.ShapeDtypeStruct((m, s), BF16) for s in splits]
    out_specs = [pl.BlockSpec((tm, s), lambda i: (i, 0)) for s in splits]
    args = [h, g.reshape(1, d), shift, scale, w]
    if wdt is not None:
        in_specs.append(pl.BlockSpec((d, LANES), lambda i: (0, 0)))
        out_shape.append(jax.ShapeDtypeStruct((m, LANES), F32))
        out_specs.append(pl.BlockSpec((tm, LANES), lambda i: (i, 0)))
        args.append(wdt)
    return pl.pallas_call(
        functools.partial(_inproj_kernel, n_chunk=1024, with_dt=wdt is not None),
        out_shape=out_shape, grid=(m // tm,), in_specs=in_specs, out_specs=out_specs,
        compiler_params=_cparams(("parallel",)),
    )(*args)


def _lru_kernel(xa_ref, ga_ref, cw_ref, cb_ref, wr_ref, br_ref, wi_ref, bi_ref, lam_ref,
                o_ref, tail_ref, hc_ref):
    @pl.when(pl.program_id(1) == 0)
    def _():
        tail_ref[...] = jnp.zeros_like(tail_ref)
        hc_ref[...] = jnp.zeros_like(hc_ref)

    t = xa_ref.shape[0]
    row = lax.broadcasted_iota(I32, (t, LANES), 0)
    for hh in range(LRU_HEADS):
        sl = slice(hh * LANES, (hh + 1) * LANES)
        xc = _causal_conv(xa_ref[:, sl].astype(F32), tail_ref, cw_ref, cb_ref, sl)
        xcb = xc.astype(BF16)
        r = jax.nn.sigmoid(_dot(xcb, wr_ref[hh]) + br_ref[:, sl])
        i = jax.nn.sigmoid(_dot(xcb, wi_ref[hh]) + bi_ref[:, sl])
        log_a = LRU_C * r * _log_sigmoid(lam_ref[:, sl])
        a = jnp.exp(log_a)
        th = jnp.tanh(log_a)
        u = jnp.sqrt(-2.0 * th / (1.0 - th)) * (i * xc)
        s = 1
        while s < t:
            m = row >= s
            u = jnp.where(m, u + a * pltpu.roll(u, s, axis=0), u)
            a = jnp.where(m, a * pltpu.roll(a, s, axis=0), a)
            s *= 2
        h = u + a * hc_ref[:, sl]
        hc_ref[:, sl] = h[t - 1:t]
        ga = ga_ref[:, sl].astype(F32)
        o_ref[:, sl] = (h * jax.nn.gelu(ga, approximate=True)).astype(o_ref.dtype)


def _lru(proj, p, bsz, seq, tm):
    m = proj.shape[0]
    w = LRU_HEADS * LANES
    nt = seq // tm
    vec = lambda: pl.BlockSpec((1, w), lambda b, j: (0, 0))
    return pl.pallas_call(
        _lru_kernel,
        out_shape=jax.ShapeDtypeStruct((m, w), BF16),
        grid=(bsz, nt),
        in_specs=[pl.BlockSpec((tm, w), lambda b, j: (b * nt + j, 0)),
                  pl.BlockSpec((tm, w), lambda b, j: (b * nt + j, 1)),
                  pl.BlockSpec((CONV_WIDTH, w), lambda b, j: (0, 0)), vec(),
                  pl.BlockSpec((LRU_HEADS, LANES, LANES), lambda b, j: (0, 0, 0)), vec(),
                  pl.BlockSpec((LRU_HEADS, LANES, LANES), lambda b, j: (0, 0, 0)), vec(), vec()],
        out_specs=pl.BlockSpec((tm, w), lambda b, j: (b * nt + j, 0)),
        scratch_shapes=[pltpu.VMEM((SUBLANES, w), F32), pltpu.VMEM((1, w), F32)],
        compiler_params=_cparams(("parallel", "arbitrary")),
    )(proj, proj, p["conv_w"], p["conv_b"], p["w_r"], p["b_r"], p["w_i"], p["b_i"], p["lam"])


def _mlstm_kernel(xb_ref, zb_ref, cw_ref, cb_ref, wq_ref, wk_ref, wv_ref, wg_ref, bg_ref,
                  nw_ref, sk_ref, o_ref, tail_ref, qkv_ref, xc_ref, caug_ref, m_ref):
    @pl.when(pl.program_id(1) == 0)
    def _():
        tail_ref[...] = jnp.zeros_like(tail_ref)
        caug_ref[...] = jnp.zeros_like(caug_ref)
        m_ref[...] = jnp.full(m_ref.shape, -jnp.inf, F32)

    L = CHUNK
    width = ML_HEADS * LANES
    scale = LANES ** -0.5
    for hh in range(ML_HEADS):
        sl = slice(hh * LANES, (hh + 1) * LANES)
        xb = xb_ref[:, sl].astype(F32)
        xc = _silu(_causal_conv(xb, tail_ref, cw_ref, cb_ref, sl))
        xc_ref[:, sl] = xc
        xcb = xc.astype(BF16)
        qkv_ref[:, sl] = _dot(xcb, wq_ref[hh]).astype(BF16)
        qkv_ref[:, width + hh * LANES:width + (hh + 1) * LANES] = _dot(xcb, wk_ref[hh]).astype(BF16)
        qkv_ref[:, 2 * width + hh * LANES:2 * width + (hh + 1) * LANES] = (
            _dot(xb.astype(BF16), wv_ref[hh]).astype(BF16))

    gates = _dot(qkv_ref[...], wg_ref[...]) + bg_ref[...]
    rowi = lax.broadcasted_iota(I32, (L, L), 0)
    coli = lax.broadcasted_iota(I32, (L, L), 1)
    causal = rowi >= coli
    lf = jnp.where((coli >= ML_HEADS) & (coli < 2 * ML_HEADS), _log_sigmoid(gates), 0.0)
    gcum = _dot(causal.astype(F32), lf, precision=HIGHEST)
    x_col = jnp.where(coli < ML_HEADS, gates, gcum)
    x_row = x_col.T
    ones = jnp.ones((L, LANES), BF16)

    for hh in range(ML_HEADS):
        sl = slice(hh * LANES, (hh + 1) * LANES)
        q = qkv_ref[:, sl]
        k = qkv_ref[:, width + hh * LANES:width + (hh + 1) * LANES]
        v = qkv_ref[:, 2 * width + hh * LANES:2 * width + (hh + 1) * LANES]
        ic = x_col[:, hh:hh + 1]
        gc = x_col[:, ML_HEADS + hh:ML_HEADS + hh + 1]
        ir = x_row[hh:hh + 1, :]
        gr = x_row[ML_HEADS + hh:ML_HEADS + hh + 1, :]
        mp = m_ref[hh][:, 0:1]
        dmat = jnp.where(causal, gc - gr + ir, -jnp.inf)
        m_inter = mp + gc
        m_t = jnp.maximum(m_inter, jnp.max(dmat, axis=1, keepdims=True))
        w_intra = jnp.exp(dmat - m_t)
        w_inter = jnp.exp(m_inter - m_t)
        qk = (_dot_nt(q, k) * scale * w_intra).astype(BF16)
        v_aug = jnp.concatenate([v, ones], axis=1)
        caug = caug_ref[hh]
        nd = _dot(qk, v_aug) + w_inter * _dot(q, caug.astype(BF16))
        hval = nd[:, :LANES] / jnp.maximum(jnp.abs(nd[:, LANES:]), jnp.exp(-m_t))

        g_last = gc[L - 1:L, :]
        m_new = jnp.maximum(mp + g_last, jnp.max(g_last - gr + ir, axis=1, keepdims=True))
        ws = jnp.exp(g_last - gc + ic - m_new)
        wc = jnp.exp(mp + g_last - m_new)
        kw_t = (k.astype(F32) * (ws * scale)).T.astype(BF16)
        caug_ref[hh] = wc * caug + _dot(kw_t, v_aug)
        m_ref[hh] = jnp.broadcast_to(m_new, (1, LANES))

        mu = jnp.mean(hval, axis=1, keepdims=True)
        dv = hval - mu
        var = jnp.mean(dv * dv, axis=1, keepdims=True)
        hn = dv * lax.rsqrt(var + EPS) * nw_ref[:, sl]
        zb = zb_ref[:, sl].astype(F32)
        o_ref[:, sl] = ((hn + sk_ref[:, sl] * xc_ref[:, sl]) * _silu(zb)).astype(o_ref.dtype)


def _mlstm(proj, p, bsz, seq):
    m = proj.shape[0]
    w = ML_HEADS * LANES
    nt = seq // CHUNK
    vec = lambda: pl.BlockSpec((1, w), lambda b, j: (0, 0))
    blk = lambda: pl.BlockSpec((ML_HEADS, LANES, LANES), lambda b, j: (0, 0, 0))
    return pl.pallas_call(
        _mlstm_kernel,
        out_shape=jax.ShapeDtypeStruct((m, w), BF16),
        grid=(bsz, nt),
        in_specs=[pl.BlockSpec((CHUNK, w), lambda b, j: (b * nt + j, 2)),
                  pl.BlockSpec((CHUNK, w), lambda b, j: (b * nt + j, 3)),
                  pl.BlockSpec((CONV_WIDTH, w), lambda b, j: (0, 0)), vec(),
                  blk(), blk(), blk(),
                  pl.BlockSpec((3 * w, LANES), lambda b, j: (0, 0)),
                  pl.BlockSpec((1, LANES), lambda b, j: (0, 0)),
                  vec(), vec()],
        out_specs=pl.BlockSpec((CHUNK, w), lambda b, j: (b * nt + j, 0)),
        scratch_shapes=[pltpu.VMEM((SUBLANES, w), F32),
                        pltpu.VMEM((CHUNK, 3 * w), BF16),
                        pltpu.VMEM((CHUNK, w), F32),
                        pltpu.VMEM((ML_HEADS, LANES, 2 * LANES), F32),
                        pltpu.VMEM((ML_HEADS, 1, LANES), F32)],
        compiler_params=_cparams(("parallel", "arbitrary")),
    )(proj, proj, p["conv_w"], p["conv_b"], p["w_q"], p["w_k"], p["w_v"], p["w_g"], p["b_g"],
      p["norm"], p["skip"])


def _ssd_kernel(z_ref, xbc_ref, dt_ref, cw_ref, cb_ref, dtb_ref, alog_ref, dsk_ref, nw_ref,
                o_ref, tail_ref, act_ref, st_ref):
    @pl.when(pl.program_id(1) == 0)
    def _():
        tail_ref[...] = jnp.zeros_like(tail_ref)
        st_ref[...] = jnp.zeros_like(st_ref)

    L = CHUNK
    inner = o_ref.shape[1]
    gw = inner // SSD_GROUPS
    hpg = gw // SSD_HEAD_DIM
    b_off = inner
    c_off = inner + SSD_GROUPS * SSD_STATE
    for cg in range(xbc_ref.shape[1] // LANES):
        sl = slice(cg * LANES, (cg + 1) * LANES)
        act_ref[:, sl] = _silu(_causal_conv(xbc_ref[:, sl].astype(F32), tail_ref, cw_ref, cb_ref, sl))

    rowi = lax.broadcasted_iota(I32, (L, L), 0)
    coli = lax.broadcasted_iota(I32, (L, L), 1)
    causal = rowi >= coli
    dt = _softplus(dt_ref[...] + dtb_ref[...])
    a = _dot(causal.astype(F32), dt * (-jnp.exp(alog_ref[...])), precision=HIGHEST)
    a_t = a.T
    dt_t = dt.T
    ea = jnp.exp(a)
    wsd = jnp.exp(a[L - 1:L, :] - a) * dt
    lane = lax.broadcasted_iota(I32, (L, gw), 1)

    def expand(cols, g):
        out = jnp.broadcast_to(cols[:, g * hpg + hpg - 1:g * hpg + hpg], (L, gw))
        for jj in range(hpg - 2, -1, -1):
            bc = jnp.broadcast_to(cols[:, g * hpg + jj:g * hpg + jj + 1], (L, gw))
            out = jnp.where(lane < (jj + 1) * SSD_HEAD_DIM, bc, out)
        return out

    for g in range(SSD_GROUPS):
        gsl = slice(g * gw, (g + 1) * gw)
        xg = act_ref[:, gsl]
        bg = act_ref[:, b_off + g * SSD_STATE:b_off + (g + 1) * SSD_STATE]
        cg_ = act_ref[:, c_off + g * SSD_STATE:c_off + (g + 1) * SSD_STATE].astype(BF16)
        cb = _dot_nt(cg_, bg.astype(BF16))
        ea_x = expand(ea, g)
        state = st_ref[g]
        acc = _dot(cg_, state.astype(BF16)) * ea_x
        for jj in range(hpg):
            hd = g * hpg + jj
            seg = jnp.where(causal, a[:, hd:hd + 1] - a_t[hd:hd + 1, :], -jnp.inf)
            w = (cb * jnp.exp(seg) * dt_t[hd:hd + 1, :]).astype(BF16)
            in_head = (lane >= jj * SSD_HEAD_DIM) & (lane < (jj + 1) * SSD_HEAD_DIM)
            acc = acc + _dot(w, jnp.where(in_head, xg, 0.0).astype(BF16))
        y = (acc + dsk_ref[:, gsl] * xg) * _silu(z_ref[:, gsl].astype(F32))
        y = y * lax.rsqrt(jnp.mean(y * y, axis=1, keepdims=True) + EPS) * nw_ref[:, gsl]
        o_ref[:, gsl] = y.astype(o_ref.dtype)
        xw = (xg * expand(wsd, g)).astype(BF16)
        st_ref[g] = ea_x[L - 1:L, :] * state + _dot(bg.T.astype(BF16), xw)


def _ssd(z, xbc, dt_raw, p, bsz, seq):
    m, inner = z.shape
    nt = seq // CHUNK
    conv_ch = xbc.shape[1]
    vec = lambda n: pl.BlockSpec((1, n), lambda b, j: (0, 0))
    return pl.pallas_call(
        _ssd_kernel,
        out_shape=jax.ShapeDtypeStruct((m, inner), BF16),
        grid=(bsz, nt),
        in_specs=[pl.BlockSpec((CHUNK, inner), lambda b, j: (b * nt + j, 0)),
                  pl.BlockSpec((CHUNK, conv_ch), lambda b, j: (b * nt + j, 0)),
                  pl.BlockSpec((CHUNK, LANES), lambda b, j: (b * nt + j, 0)),
                  pl.BlockSpec((CONV_WIDTH, conv_ch), lambda b, j: (0, 0)), vec(conv_ch),
                  vec(LANES), vec(LANES), vec(inner), vec(inner)],
        out_specs=pl.BlockSpec((CHUNK, inner), lambda b, j: (b * nt + j, 0)),
        scratch_shapes=[pltpu.VMEM((SUBLANES, conv_ch), F32),
                        pltpu.VMEM((CHUNK, conv_ch), F32),
                        pltpu.VMEM((SSD_GROUPS, SSD_STATE, inner // SSD_GROUPS), F32)],
        compiler_params=_cparams(("parallel", "arbitrary")),
    )(z, xbc, dt_raw, p["conv_w"], p["conv_b"], p["dt_bias"], p["a_log"], p["d_skip"],
      p["norm"])


def _outproj_kernel(*refs, n_in):
    y_refs, w_refs = refs[:n_in], refs[n_in:2 * n_in]
    h_ref, g_ref, o_ref = refs[2 * n_in:]
    acc = _dot(y_refs[0][...], w_refs[0][...])
    for y_ref, w_ref in zip(y_refs[1:], w_refs[1:]):
        acc = acc + _dot(y_ref[...], w_ref[...])
    o_ref[...] = h_ref[...] + g_ref[0] * acc


def _outproj(ys, w, h, gate, seq, tm):
    m, d = h.shape
    tiles_per_seq = seq // tm
    in_specs, args, k0 = [], [], 0
    for y in ys:
        in_specs.append(pl.BlockSpec((tm, y.shape[1]), lambda i: (i, 0)))
        args.append(y)
    for y in ys:
        kk = y.shape[1]
        in_specs.append(pl.BlockSpec((kk, d), lambda i, kb=k0 // kk: (kb, 0)))
        args.append(w)
        k0 += kk
    in_specs += [pl.BlockSpec((tm, d), lambda i: (i, 0)),
                 pl.BlockSpec((1, 1, d), lambda i: (i // tiles_per_seq, 0, 0))]
    args += [h, gate]
    return pl.pallas_call(
        functools.partial(_outproj_kernel, n_in=len(ys)),
        out_shape=jax.ShapeDtypeStruct((m, d), F32),
        grid=(m // tm,), in_specs=in_specs,
        out_specs=pl.BlockSpec((tm, d), lambda i: (i, 0)),
        compiler_params=_cparams(("parallel",)),
    )(*args)


def _router_kernel(h_ref, g_ref, sh_ref, sc_ref, wr_ref, br_ref,
                   up_ref, topi_ref, gate_ref, rank_ref, cnt_ref, carry_ref):
    @pl.when(pl.program_id(0) == 0)
    def _():
        carry_ref[...] = jnp.zeros_like(carry_ref)

    tm = h_ref.shape[0]
    u = _norm_mod(h_ref[...], g_ref[...], sh_ref[0], sc_ref[0])
    up_ref[...] = _pack_pairs(u)
    logits = _dot(u, wr_ref[...], precision=HIGHEST) + br_ref[...]
    lt = jnp.concatenate([logits[r0:r0 + LANES].T for r0 in range(0, tm, LANES)], axis=1)
    l = lt[:N_EXPERTS]
    e_iota = lax.broadcasted_iota(I32, (N_EXPERTS, tm), 0).astype(F32)
    vals, idxs, hots = [], [], []
    for _ in range(TOP_K):
        mx = jnp.max(l, axis=0, keepdims=True)
        idx = jnp.min(jnp.where(l == mx, e_iota, float(N_EXPERTS)), axis=0, keepdims=True)
        hot = e_iota == idx
        l = jnp.where(hot, -jnp.inf, l)
        vals.append(mx)
        idxs.append(idx)
        hots.append(hot)
    exps = [jnp.exp(v - vals[0]) for v in vals]
    den = exps[0] + exps[1] + exps[2] + exps[3]
    gate_ref[...] = jnp.concatenate([e / den for e in exps], axis=0)
    topi_ref[...] = jnp.concatenate(idxs, axis=0).astype(I32)

    sel = jnp.zeros((N_EXPERTS, tm), F32)
    for hot in hots:
        sel = jnp.where(hot, 1.0, sel)
    r_i = lax.broadcasted_iota(I32, (tm, tm), 0)
    c_i = lax.broadcasted_iota(I32, (tm, tm), 1)
    before = (r_i < c_i).astype(BF16)
    carry = carry_ref[:, 0:1]
    cum = _dot(sel.astype(BF16), before) + carry
    rank_ref[...] = jnp.concatenate(
        [jnp.sum(jnp.where(hot, cum, 0.0), axis=0, keepdims=True) for hot in hots], axis=0).astype(I32)
    total = carry + jnp.sum(sel, axis=1, keepdims=True)
    carry_ref[...] = jnp.broadcast_to(total, carry_ref.shape)
    cnt_ref[...] = jnp.broadcast_to(total, cnt_ref.shape)


def _router(h, g, shift, scale, wr, br, seq, tm):
    m, d = h.shape
    tiles_per_seq = seq // tm
    bmap = lambda i: (i // tiles_per_seq, 0, 0)
    row4 = lambda: pl.BlockSpec((TOP_K, tm), lambda i: (0, i))
    return pl.pallas_call(
        _router_kernel,
        out_shape=[jax.ShapeDtypeStruct((m, d // 2), I32),
                   jax.ShapeDtypeStruct((TOP_K, m), I32),
                   jax.ShapeDtypeStruct((TOP_K, m), F32),
                   jax.ShapeDtypeStruct((TOP_K, m), I32),
                   jax.ShapeDtypeStruct((N_EXPERTS, LANES), F32)],
        grid=(m // tm,),
        in_specs=[pl.BlockSpec((tm, d), lambda i: (i, 0)),
                  pl.BlockSpec((1, d), lambda i: (0, 0)),
                  pl.BlockSpec((1, 1, d), bmap), pl.BlockSpec((1, 1, d), bmap),
                  pl.BlockSpec((d, LANES), lambda i: (0, 0)),
                  pl.BlockSpec((1, LANES), lambda i: (0, 0))],
        out_specs=[pl.BlockSpec((tm, d // 2), lambda i: (i, 0)), row4(), row4(), row4(),
                   pl.BlockSpec((N_EXPERTS, LANES), lambda i: (0, 0))],
        scratch_shapes=[pltpu.VMEM((N_EXPERTS, LANES), F32)],
        compiler_params=_cparams(("arbitrary",)),
    )(h, g.reshape(1, d), shift, scale, wr, br)


def _dispatch_kernel(dest_ref, up_ref, xs_ref, sem):
    tm = up_ref.shape[0]

    def copy(t, k):
        d = dest_ref[k * tm + t]
        return pltpu.make_async_copy(up_ref.at[pl.ds(t, 1)], xs_ref.at[pl.ds(d, 1)], sem)

    def issue(t, c):
        for k in range(TOP_K):
            copy(t, k).start()
        return c

    def drain(t, c):
        for k in range(TOP_K):
            copy(t, k).wait()
        return c

    lax.fori_loop(0, tm, issue, 0)
    lax.fori_loop(0, tm, drain, 0)


def _dispatch(dest_tiles, up, n_rows, tm):
    m, wp = up.shape
    return pl.pallas_call(
        _dispatch_kernel,
        out_shape=jax.ShapeDtypeStruct((n_rows, wp), I32),
        grid=(m // tm,),
        in_specs=[pl.BlockSpec((TOP_K * tm,), lambda i: (i,), memory_space=pltpu.SMEM),
                  pl.BlockSpec((tm, wp), lambda i: (i, 0))],
        out_specs=pl.BlockSpec(memory_space=pl.ANY),
        scratch_shapes=[pltpu.SemaphoreType.DMA],
        compiler_params=_cparams(("arbitrary",), has_side_effects=True),
    )(dest_tiles, up)


def _combine_kernel(dest_ref, gate_ref, h_ref, gf_ref, fn_ref, y_ref, o_ref, buf_ref, sem, *, final):
    tm = h_ref.shape[0]

    def copy(t, k):
        d = dest_ref[k * tm + t]
        return pltpu.make_async_copy(y_ref.at[pl.ds(d, 1)], buf_ref.at[k, pl.ds(t, 1)], sem)

    def issue(t, c):
        for k in range(TOP_K):
            copy(t, k).start()
        return c

    def drain(t, c):
        for k in range(TOP_K):
            copy(t, k).wait()
        return c

    lax.fori_loop(0, tm, issue, 0)
    lax.fori_loop(0, tm, drain, 0)
    acc = gate_ref[:, 0:1] * _unpack_pairs(buf_ref[0])
    for k in range(1, TOP_K):
        acc = acc + gate_ref[:, k:k + 1] * _unpack_pairs(buf_ref[k])
    hn = h_ref[...] + gf_ref[0] * acc
    if final:
        hn = hn * lax.rsqrt(jnp.mean(hn * hn, axis=-1, keepdims=True) + EPS) * fn_ref[...]
    o_ref[...] = hn


def _combine(dest_tiles, gates_col, h, gf, fnorm, y, seq, tm, final):
    m, d = h.shape
    tiles_per_seq = seq // tm
    return pl.pallas_call(
        functools.partial(_combine_kernel, final=final),
        out_shape=jax.ShapeDtypeStruct((m, d), F32),
        grid=(m // tm,),
        in_specs=[pl.BlockSpec((TOP_K * tm,), lambda i: (i,), memory_space=pltpu.SMEM),
                  pl.BlockSpec((tm, TOP_K), lambda i: (i, 0)),
                  pl.BlockSpec((tm, d), lambda i: (i, 0)),
                  pl.BlockSpec((1, 1, d), lambda i: (i // tiles_per_seq, 0, 0)),
                  pl.BlockSpec((1, d), lambda i: (0, 0)),
                  pl.BlockSpec(memory_space=pl.ANY)],
        out_specs=pl.BlockSpec((tm, d), lambda i: (i, 0)),
        scratch_shapes=[pltpu.VMEM((TOP_K, tm, d // 2), I32), pltpu.SemaphoreType.DMA],
        compiler_params=_cparams(("arbitrary",)),
    )(dest_tiles, gates_col, h, gf, fnorm.reshape(1, d), y)


def _expert_kernel(be_ref, nb_ref, x_ref, wgu_ref, bgu_ref, wd_ref, bd_ref, y_ref):
    @pl.when(pl.program_id(0) < nb_ref[0])
    def _():
        dff = wd_ref.shape[1]
        x = _unpack_pairs(x_ref[...]).astype(BF16)
        hb = _dot(x, wgu_ref[0]) + bgu_ref[0]
        h_glu = jnp.minimum(hb[:, :dff], SWIGLU_LIMIT)
        h_lin = jnp.clip(hb[:, dff:], -SWIGLU_LIMIT, SWIGLU_LIMIT)
        act = h_glu * jax.nn.sigmoid(SWIGLU_ALPHA * h_glu) * (h_lin + 1.0)
        y_ref[...] = _pack_pairs(_dot(act.astype(BF16), wd_ref[0]) + bd_ref[0])


def _experts(block_e, n_used, xs, wgu, bgu, wd, bd):
    n_rows, wp = xs.shape
    ne, d, ff2 = wgu.shape
    nblk = n_rows // EXPERT_BLOCK

    def xmap(i, be, nb):
        return (jnp.minimum(i, nb[0] - 1), 0)

    emap = lambda i, be, nb: (be[i], 0, 0)
    grid_spec = pltpu.PrefetchScalarGridSpec(
        num_scalar_prefetch=2, grid=(nblk,),
        in_specs=[pl.BlockSpec((EXPERT_BLOCK, wp), xmap),
                  pl.BlockSpec((1, d, ff2), emap), pl.BlockSpec((1, 1, ff2), emap),
                  pl.BlockSpec((1, ff2 // 2, d), emap), pl.BlockSpec((1, 1, d), emap)],
        out_specs=pl.BlockSpec((EXPERT_BLOCK, wp), xmap))
    return pl.pallas_call(
        _expert_kernel,
        out_shape=jax.ShapeDtypeStruct((n_rows, wp), I32),
        grid_spec=grid_spec,
        compiler_params=_cparams(("arbitrary",)),
    )(block_e, n_used, xs, wgu, bgu.reshape(ne, 1, ff2), wd, bd.reshape(ne, 1, d))


def _moe(h, g, shift, scale, gf, fnorm, wr, br, wgu, bgu, wd, bd, seq, final):
    m, d = h.shape
    tm = 256
    wr_p = jnp.zeros((d, LANES), F32).at[:, :N_EXPERTS].set(wr)
    br_p = jnp.zeros((1, LANES), F32).at[0, :N_EXPERTS].set(br)
    up, topi, gates, rank, cnt = _router(h, g, shift, scale, wr_p, br_p, seq, tm)

    counts = cnt[:, 0].astype(I32)
    padded = (counts + EXPERT_BLOCK - 1) // EXPERT_BLOCK * EXPERT_BLOCK
    pad_end = jnp.cumsum(padded)
    pad_start = pad_end - padded
    n_rows = (m * TOP_K // EXPERT_BLOCK + N_EXPERTS) * EXPERT_BLOCK
    nblk = n_rows // EXPERT_BLOCK
    dest = pad_start[topi] + rank
    dest_tiles = dest.reshape(TOP_K, m // tm, tm).transpose(1, 0, 2).reshape(-1)
    blk_start = jnp.arange(nblk, dtype=I32) * EXPERT_BLOCK
    block_e = jnp.minimum(jnp.sum(blk_start[:, None] >= pad_end[None, :], axis=1), N_EXPERTS - 1).astype(I32)
    n_used = (pad_end[-1:] // EXPERT_BLOCK).astype(I32)

    xs = _dispatch(dest_tiles, up, n_rows, tm)
    y = _experts(block_e, n_used, xs, wgu, bgu, wd, bd)
    return _combine(dest_tiles, gates.T, h, gf, fnorm, y, seq, tm, final)


def _block_diag(w, group):
    nb, b, _ = w.shape
    per = group // b
    wg = w.reshape(nb // per, per, b, b)
    dense = jnp.einsum("gnde,nm->gndme", wg, jnp.eye(per, dtype=w.dtype))
    return dense.reshape(nb // per, group, group)


def kernel(x, c, mod_w, mod_b, norm_mix, norm_ffn, ev_w_in, ev_lru_conv_w, ev_lru_conv_b, ev_lru_w_r, ev_lru_b_r, ev_lru_w_i, ev_lru_b_i, ev_lru_lambda, ev_ml_conv_w, ev_ml_conv_b, ev_ml_w_q, ev_ml_w_k, ev_ml_w_v, ev_ml_w_ig, ev_ml_b_ig, ev_ml_w_fg, ev_ml_b_fg, ev_ml_norm, ev_ml_skip, ev_w_out, od_w_in, od_conv_w, od_conv_b, od_dt_bias, od_a_log, od_d, od_norm, od_w_out, moe_router_w, moe_router_b, moe_w_gu, moe_b_gu, moe_w_down, moe_b_down, final_norm):
    bsz, seq, d = x.shape
    depth = mod_w.shape[0]
    m = bsz * seq
    mod = _modulation(c, mod_w, mod_b)
    h = x.reshape(m, d).astype(F32)
    for layer in range(depth):
        sh_m, sc_m, g_m, sh_f, sc_f, g_f = (mod[layer, i] for i in range(6))
        j = layer // 2
        if layer % 2 == 0:
            w = ev_lru_lambda.shape[1]
            w_in = ev_w_in[j].astype(BF16)
            proj = _inproj(h, norm_mix[layer], sh_m, sc_m, w_in, None, [w_in.shape[1]], seq, 512)[0]
            lru_p = dict(conv_w=ev_lru_conv_w[j], conv_b=ev_lru_conv_b[j].reshape(1, w),
                         w_r=ev_lru_w_r[j].astype(BF16), b_r=ev_lru_b_r[j].reshape(1, w),
                         w_i=ev_lru_w_i[j].astype(BF16), b_i=ev_lru_b_i[j].reshape(1, w),
                         lam=ev_lru_lambda[j].reshape(1, w))
            ya = _lru(proj, lru_p, bsz, seq, 256)
            wg = jnp.zeros((3 * w, LANES), F32)
            wg = wg.at[:, :ML_HEADS].set(ev_ml_w_ig[j]).at[:, ML_HEADS:2 * ML_HEADS].set(ev_ml_w_fg[j])
            bg = jnp.zeros((1, LANES), F32)
            bg = bg.at[0, :ML_HEADS].set(ev_ml_b_ig[j]).at[0, ML_HEADS:2 * ML_HEADS].set(ev_ml_b_fg[j])
            ml_p = dict(conv_w=ev_ml_conv_w[j], conv_b=ev_ml_conv_b[j].reshape(1, w),
                        w_q=_block_diag(ev_ml_w_q[j], LANES).astype(BF16),
                        w_k=_block_diag(ev_ml_w_k[j], LANES).astype(BF16),
                        w_v=_block_diag(ev_ml_w_v[j], LANES).astype(BF16),
                        w_g=wg.astype(BF16), b_g=bg,
                        norm=ev_ml_norm[j].reshape(1, w), skip=ev_ml_skip[j].reshape(1, w))
            yb = _mlstm(proj, ml_p, bsz, seq)
            h = _outproj([ya, yb], ev_w_out[j].astype(BF16), h, g_m, seq, 512)
        else:
            inner = od_norm.shape[1]
            heads = od_dt_bias.shape[1]
            conv_ch = od_conv_w.shape[2]
            w_in = od_w_in[j]
            wdt = jnp.zeros((d, LANES), F32).at[:, :heads].set(w_in[:, inner + conv_ch:])
            z, xbc, dt_raw = _inproj(h, norm_mix[layer], sh_m, sc_m, w_in[:, :inner + conv_ch].astype(BF16),
                                     wdt.astype(BF16), [inner, conv_ch], seq, 256)
            pad = lambda v: jnp.zeros((1, LANES), F32).at[0, :heads].set(v)
            ssd_p = dict(conv_w=od_conv_w[j], conv_b=od_conv_b[j].reshape(1, conv_ch),
                         dt_bias=pad(od_dt_bias[j]), a_log=pad(od_a_log[j]),
                         d_skip=jnp.repeat(od_d[j], SSD_HEAD_DIM).reshape(1, inner),
                         norm=od_norm[j].reshape(1, inner))
            y = _ssd(z, xbc, dt_raw, ssd_p, bsz, seq)
            h = _outproj([y], od_w_out[j].astype(BF16), h, g_m, seq, 512)
        h = _moe(h, norm_ffn[layer], sh_f, sc_f, g_f, final_norm,
                 moe_router_w[layer], moe_router_b[layer],
                 moe_w_gu[layer].astype(BF16), moe_b_gu[layer], moe_w_down[layer].astype(BF16),
                 moe_b_down[layer], seq, final=(layer == depth - 1))
    return h.reshape(bsz, seq, d)
```

```python
import functools

import jax
import jax.numpy as jnp
from jax import lax
from jax.experimental import pallas as pl
from jax.experimental.pallas import tpu as pltpu

F32 = jnp.float32
BF16 = jnp.bfloat16
I32 = jnp.int32
HIGHEST = lax.Precision.HIGHEST

EPS = 1e-6
CONV_WIDTH = 4
LANES = 128
SUBLANES = 8
LRU_HEADS = 8
LRU_C = 8.0
ML_HEADS = 8
ML_QKV_BLOCK = 4
CHUNK = 128
SSD_HEAD_DIM = 64
SSD_GROUPS = 8
SSD_STATE = 128
N_EXPERTS = 32
TOP_K = 4
SWIGLU_ALPHA = 1.702
SWIGLU_LIMIT = 7.0
EXPERT_BLOCK = 512
VMEM_LIMIT = 56 * 1024 * 1024


def _cparams(sem, **kw):
    return pltpu.CompilerParams(dimension_semantics=sem, vmem_limit_bytes=VMEM_LIMIT, **kw)


def _silu(x):
    return x * jax.nn.sigmoid(x)


def _log_sigmoid(x):
    return jnp.minimum(x, 0.0) - jnp.log1p(jnp.exp(-jnp.abs(x)))


def _softplus(x):
    return jnp.maximum(x, 0.0) + jnp.log1p(jnp.exp(-jnp.abs(x)))


def _dot(a, b, **kw):
    return jnp.dot(a, b, preferred_element_type=F32, **kw)


def _dot_nt(a, b):
    return lax.dot_general(a, b, (((1,), (1,)), ((), ())), preferred_element_type=F32)


def _pack_pairs(x):
    w = x.shape[1] // 2
    lo = lax.bitcast_convert_type(x[:, :w].astype(BF16).astype(F32), I32)
    hi = lax.bitcast_convert_type(x[:, w:].astype(BF16).astype(F32), I32)
    return lax.shift_right_logical(lo, 16) | (hi & jnp.int32(-65536))


def _unpack_pairs(p):
    lo = lax.bitcast_convert_type(lax.shift_left(p, 16), F32)
    hi = lax.bitcast_convert_type(p & jnp.int32(-65536), F32)
    return jnp.concatenate([lo, hi], axis=1)


def _norm_mod(h, g, shift, scale):
    y = h * lax.rsqrt(jnp.mean(h * h, axis=-1, keepdims=True) + EPS)
    return (y * g) * (1.0 + scale) + shift


def _causal_conv(x, tail_ref, w_ref, b_ref, sl):
    t = x.shape[0]
    tail = tail_ref[:, sl]
    row8 = lax.broadcasted_iota(I32, tail.shape, 0)
    out = b_ref[:, sl] + x * w_ref[CONV_WIDTH - 1:CONV_WIDTH, sl]
    for k in range(1, CONV_WIDTH):
        xs = pltpu.roll(x, k, axis=0)
        first = jnp.where(row8 < k, pltpu.roll(tail, k, axis=0), xs[:SUBLANES])
        xs = jnp.concatenate([first, xs[SUBLANES:]], axis=0)
        out = out + xs * w_ref[CONV_WIDTH - 1 - k:CONV_WIDTH - k, sl]
    tail_ref[:, sl] = x[t - SUBLANES:]
    return out


def _mod_kernel(c_ref, w_ref, b_ref, o_ref):
    cond = _silu(c_ref[...])
    o_ref[0, 0] = _dot(cond, w_ref[0], precision=HIGHEST) + b_ref[0, 0]


def _modulation(c, mod_w, mod_b):
    depth, d, _ = mod_w.shape
    bsz = c.shape[0]
    out = pl.pallas_call(
        _mod_kernel,
        out_shape=jax.ShapeDtypeStruct((depth, 6, bsz, d), F32),
        grid=(depth, 6),
        in_specs=[pl.BlockSpec((bsz, d), lambda l, j: (0, 0)),
                  pl.BlockSpec((1, d, d), lambda l, j: (l, 0, j)),
                  pl.BlockSpec((1, 1, 1, d), lambda l, j: (l, j, 0, 0))],
        out_specs=pl.BlockSpec((1, 1, bsz, d), lambda l, j: (l, j, 0, 0)),
        compiler_params=_cparams(("parallel", "parallel")),
    )(c.astype(F32), mod_w, mod_b.reshape(depth, 6, 1, d))
    return out.reshape(depth, 6, bsz, 1, d)


def _inproj_kernel(h_ref, g_ref, sh_ref, sc_ref, w_ref, *rest, n_chunk, with_dt):
    if with_dt:
        wdt_ref, *o_refs, odt_ref = rest
    else:
        o_refs = rest
    u = _norm_mod(h_ref[...], g_ref[...], sh_ref[0], sc_ref[0]).astype(BF16)
    off = 0
    for o_ref in o_refs:
        for n0 in range(0, o_ref.shape[1], n_chunk):
            o_ref[:, n0:n0 + n_chunk] = _dot(u, w_ref[:, off + n0:off + n0 + n_chunk]).astype(o_ref.dtype)
        off += o_ref.shape[1]
    if with_dt:
        odt_ref[...] = _dot(u, wdt_ref[...])


def _inproj(h, g, shift, scale, w, wdt, splits, seq, tm):
    m, d = h.shape
    n = w.shape[1]
    assert sum(splits) == n
    tiles_per_seq = seq // tm
    bmap = lambda i: (i // tiles_per_seq, 0, 0)
    in_specs = [pl.BlockSpec((tm, d), lambda i: (i, 0)),
                pl.BlockSpec((1, d), lambda i: (0, 0)),
                pl.BlockSpec((1, 1, d), bmap),
                pl.BlockSpec((1, 1, d), bmap),
                pl.BlockSpec((d, n), lambda i: (0, 0), pipeline_mode=pl.Buffered(1))]
    out_shape = [jax.ShapeDtypeStruct((m, s), BF16) for s in splits]
    out_specs = [pl.BlockSpec((tm, s), lambda i: (i, 0)) for s in splits]
    args = [h, g.reshape(1, d), shift, scale, w]
    if wdt is not None:
        in_specs.append(pl.BlockSpec((d, LANES), lambda i: (0, 0)))
        out_shape.append(jax.ShapeDtypeStruct((m, LANES), F32))
        out_specs.append(pl.BlockSpec((tm, LANES), lambda i: (i, 0)))
        args.append(wdt)
    return pl.pallas_call(
        functools.partial(_inproj_kernel, n_chunk=1024, with_dt=wdt is not None),
        out_shape=out_shape, grid=(m // tm,), in_specs=in_specs, out_specs=out_specs,
        compiler_params=_cparams(("parallel",)),
    )(*args)


def _lru_kernel(xa_ref, ga_ref, cw_ref, cb_ref, wr_ref, br_ref, wi_ref, bi_ref, lam_ref,
                o_ref, tail_ref, hc_ref):
    @pl.when(pl.program_id(1) == 0)
    def _():
        tail_ref[...] = jnp.zeros_like(tail_ref)
        hc_ref[...] = jnp.zeros_like(hc_ref)

    t = xa_ref.shape[0]
    row = lax.broadcasted_iota(I32, (t, LANES), 0)
    for hh in range(LRU_HEADS):
        sl = slice(hh * LANES, (hh + 1) * LANES)
        xc = _causal_conv(xa_ref[:, sl].astype(F32), tail_ref, cw_ref, cb_ref, sl)
        xcb = xc.astype(BF16)
        r = jax.nn.sigmoid(_dot(xcb, wr_ref[hh]) + br_ref[:, sl])
        i = jax.nn.sigmoid(_dot(xcb, wi_ref[hh]) + bi_ref[:, sl])
        log_a = LRU_C * r * _log_sigmoid(lam_ref[:, sl])
        a = jnp.exp(log_a)
        th = jnp.tanh(log_a)
        u = jnp.sqrt(-2.0 * th / (1.0 - th)) * (i * xc)
        s = 1
        while s < t:
            m = row >= s
            u = jnp.where(m, u + a * pltpu.roll(u, s, axis=0), u)
            a = jnp.where(m, a * pltpu.roll(a, s, axis=0), a)
            s *= 2
        h = u + a * hc_ref[:, sl]
        hc_ref[:, sl] = h[t - 1:t]
        ga = ga_ref[:, sl].astype(F32)
        o_ref[:, sl] = (h * jax.nn.gelu(ga, approximate=True)).astype(o_ref.dtype)


def _lru(proj, p, bsz, seq, tm):
    m = proj.shape[0]
    w = LRU_HEADS * LANES
    nt = seq // tm
    vec = lambda: pl.BlockSpec((1, w), lambda b, j: (0, 0))
    return pl.pallas_call(
        _lru_kernel,
        out_shape=jax.ShapeDtypeStruct((m, w), BF16),
        grid=(bsz, nt),
        in_specs=[pl.BlockSpec((tm, w), lambda b, j: (b * nt + j, 0)),
                  pl.BlockSpec((tm, w), lambda b, j: (b * nt + j, 1)),
                  pl.BlockSpec((CONV_WIDTH, w), lambda b, j: (0, 0)), vec(),
                  pl.BlockSpec((LRU_HEADS, LANES, LANES), lambda b, j: (0, 0, 0)), vec(),
                  pl.BlockSpec((LRU_HEADS, LANES, LANES), lambda b, j: (0, 0, 0)), vec(), vec()],
        out_specs=pl.BlockSpec((tm, w), lambda b, j: (b * nt + j, 0)),
        scratch_shapes=[pltpu.VMEM((SUBLANES, w), F32), pltpu.VMEM((1, w), F32)],
        compiler_params=_cparams(("parallel", "arbitrary")),
    )(proj, proj, p["conv_w"], p["conv_b"], p["w_r"], p["b_r"], p["w_i"], p["b_i"], p["lam"])


def _mlstm_kernel(xb_ref, zb_ref, cw_ref, cb_ref, wq_ref, wk_ref, wv_ref, wg_ref, bg_ref,
                  nw_ref, sk_ref, o_ref, tail_ref, qkv_ref, xc_ref, caug_ref, m_ref):
    @pl.when(pl.program_id(1) == 0)
    def _():
        tail_ref[...] = jnp.zeros_like(tail_ref)
        caug_ref[...] = jnp.zeros_like(caug_ref)
        m_ref[...] = jnp.full(m_ref.shape, -jnp.inf, F32)

    L = CHUNK
    width = ML_HEADS * LANES
    scale = LANES ** -0.5
    for hh in range(ML_HEADS):
        sl = slice(hh * LANES, (hh + 1) * LANES)
        xb = xb_ref[:, sl].astype(F32)
        xc = _silu(_causal_conv(xb, tail_ref, cw_ref, cb_ref, sl))
        xc_ref[:, sl] = xc
        xcb = xc.astype(BF16)
        qkv_ref[:, sl] = _dot(xcb, wq_ref[hh]).astype(BF16)
        qkv_ref[:, width + hh * LANES:width + (hh + 1) * LANES] = _dot(xcb, wk_ref[hh]).astype(BF16)
        qkv_ref[:, 2 * width + hh * LANES:2 * width + (hh + 1) * LANES] = (
            _dot(xb.astype(BF16), wv_ref[hh]).astype(BF16))

    gates = _dot(qkv_ref[...], wg_ref[...]) + bg_ref[...]
    rowi = lax.broadcasted_iota(I32, (L, L), 0)
    coli = lax.broadcasted_iota(I32, (L, L), 1)
    causal = rowi >= coli
    lf = jnp.where((coli >= ML_HEADS) & (coli < 2 * ML_HEADS), _log_sigmoid(gates), 0.0)
    gcum = _dot(causal.astype(F32), lf, precision=HIGHEST)
    x_col = jnp.where(coli < ML_HEADS, gates, gcum)
    x_row = x_col.T
    ones = jnp.ones((L, LANES), BF16)

    for hh in range(ML_HEADS):
        sl = slice(hh * LANES, (hh + 1) * LANES)
        q = qkv_ref[:, sl]
        k = qkv_ref[:, width + hh * LANES:width + (hh + 1) * LANES]
        v = qkv_ref[:, 2 * width + hh * LANES:2 * width + (hh + 1) * LANES]
        ic = x_col[:, hh:hh + 1]
        gc = x_col[:, ML_HEADS + hh:ML_HEADS + hh + 1]
        ir = x_row[hh:hh + 1, :]
        gr = x_row[ML_HEADS + hh:ML_HEADS + hh + 1, :]
        mp = m_ref[hh][:, 0:1]
        dmat = jnp.where(causal, gc - gr + ir, -jnp.inf)
        m_inter = mp + gc
        m_t = jnp.maximum(m_inter, jnp.max(dmat, axis=1, keepdims=True))
        w_intra = jnp.exp(dmat - m_t)
        w_inter = jnp.exp(m_inter - m_t)
        qk = (_dot_nt(q, k) * scale * w_intra).astype(BF16)
        v_aug = jnp.concatenate([v, ones], axis=1)
        caug = caug_ref[hh]
        nd = _dot(qk, v_aug) + w_inter * _dot(q, caug.astype(BF16))
        hval = nd[:, :LANES] / jnp.maximum(jnp.abs(nd[:, LANES:]), jnp.exp(-m_t))

        g_last = gc[L - 1:L, :]
        m_new = jnp.maximum(mp + g_last, jnp.max(g_last - gr + ir, axis=1, keepdims=True))
        ws = jnp.exp(g_last - gc + ic - m_new)
        wc = jnp.exp(mp + g_last - m_new)
        kw_t = (k.astype(F32) * (ws * scale)).T.astype(BF16)
        caug_ref[hh] = wc * caug + _dot(kw_t, v_aug)
        m_ref[hh] = jnp.broadcast_to(m_new, (1, LANES))

        mu = jnp.mean(hval, axis=1, keepdims=True)
        dv = hval - mu
        var = jnp.mean(dv * dv, axis=1, keepdims=True)
        hn = dv * lax.rsqrt(var + EPS) * nw_ref[:, sl]
        zb = zb_ref[:, sl].astype(F32)
        o_ref[:, sl] = ((hn + sk_ref[:, sl] * xc_ref[:, sl]) * _silu(zb)).astype(o_ref.dtype)


def _mlstm(proj, p, bsz, seq):
    m = proj.shape[0]
    w = ML_HEADS * LANES
    nt = seq // CHUNK
    vec = lambda: pl.BlockSpec((1, w), lambda b, j: (0, 0))
    blk = lambda: pl.BlockSpec((ML_HEADS, LANES, LANES), lambda b, j: (0, 0, 0))
    return pl.pallas_call(
        _mlstm_kernel,
        out_shape=jax.ShapeDtypeStruct((m, w), BF16),
        grid=(bsz, nt),
        in_specs=[pl.BlockSpec((CHUNK, w), lambda b, j: (b * nt + j, 2)),
                  pl.BlockSpec((CHUNK, w), lambda b, j: (b * nt + j, 3)),
                  pl.BlockSpec((CONV_WIDTH, w), lambda b, j: (0, 0)), vec(),
                  blk(), blk(), blk(),
                  pl.BlockSpec((3 * w, LANES), lambda b, j: (0, 0)),
                  pl.BlockSpec((1, LANES), lambda b, j: (0, 0)),
                  vec(), vec()],
        out_specs=pl.BlockSpec((CHUNK, w), lambda b, j: (b * nt + j, 0)),
        scratch_shapes=[pltpu.VMEM((SUBLANES, w), F32),
                        pltpu.VMEM((CHUNK, 3 * w), BF16),
                        pltpu.VMEM((CHUNK, w), F32),
                        pltpu.VMEM((ML_HEADS, LANES, 2 * LANES), F32),
                        pltpu.VMEM((ML_HEADS, 1, LANES), F32)],
        compiler_params=_cparams(("parallel", "arbitrary")),
    )(proj, proj, p["conv_w"], p["conv_b"], p["w_q"], p["w_k"], p["w_v"], p["w_g"], p["b_g"],
      p["norm"], p["skip"])


def _ssd_kernel(z_ref, xbc_ref, dt_ref, cw_ref, cb_ref, dtb_ref, alog_ref, dsk_ref, nw_ref,
                o_ref, tail_ref, act_ref, st_ref):
    @pl.when(pl.program_id(1) == 0)
    def _():
        tail_ref[...] = jnp.zeros_like(tail_ref)
        st_ref[...] = jnp.zeros_like(st_ref)

    L = CHUNK
    inner = o_ref.shape[1]
    gw = inner // SSD_GROUPS
    hpg = gw // SSD_HEAD_DIM
    b_off = inner
    c_off = inner + SSD_GROUPS * SSD_STATE
    for cg in range(xbc_ref.shape[1] // LANES):
        sl = slice(cg * LANES, (cg + 1) * LANES)
        act_ref[:, sl] = _silu(_causal_conv(xbc_ref[:, sl].astype(F32), tail_ref, cw_ref, cb_ref, sl))

    rowi = lax.broadcasted_iota(I32, (L, L), 0)
    coli = lax.broadcasted_iota(I32, (L, L), 1)
    causal = rowi >= coli
    dt = _softplus(dt_ref[...] + dtb_ref[...])
    a = _dot(causal.astype(F32), dt * (-jnp.exp(alog_ref[...])), precision=HIGHEST)
    a_t = a.T
    dt_t = dt.T
    ea = jnp.exp(a)
    wsd = jnp.exp(a[L - 1:L, :] - a) * dt
    lane = lax.broadcasted_iota(I32, (L, gw), 1)

    def expand(cols, g):
        out = jnp.broadcast_to(cols[:, g * hpg + hpg - 1:g * hpg + hpg], (L, gw))
        for jj in range(hpg - 2, -1, -1):
            bc = jnp.broadcast_to(cols[:, g * hpg + jj:g * hpg + jj + 1], (L, gw))
            out = jnp.where(lane < (jj + 1) * SSD_HEAD_DIM, bc, out)
        return out

    for g in range(SSD_GROUPS):
        gsl = slice(g * gw, (g + 1) * gw)
        xg = act_ref[:, gsl]
        bg = act_ref[:, b_off + g * SSD_STATE:b_off + (g + 1) * SSD_STATE]
        cg_ = act_ref[:, c_off + g * SSD_STATE:c_off + (g + 1) * SSD_STATE].astype(BF16)
        cb = _dot_nt(cg_, bg.astype(BF16))
        ea_x = expand(ea, g)
        state = st_ref[g]
        acc = _dot(cg_, state.astype(BF16)) * ea_x
        for jj in range(hpg):
            hd = g * hpg + jj
            seg = jnp.where(causal, a[:, hd:hd + 1] - a_t[hd:hd + 1, :], -jnp.inf)
            w = (cb * jnp.exp(seg) * dt_t[hd:hd + 1, :]).astype(BF16)
            in_head = (lane >= jj * SSD_HEAD_DIM) & (lane < (jj + 1) * SSD_HEAD_DIM)
            acc = acc + _dot(w, jnp.where(in_head, xg, 0.0).astype(BF16))
        y = (acc + dsk_ref[:, gsl] * xg) * _silu(z_ref[:, gsl].astype(F32))
        y = y * lax.rsqrt(jnp.mean(y * y, axis=1, keepdims=True) + EPS) * nw_ref[:, gsl]
        o_ref[:, gsl] = y.astype(o_ref.dtype)
        xw = (xg * expand(wsd, g)).astype(BF16)
        st_ref[g] = ea_x[L - 1:L, :] * state + _dot(bg.T.astype(BF16), xw)


def _ssd(z, xbc, dt_raw, p, bsz, seq):
    m, inner = z.shape
    nt = seq // CHUNK
    conv_ch = xbc.shape[1]
    vec = lambda n: pl.BlockSpec((1, n), lambda b, j: (0, 0))
    return pl.pallas_call(
        _ssd_kernel,
        out_shape=jax.ShapeDtypeStruct((m, inner), BF16),
        grid=(bsz, nt),
        in_specs=[pl.BlockSpec((CHUNK, inner), lambda b, j: (b * nt + j, 0)),
                  pl.BlockSpec((CHUNK, conv_ch), lambda b, j: (b * nt + j, 0)),
                  pl.BlockSpec((CHUNK, LANES), lambda b, j: (b * nt + j, 0)),
                  pl.BlockSpec((CONV_WIDTH, conv_ch), lambda b, j: (0, 0)), vec(conv_ch),
                  vec(LANES), vec(LANES), vec(inner), vec(inner)],
        out_specs=pl.BlockSpec((CHUNK, inner), lambda b, j: (b * nt + j, 0)),
        scratch_shapes=[pltpu.VMEM((SUBLANES, conv_ch), F32),
                        pltpu.VMEM((CHUNK, conv_ch), F32),
                        pltpu.VMEM((SSD_GROUPS, SSD_STATE, inner // SSD_GROUPS), F32)],
        compiler_params=_cparams(("parallel", "arbitrary")),
    )(z, xbc, dt_raw, p["conv_w"], p["conv_b"], p["dt_bias"], p["a_log"], p["d_skip"],
      p["norm"])


def _outproj_kernel(*refs, n_in):
    y_refs, w_refs = refs[:n_in], refs[n_in:2 * n_in]
    h_ref, g_ref, o_ref = refs[2 * n_in:]
    acc = _dot(y_refs[0][...], w_refs[0][...])
    for y_ref, w_ref in zip(y_refs[1:], w_refs[1:]):
        acc = acc + _dot(y_ref[...], w_ref[...])
    o_ref[...] = h_ref[...] + g_ref[0] * acc


def _outproj(ys, w, h, gate, seq, tm):
    m, d = h.shape
    tiles_per_seq = seq // tm
    in_specs, args, k0 = [], [], 0
    for y in ys:
        in_specs.append(pl.BlockSpec((tm, y.shape[1]), lambda i: (i, 0)))
        args.append(y)
    for y in ys:
        kk = y.shape[1]
        in_specs.append(pl.BlockSpec((kk, d), lambda i, kb=k0 // kk: (kb, 0)))
        args.append(w)
        k0 += kk
    in_specs += [pl.BlockSpec((tm, d), lambda i: (i, 0)),
                 pl.BlockSpec((1, 1, d), lambda i: (i // tiles_per_seq, 0, 0))]
    args += [h, gate]
    return pl.pallas_call(
        functools.partial(_outproj_kernel, n_in=len(ys)),
        out_shape=jax.ShapeDtypeStruct((m, d), F32),
        grid=(m // tm,), in_specs=in_specs,
        out_specs=pl.BlockSpec((tm, d), lambda i: (i, 0)),
        compiler_params=_cparams(("parallel",)),
    )(*args)


def _router_kernel(h_ref, g_ref, sh_ref, sc_ref, wr_ref, br_ref,
                   up_ref, topi_ref, gate_ref, rank_ref, cnt_ref, carry_ref):
    @pl.when(pl.program_id(0) == 0)
    def _():
        carry_ref[...] = jnp.zeros_like(carry_ref)

    tm = h_ref.shape[0]
    u = _norm_mod(h_ref[...], g_ref[...], sh_ref[0], sc_ref[0])
    up_ref[...] = _pack_pairs(u)
    logits = _dot(u, wr_ref[...], precision=HIGHEST) + br_ref[...]
    lt = jnp.concatenate([logits[r0:r0 + LANES].T for r0 in range(0, tm, LANES)], axis=1)
    l = lt[:N_EXPERTS]
    e_iota = lax.broadcasted_iota(I32, (N_EXPERTS, tm), 0).astype(F32)
    vals, idxs, hots = [], [], []
    for _ in range(TOP_K):
        mx = jnp.max(l, axis=0, keepdims=True)
        idx = jnp.min(jnp.where(l == mx, e_iota, float(N_EXPERTS)), axis=0, keepdims=True)
        hot = e_iota == idx
        l = jnp.where(hot, -jnp.inf, l)
        vals.append(mx)
        idxs.append(idx)
        hots.append(hot)
    exps = [jnp.exp(v - vals[0]) for v in vals]
    den = exps[0] + exps[1] + exps[2] + exps[3]
    gate_ref[...] = jnp.concatenate([e / den for e in exps], axis=0)
    topi_ref[...] = jnp.concatenate(idxs, axis=0).astype(I32)

    sel = jnp.zeros((N_EXPERTS, tm), F32)
    for hot in hots:
        sel = jnp.where(hot, 1.0, sel)
    r_i = lax.broadcasted_iota(I32, (tm, tm), 0)
    c_i = lax.broadcasted_iota(I32, (tm, tm), 1)
    before = (r_i < c_i).astype(BF16)
    carry = carry_ref[:, 0:1]
    cum = _dot(sel.astype(BF16), before) + carry
    rank_ref[...] = jnp.concatenate(
        [jnp.sum(jnp.where(hot, cum, 0.0), axis=0, keepdims=True) for hot in hots], axis=0).astype(I32)
    total = carry + jnp.sum(sel, axis=1, keepdims=True)
    carry_ref[...] = jnp.broadcast_to(total, carry_ref.shape)
    cnt_ref[...] = jnp.broadcast_to(total, cnt_ref.shape)


def _router(h, g, shift, scale, wr, br, seq, tm):
    m, d = h.shape
    tiles_per_seq = seq // tm
    bmap = lambda i: (i // tiles_per_seq, 0, 0)
    row4 = lambda: pl.BlockSpec((TOP_K, tm), lambda i: (0, i))
    return pl.pallas_call(
        _router_kernel,
        out_shape=[jax.ShapeDtypeStruct((m, d // 2), I32),
                   jax.ShapeDtypeStruct((TOP_K, m), I32),
                   jax.ShapeDtypeStruct((TOP_K, m), F32),
                   jax.ShapeDtypeStruct((TOP_K, m), I32),
                   jax.ShapeDtypeStruct((N_EXPERTS, LANES), F32)],
        grid=(m // tm,),
        in_specs=[pl.BlockSpec((tm, d), lambda i: (i, 0)),
                  pl.BlockSpec((1, d), lambda i: (0, 0)),
                  pl.BlockSpec((1, 1, d), bmap), pl.BlockSpec((1, 1, d), bmap),
                  pl.BlockSpec((d, LANES), lambda i: (0, 0)),
                  pl.BlockSpec((1, LANES), lambda i: (0, 0))],
        out_specs=[pl.BlockSpec((tm, d // 2), lambda i: (i, 0)), row4(), row4(), row4(),
                   pl.BlockSpec((N_EXPERTS, LANES), lambda i: (0, 0))],
        scratch_shapes=[pltpu.VMEM((N_EXPERTS, LANES), F32)],
        compiler_params=_cparams(("arbitrary",)),
    )(h, g.reshape(1, d), shift, scale, wr, br)


def _dest_kernel(ps_ref, topi_ref, rank_ref, o_ref):
    topi = topi_ref[...]
    acc = rank_ref[...]
    for e in range(N_EXPERTS):
        acc = acc + jnp.where(topi == e, ps_ref[e], 0)
    o_ref[...] = acc


def _dest_rows(pad_start, topi, rank, tw):
    k, m = topi.shape
    blk = lambda: pl.BlockSpec((k, tw), lambda i, ps: (0, i))
    return pl.pallas_call(
        _dest_kernel,
        out_shape=jax.ShapeDtypeStruct((k, m), I32),
        grid_spec=pltpu.PrefetchScalarGridSpec(
            num_scalar_prefetch=1, grid=(m // tw,), in_specs=[blk(), blk()], out_specs=blk()),
        compiler_params=_cparams(("parallel",)),
    )(pad_start, topi, rank)


def _dispatch_kernel(dest_ref, up_ref, xs_ref, sem):
    i = pl.program_id(0)
    tm = dest_ref.shape[0] // TOP_K

    def issue(t, c):
        for k in range(TOP_K):
            pltpu.make_async_copy(up_ref.at[pl.ds(i * tm + t, 1)],
                                  xs_ref.at[pl.ds(dest_ref[k * tm + t], 1)], sem).start()
        return c

    def wait_tile():
        for _ in range(TOP_K):
            pltpu.make_async_copy(up_ref.at[pl.ds(0, tm)], xs_ref.at[pl.ds(0, tm)], sem).wait()

    lax.fori_loop(0, tm, issue, 0)
    pl.when(i > 0)(wait_tile)
    pl.when(i == pl.num_programs(0) - 1)(wait_tile)


def _dispatch(dest_tiles, up, n_rows, tm):
    m, wp = up.shape
    return pl.pallas_call(
        _dispatch_kernel,
        out_shape=jax.ShapeDtypeStruct((n_rows, wp), I32),
        grid=(m // tm,),
        in_specs=[pl.BlockSpec((TOP_K * tm,), lambda i: (i,), memory_space=pltpu.SMEM),
                  pl.BlockSpec(memory_space=pl.ANY)],
        out_specs=pl.BlockSpec(memory_space=pl.ANY),
        scratch_shapes=[pltpu.SemaphoreType.DMA],
        compiler_params=_cparams(("arbitrary",), has_side_effects=True),
    )(dest_tiles, up)


def _combine_kernel(dcur_ref, dnext_ref, gate_ref, h_ref, gf_ref, fn_ref, y_ref, o_ref, buf_ref, sem,
                    *, final):
    i = pl.program_id(0)
    tm = h_ref.shape[0]
    slot = i % 2

    def issue(d_ref, s):
        def body(t, c):
            for k in range(TOP_K):
                pltpu.make_async_copy(y_ref.at[pl.ds(d_ref[k * tm + t], 1)],
                                      buf_ref.at[s, k, pl.ds(t, 1)], sem.at[s]).start()
            return c
        lax.fori_loop(0, tm, body, 0)

    pl.when(i == 0)(lambda: issue(dcur_ref, slot))
    pl.when(i + 1 < pl.num_programs(0))(lambda: issue(dnext_ref, 1 - slot))
    pltpu.make_async_copy(buf_ref.at[slot], buf_ref.at[slot], sem.at[slot]).wait()
    acc = gate_ref[:, 0:1] * _unpack_pairs(buf_ref[slot, 0])
    for k in range(1, TOP_K):
        acc = acc + gate_ref[:, k:k + 1] * _unpack_pairs(buf_ref[slot, k])
    hn = h_ref[...] + gf_ref[0] * acc
    if final:
        hn = hn * lax.rsqrt(jnp.mean(hn * hn, axis=-1, keepdims=True) + EPS) * fn_ref[...]
    o_ref[...] = hn


def _combine(dest_tiles, gates_col, h, gf, fnorm, y, seq, tm, final):
    m, d = h.shape
    tiles_per_seq = seq // tm
    nt = m // tm
    return pl.pallas_call(
        functools.partial(_combine_kernel, final=final),
        out_shape=jax.ShapeDtypeStruct((m, d), F32),
        grid=(nt,),
        in_specs=[pl.BlockSpec((TOP_K * tm,), lambda i: (i,), memory_space=pltpu.SMEM),
                  pl.BlockSpec((TOP_K * tm,), lambda i: (jnp.minimum(i + 1, nt - 1),),
                               memory_space=pltpu.SMEM),
                  pl.BlockSpec((tm, TOP_K), lambda i: (i, 0)),
                  pl.BlockSpec((tm, d), lambda i: (i, 0)),
                  pl.BlockSpec((1, 1, d), lambda i: (i // tiles_per_seq, 0, 0)),
                  pl.BlockSpec((1, d), lambda i: (0, 0)),
                  pl.BlockSpec(memory_space=pl.ANY)],
        out_specs=pl.BlockSpec((tm, d), lambda i: (i, 0)),
        scratch_shapes=[pltpu.VMEM((2, TOP_K, tm, d // 2), I32), pltpu.SemaphoreType.DMA((2,))],
        compiler_params=_cparams(("arbitrary",)),
    )(dest_tiles, dest_tiles, gates_col, h, gf, fnorm.reshape(1, d), y)


def _expert_kernel(be_ref, nb_ref, first_ref, x_ref, wgu_ref, bgu_ref, wd_ref, bd_ref, y_ref,
                   wgu_bf, wd_bf):
    i = pl.program_id(0)

    @pl.when(i < nb_ref[0])
    def _():
        dff = wd_bf.shape[0]

        @pl.when(first_ref[i] == 1)
        def _():
            rows = 64

            def cast(r, c):
                r0 = pl.multiple_of(r * rows, rows)
                wgu_bf[pl.ds(r0, rows), :] = wgu_ref[0, 0, pl.ds(r0, rows), :].astype(BF16)
                wd_bf[pl.ds(r0, rows), :] = wd_ref[0, 0, pl.ds(r0, rows), :].astype(BF16)
                return c

            lax.fori_loop(0, dff // rows, cast, 0)

        x = _unpack_pairs(x_ref[...]).astype(BF16)
        hb = _dot(x, wgu_bf[...]) + bgu_ref[0, 0]
        h_glu = jnp.minimum(hb[:, :dff], SWIGLU_LIMIT)
        h_lin = jnp.clip(hb[:, dff:], -SWIGLU_LIMIT, SWIGLU_LIMIT)
        act = h_glu * jax.nn.sigmoid(SWIGLU_ALPHA * h_glu) * (h_lin + 1.0)
        y_ref[...] = _pack_pairs(_dot(act.astype(BF16), wd_bf[...]) + bd_ref[0, 0])


def _experts(block_e, n_used, first, xs, wgu, bgu, wd, bd, layer):
    n_rows, wp = xs.shape
    _, ne, d, ff2 = wgu.shape
    assert d == ff2 // 2
    nblk = n_rows // EXPERT_BLOCK

    def xmap(i, be, nb, fi):
        return (jnp.minimum(i, nb[0] - 1), 0)

    emap = lambda i, be, nb, fi: (layer, be[i], 0, 0)
    grid_spec = pltpu.PrefetchScalarGridSpec(
        num_scalar_prefetch=3, grid=(nblk,),
        in_specs=[pl.BlockSpec((EXPERT_BLOCK, wp), xmap),
                  pl.BlockSpec((1, 1, d, ff2), emap), pl.BlockSpec((1, 1, 1, ff2), emap),
                  pl.BlockSpec((1, 1, ff2 // 2, d), emap), pl.BlockSpec((1, 1, 1, d), emap)],
        out_specs=pl.BlockSpec((EXPERT_BLOCK, wp), xmap),
        scratch_shapes=[pltpu.VMEM((d, ff2), BF16), pltpu.VMEM((ff2 // 2, d), BF16)])
    depth = wgu.shape[0]
    return pl.pallas_call(
        _expert_kernel,
        out_shape=jax.ShapeDtypeStruct((n_rows, wp), I32),
        grid_spec=grid_spec,
        compiler_params=_cparams(("arbitrary",)),
    )(block_e, n_used, first, xs, wgu, bgu.reshape(depth, ne, 1, ff2), wd, bd.reshape(depth, ne, 1, d))


def _moe(h, g, shift, scale, gf, fnorm, wr, br, wgu, bgu, wd, bd, layer, seq, final):
    m, d = h.shape
    tm = 256
    wr_p = jnp.zeros((d, LANES), F32).at[:, :N_EXPERTS].set(wr)
    br_p = jnp.zeros((1, LANES), F32).at[0, :N_EXPERTS].set(br)
    up, topi, gates, rank, cnt = _router(h, g, shift, scale, wr_p, br_p, seq, tm)

    counts = cnt[:, 0].astype(I32)
    padded = (counts + EXPERT_BLOCK - 1) // EXPERT_BLOCK * EXPERT_BLOCK
    pad_end = jnp.cumsum(padded)
    pad_start = pad_end - padded
    nblk = m * TOP_K // EXPERT_BLOCK + N_EXPERTS
    n_rows = nblk * EXPERT_BLOCK
    n_used = pad_end[-1:] // EXPERT_BLOCK
    blk = jnp.arange(nblk, dtype=I32)
    blk_c = jnp.minimum(blk, n_used - 1)
    block_e = jnp.minimum(jnp.sum(blk_c[:, None] * EXPERT_BLOCK >= pad_end[None, :], axis=1),
                          N_EXPERTS - 1).astype(I32)
    first = jnp.concatenate([jnp.ones((1,), I32), (block_e[1:] != block_e[:-1]).astype(I32)])

    dest = _dest_rows(pad_start, topi, rank, min(m, 8192))
    dest_tiles = dest.reshape(TOP_K, m // tm, tm).transpose(1, 0, 2).reshape(-1)
    xs = _dispatch(dest_tiles, up, n_rows, tm)
    y = _experts(block_e, n_used.astype(I32), first, xs, wgu, bgu, wd, bd, layer)
    return _combine(dest_tiles, gates.T, h, gf, fnorm, y, seq, tm, final)


def _block_diag(w, group):
    nb, b, _ = w.shape
    per = group // b
    wg = w.reshape(nb // per, per, b, b)
    dense = jnp.einsum("gnde,nm->gndme", wg, jnp.eye(per, dtype=w.dtype))
    return dense.reshape(nb // per, group, group)


def kernel(x, c, mod_w, mod_b, norm_mix, norm_ffn, ev_w_in, ev_lru_conv_w, ev_lru_conv_b, ev_lru_w_r, ev_lru_b_r, ev_lru_w_i, ev_lru_b_i, ev_lru_lambda, ev_ml_conv_w, ev_ml_conv_b, ev_ml_w_q, ev_ml_w_k, ev_ml_w_v, ev_ml_w_ig, ev_ml_b_ig, ev_ml_w_fg, ev_ml_b_fg, ev_ml_norm, ev_ml_skip, ev_w_out, od_w_in, od_conv_w, od_conv_b, od_dt_bias, od_a_log, od_d, od_norm, od_w_out, moe_router_w, moe_router_b, moe_w_gu, moe_b_gu, moe_w_down, moe_b_down, final_norm):
    bsz, seq, d = x.shape
    depth = mod_w.shape[0]
    m = bsz * seq
    mod = _modulation(c, mod_w, mod_b)
    h = x.reshape(m, d).astype(F32)
    for layer in range(depth):
        sh_m, sc_m, g_m, sh_f, sc_f, g_f = (mod[layer, i] for i in range(6))
        j = layer // 2
        if layer % 2 == 0:
            w = ev_lru_lambda.shape[1]
            w_in = ev_w_in[j].astype(BF16)
            proj = _inproj(h, norm_mix[layer], sh_m, sc_m, w_in, None, [w_in.shape[1]], seq, 512)[0]
            lru_p = dict(conv_w=ev_lru_conv_w[j], conv_b=ev_lru_conv_b[j].reshape(1, w),
                         w_r=ev_lru_w_r[j].astype(BF16), b_r=ev_lru_b_r[j].reshape(1, w),
                         w_i=ev_lru_w_i[j].astype(BF16), b_i=ev_lru_b_i[j].reshape(1, w),
                         lam=ev_lru_lambda[j].reshape(1, w))
            ya = _lru(proj, lru_p, bsz, seq, 256)
            wg = jnp.zeros((3 * w, LANES), F32)
            wg = wg.at[:, :ML_HEADS].set(ev_ml_w_ig[j]).at[:, ML_HEADS:2 * ML_HEADS].set(ev_ml_w_fg[j])
            bg = jnp.zeros((1, LANES), F32)
            bg = bg.at[0, :ML_HEADS].set(ev_ml_b_ig[j]).at[0, ML_HEADS:2 * ML_HEADS].set(ev_ml_b_fg[j])
            ml_p = dict(conv_w=ev_ml_conv_w[j], conv_b=ev_ml_conv_b[j].reshape(1, w),
                        w_q=_block_diag(ev_ml_w_q[j], LANES).astype(BF16),
                        w_k=_block_diag(ev_ml_w_k[j], LANES).astype(BF16),
                        w_v=_block_diag(ev_ml_w_v[j], LANES).astype(BF16),
                        w_g=wg.astype(BF16), b_g=bg,
                        norm=ev_ml_norm[j].reshape(1, w), skip=ev_ml_skip[j].reshape(1, w))
            yb = _mlstm(proj, ml_p, bsz, seq)
            h = _outproj([ya, yb], ev_w_out[j].astype(BF16), h, g_m, seq, 512)
        else:
            inner = od_norm.shape[1]
            heads = od_dt_bias.shape[1]
            conv_ch = od_conv_w.shape[2]
            w_in = od_w_in[j]
            wdt = jnp.zeros((d, LANES), F32).at[:, :heads].set(w_in[:, inner + conv_ch:])
            z, xbc, dt_raw = _inproj(h, norm_mix[layer], sh_m, sc_m, w_in[:, :inner + conv_ch].astype(BF16),
                                     wdt.astype(BF16), [inner, conv_ch], seq, 256)
            pad = lambda v: jnp.zeros((1, LANES), F32).at[0, :heads].set(v)
            ssd_p = dict(conv_w=od_conv_w[j], conv_b=od_conv_b[j].reshape(1, conv_ch),
                         dt_bias=pad(od_dt_bias[j]), a_log=pad(od_a_log[j]),
                         d_skip=jnp.repeat(od_d[j], SSD_HEAD_DIM).reshape(1, inner),
                         norm=od_norm[j].reshape(1, inner))
            y = _ssd(z, xbc, dt_raw, ssd_p, bsz, seq)
            h = _outproj([y], od_w_out[j].astype(BF16), h, g_m, seq, 512)
        h = _moe(h, norm_ffn[layer], sh_f, sc_f, g_f, final_norm,
                 moe_router_w[layer], moe_router_b[layer],
                 moe_w_gu, moe_b_gu, moe_w_down, moe_b_down, layer, seq, final=(layer == depth - 1))
    return h.reshape(bsz, seq, d)
```

```python
import functools

import jax
import jax.numpy as jnp
from jax import lax
from jax.experimental import pallas as pl
from jax.experimental.pallas import tpu as pltpu

F32 = jnp.float32
BF16 = jnp.bfloat16
I32 = jnp.int32
HIGHEST = lax.Precision.HIGHEST

EPS = 1e-6
CONV_WIDTH = 4
LANES = 128
SUBLANES = 8
LRU_HEADS = 8
LRU_C = 8.0
ML_HEADS = 8
ML_QKV_BLOCK = 4
CHUNK = 128
SSD_HEAD_DIM = 64
SSD_GROUPS = 8
SSD_STATE = 128
N_EXPERTS = 32
TOP_K = 4
SWIGLU_ALPHA = 1.702
SWIGLU_LIMIT = 7.0
EXPERT_BLOCK = 512
VMEM_LIMIT = 56 * 1024 * 1024


def _cparams(sem, **kw):
    return pltpu.CompilerParams(dimension_semantics=sem, vmem_limit_bytes=VMEM_LIMIT, **kw)


def _silu(x):
    return x * jax.nn.sigmoid(x)


def _log_sigmoid(x):
    return jnp.minimum(x, 0.0) - jnp.log1p(jnp.exp(-jnp.abs(x)))


def _softplus(x):
    return jnp.maximum(x, 0.0) + jnp.log1p(jnp.exp(-jnp.abs(x)))


def _dot(a, b, **kw):
    return jnp.dot(a, b, preferred_element_type=F32, **kw)


def _dot_nt(a, b):
    return lax.dot_general(a, b, (((1,), (1,)), ((), ())), preferred_element_type=F32)


def _pack_pairs(x):
    w = x.shape[1] // 2
    lo = lax.bitcast_convert_type(x[:, :w].astype(BF16).astype(F32), I32)
    hi = lax.bitcast_convert_type(x[:, w:].astype(BF16).astype(F32), I32)
    return lax.shift_right_logical(lo, 16) | (hi & jnp.int32(-65536))


def _unpack_pairs(p):
    lo = lax.bitcast_convert_type(lax.shift_left(p, 16), F32)
    hi = lax.bitcast_convert_type(p & jnp.int32(-65536), F32)
    return jnp.concatenate([lo, hi], axis=1)


def _norm_mod(h, g, shift, scale):
    y = h * lax.rsqrt(jnp.mean(h * h, axis=-1, keepdims=True) + EPS)
    return (y * g) * (1.0 + scale) + shift


def _causal_conv(x, tail_ref, w_ref, b_ref, sl):
    t = x.shape[0]
    tail = tail_ref[:, sl]
    row8 = lax.broadcasted_iota(I32, tail.shape, 0)
    out = b_ref[:, sl] + x * w_ref[CONV_WIDTH - 1:CONV_WIDTH, sl]
    for k in range(1, CONV_WIDTH):
        xs = pltpu.roll(x, k, axis=0)
        first = jnp.where(row8 < k, pltpu.roll(tail, k, axis=0), xs[:SUBLANES])
        xs = jnp.concatenate([first, xs[SUBLANES:]], axis=0)
        out = out + xs * w_ref[CONV_WIDTH - 1 - k:CONV_WIDTH - k, sl]
    tail_ref[:, sl] = x[t - SUBLANES:]
    return out


def _mod_kernel(c_ref, w_ref, b_ref, o_ref):
    cond = _silu(c_ref[...])
    o_ref[0, 0] = _dot(cond, w_ref[0], precision=HIGHEST) + b_ref[0, 0]


def _modulation(c, mod_w, mod_b):
    depth, d, _ = mod_w.shape
    bsz = c.shape[0]
    out = pl.pallas_call(
        _mod_kernel,
        out_shape=jax.ShapeDtypeStruct((depth, 6, bsz, d), F32),
        grid=(depth, 6),
        in_specs=[pl.BlockSpec((bsz, d), lambda l, j: (0, 0)),
                  pl.BlockSpec((1, d, d), lambda l, j: (l, 0, j)),
                  pl.BlockSpec((1, 1, 1, d), lambda l, j: (l, j, 0, 0))],
        out_specs=pl.BlockSpec((1, 1, bsz, d), lambda l, j: (l, j, 0, 0)),
        compiler_params=_cparams(("parallel", "parallel")),
    )(c.astype(F32), mod_w, mod_b.reshape(depth, 6, 1, d))
    return out.reshape(depth, 6, bsz, 1, d)


def _inproj_kernel(h_ref, g_ref, sh_ref, sc_ref, w_ref, *rest, n_chunk, with_dt):
    if with_dt:
        wdt_ref, *o_refs, odt_ref = rest
    else:
        o_refs = rest
    u = _norm_mod(h_ref[...], g_ref[...], sh_ref[0], sc_ref[0]).astype(BF16)
    off = 0
    for o_ref in o_refs:
        for n0 in range(0, o_ref.shape[1], n_chunk):
            o_ref[:, n0:n0 + n_chunk] = _dot(u, w_ref[:, off + n0:off + n0 + n_chunk]).astype(o_ref.dtype)
        off += o_ref.shape[1]
    if with_dt:
        odt_ref[...] = _dot(u, wdt_ref[...])


def _inproj(h, g, shift, scale, w, wdt, splits, seq, tm):
    m, d = h.shape
    n = w.shape[1]
    assert sum(splits) == n
    tiles_per_seq = seq // tm
    bmap = lambda i: (i // tiles_per_seq, 0, 0)
    in_specs = [pl.BlockSpec((tm, d), lambda i: (i, 0)),
                pl.BlockSpec((1, d), lambda i: (0, 0)),
                pl.BlockSpec((1, 1, d), bmap),
                pl.BlockSpec((1, 1, d), bmap),
                pl.BlockSpec((d, n), lambda i: (0, 0), pipeline_mode=pl.Buffered(1))]
    out_shape = [jax.ShapeDtypeStruct((m, s), BF16) for s in splits]
    out_specs = [pl.BlockSpec((tm, s), lambda i: (i, 0)) for s in splits]
    args = [h, g.reshape(1, d), shift, scale, w]
    if wdt is not None:
        in_specs.append(pl.BlockSpec((d, LANES), lambda i: (0, 0)))
        out_shape.append(jax.ShapeDtypeStruct((m, LANES), F32))
        out_specs.append(pl.BlockSpec((tm, LANES), lambda i: (i, 0)))
        args.append(wdt)
    return pl.pallas_call(
        functools.partial(_inproj_kernel, n_chunk=1024, with_dt=wdt is not None),
        out_shape=out_shape, grid=(m // tm,), in_specs=in_specs, out_specs=out_specs,
        compiler_params=_cparams(("parallel",)),
    )(*args)


def _lru_kernel(xa_ref, ga_ref, cw_ref, cb_ref, wr_ref, br_ref, wi_ref, bi_ref, lam_ref,
                o_ref, tail_ref, hc_ref):
    @pl.when(pl.program_id(1) == 0)
    def _():
        tail_ref[...] = jnp.zeros_like(tail_ref)
        hc_ref[...] = jnp.zeros_like(hc_ref)

    t = xa_ref.shape[0]
    row = lax.broadcasted_iota(I32, (t, LANES), 0)
    for hh in range(LRU_HEADS):
        sl = slice(hh * LANES, (hh + 1) * LANES)
        xc = _causal_conv(xa_ref[:, sl].astype(F32), tail_ref, cw_ref, cb_ref, sl)
        xcb = xc.astype(BF16)
        r = jax.nn.sigmoid(_dot(xcb, wr_ref[hh]) + br_ref[:, sl])
        i = jax.nn.sigmoid(_dot(xcb, wi_ref[hh]) + bi_ref[:, sl])
        log_a = LRU_C * r * _log_sigmoid(lam_ref[:, sl])
        a = jnp.exp(log_a)
        th = jnp.tanh(log_a)
        u = jnp.sqrt(-2.0 * th / (1.0 - th)) * (i * xc)
        s = 1
        while s < t:
            m = row >= s
            u = jnp.where(m, u + a * pltpu.roll(u, s, axis=0), u)
            a = jnp.where(m, a * pltpu.roll(a, s, axis=0), a)
            s *= 2
        h = u + a * hc_ref[:, sl]
        hc_ref[:, sl] = h[t - 1:t]
        ga = ga_ref[:, sl].astype(F32)
        o_ref[:, sl] = (h * jax.nn.gelu(ga, approximate=True)).astype(o_ref.dtype)


def _lru(proj, p, bsz, seq, tm):
    m = proj.shape[0]
    w = LRU_HEADS * LANES
    nt = seq // tm
    vec = lambda: pl.BlockSpec((1, w), lambda b, j: (0, 0))
    return pl.pallas_call(
        _lru_kernel,
        out_shape=jax.ShapeDtypeStruct((m, w), BF16),
        grid=(bsz, nt),
        in_specs=[pl.BlockSpec((tm, w), lambda b, j: (b * nt + j, 0)),
                  pl.BlockSpec((tm, w), lambda b, j: (b * nt + j, 1)),
                  pl.BlockSpec((CONV_WIDTH, w), lambda b, j: (0, 0)), vec(),
                  pl.BlockSpec((LRU_HEADS, LANES, LANES), lambda b, j: (0, 0, 0)), vec(),
                  pl.BlockSpec((LRU_HEADS, LANES, LANES), lambda b, j: (0, 0, 0)), vec(), vec()],
        out_specs=pl.BlockSpec((tm, w), lambda b, j: (b * nt + j, 0)),
        scratch_shapes=[pltpu.VMEM((SUBLANES, w), F32), pltpu.VMEM((1, w), F32)],
        compiler_params=_cparams(("parallel", "arbitrary")),
    )(proj, proj, p["conv_w"], p["conv_b"], p["w_r"], p["b_r"], p["w_i"], p["b_i"], p["lam"])


def _mlstm_kernel(xb_ref, zb_ref, cw_ref, cb_ref, wq_ref, wk_ref, wv_ref, wg_ref, bg_ref,
                  nw_ref, sk_ref, o_ref, tail_ref, qkv_ref, xc_ref, caug_ref, m_ref):
    @pl.when(pl.program_id(1) == 0)
    def _():
        tail_ref[...] = jnp.zeros_like(tail_ref)
        caug_ref[...] = jnp.zeros_like(caug_ref)
        m_ref[...] = jnp.full(m_ref.shape, -jnp.inf, F32)

    L = CHUNK
    width = ML_HEADS * LANES
    scale = LANES ** -0.5
    for hh in range(ML_HEADS):
        sl = slice(hh * LANES, (hh + 1) * LANES)
        xb = xb_ref[:, sl].astype(F32)
        xc = _silu(_causal_conv(xb, tail_ref, cw_ref, cb_ref, sl))
        xc_ref[:, sl] = xc
        xcb = xc.astype(BF16)
        qkv_ref[:, sl] = _dot(xcb, wq_ref[hh]).astype(BF16)
        qkv_ref[:, width + hh * LANES:width + (hh + 1) * LANES] = _dot(xcb, wk_ref[hh]).astype(BF16)
        qkv_ref[:, 2 * width + hh * LANES:2 * width + (hh + 1) * LANES] = (
            _dot(xb.astype(BF16), wv_ref[hh]).astype(BF16))

    gates = _dot(qkv_ref[...], wg_ref[...]) + bg_ref[...]
    rowi = lax.broadcasted_iota(I32, (L, L), 0)
    coli = lax.broadcasted_iota(I32, (L, L), 1)
    causal = rowi >= coli
    lf = jnp.where((coli >= ML_HEADS) & (coli < 2 * ML_HEADS), _log_sigmoid(gates), 0.0)
    gcum = _dot(causal.astype(F32), lf, precision=HIGHEST)
    x_col = jnp.where(coli < ML_HEADS, gates, gcum)
    x_row = x_col.T
    ones = jnp.ones((L, LANES), BF16)

    for hh in range(ML_HEADS):
        sl = slice(hh * LANES, (hh + 1) * LANES)
        q = qkv_ref[:, sl]
        k = qkv_ref[:, width + hh * LANES:width + (hh + 1) * LANES]
        v = qkv_ref[:, 2 * width + hh * LANES:2 * width + (hh + 1) * LANES]
        ic = x_col[:, hh:hh + 1]
        gc = x_col[:, ML_HEADS + hh:ML_HEADS + hh + 1]
        ir = x_row[hh:hh + 1, :]
        gr = x_row[ML_HEADS + hh:ML_HEADS + hh + 1, :]
        mp = m_ref[hh][:, 0:1]
        dmat = jnp.where(causal, gc - gr + ir, -jnp.inf)
        m_inter = mp + gc
        m_t = jnp.maximum(m_inter, jnp.max(dmat, axis=1, keepdims=True))
        w_intra = jnp.exp(dmat - m_t)
        w_inter = jnp.exp(m_inter - m_t)
        qk = (_dot_nt(q, k) * scale * w_intra).astype(BF16)
        v_aug = jnp.concatenate([v, ones], axis=1)
        caug = caug_ref[hh]
        nd = _dot(qk, v_aug) + w_inter * _dot(q, caug.astype(BF16))
        hval = nd[:, :LANES] / jnp.maximum(jnp.abs(nd[:, LANES:]), jnp.exp(-m_t))

        g_last = gc[L - 1:L, :]
        m_new = jnp.maximum(mp + g_last, jnp.max(g_last - gr + ir, axis=1, keepdims=True))
        ws = jnp.exp(g_last - gc + ic - m_new)
        wc = jnp.exp(mp + g_last - m_new)
        kw_t = (k.astype(F32) * (ws * scale)).T.astype(BF16)
        caug_ref[hh] = wc * caug + _dot(kw_t, v_aug)
        m_ref[hh] = jnp.broadcast_to(m_new, (1, LANES))

        mu = jnp.mean(hval, axis=1, keepdims=True)
        dv = hval - mu
        var = jnp.mean(dv * dv, axis=1, keepdims=True)
        hn = dv * lax.rsqrt(var + EPS) * nw_ref[:, sl]
        zb = zb_ref[:, sl].astype(F32)
        o_ref[:, sl] = ((hn + sk_ref[:, sl] * xc_ref[:, sl]) * _silu(zb)).astype(o_ref.dtype)


def _mlstm(proj, p, bsz, seq):
    m = proj.shape[0]
    w = ML_HEADS * LANES
    nt = seq // CHUNK
    vec = lambda: pl.BlockSpec((1, w), lambda b, j: (0, 0))
    blk = lambda: pl.BlockSpec((ML_HEADS, LANES, LANES), lambda b, j: (0, 0, 0))
    return pl.pallas_call(
        _mlstm_kernel,
        out_shape=jax.ShapeDtypeStruct((m, w), BF16),
        grid=(bsz, nt),
        in_specs=[pl.BlockSpec((CHUNK, w), lambda b, j: (b * nt + j, 2)),
                  pl.BlockSpec((CHUNK, w), lambda b, j: (b * nt + j, 3)),
                  pl.BlockSpec((CONV_WIDTH, w), lambda b, j: (0, 0)), vec(),
                  blk(), blk(), blk(),
                  pl.BlockSpec((3 * w, LANES), lambda b, j: (0, 0)),
                  pl.BlockSpec((1, LANES), lambda b, j: (0, 0)),
                  vec(), vec()],
        out_specs=pl.BlockSpec((CHUNK, w), lambda b, j: (b * nt + j, 0)),
        scratch_shapes=[pltpu.VMEM((SUBLANES, w), F32),
                        pltpu.VMEM((CHUNK, 3 * w), BF16),
                        pltpu.VMEM((CHUNK, w), F32),
                        pltpu.VMEM((ML_HEADS, LANES, 2 * LANES), F32),
                        pltpu.VMEM((ML_HEADS, 1, LANES), F32)],
        compiler_params=_cparams(("parallel", "arbitrary")),
    )(proj, proj, p["conv_w"], p["conv_b"], p["w_q"], p["w_k"], p["w_v"], p["w_g"], p["b_g"],
      p["norm"], p["skip"])


def _ssd_kernel(z_ref, xbc_ref, dt_ref, cw_ref, cb_ref, dtb_ref, alog_ref, dsk_ref, nw_ref,
                o_ref, tail_ref, act_ref, st_ref):
    @pl.when(pl.program_id(1) == 0)
    def _():
        tail_ref[...] = jnp.zeros_like(tail_ref)
        st_ref[...] = jnp.zeros_like(st_ref)

    L = CHUNK
    inner = o_ref.shape[1]
    gw = inner // SSD_GROUPS
    hpg = gw // SSD_HEAD_DIM
    b_off = inner
    c_off = inner + SSD_GROUPS * SSD_STATE
    for cg in range(xbc_ref.shape[1] // LANES):
        sl = slice(cg * LANES, (cg + 1) * LANES)
        act_ref[:, sl] = _silu(_causal_conv(xbc_ref[:, sl].astype(F32), tail_ref, cw_ref, cb_ref, sl))

    rowi = lax.broadcasted_iota(I32, (L, L), 0)
    coli = lax.broadcasted_iota(I32, (L, L), 1)
    causal = rowi >= coli
    dt = _softplus(dt_ref[...] + dtb_ref[...])
    a = _dot(causal.astype(F32), dt * (-jnp.exp(alog_ref[...])), precision=HIGHEST)
    a_t = a.T
    dt_t = dt.T
    ea = jnp.exp(a)
    wsd = jnp.exp(a[L - 1:L, :] - a) * dt
    lane = lax.broadcasted_iota(I32, (L, gw), 1)

    def expand(cols, g):
        out = jnp.broadcast_to(cols[:, g * hpg + hpg - 1:g * hpg + hpg], (L, gw))
        for jj in range(hpg - 2, -1, -1):
            bc = jnp.broadcast_to(cols[:, g * hpg + jj:g * hpg + jj + 1], (L, gw))
            out = jnp.where(lane < (jj + 1) * SSD_HEAD_DIM, bc, out)
        return out

    for g in range(SSD_GROUPS):
        gsl = slice(g * gw, (g + 1) * gw)
        xg = act_ref[:, gsl]
        bg = act_ref[:, b_off + g * SSD_STATE:b_off + (g + 1) * SSD_STATE]
        cg_ = act_ref[:, c_off + g * SSD_STATE:c_off + (g + 1) * SSD_STATE].astype(BF16)
        cb = _dot_nt(cg_, bg.astype(BF16))
        ea_x = expand(ea, g)
        state = st_ref[g]
        acc = _dot(cg_, state.astype(BF16)) * ea_x
        for jj in range(hpg):
            hd = g * hpg + jj
            seg = jnp.where(causal, a[:, hd:hd + 1] - a_t[hd:hd + 1, :], -jnp.inf)
            w = (cb * jnp.exp(seg) * dt_t[hd:hd + 1, :]).astype(BF16)
            in_head = (lane >= jj * SSD_HEAD_DIM) & (lane < (jj + 1) * SSD_HEAD_DIM)
            acc = acc + _dot(w, jnp.where(in_head, xg, 0.0).astype(BF16))
        y = (acc + dsk_ref[:, gsl] * xg) * _silu(z_ref[:, gsl].astype(F32))
        y = y * lax.rsqrt(jnp.mean(y * y, axis=1, keepdims=True) + EPS) * nw_ref[:, gsl]
        o_ref[:, gsl] = y.astype(o_ref.dtype)
        xw = (xg * expand(wsd, g)).astype(BF16)
        st_ref[g] = ea_x[L - 1:L, :] * state + _dot(bg.T.astype(BF16), xw)


def _ssd(z, xbc, dt_raw, p, bsz, seq):
    m, inner = z.shape
    nt = seq // CHUNK
    conv_ch = xbc.shape[1]
    vec = lambda n: pl.BlockSpec((1, n), lambda b, j: (0, 0))
    return pl.pallas_call(
        _ssd_kernel,
        out_shape=jax.ShapeDtypeStruct((m, inner), BF16),
        grid=(bsz, nt),
        in_specs=[pl.BlockSpec((CHUNK, inner), lambda b, j: (b * nt + j, 0)),
                  pl.BlockSpec((CHUNK, conv_ch), lambda b, j: (b * nt + j, 0)),
                  pl.BlockSpec((CHUNK, LANES), lambda b, j: (b * nt + j, 0)),
                  pl.BlockSpec((CONV_WIDTH, conv_ch), lambda b, j: (0, 0)), vec(conv_ch),
                  vec(LANES), vec(LANES), vec(inner), vec(inner)],
        out_specs=pl.BlockSpec((CHUNK, inner), lambda b, j: (b * nt + j, 0)),
        scratch_shapes=[pltpu.VMEM((SUBLANES, conv_ch), F32),
                        pltpu.VMEM((CHUNK, conv_ch), F32),
                        pltpu.VMEM((SSD_GROUPS, SSD_STATE, inner // SSD_GROUPS), F32)],
        compiler_params=_cparams(("parallel", "arbitrary")),
    )(z, xbc, dt_raw, p["conv_w"], p["conv_b"], p["dt_bias"], p["a_log"], p["d_skip"],
      p["norm"])


def _outproj_kernel(*refs, n_in):
    y_refs, w_refs = refs[:n_in], refs[n_in:2 * n_in]
    h_ref, g_ref, o_ref = refs[2 * n_in:]
    acc = _dot(y_refs[0][...], w_refs[0][...])
    for y_ref, w_ref in zip(y_refs[1:], w_refs[1:]):
        acc = acc + _dot(y_ref[...], w_ref[...])
    o_ref[...] = h_ref[...] + g_ref[0] * acc


def _outproj(ys, w, h, gate, seq, tm):
    m, d = h.shape
    tiles_per_seq = seq // tm
    in_specs, args, k0 = [], [], 0
    for y in ys:
        in_specs.append(pl.BlockSpec((tm, y.shape[1]), lambda i: (i, 0)))
        args.append(y)
    for y in ys:
        kk = y.shape[1]
        in_specs.append(pl.BlockSpec((kk, d), lambda i, kb=k0 // kk: (kb, 0)))
        args.append(w)
        k0 += kk
    in_specs += [pl.BlockSpec((tm, d), lambda i: (i, 0)),
                 pl.BlockSpec((1, 1, d), lambda i: (i // tiles_per_seq, 0, 0))]
    args += [h, gate]
    return pl.pallas_call(
        functools.partial(_outproj_kernel, n_in=len(ys)),
        out_shape=jax.ShapeDtypeStruct((m, d), F32),
        grid=(m // tm,), in_specs=in_specs,
        out_specs=pl.BlockSpec((tm, d), lambda i: (i, 0)),
        compiler_params=_cparams(("parallel",)),
    )(*args)


def _router_kernel(h_ref, g_ref, sh_ref, sc_ref, wr_ref, br_ref,
                   up_ref, topi_ref, gate_ref, rank_ref, cnt_ref, carry_ref):
    @pl.when(pl.program_id(0) == 0)
    def _():
        carry_ref[...] = jnp.zeros_like(carry_ref)

    tm = h_ref.shape[0]
    u = _norm_mod(h_ref[...], g_ref[...], sh_ref[0], sc_ref[0])
    up_ref[...] = _pack_pairs(u)
    logits = _dot(u, wr_ref[...], precision=HIGHEST) + br_ref[...]
    lt = jnp.concatenate([logits[r0:r0 + LANES].T for r0 in range(0, tm, LANES)], axis=1)
    l = lt[:N_EXPERTS]
    e_iota = lax.broadcasted_iota(I32, (N_EXPERTS, tm), 0).astype(F32)
    vals, idxs, hots = [], [], []
    for _ in range(TOP_K):
        mx = jnp.max(l, axis=0, keepdims=True)
        idx = jnp.min(jnp.where(l == mx, e_iota, float(N_EXPERTS)), axis=0, keepdims=True)
        hot = e_iota == idx
        l = jnp.where(hot, -jnp.inf, l)
        vals.append(mx)
        idxs.append(idx)
        hots.append(hot)
    exps = [jnp.exp(v - vals[0]) for v in vals]
    den = exps[0] + exps[1] + exps[2] + exps[3]
    gate_ref[...] = jnp.concatenate([e / den for e in exps], axis=0)
    topi_ref[...] = jnp.concatenate(idxs, axis=0).astype(I32)

    sel = jnp.zeros((N_EXPERTS, tm), F32)
    for hot in hots:
        sel = jnp.where(hot, 1.0, sel)
    r_i = lax.broadcasted_iota(I32, (tm, tm), 0)
    c_i = lax.broadcasted_iota(I32, (tm, tm), 1)
    before = (r_i < c_i).astype(BF16)
    carry = carry_ref[:, 0:1]
    cum = _dot(sel.astype(BF16), before) + carry
    rank_ref[...] = jnp.concatenate(
        [jnp.sum(jnp.where(hot, cum, 0.0), axis=0, keepdims=True) for hot in hots], axis=0).astype(I32)
    total = carry + jnp.sum(sel, axis=1, keepdims=True)
    carry_ref[...] = jnp.broadcast_to(total, carry_ref.shape)
    cnt_ref[...] = jnp.broadcast_to(total, cnt_ref.shape)


def _router(h, g, shift, scale, wr, br, seq, tm):
    m, d = h.shape
    tiles_per_seq = seq // tm
    bmap = lambda i: (i // tiles_per_seq, 0, 0)
    row4 = lambda: pl.BlockSpec((TOP_K, tm), lambda i: (0, i))
    return pl.pallas_call(
        _router_kernel,
        out_shape=[jax.ShapeDtypeStruct((m, d // 2), I32),
                   jax.ShapeDtypeStruct((TOP_K, m), I32),
                   jax.ShapeDtypeStruct((TOP_K, m), F32),
                   jax.ShapeDtypeStruct((TOP_K, m), I32),
                   jax.ShapeDtypeStruct((N_EXPERTS, LANES), F32)],
        grid=(m // tm,),
        in_specs=[pl.BlockSpec((tm, d), lambda i: (i, 0)),
                  pl.BlockSpec((1, d), lambda i: (0, 0)),
                  pl.BlockSpec((1, 1, d), bmap), pl.BlockSpec((1, 1, d), bmap),
                  pl.BlockSpec((d, LANES), lambda i: (0, 0)),
                  pl.BlockSpec((1, LANES), lambda i: (0, 0))],
        out_specs=[pl.BlockSpec((tm, d // 2), lambda i: (i, 0)), row4(), row4(), row4(),
                   pl.BlockSpec((N_EXPERTS, LANES), lambda i: (0, 0))],
        scratch_shapes=[pltpu.VMEM((N_EXPERTS, LANES), F32)],
        compiler_params=_cparams(("arbitrary",)),
    )(h, g.reshape(1, d), shift, scale, wr, br)


def _dest_kernel(ps_ref, topi_ref, rank_ref, o_ref):
    topi = topi_ref[...]
    acc = rank_ref[...]
    for e in range(N_EXPERTS):
        acc = acc + jnp.where(topi == e, ps_ref[e], 0)
    o_ref[...] = acc


def _dest_rows(pad_start, topi, rank, tw):
    k, m = topi.shape
    blk = lambda: pl.BlockSpec((k, tw), lambda i, ps: (0, i))
    return pl.pallas_call(
        _dest_kernel,
        out_shape=jax.ShapeDtypeStruct((k, m), I32),
        grid_spec=pltpu.PrefetchScalarGridSpec(
            num_scalar_prefetch=1, grid=(m // tw,), in_specs=[blk(), blk()], out_specs=blk()),
        compiler_params=_cparams(("parallel",)),
    )(pad_start, topi, rank)


def _dispatch_kernel(dest_ref, up_ref, xs_ref, buf_ref, lsem, ssem):
    i = pl.program_id(0)
    n = pl.num_programs(0)
    tm = buf_ref.shape[1]
    slot = i % 2

    def load(tile, s):
        return pltpu.make_async_copy(up_ref.at[pl.ds(tile * tm, tm)], buf_ref.at[s], lsem.at[s])

    def wait_rows(s):
        for _ in range(TOP_K):
            pltpu.make_async_copy(buf_ref.at[s], xs_ref.at[pl.ds(0, tm)], ssem.at[s]).wait()

    pl.when(i == 0)(lambda: load(0, 0).start())
    pl.when(i > 0)(lambda: wait_rows(1 - slot))
    pl.when(i + 1 < n)(lambda: load(i + 1, 1 - slot).start())
    load(i, slot).wait()

    def issue(t, c):
        for k in range(TOP_K):
            pltpu.make_async_copy(buf_ref.at[slot, pl.ds(t, 1)],
                                  xs_ref.at[pl.ds(dest_ref[k * tm + t], 1)], ssem.at[slot]).start()
        return c

    lax.fori_loop(0, tm, issue, 0)
    pl.when(i == n - 1)(lambda: wait_rows(slot))


def _dispatch(dest_tiles, up, n_rows, tm):
    m, wp = up.shape
    return pl.pallas_call(
        _dispatch_kernel,
        out_shape=jax.ShapeDtypeStruct((n_rows, wp), I32),
        grid=(m // tm,),
        in_specs=[pl.BlockSpec((TOP_K * tm,), lambda i: (i,), memory_space=pltpu.SMEM),
                  pl.BlockSpec(memory_space=pl.ANY)],
        out_specs=pl.BlockSpec(memory_space=pl.ANY),
        scratch_shapes=[pltpu.VMEM((2, tm, wp), I32), pltpu.SemaphoreType.DMA((2,)),
                        pltpu.SemaphoreType.DMA((2,))],
        compiler_params=_cparams(("arbitrary",), has_side_effects=True),
    )(dest_tiles, up)


def _combine_kernel(dcur_ref, dnext_ref, gate_ref, h_ref, gf_ref, fn_ref, y_ref, o_ref, buf_ref, sem,
                    *, final):
    i = pl.program_id(0)
    tm = h_ref.shape[0]
    slot = i % 2

    def issue(d_ref, s):
        def body(t, c):
            for k in range(TOP_K):
                pltpu.make_async_copy(y_ref.at[pl.ds(d_ref[k * tm + t], 1)],
                                      buf_ref.at[s, k, pl.ds(t, 1)], sem.at[s]).start()
            return c
        lax.fori_loop(0, tm, body, 0)

    pl.when(i == 0)(lambda: issue(dcur_ref, slot))
    pl.when(i + 1 < pl.num_programs(0))(lambda: issue(dnext_ref, 1 - slot))
    pltpu.make_async_copy(buf_ref.at[slot], buf_ref.at[slot], sem.at[slot]).wait()
    acc = gate_ref[:, 0:1] * _unpack_pairs(buf_ref[slot, 0])
    for k in range(1, TOP_K):
        acc = acc + gate_ref[:, k:k + 1] * _unpack_pairs(buf_ref[slot, k])
    hn = h_ref[...] + gf_ref[0] * acc
    if final:
        hn = hn * lax.rsqrt(jnp.mean(hn * hn, axis=-1, keepdims=True) + EPS) * fn_ref[...]
    o_ref[...] = hn


def _combine(dest_tiles, gates_col, h, gf, fnorm, y, seq, tm, final):
    m, d = h.shape
    tiles_per_seq = seq // tm
    nt = m // tm
    return pl.pallas_call(
        functools.partial(_combine_kernel, final=final),
        out_shape=jax.ShapeDtypeStruct((m, d), F32),
        grid=(nt,),
        in_specs=[pl.BlockSpec((TOP_K * tm,), lambda i: (i,), memory_space=pltpu.SMEM),
                  pl.BlockSpec((TOP_K * tm,), lambda i: (jnp.minimum(i + 1, nt - 1),),
                               memory_space=pltpu.SMEM),
                  pl.BlockSpec((tm, TOP_K), lambda i: (i, 0)),
                  pl.BlockSpec((tm, d), lambda i: (i, 0)),
                  pl.BlockSpec((1, 1, d), lambda i: (i // tiles_per_seq, 0, 0)),
                  pl.BlockSpec((1, d), lambda i: (0, 0)),
                  pl.BlockSpec(memory_space=pl.ANY)],
        out_specs=pl.BlockSpec((tm, d), lambda i: (i, 0)),
        scratch_shapes=[pltpu.VMEM((2, TOP_K, tm, d // 2), I32), pltpu.SemaphoreType.DMA((2,))],
        compiler_params=_cparams(("arbitrary",)),
    )(dest_tiles, dest_tiles, gates_col, h, gf, fnorm.reshape(1, d), y)


def _expert_kernel(be_ref, nb_ref, first_ref, x_ref, wgu_ref, bgu_ref, wd_ref, bd_ref, y_ref,
                   wgu_bf, wd_bf):
    i = pl.program_id(0)

    @pl.when(i < nb_ref[0])
    def _():
        dff = wd_bf.shape[0]

        @pl.when(first_ref[i] == 1)
        def _():
            rows = 64

            def cast(r, c):
                r0 = pl.multiple_of(r * rows, rows)
                wgu_bf[pl.ds(r0, rows), :] = wgu_ref[0, 0, pl.ds(r0, rows), :].astype(BF16)
                wd_bf[pl.ds(r0, rows), :] = wd_ref[0, 0, pl.ds(r0, rows), :].astype(BF16)
                return c

            lax.fori_loop(0, dff // rows, cast, 0)

        x = _unpack_pairs(x_ref[...]).astype(BF16)
        hb = _dot(x, wgu_bf[...]) + bgu_ref[0, 0]
        h_glu = jnp.minimum(hb[:, :dff], SWIGLU_LIMIT)
        h_lin = jnp.clip(hb[:, dff:], -SWIGLU_LIMIT, SWIGLU_LIMIT)
        act = h_glu * jax.nn.sigmoid(SWIGLU_ALPHA * h_glu) * (h_lin + 1.0)
        y_ref[...] = _pack_pairs(_dot(act.astype(BF16), wd_bf[...]) + bd_ref[0, 0])


def _experts(block_e, n_used, first, xs, wgu, bgu, wd, bd, layer):
    n_rows, wp = xs.shape
    _, ne, d, ff2 = wgu.shape
    assert d == ff2 // 2
    nblk = n_rows // EXPERT_BLOCK

    def xmap(i, be, nb, fi):
        return (jnp.minimum(i, nb[0] - 1), 0)

    emap = lambda i, be, nb, fi: (layer, be[i], 0, 0)
    grid_spec = pltpu.PrefetchScalarGridSpec(
        num_scalar_prefetch=3, grid=(nblk,),
        in_specs=[pl.BlockSpec((EXPERT_BLOCK, wp), xmap),
                  pl.BlockSpec((1, 1, d, ff2), emap), pl.BlockSpec((1, 1, 1, ff2), emap),
                  pl.BlockSpec((1, 1, ff2 // 2, d), emap), pl.BlockSpec((1, 1, 1, d), emap)],
        out_specs=pl.BlockSpec((EXPERT_BLOCK, wp), xmap),
        scratch_shapes=[pltpu.VMEM((d, ff2), BF16), pltpu.VMEM((ff2 // 2, d), BF16)])
    depth = wgu.shape[0]
    return pl.pallas_call(
        _expert_kernel,
        out_shape=jax.ShapeDtypeStruct((n_rows, wp), I32),
        grid_spec=grid_spec,
        compiler_params=_cparams(("arbitrary",)),
    )(block_e, n_used, first, xs, wgu, bgu.reshape(depth, ne, 1, ff2), wd, bd.reshape(depth, ne, 1, d))


def _moe(h, g, shift, scale, gf, fnorm, wr, br, wgu, bgu, wd, bd, layer, seq, final):
    m, d = h.shape
    tm = 256
    wr_p = jnp.zeros((d, LANES), F32).at[:, :N_EXPERTS].set(wr)
    br_p = jnp.zeros((1, LANES), F32).at[0, :N_EXPERTS].set(br)
    up, topi, gates, rank, cnt = _router(h, g, shift, scale, wr_p, br_p, seq, tm)

    counts = cnt[:, 0].astype(I32)
    padded = (counts + EXPERT_BLOCK - 1) // EXPERT_BLOCK * EXPERT_BLOCK
    pad_end = jnp.cumsum(padded)
    pad_start = pad_end - padded
    nblk = m * TOP_K // EXPERT_BLOCK + N_EXPERTS
    n_rows = nblk * EXPERT_BLOCK
    n_used = pad_end[-1:] // EXPERT_BLOCK
    blk = jnp.arange(nblk, dtype=I32)
    blk_c = jnp.minimum(blk, n_used - 1)
    block_e = jnp.minimum(jnp.sum(blk_c[:, None] * EXPERT_BLOCK >= pad_end[None, :], axis=1),
                          N_EXPERTS - 1).astype(I32)
    first = jnp.concatenate([jnp.ones((1,), I32), (block_e[1:] != block_e[:-1]).astype(I32)])

    dest = _dest_rows(pad_start, topi, rank, min(m, 8192))
    dest_tiles = dest.reshape(TOP_K, m // tm, tm).transpose(1, 0, 2).reshape(-1)
    xs = _dispatch(dest_tiles, up, n_rows, tm)
    y = _experts(block_e, n_used.astype(I32), first, xs, wgu, bgu, wd, bd, layer)
    return _combine(dest_tiles, gates.T, h, gf, fnorm, y, seq, tm, final)


def _block_diag(w, group):
    nb, b, _ = w.shape
    per = group // b
    wg = w.reshape(nb // per, per, b, b)
    dense = jnp.einsum("gnde,nm->gndme", wg, jnp.eye(per, dtype=w.dtype))
    return dense.reshape(nb // per, group, group)


def kernel(x, c, mod_w, mod_b, norm_mix, norm_ffn, ev_w_in, ev_lru_conv_w, ev_lru_conv_b, ev_lru_w_r, ev_lru_b_r, ev_lru_w_i, ev_lru_b_i, ev_lru_lambda, ev_ml_conv_w, ev_ml_conv_b, ev_ml_w_q, ev_ml_w_k, ev_ml_w_v, ev_ml_w_ig, ev_ml_b_ig, ev_ml_w_fg, ev_ml_b_fg, ev_ml_norm, ev_ml_skip, ev_w_out, od_w_in, od_conv_w, od_conv_b, od_dt_bias, od_a_log, od_d, od_norm, od_w_out, moe_router_w, moe_router_b, moe_w_gu, moe_b_gu, moe_w_down, moe_b_down, final_norm):
    bsz, seq, d = x.shape
    depth = mod_w.shape[0]
    m = bsz * seq
    mod = _modulation(c, mod_w, mod_b)
    h = x.reshape(m, d).astype(F32)
    for layer in range(depth):
        sh_m, sc_m, g_m, sh_f, sc_f, g_f = (mod[layer, i] for i in range(6))
        j = layer // 2
        if layer % 2 == 0:
            w = ev_lru_lambda.shape[1]
            w_in = ev_w_in[j].astype(BF16)
            proj = _inproj(h, norm_mix[layer], sh_m, sc_m, w_in, None, [w_in.shape[1]], seq, 512)[0]
            lru_p = dict(conv_w=ev_lru_conv_w[j], conv_b=ev_lru_conv_b[j].reshape(1, w),
                         w_r=ev_lru_w_r[j].astype(BF16), b_r=ev_lru_b_r[j].reshape(1, w),
                         w_i=ev_lru_w_i[j].astype(BF16), b_i=ev_lru_b_i[j].reshape(1, w),
                         lam=ev_lru_lambda[j].reshape(1, w))
            ya = _lru(proj, lru_p, bsz, seq, 256)
            wg = jnp.zeros((3 * w, LANES), F32)
            wg = wg.at[:, :ML_HEADS].set(ev_ml_w_ig[j]).at[:, ML_HEADS:2 * ML_HEADS].set(ev_ml_w_fg[j])
            bg = jnp.zeros((1, LANES), F32)
            bg = bg.at[0, :ML_HEADS].set(ev_ml_b_ig[j]).at[0, ML_HEADS:2 * ML_HEADS].set(ev_ml_b_fg[j])
            ml_p = dict(conv_w=ev_ml_conv_w[j], conv_b=ev_ml_conv_b[j].reshape(1, w),
                        w_q=_block_diag(ev_ml_w_q[j], LANES).astype(BF16),
                        w_k=_block_diag(ev_ml_w_k[j], LANES).astype(BF16),
                        w_v=_block_diag(ev_ml_w_v[j], LANES).astype(BF16),
                        w_g=wg.astype(BF16), b_g=bg,
                        norm=ev_ml_norm[j].reshape(1, w), skip=ev_ml_skip[j].reshape(1, w))
            yb = _mlstm(proj, ml_p, bsz, seq)
            h = _outproj([ya, yb], ev_w_out[j].astype(BF16), h, g_m, seq, 512)
        else:
            inner = od_norm.shape[1]
            heads = od_dt_bias.shape[1]
            conv_ch = od_conv_w.shape[2]
            w_in = od_w_in[j]
            wdt = jnp.zeros((d, LANES), F32).at[:, :heads].set(w_in[:, inner + conv_ch:])
            z, xbc, dt_raw = _inproj(h, norm_mix[layer], sh_m, sc_m, w_in[:, :inner + conv_ch].astype(BF16),
                                     wdt.astype(BF16), [inner, conv_ch], seq, 256)
            pad = lambda v: jnp.zeros((1, LANES), F32).at[0, :heads].set(v)
            ssd_p = dict(conv_w=od_conv_w[j], conv_b=od_conv_b[j].reshape(1, conv_ch),
                         dt_bias=pad(od_dt_bias[j]), a_log=pad(od_a_log[j]),
                         d_skip=jnp.repeat(od_d[j], SSD_HEAD_DIM).reshape(1, inner),
                         norm=od_norm[j].reshape(1, inner))
            y = _ssd(z, xbc, dt_raw, ssd_p, bsz, seq)
            h = _outproj([y], od_w_out[j].astype(BF16), h, g_m, seq, 512)
        h = _moe(h, norm_ffn[layer], sh_f, sc_f, g_f, final_norm,
                 moe_router_w[layer], moe_router_b[layer],
                 moe_w_gu, moe_b_gu, moe_w_down, moe_b_down, layer, seq, final=(layer == depth - 1))
    return h.reshape(bsz, seq, d)
```

```python
import functools

import jax
import jax.numpy as jnp
from jax import lax
from jax.experimental import pallas as pl
from jax.experimental.pallas import tpu as pltpu
from jax.experimental.pallas import tpu_sc as plsc

F32 = jnp.float32
BF16 = jnp.bfloat16
I32 = jnp.int32
HIGHEST = lax.Precision.HIGHEST

EPS = 1e-6
CONV_WIDTH = 4
LANES = 128
SUBLANES = 8
LRU_HEADS = 8
LRU_C = 8.0
ML_HEADS = 8
ML_QKV_BLOCK = 4
CHUNK = 128
SSD_HEAD_DIM = 64
SSD_GROUPS = 8
SSD_STATE = 128
N_EXPERTS = 32
TOP_K = 4
SWIGLU_ALPHA = 1.702
SWIGLU_LIMIT = 7.0
EXPERT_BLOCK = 512
VMEM_LIMIT = 56 * 1024 * 1024


def _cparams(sem, **kw):
    return pltpu.CompilerParams(dimension_semantics=sem, vmem_limit_bytes=VMEM_LIMIT, **kw)


def _silu(x):
    return x * jax.nn.sigmoid(x)


def _log_sigmoid(x):
    return jnp.minimum(x, 0.0) - jnp.log1p(jnp.exp(-jnp.abs(x)))


def _softplus(x):
    return jnp.maximum(x, 0.0) + jnp.log1p(jnp.exp(-jnp.abs(x)))


def _dot(a, b, **kw):
    return jnp.dot(a, b, preferred_element_type=F32, **kw)


def _dot_nt(a, b):
    return lax.dot_general(a, b, (((1,), (1,)), ((), ())), preferred_element_type=F32)


def _pack_pairs(x):
    w = x.shape[1] // 2
    lo = lax.bitcast_convert_type(x[:, :w].astype(BF16).astype(F32), I32)
    hi = lax.bitcast_convert_type(x[:, w:].astype(BF16).astype(F32), I32)
    return lax.shift_right_logical(lo, 16) | (hi & jnp.int32(-65536))


def _unpack_pairs(p):
    lo = lax.bitcast_convert_type(lax.shift_left(p, 16), F32)
    hi = lax.bitcast_convert_type(p & jnp.int32(-65536), F32)
    return jnp.concatenate([lo, hi], axis=1)


def _norm_mod(h, g, shift, scale):
    y = h * lax.rsqrt(jnp.mean(h * h, axis=-1, keepdims=True) + EPS)
    return (y * g) * (1.0 + scale) + shift


def _causal_conv(x, tail_ref, w_ref, b_ref, sl):
    t = x.shape[0]
    tail = tail_ref[:, sl]
    row8 = lax.broadcasted_iota(I32, tail.shape, 0)
    out = b_ref[:, sl] + x * w_ref[CONV_WIDTH - 1:CONV_WIDTH, sl]
    for k in range(1, CONV_WIDTH):
        xs = pltpu.roll(x, k, axis=0)
        first = jnp.where(row8 < k, pltpu.roll(tail, k, axis=0), xs[:SUBLANES])
        xs = jnp.concatenate([first, xs[SUBLANES:]], axis=0)
        out = out + xs * w_ref[CONV_WIDTH - 1 - k:CONV_WIDTH - k, sl]
    tail_ref[:, sl] = x[t - SUBLANES:]
    return out


def _mod_kernel(c_ref, w_ref, b_ref, o_ref):
    cond = _silu(c_ref[...])
    o_ref[0, 0] = _dot(cond, w_ref[0], precision=HIGHEST) + b_ref[0, 0]


def _modulation(c, mod_w, mod_b):
    depth, d, _ = mod_w.shape
    bsz = c.shape[0]
    out = pl.pallas_call(
        _mod_kernel,
        out_shape=jax.ShapeDtypeStruct((depth, 6, bsz, d), F32),
        grid=(depth, 6),
        in_specs=[pl.BlockSpec((bsz, d), lambda l, j: (0, 0)),
                  pl.BlockSpec((1, d, d), lambda l, j: (l, 0, j)),
                  pl.BlockSpec((1, 1, 1, d), lambda l, j: (l, j, 0, 0))],
        out_specs=pl.BlockSpec((1, 1, bsz, d), lambda l, j: (l, j, 0, 0)),
        compiler_params=_cparams(("parallel", "parallel")),
    )(c.astype(F32), mod_w, mod_b.reshape(depth, 6, 1, d))
    return out.reshape(depth, 6, bsz, 1, d)


def _inproj_kernel(h_ref, g_ref, sh_ref, sc_ref, w_ref, *rest, n_chunk, with_dt):
    if with_dt:
        wdt_ref, *o_refs, odt_ref = rest
    else:
        o_refs = rest
    u = _norm_mod(h_ref[...], g_ref[...], sh_ref[0], sc_ref[0]).astype(BF16)
    off = 0
    for o_ref in o_refs:
        for n0 in range(0, o_ref.shape[1], n_chunk):
            o_ref[:, n0:n0 + n_chunk] = _dot(u, w_ref[:, off + n0:off + n0 + n_chunk]).astype(o_ref.dtype)
        off += o_ref.shape[1]
    if with_dt:
        odt_ref[...] = _dot(u, wdt_ref[...])


def _inproj(h, g, shift, scale, w, wdt, splits, seq, tm):
    m, d = h.shape
    n = w.shape[1]
    assert sum(splits) == n
    tiles_per_seq = seq // tm
    bmap = lambda i: (i // tiles_per_seq, 0, 0)
    in_specs = [pl.BlockSpec((tm, d), lambda i: (i, 0)),
                pl.BlockSpec((1, d), lambda i: (0, 0)),
                pl.BlockSpec((1, 1, d), bmap),
                pl.BlockSpec((1, 1, d), bmap),
                pl.BlockSpec((d, n), lambda i: (0, 0), pipeline_mode=pl.Buffered(1))]
    out_shape = [jax.ShapeDtypeStruct((m, s), BF16) for s in splits]
    out_specs = [pl.BlockSpec((tm, s), lambda i: (i, 0)) for s in splits]
    args = [h, g.reshape(1, d), shift, scale, w]
    if wdt is not None:
        in_specs.append(pl.BlockSpec((d, LANES), lambda i: (0, 0)))
        out_shape.append(jax.ShapeDtypeStruct((m, LANES), F32))
        out_specs.append(pl.BlockSpec((tm, LANES), lambda i: (i, 0)))
        args.append(wdt)
    return pl.pallas_call(
        functools.partial(_inproj_kernel, n_chunk=1024, with_dt=wdt is not None),
        out_shape=out_shape, grid=(m // tm,), in_specs=in_specs, out_specs=out_specs,
        compiler_params=_cparams(("parallel",)),
    )(*args)


def _lru_kernel(xa_ref, ga_ref, cw_ref, cb_ref, wr_ref, br_ref, wi_ref, bi_ref, lam_ref,
                o_ref, tail_ref, hc_ref):
    @pl.when(pl.program_id(1) == 0)
    def _():
        tail_ref[...] = jnp.zeros_like(tail_ref)
        hc_ref[...] = jnp.zeros_like(hc_ref)

    t = xa_ref.shape[0]
    row = lax.broadcasted_iota(I32, (t, LANES), 0)
    for hh in range(LRU_HEADS):
        sl = slice(hh * LANES, (hh + 1) * LANES)
        xc = _causal_conv(xa_ref[:, sl].astype(F32), tail_ref, cw_ref, cb_ref, sl)
        xcb = xc.astype(BF16)
        r = jax.nn.sigmoid(_dot(xcb, wr_ref[hh]) + br_ref[:, sl])
        i = jax.nn.sigmoid(_dot(xcb, wi_ref[hh]) + bi_ref[:, sl])
        log_a = LRU_C * r * _log_sigmoid(lam_ref[:, sl])
        a = jnp.exp(log_a)
        th = jnp.tanh(log_a)
        u = jnp.sqrt(-2.0 * th / (1.0 - th)) * (i * xc)
        s = 1
        while s < t:
            m = row >= s
            u = jnp.where(m, u + a * pltpu.roll(u, s, axis=0), u)
            a = jnp.where(m, a * pltpu.roll(a, s, axis=0), a)
            s *= 2
        h = u + a * hc_ref[:, sl]
        hc_ref[:, sl] = h[t - 1:t]
        ga = ga_ref[:, sl].astype(F32)
        o_ref[:, sl] = (h * jax.nn.gelu(ga, approximate=True)).astype(o_ref.dtype)


def _lru(proj, p, bsz, seq, tm):
    m = proj.shape[0]
    w = LRU_HEADS * LANES
    nt = seq // tm
    vec = lambda: pl.BlockSpec((1, w), lambda b, j: (0, 0))
    return pl.pallas_call(
        _lru_kernel,
        out_shape=jax.ShapeDtypeStruct((m, w), BF16),
        grid=(bsz, nt),
        in_specs=[pl.BlockSpec((tm, w), lambda b, j: (b * nt + j, 0)),
                  pl.BlockSpec((tm, w), lambda b, j: (b * nt + j, 1)),
                  pl.BlockSpec((CONV_WIDTH, w), lambda b, j: (0, 0)), vec(),
                  pl.BlockSpec((LRU_HEADS, LANES, LANES), lambda b, j: (0, 0, 0)), vec(),
                  pl.BlockSpec((LRU_HEADS, LANES, LANES), lambda b, j: (0, 0, 0)), vec(), vec()],
        out_specs=pl.BlockSpec((tm, w), lambda b, j: (b * nt + j, 0)),
        scratch_shapes=[pltpu.VMEM((SUBLANES, w), F32), pltpu.VMEM((1, w), F32)],
        compiler_params=_cparams(("parallel", "arbitrary")),
    )(proj, proj, p["conv_w"], p["conv_b"], p["w_r"], p["b_r"], p["w_i"], p["b_i"], p["lam"])


def _mlstm_kernel(xb_ref, zb_ref, cw_ref, cb_ref, wq_ref, wk_ref, wv_ref, wg_ref, bg_ref,
                  nw_ref, sk_ref, o_ref, tail_ref, qkv_ref, xc_ref, caug_ref, m_ref):
    @pl.when(pl.program_id(1) == 0)
    def _():
        tail_ref[...] = jnp.zeros_like(tail_ref)
        caug_ref[...] = jnp.zeros_like(caug_ref)
        m_ref[...] = jnp.full(m_ref.shape, -jnp.inf, F32)

    L = CHUNK
    width = ML_HEADS * LANES
    scale = LANES ** -0.5
    for hh in range(ML_HEADS):
        sl = slice(hh * LANES, (hh + 1) * LANES)
        xb = xb_ref[:, sl].astype(F32)
        xc = _silu(_causal_conv(xb, tail_ref, cw_ref, cb_ref, sl))
        xc_ref[:, sl] = xc
        xcb = xc.astype(BF16)
        qkv_ref[:, sl] = _dot(xcb, wq_ref[hh]).astype(BF16)
        qkv_ref[:, width + hh * LANES:width + (hh + 1) * LANES] = _dot(xcb, wk_ref[hh]).astype(BF16)
        qkv_ref[:, 2 * width + hh * LANES:2 * width + (hh + 1) * LANES] = (
            _dot(xb.astype(BF16), wv_ref[hh]).astype(BF16))

    gates = _dot(qkv_ref[...], wg_ref[...]) + bg_ref[...]
    rowi = lax.broadcasted_iota(I32, (L, L), 0)
    coli = lax.broadcasted_iota(I32, (L, L), 1)
    causal = rowi >= coli
    lf = jnp.where((coli >= ML_HEADS) & (coli < 2 * ML_HEADS), _log_sigmoid(gates), 0.0)
    gcum = _dot(causal.astype(F32), lf, precision=HIGHEST)
    x_col = jnp.where(coli < ML_HEADS, gates, gcum)
    x_row = x_col.T
    ones = jnp.ones((L, LANES), BF16)

    for hh in range(ML_HEADS):
        sl = slice(hh * LANES, (hh + 1) * LANES)
        q = qkv_ref[:, sl]
        k = qkv_ref[:, width + hh * LANES:width + (hh + 1) * LANES]
        v = qkv_ref[:, 2 * width + hh * LANES:2 * width + (hh + 1) * LANES]
        ic = x_col[:, hh:hh + 1]
        gc = x_col[:, ML_HEADS + hh:ML_HEADS + hh + 1]
        ir = x_row[hh:hh + 1, :]
        gr = x_row[ML_HEADS + hh:ML_HEADS + hh + 1, :]
        mp = m_ref[hh][:, 0:1]
        dmat = jnp.where(causal, gc - gr + ir, -jnp.inf)
        m_inter = mp + gc
        m_t = jnp.maximum(m_inter, jnp.max(dmat, axis=1, keepdims=True))
        w_intra = jnp.exp(dmat - m_t)
        w_inter = jnp.exp(m_inter - m_t)
        qk = (_dot_nt(q, k) * scale * w_intra).astype(BF16)
        v_aug = jnp.concatenate([v, ones], axis=1)
        caug = caug_ref[hh]
        nd = _dot(qk, v_aug) + w_inter * _dot(q, caug.astype(BF16))
        hval = nd[:, :LANES] / jnp.maximum(jnp.abs(nd[:, LANES:]), jnp.exp(-m_t))

        g_last = gc[L - 1:L, :]
        m_new = jnp.maximum(mp + g_last, jnp.max(g_last - gr + ir, axis=1, keepdims=True))
        ws = jnp.exp(g_last - gc + ic - m_new)
        wc = jnp.exp(mp + g_last - m_new)
        kw_t = (k.astype(F32) * (ws * scale)).T.astype(BF16)
        caug_ref[hh] = wc * caug + _dot(kw_t, v_aug)
        m_ref[hh] = jnp.broadcast_to(m_new, (1, LANES))

        mu = jnp.mean(hval, axis=1, keepdims=True)
        dv = hval - mu
        var = jnp.mean(dv * dv, axis=1, keepdims=True)
        hn = dv * lax.rsqrt(var + EPS) * nw_ref[:, sl]
        zb = zb_ref[:, sl].astype(F32)
        o_ref[:, sl] = ((hn + sk_ref[:, sl] * xc_ref[:, sl]) * _silu(zb)).astype(o_ref.dtype)


def _mlstm(proj, p, bsz, seq):
    m = proj.shape[0]
    w = ML_HEADS * LANES
    nt = seq // CHUNK
    vec = lambda: pl.BlockSpec((1, w), lambda b, j: (0, 0))
    blk = lambda: pl.BlockSpec((ML_HEADS, LANES, LANES), lambda b, j: (0, 0, 0))
    return pl.pallas_call(
        _mlstm_kernel,
        out_shape=jax.ShapeDtypeStruct((m, w), BF16),
        grid=(bsz, nt),
        in_specs=[pl.BlockSpec((CHUNK, w), lambda b, j: (b * nt + j, 2)),
                  pl.BlockSpec((CHUNK, w), lambda b, j: (b * nt + j, 3)),
                  pl.BlockSpec((CONV_WIDTH, w), lambda b, j: (0, 0)), vec(),
                  blk(), blk(), blk(),
                  pl.BlockSpec((3 * w, LANES), lambda b, j: (0, 0)),
                  pl.BlockSpec((1, LANES), lambda b, j: (0, 0)),
                  vec(), vec()],
        out_specs=pl.BlockSpec((CHUNK, w), lambda b, j: (b * nt + j, 0)),
        scratch_shapes=[pltpu.VMEM((SUBLANES, w), F32),
                        pltpu.VMEM((CHUNK, 3 * w), BF16),
                        pltpu.VMEM((CHUNK, w), F32),
                        pltpu.VMEM((ML_HEADS, LANES, 2 * LANES), F32),
                        pltpu.VMEM((ML_HEADS, 1, LANES), F32)],
        compiler_params=_cparams(("parallel", "arbitrary")),
    )(proj, proj, p["conv_w"], p["conv_b"], p["w_q"], p["w_k"], p["w_v"], p["w_g"], p["b_g"],
      p["norm"], p["skip"])


def _ssd_kernel(z_ref, xbc_ref, dt_ref, cw_ref, cb_ref, dtb_ref, alog_ref, dsk_ref, nw_ref,
                o_ref, tail_ref, act_ref, st_ref):
    @pl.when(pl.program_id(1) == 0)
    def _():
        tail_ref[...] = jnp.zeros_like(tail_ref)
        st_ref[...] = jnp.zeros_like(st_ref)

    L = CHUNK
    inner = o_ref.shape[1]
    gw = inner // SSD_GROUPS
    hpg = gw // SSD_HEAD_DIM
    b_off = inner
    c_off = inner + SSD_GROUPS * SSD_STATE
    for cg in range(xbc_ref.shape[1] // LANES):
        sl = slice(cg * LANES, (cg + 1) * LANES)
        act_ref[:, sl] = _silu(_causal_conv(xbc_ref[:, sl].astype(F32), tail_ref, cw_ref, cb_ref, sl))

    rowi = lax.broadcasted_iota(I32, (L, L), 0)
    coli = lax.broadcasted_iota(I32, (L, L), 1)
    causal = rowi >= coli
    dt = _softplus(dt_ref[...] + dtb_ref[...])
    a = _dot(causal.astype(F32), dt * (-jnp.exp(alog_ref[...])), precision=HIGHEST)
    a_t = a.T
    dt_t = dt.T
    ea = jnp.exp(a)
    wsd = jnp.exp(a[L - 1:L, :] - a) * dt
    lane = lax.broadcasted_iota(I32, (L, gw), 1)

    def expand(cols, g):
        out = jnp.broadcast_to(cols[:, g * hpg + hpg - 1:g * hpg + hpg], (L, gw))
        for jj in range(hpg - 2, -1, -1):
            bc = jnp.broadcast_to(cols[:, g * hpg + jj:g * hpg + jj + 1], (L, gw))
            out = jnp.where(lane < (jj + 1) * SSD_HEAD_DIM, bc, out)
        return out

    for g in range(SSD_GROUPS):
        gsl = slice(g * gw, (g + 1) * gw)
        xg = act_ref[:, gsl]
        bg = act_ref[:, b_off + g * SSD_STATE:b_off + (g + 1) * SSD_STATE]
        cg_ = act_ref[:, c_off + g * SSD_STATE:c_off + (g + 1) * SSD_STATE].astype(BF16)
        cb = _dot_nt(cg_, bg.astype(BF16))
        ea_x = expand(ea, g)
        state = st_ref[g]
        acc = _dot(cg_, state.astype(BF16)) * ea_x
        for jj in range(hpg):
            hd = g * hpg + jj
            seg = jnp.where(causal, a[:, hd:hd + 1] - a_t[hd:hd + 1, :], -jnp.inf)
            w = (cb * jnp.exp(seg) * dt_t[hd:hd + 1, :]).astype(BF16)
            in_head = (lane >= jj * SSD_HEAD_DIM) & (lane < (jj + 1) * SSD_HEAD_DIM)
            acc = acc + _dot(w, jnp.where(in_head, xg, 0.0).astype(BF16))
        y = (acc + dsk_ref[:, gsl] * xg) * _silu(z_ref[:, gsl].astype(F32))
        y = y * lax.rsqrt(jnp.mean(y * y, axis=1, keepdims=True) + EPS) * nw_ref[:, gsl]
        o_ref[:, gsl] = y.astype(o_ref.dtype)
        xw = (xg * expand(wsd, g)).astype(BF16)
        st_ref[g] = ea_x[L - 1:L, :] * state + _dot(bg.T.astype(BF16), xw)


def _ssd(z, xbc, dt_raw, p, bsz, seq):
    m, inner = z.shape
    nt = seq // CHUNK
    conv_ch = xbc.shape[1]
    vec = lambda n: pl.BlockSpec((1, n), lambda b, j: (0, 0))
    return pl.pallas_call(
        _ssd_kernel,
        out_shape=jax.ShapeDtypeStruct((m, inner), BF16),
        grid=(bsz, nt),
        in_specs=[pl.BlockSpec((CHUNK, inner), lambda b, j: (b * nt + j, 0)),
                  pl.BlockSpec((CHUNK, conv_ch), lambda b, j: (b * nt + j, 0)),
                  pl.BlockSpec((CHUNK, LANES), lambda b, j: (b * nt + j, 0)),
                  pl.BlockSpec((CONV_WIDTH, conv_ch), lambda b, j: (0, 0)), vec(conv_ch),
                  vec(LANES), vec(LANES), vec(inner), vec(inner)],
        out_specs=pl.BlockSpec((CHUNK, inner), lambda b, j: (b * nt + j, 0)),
        scratch_shapes=[pltpu.VMEM((SUBLANES, conv_ch), F32),
                        pltpu.VMEM((CHUNK, conv_ch), F32),
                        pltpu.VMEM((SSD_GROUPS, SSD_STATE, inner // SSD_GROUPS), F32)],
        compiler_params=_cparams(("parallel", "arbitrary")),
    )(z, xbc, dt_raw, p["conv_w"], p["conv_b"], p["dt_bias"], p["a_log"], p["d_skip"],
      p["norm"])


def _outproj_kernel(*refs, n_in):
    y_refs, w_refs = refs[:n_in], refs[n_in:2 * n_in]
    h_ref, g_ref, o_ref = refs[2 * n_in:]
    acc = _dot(y_refs[0][...], w_refs[0][...])
    for y_ref, w_ref in zip(y_refs[1:], w_refs[1:]):
        acc = acc + _dot(y_ref[...], w_ref[...])
    o_ref[...] = h_ref[...] + g_ref[0] * acc


def _outproj(ys, w, h, gate, seq, tm):
    m, d = h.shape
    tiles_per_seq = seq // tm
    in_specs, args, k0 = [], [], 0
    for y in ys:
        in_specs.append(pl.BlockSpec((tm, y.shape[1]), lambda i: (i, 0)))
        args.append(y)
    for y in ys:
        kk = y.shape[1]
        in_specs.append(pl.BlockSpec((kk, d), lambda i, kb=k0 // kk: (kb, 0)))
        args.append(w)
        k0 += kk
    in_specs += [pl.BlockSpec((tm, d), lambda i: (i, 0)),
                 pl.BlockSpec((1, 1, d), lambda i: (i // tiles_per_seq, 0, 0))]
    args += [h, gate]
    return pl.pallas_call(
        functools.partial(_outproj_kernel, n_in=len(ys)),
        out_shape=jax.ShapeDtypeStruct((m, d), F32),
        grid=(m // tm,), in_specs=in_specs,
        out_specs=pl.BlockSpec((tm, d), lambda i: (i, 0)),
        compiler_params=_cparams(("parallel",)),
    )(*args)


def _router_kernel(h_ref, g_ref, sh_ref, sc_ref, wr_ref, br_ref,
                   up_ref, topi_ref, gate_ref, rank_ref, cnt_ref, carry_ref):
    @pl.when(pl.program_id(0) == 0)
    def _():
        carry_ref[...] = jnp.zeros_like(carry_ref)

    tm = h_ref.shape[0]
    u = _norm_mod(h_ref[...], g_ref[...], sh_ref[0], sc_ref[0])
    up_ref[...] = _pack_pairs(u)
    logits = _dot(u, wr_ref[...], precision=HIGHEST) + br_ref[...]
    lt = jnp.concatenate([logits[r0:r0 + LANES].T for r0 in range(0, tm, LANES)], axis=1)
    l = lt[:N_EXPERTS]
    e_iota = lax.broadcasted_iota(I32, (N_EXPERTS, tm), 0).astype(F32)
    vals, idxs, hots = [], [], []
    for _ in range(TOP_K):
        mx = jnp.max(l, axis=0, keepdims=True)
        idx = jnp.min(jnp.where(l == mx, e_iota, float(N_EXPERTS)), axis=0, keepdims=True)
        hot = e_iota == idx
        l = jnp.where(hot, -jnp.inf, l)
        vals.append(mx)
        idxs.append(idx)
        hots.append(hot)
    exps = [jnp.exp(v - vals[0]) for v in vals]
    den = exps[0] + exps[1] + exps[2] + exps[3]
    gate_ref[...] = jnp.concatenate([e / den for e in exps], axis=0)
    topi_ref[...] = jnp.concatenate(idxs, axis=0).astype(I32)

    sel = jnp.zeros((N_EXPERTS, tm), F32)
    for hot in hots:
        sel = jnp.where(hot, 1.0, sel)
    r_i = lax.broadcasted_iota(I32, (tm, tm), 0)
    c_i = lax.broadcasted_iota(I32, (tm, tm), 1)
    before = (r_i < c_i).astype(BF16)
    carry = carry_ref[:, 0:1]
    cum = _dot(sel.astype(BF16), before) + carry
    rank_ref[...] = jnp.concatenate(
        [jnp.sum(jnp.where(hot, cum, 0.0), axis=0, keepdims=True) for hot in hots], axis=0).astype(I32)
    total = carry + jnp.sum(sel, axis=1, keepdims=True)
    carry_ref[...] = jnp.broadcast_to(total, carry_ref.shape)
    cnt_ref[...] = jnp.broadcast_to(total, cnt_ref.shape)


def _router(h, g, shift, scale, wr, br, seq, tm):
    m, d = h.shape
    tiles_per_seq = seq // tm
    bmap = lambda i: (i // tiles_per_seq, 0, 0)
    row4 = lambda: pl.BlockSpec((TOP_K, tm), lambda i: (0, i))
    return pl.pallas_call(
        _router_kernel,
        out_shape=[jax.ShapeDtypeStruct((m, d // 2), I32),
                   jax.ShapeDtypeStruct((TOP_K, m), I32),
                   jax.ShapeDtypeStruct((TOP_K, m), F32),
                   jax.ShapeDtypeStruct((TOP_K, m), I32),
                   jax.ShapeDtypeStruct((N_EXPERTS, LANES), F32)],
        grid=(m // tm,),
        in_specs=[pl.BlockSpec((tm, d), lambda i: (i, 0)),
                  pl.BlockSpec((1, d), lambda i: (0, 0)),
                  pl.BlockSpec((1, 1, d), bmap), pl.BlockSpec((1, 1, d), bmap),
                  pl.BlockSpec((d, LANES), lambda i: (0, 0)),
                  pl.BlockSpec((1, LANES), lambda i: (0, 0))],
        out_specs=[pl.BlockSpec((tm, d // 2), lambda i: (i, 0)), row4(), row4(), row4(),
                   pl.BlockSpec((N_EXPERTS, LANES), lambda i: (0, 0))],
        scratch_shapes=[pltpu.VMEM((N_EXPERTS, LANES), F32)],
        compiler_params=_cparams(("arbitrary",)),
    )(h, g.reshape(1, d), shift, scale, wr, br)


def _dest_kernel(ps_ref, topi_ref, rank_ref, o_ref):
    topi = topi_ref[...]
    acc = rank_ref[...]
    for e in range(N_EXPERTS):
        acc = acc + jnp.where(topi == e, ps_ref[e], 0)
    o_ref[...] = acc


def _dest_rows(pad_start, topi, rank, tw):
    k, m = topi.shape
    blk = lambda: pl.BlockSpec((k, tw), lambda i, ps: (0, i))
    return pl.pallas_call(
        _dest_kernel,
        out_shape=jax.ShapeDtypeStruct((k, m), I32),
        grid_spec=pltpu.PrefetchScalarGridSpec(
            num_scalar_prefetch=1, grid=(m // tw,), in_specs=[blk(), blk()], out_specs=blk()),
        compiler_params=_cparams(("parallel",)),
    )(pad_start, topi, rank)


def _dispatch_kernel(dest_ref, up_ref, xs_ref, buf_ref, lsem, ssem):
    i = pl.program_id(0)
    n = pl.num_programs(0)
    tm = buf_ref.shape[1]
    slot = i % 2

    def load(tile, s):
        return pltpu.make_async_copy(up_ref.at[pl.ds(tile * tm, tm)], buf_ref.at[s], lsem.at[s])

    def wait_rows(s):
        for _ in range(TOP_K):
            pltpu.make_async_copy(buf_ref.at[s], xs_ref.at[pl.ds(0, tm)], ssem.at[s]).wait()

    pl.when(i == 0)(lambda: load(0, 0).start())
    pl.when(i > 0)(lambda: wait_rows(1 - slot))
    pl.when(i + 1 < n)(lambda: load(i + 1, 1 - slot).start())
    load(i, slot).wait()

    def issue(t, c):
        for k in range(TOP_K):
            pltpu.make_async_copy(buf_ref.at[slot, pl.ds(t, 1)],
                                  xs_ref.at[pl.ds(dest_ref[k * tm + t], 1)], ssem.at[slot]).start()
        return c

    lax.fori_loop(0, tm, issue, 0)
    pl.when(i == n - 1)(lambda: wait_rows(slot))


def _dispatch(dest_tiles, up, n_rows, tm):
    m, wp = up.shape
    return pl.pallas_call(
        _dispatch_kernel,
        out_shape=jax.ShapeDtypeStruct((n_rows, wp), I32),
        grid=(m // tm,),
        in_specs=[pl.BlockSpec((TOP_K * tm,), lambda i: (i,), memory_space=pltpu.SMEM),
                  pl.BlockSpec(memory_space=pl.ANY)],
        out_specs=pl.BlockSpec(memory_space=pl.ANY),
        scratch_shapes=[pltpu.VMEM((2, tm, wp), I32), pltpu.SemaphoreType.DMA((2,)),
                        pltpu.SemaphoreType.DMA((2,))],
        compiler_params=_cparams(("arbitrary",), has_side_effects=True),
    )(dest_tiles, up)


def _combine_kernel(dcur_ref, dnext_ref, gate_ref, h_ref, gf_ref, fn_ref, y_ref, o_ref, buf_ref, sem,
                    *, final):
    i = pl.program_id(0)
    tm = h_ref.shape[0]
    slot = i % 2

    def issue(d_ref, s):
        def body(t, c):
            for k in range(TOP_K):
                pltpu.make_async_copy(y_ref.at[pl.ds(d_ref[k * tm + t], 1)],
                                      buf_ref.at[s, k, pl.ds(t, 1)], sem.at[s]).start()
            return c
        lax.fori_loop(0, tm, body, 0)

    pl.when(i == 0)(lambda: issue(dcur_ref, slot))
    pl.when(i + 1 < pl.num_programs(0))(lambda: issue(dnext_ref, 1 - slot))
    pltpu.make_async_copy(buf_ref.at[slot], buf_ref.at[slot], sem.at[slot]).wait()
    acc = gate_ref[:, 0:1] * _unpack_pairs(buf_ref[slot, 0])
    for k in range(1, TOP_K):
        acc = acc + gate_ref[:, k:k + 1] * _unpack_pairs(buf_ref[slot, k])
    hn = h_ref[...] + gf_ref[0] * acc
    if final:
        hn = hn * lax.rsqrt(jnp.mean(hn * hn, axis=-1, keepdims=True) + EPS) * fn_ref[...]
    o_ref[...] = hn


def _combine(dest_tiles, gates_col, h, gf, fnorm, y, seq, tm, final):
    m, d = h.shape
    tiles_per_seq = seq // tm
    nt = m // tm
    return pl.pallas_call(
        functools.partial(_combine_kernel, final=final),
        out_shape=jax.ShapeDtypeStruct((m, d), F32),
        grid=(nt,),
        in_specs=[pl.BlockSpec((TOP_K * tm,), lambda i: (i,), memory_space=pltpu.SMEM),
                  pl.BlockSpec((TOP_K * tm,), lambda i: (jnp.minimum(i + 1, nt - 1),),
                               memory_space=pltpu.SMEM),
                  pl.BlockSpec((tm, TOP_K), lambda i: (i, 0)),
                  pl.BlockSpec((tm, d), lambda i: (i, 0)),
                  pl.BlockSpec((1, 1, d), lambda i: (i // tiles_per_seq, 0, 0)),
                  pl.BlockSpec((1, d), lambda i: (0, 0)),
                  pl.BlockSpec(memory_space=pl.ANY)],
        out_specs=pl.BlockSpec((tm, d), lambda i: (i, 0)),
        scratch_shapes=[pltpu.VMEM((2, TOP_K, tm, d // 2), I32), pltpu.SemaphoreType.DMA((2,))],
        compiler_params=_cparams(("arbitrary",)),
    )(dest_tiles, dest_tiles, gates_col, h, gf, fnorm.reshape(1, d), y)


SC_CORES = 2
SC_SUBCORES = 16
SC_ROWS = 128


def _sc_gather_rows(table, idx):
    b = idx.shape[0]
    w = table.shape[1]
    workers = SC_CORES * SC_SUBCORES
    per_w = b // workers
    assert per_w * workers == b and per_w % SC_ROWS == 0
    mesh = plsc.VectorSubcoreMesh(core_axis_name="c", subcore_axis_name="s")

    @functools.partial(
        pl.kernel, mesh=mesh, out_type=jax.ShapeDtypeStruct((b, w), I32),
        scratch_types=[pltpu.VMEM((SC_ROWS,), I32), pltpu.VMEM((SC_ROWS, w), I32),
                       pltpu.SemaphoreType.DMA])
    def gather(table_hbm, idx_hbm, out_hbm, idx_v, rows_v, sem):
        base = (lax.axis_index("s") * SC_CORES + lax.axis_index("c")) * per_w

        @pl.loop(0, per_w // SC_ROWS)
        def _(c):
            off = base + c * SC_ROWS
            pltpu.sync_copy(idx_hbm.at[pl.ds(off, SC_ROWS)], idx_v)
            pltpu.async_copy(table_hbm.at[idx_v], rows_v, sem).wait()
            pltpu.sync_copy(rows_v, out_hbm.at[pl.ds(off, SC_ROWS)])

    return gather(table, idx)


def _sc_scatter_rows(rows, dest, n_rows):
    m, w = rows.shape
    kk = dest.shape[0]
    workers = SC_CORES * SC_SUBCORES
    per_w = m // workers
    assert per_w * workers == m and per_w % SC_ROWS == 0
    mesh = plsc.VectorSubcoreMesh(core_axis_name="c", subcore_axis_name="s")

    @functools.partial(
        pl.kernel, mesh=mesh, out_type=jax.ShapeDtypeStruct((n_rows, w), I32),
        scratch_types=[pltpu.VMEM((SC_ROWS,), I32), pltpu.VMEM((SC_ROWS, w), I32),
                       pltpu.SemaphoreType.DMA])
    def scatter(rows_hbm, dest_hbm, out_hbm, idx_v, rows_v, sem):
        base = (lax.axis_index("s") * SC_CORES + lax.axis_index("c")) * per_w

        @pl.loop(0, per_w // SC_ROWS)
        def _(c):
            off = base + c * SC_ROWS
            pltpu.sync_copy(rows_hbm.at[pl.ds(off, SC_ROWS)], rows_v)
            for k in range(kk):
                pltpu.sync_copy(dest_hbm.at[pl.ds(k * m + off, SC_ROWS)], idx_v)
                pltpu.async_copy(rows_v, out_hbm.at[idx_v], sem).wait()

    return scatter(rows, dest.reshape(-1))


def _combine_dense_kernel(y_ref, gate_ref, h_ref, gf_ref, fn_ref, o_ref, *, final):
    acc = gate_ref[:, 0:1] * _unpack_pairs(y_ref[0])
    for k in range(1, TOP_K):
        acc = acc + gate_ref[:, k:k + 1] * _unpack_pairs(y_ref[k])
    hn = h_ref[...] + gf_ref[0] * acc
    if final:
        hn = hn * lax.rsqrt(jnp.mean(hn * hn, axis=-1, keepdims=True) + EPS) * fn_ref[...]
    o_ref[...] = hn


def _combine_dense(y4, gates_col, h, gf, fnorm, seq, tm, final):
    m, d = h.shape
    tiles_per_seq = seq // tm
    return pl.pallas_call(
        functools.partial(_combine_dense_kernel, final=final),
        out_shape=jax.ShapeDtypeStruct((m, d), F32),
        grid=(m // tm,),
        in_specs=[pl.BlockSpec((TOP_K, tm, d // 2), lambda i: (0, i, 0)),
                  pl.BlockSpec((tm, TOP_K), lambda i: (i, 0)),
                  pl.BlockSpec((tm, d), lambda i: (i, 0)),
                  pl.BlockSpec((1, 1, d), lambda i: (i // tiles_per_seq, 0, 0)),
                  pl.BlockSpec((1, d), lambda i: (0, 0))],
        out_specs=pl.BlockSpec((tm, d), lambda i: (i, 0)),
        compiler_params=_cparams(("parallel",)),
    )(y4, gates_col, h, gf, fnorm.reshape(1, d))


def _expert_kernel(be_ref, nb_ref, first_ref, x_ref, wgu_ref, bgu_ref, wd_ref, bd_ref, y_ref,
                   wgu_bf, wd_bf):
    i = pl.program_id(0)

    @pl.when(i < nb_ref[0])
    def _():
        dff = wd_bf.shape[0]

        @pl.when(first_ref[i] == 1)
        def _():
            rows = 64

            def cast(r, c):
                r0 = pl.multiple_of(r * rows, rows)
                wgu_bf[pl.ds(r0, rows), :] = wgu_ref[0, 0, pl.ds(r0, rows), :].astype(BF16)
                wd_bf[pl.ds(r0, rows), :] = wd_ref[0, 0, pl.ds(r0, rows), :].astype(BF16)
                return c

            lax.fori_loop(0, dff // rows, cast, 0)

        x = _unpack_pairs(x_ref[...]).astype(BF16)
        hb = _dot(x, wgu_bf[...]) + bgu_ref[0, 0]
        h_glu = jnp.minimum(hb[:, :dff], SWIGLU_LIMIT)
        h_lin = jnp.clip(hb[:, dff:], -SWIGLU_LIMIT, SWIGLU_LIMIT)
        act = h_glu * jax.nn.sigmoid(SWIGLU_ALPHA * h_glu) * (h_lin + 1.0)
        y_ref[...] = _pack_pairs(_dot(act.astype(BF16), wd_bf[...]) + bd_ref[0, 0])


def _experts(block_e, n_used, first, xs, wgu, bgu, wd, bd, layer):
    n_rows, wp = xs.shape
    _, ne, d, ff2 = wgu.shape
    assert d == ff2 // 2
    nblk = n_rows // EXPERT_BLOCK

    def xmap(i, be, nb, fi):
        return (jnp.minimum(i, nb[0] - 1), 0)

    emap = lambda i, be, nb, fi: (layer, be[i], 0, 0)
    grid_spec = pltpu.PrefetchScalarGridSpec(
        num_scalar_prefetch=3, grid=(nblk,),
        in_specs=[pl.BlockSpec((EXPERT_BLOCK, wp), xmap),
                  pl.BlockSpec((1, 1, d, ff2), emap), pl.BlockSpec((1, 1, 1, ff2), emap),
                  pl.BlockSpec((1, 1, ff2 // 2, d), emap), pl.BlockSpec((1, 1, 1, d), emap)],
        out_specs=pl.BlockSpec((EXPERT_BLOCK, wp), xmap),
        scratch_shapes=[pltpu.VMEM((d, ff2), BF16), pltpu.VMEM((ff2 // 2, d), BF16)])
    depth = wgu.shape[0]
    return pl.pallas_call(
        _expert_kernel,
        out_shape=jax.ShapeDtypeStruct((n_rows, wp), I32),
        grid_spec=grid_spec,
        compiler_params=_cparams(("arbitrary",)),
    )(block_e, n_used, first, xs, wgu, bgu.reshape(depth, ne, 1, ff2), wd, bd.reshape(depth, ne, 1, d))


def _moe(h, g, shift, scale, gf, fnorm, wr, br, wgu, bgu, wd, bd, layer, seq, final):
    m, d = h.shape
    tm = 256
    wr_p = jnp.zeros((d, LANES), F32).at[:, :N_EXPERTS].set(wr)
    br_p = jnp.zeros((1, LANES), F32).at[0, :N_EXPERTS].set(br)
    up, topi, gates, rank, cnt = _router(h, g, shift, scale, wr_p, br_p, seq, tm)

    counts = cnt[:, 0].astype(I32)
    padded = (counts + EXPERT_BLOCK - 1) // EXPERT_BLOCK * EXPERT_BLOCK
    pad_end = jnp.cumsum(padded)
    pad_start = pad_end - padded
    nblk = m * TOP_K // EXPERT_BLOCK + N_EXPERTS
    n_rows = nblk * EXPERT_BLOCK
    n_used = pad_end[-1:] // EXPERT_BLOCK
    blk = jnp.arange(nblk, dtype=I32)
    blk_c = jnp.minimum(blk, n_used - 1)
    block_e = jnp.minimum(jnp.sum(blk_c[:, None] * EXPERT_BLOCK >= pad_end[None, :], axis=1),
                          N_EXPERTS - 1).astype(I32)
    first = jnp.concatenate([jnp.ones((1,), I32), (block_e[1:] != block_e[:-1]).astype(I32)])

    dest = _dest_rows(pad_start, topi, rank, min(m, 8192))
    xs = _sc_scatter_rows(up, dest, n_rows)
    y = _experts(block_e, n_used.astype(I32), first, xs, wgu, bgu, wd, bd, layer)
    y4 = _sc_gather_rows(y, dest.reshape(-1)).reshape(TOP_K, m, d // 2)
    return _combine_dense(y4, gates.T, h, gf, fnorm, seq, tm, final)


def _block_diag(w, group):
    nb, b, _ = w.shape
    per = group // b
    wg = w.reshape(nb // per, per, b, b)
    dense = jnp.einsum("gnde,nm->gndme", wg, jnp.eye(per, dtype=w.dtype))
    return dense.reshape(nb // per, group, group)


def kernel(x, c, mod_w, mod_b, norm_mix, norm_ffn, ev_w_in, ev_lru_conv_w, ev_lru_conv_b, ev_lru_w_r, ev_lru_b_r, ev_lru_w_i, ev_lru_b_i, ev_lru_lambda, ev_ml_conv_w, ev_ml_conv_b, ev_ml_w_q, ev_ml_w_k, ev_ml_w_v, ev_ml_w_ig, ev_ml_b_ig, ev_ml_w_fg, ev_ml_b_fg, ev_ml_norm, ev_ml_skip, ev_w_out, od_w_in, od_conv_w, od_conv_b, od_dt_bias, od_a_log, od_d, od_norm, od_w_out, moe_router_w, moe_router_b, moe_w_gu, moe_b_gu, moe_w_down, moe_b_down, final_norm):
    bsz, seq, d = x.shape
    depth = mod_w.shape[0]
    m = bsz * seq
    mod = _modulation(c, mod_w, mod_b)
    h = x.reshape(m, d).astype(F32)
    for layer in range(depth):
        sh_m, sc_m, g_m, sh_f, sc_f, g_f = (mod[layer, i] for i in range(6))
        j = layer // 2
        if layer % 2 == 0:
            w = ev_lru_lambda.shape[1]
            w_in = ev_w_in[j].astype(BF16)
            proj = _inproj(h, norm_mix[layer], sh_m, sc_m, w_in, None, [w_in.shape[1]], seq, 512)[0]
            lru_p = dict(conv_w=ev_lru_conv_w[j], conv_b=ev_lru_conv_b[j].reshape(1, w),
                         w_r=ev_lru_w_r[j].astype(BF16), b_r=ev_lru_b_r[j].reshape(1, w),
                         w_i=ev_lru_w_i[j].astype(BF16), b_i=ev_lru_b_i[j].reshape(1, w),
                         lam=ev_lru_lambda[j].reshape(1, w))
            ya = _lru(proj, lru_p, bsz, seq, 256)
            wg = jnp.zeros((3 * w, LANES), F32)
            wg = wg.at[:, :ML_HEADS].set(ev_ml_w_ig[j]).at[:, ML_HEADS:2 * ML_HEADS].set(ev_ml_w_fg[j])
            bg = jnp.zeros((1, LANES), F32)
            bg = bg.at[0, :ML_HEADS].set(ev_ml_b_ig[j]).at[0, ML_HEADS:2 * ML_HEADS].set(ev_ml_b_fg[j])
            ml_p = dict(conv_w=ev_ml_conv_w[j], conv_b=ev_ml_conv_b[j].reshape(1, w),
                        w_q=_block_diag(ev_ml_w_q[j], LANES).astype(BF16),
                        w_k=_block_diag(ev_ml_w_k[j], LANES).astype(BF16),
                        w_v=_block_diag(ev_ml_w_v[j], LANES).astype(BF16),
                        w_g=wg.astype(BF16), b_g=bg,
                        norm=ev_ml_norm[j].reshape(1, w), skip=ev_ml_skip[j].reshape(1, w))
            yb = _mlstm(proj, ml_p, bsz, seq)
            h = _outproj([ya, yb], ev_w_out[j].astype(BF16), h, g_m, seq, 512)
        else:
            inner = od_norm.shape[1]
            heads = od_dt_bias.shape[1]
            conv_ch = od_conv_w.shape[2]
            w_in = od_w_in[j]
            wdt = jnp.zeros((d, LANES), F32).at[:, :heads].set(w_in[:, inner + conv_ch:])
            z, xbc, dt_raw = _inproj(h, norm_mix[layer], sh_m, sc_m, w_in[:, :inner + conv_ch].astype(BF16),
                                     wdt.astype(BF16), [inner, conv_ch], seq, 256)
            pad = lambda v: jnp.zeros((1, LANES), F32).at[0, :heads].set(v)
            ssd_p = dict(conv_w=od_conv_w[j], conv_b=od_conv_b[j].reshape(1, conv_ch),
                         dt_bias=pad(od_dt_bias[j]), a_log=pad(od_a_log[j]),
                         d_skip=jnp.repeat(od_d[j], SSD_HEAD_DIM).reshape(1, inner),
                         norm=od_norm[j].reshape(1, inner))
            y = _ssd(z, xbc, dt_raw, ssd_p, bsz, seq)
            h = _outproj([y], od_w_out[j].astype(BF16), h, g_m, seq, 512)
        h = _moe(h, norm_ffn[layer], sh_f, sc_f, g_f, final_norm,
                 moe_router_w[layer], moe_router_b[layer],
                 moe_w_gu, moe_b_gu, moe_w_down, moe_b_down, layer, seq, final=(layer == depth - 1))
    return h.reshape(bsz, seq, d)
```

```python
import functools

import jax
import jax.numpy as jnp
from jax import lax
from jax.experimental import pallas as pl
from jax.experimental.pallas import tpu as pltpu
from jax.experimental.pallas import tpu_sc as plsc

F32 = jnp.float32
BF16 = jnp.bfloat16
I32 = jnp.int32
HIGHEST = lax.Precision.HIGHEST

EPS = 1e-6
CONV_WIDTH = 4
LANES = 128
SUBLANES = 8
LRU_HEADS = 8
LRU_C = 8.0
ML_HEADS = 8
ML_QKV_BLOCK = 4
CHUNK = 128
SSD_HEAD_DIM = 64
SSD_GROUPS = 8
SSD_STATE = 128
N_EXPERTS = 32
TOP_K = 4
SWIGLU_ALPHA = 1.702
SWIGLU_LIMIT = 7.0
EXPERT_BLOCK = 512
SEQ_PER_STEP = 2
VMEM_LIMIT = 56 * 1024 * 1024


def _cparams(sem, **kw):
    return pltpu.CompilerParams(dimension_semantics=sem, vmem_limit_bytes=VMEM_LIMIT, **kw)


def _silu(x):
    return x * jax.nn.sigmoid(x)


def _log_sigmoid(x):
    return jnp.minimum(x, 0.0) - jnp.log1p(jnp.exp(-jnp.abs(x)))


def _softplus(x):
    return jnp.maximum(x, 0.0) + jnp.log1p(jnp.exp(-jnp.abs(x)))


def _dot(a, b, **kw):
    return jnp.dot(a, b, preferred_element_type=F32, **kw)


def _dot_nt(a, b):
    return lax.dot_general(a, b, (((1,), (1,)), ((), ())), preferred_element_type=F32)


def _pack_pairs(x):
    w = x.shape[1] // 2
    lo = lax.bitcast_convert_type(x[:, :w].astype(BF16).astype(F32), I32)
    hi = lax.bitcast_convert_type(x[:, w:].astype(BF16).astype(F32), I32)
    return lax.shift_right_logical(lo, 16) | (hi & jnp.int32(-65536))


def _unpack_pairs(p):
    lo = lax.bitcast_convert_type(lax.shift_left(p, 16), F32)
    hi = lax.bitcast_convert_type(p & jnp.int32(-65536), F32)
    return jnp.concatenate([lo, hi], axis=1)


def _norm_mod(h, g, shift, scale):
    y = h * lax.rsqrt(jnp.mean(h * h, axis=-1, keepdims=True) + EPS)
    return (y * g) * (1.0 + scale) + shift


def _causal_conv(x, tail_ref, w_ref, b_ref, sl):
    t = x.shape[0]
    tail = tail_ref[:, sl]
    row8 = lax.broadcasted_iota(I32, tail.shape, 0)
    out = b_ref[:, sl] + x * w_ref[CONV_WIDTH - 1:CONV_WIDTH, sl]
    for k in range(1, CONV_WIDTH):
        xs = pltpu.roll(x, k, axis=0)
        first = jnp.where(row8 < k, pltpu.roll(tail, k, axis=0), xs[:SUBLANES])
        xs = jnp.concatenate([first, xs[SUBLANES:]], axis=0)
        out = out + xs * w_ref[CONV_WIDTH - 1 - k:CONV_WIDTH - k, sl]
    tail_ref[:, sl] = x[t - SUBLANES:]
    return out


def _mod_kernel(c_ref, w_ref, b_ref, o_ref):
    cond = _silu(c_ref[...])
    o_ref[0, 0] = _dot(cond, w_ref[0], precision=HIGHEST) + b_ref[0, 0]


def _modulation(c, mod_w, mod_b):
    depth, d, _ = mod_w.shape
    bsz = c.shape[0]
    out = pl.pallas_call(
        _mod_kernel,
        out_shape=jax.ShapeDtypeStruct((depth, 6, bsz, d), F32),
        grid=(depth, 6),
        in_specs=[pl.BlockSpec((bsz, d), lambda l, j: (0, 0)),
                  pl.BlockSpec((1, d, d), lambda l, j: (l, 0, j)),
                  pl.BlockSpec((1, 1, 1, d), lambda l, j: (l, j, 0, 0))],
        out_specs=pl.BlockSpec((1, 1, bsz, d), lambda l, j: (l, j, 0, 0)),
        compiler_params=_cparams(("parallel", "parallel")),
    )(c.astype(F32), mod_w, mod_b.reshape(depth, 6, 1, d))
    return out.reshape(depth, 6, bsz, 1, d)


def _inproj_kernel(h_ref, g_ref, sh_ref, sc_ref, w_ref, *rest, n_chunk, with_dt):
    if with_dt:
        wdt_ref, *o_refs, odt_ref = rest
    else:
        o_refs = rest
    u = _norm_mod(h_ref[...], g_ref[...], sh_ref[0], sc_ref[0]).astype(BF16)
    off = 0
    for o_ref in o_refs:
        for n0 in range(0, o_ref.shape[1], n_chunk):
            o_ref[:, n0:n0 + n_chunk] = _dot(u, w_ref[:, off + n0:off + n0 + n_chunk]).astype(o_ref.dtype)
        off += o_ref.shape[1]
    if with_dt:
        odt_ref[...] = _dot(u, wdt_ref[...])


def _inproj(h, g, shift, scale, w, wdt, splits, seq, tm):
    m, d = h.shape
    n = w.shape[1]
    assert sum(splits) == n
    tiles_per_seq = seq // tm
    bmap = lambda i: (i // tiles_per_seq, 0, 0)
    in_specs = [pl.BlockSpec((tm, d), lambda i: (i, 0)),
                pl.BlockSpec((1, d), lambda i: (0, 0)),
                pl.BlockSpec((1, 1, d), bmap),
                pl.BlockSpec((1, 1, d), bmap),
                pl.BlockSpec((d, n), lambda i: (0, 0), pipeline_mode=pl.Buffered(1))]
    out_shape = [jax.ShapeDtypeStruct((m, s), BF16) for s in splits]
    out_specs = [pl.BlockSpec((tm, s), lambda i: (i, 0)) for s in splits]
    args = [h, g.reshape(1, d), shift, scale, w]
    if wdt is not None:
        in_specs.append(pl.BlockSpec((d, LANES), lambda i: (0, 0)))
        out_shape.append(jax.ShapeDtypeStruct((m, LANES), F32))
        out_specs.append(pl.BlockSpec((tm, LANES), lambda i: (i, 0)))
        args.append(wdt)
    return pl.pallas_call(
        functools.partial(_inproj_kernel, n_chunk=1024, with_dt=wdt is not None),
        out_shape=out_shape, grid=(m // tm,), in_specs=in_specs, out_specs=out_specs,
        compiler_params=_cparams(("parallel",)),
    )(*args)


def _lru_kernel(xa_ref, ga_ref, cw_ref, cb_ref, wr_ref, br_ref, wi_ref, bi_ref, lam_ref,
                o_ref, tail_ref, hc_ref):
    @pl.when(pl.program_id(1) == 0)
    def _():
        tail_ref[...] = jnp.zeros_like(tail_ref)
        hc_ref[...] = jnp.zeros_like(hc_ref)

    t = xa_ref.shape[0]
    row_in_group = lax.broadcasted_iota(I32, (t, LANES), 0) % SUBLANES
    for hh in range(LRU_HEADS):
        sl = slice(hh * LANES, (hh + 1) * LANES)
        xc = _causal_conv(xa_ref[:, sl].astype(F32), tail_ref, cw_ref, cb_ref, sl)
        xcb = xc.astype(BF16)
        r = jax.nn.sigmoid(_dot(xcb, wr_ref[hh]) + br_ref[:, sl])
        i = jax.nn.sigmoid(_dot(xcb, wi_ref[hh]) + bi_ref[:, sl])
        log_a = LRU_C * r * _log_sigmoid(lam_ref[:, sl])
        a = jnp.exp(log_a)
        th = jnp.tanh(log_a)
        u = jnp.sqrt(-2.0 * th / (1.0 - th)) * (i * xc)
        s = 1
        while s < SUBLANES:
            m = row_in_group >= s
            u = jnp.where(m, u + a * pltpu.roll(u, s, axis=0), u)
            a = jnp.where(m, a * pltpu.roll(a, s, axis=0), a)
            s *= 2
        carry = hc_ref[:, sl]
        groups = []
        for r0 in range(0, t, SUBLANES):
            hg = u[r0:r0 + SUBLANES] + a[r0:r0 + SUBLANES] * carry
            carry = hg[SUBLANES - 1:SUBLANES]
            groups.append(hg)
        hc_ref[:, sl] = carry
        h = jnp.concatenate(groups, axis=0)
        ga = ga_ref[:, sl].astype(F32)
        o_ref[:, sl] = (h * jax.nn.gelu(ga, approximate=True)).astype(o_ref.dtype)


def _lru(proj, p, bsz, seq, tm):
    m = proj.shape[0]
    w = LRU_HEADS * LANES
    nt = seq // tm
    vec = lambda: pl.BlockSpec((1, w), lambda b, j: (0, 0))
    return pl.pallas_call(
        _lru_kernel,
        out_shape=jax.ShapeDtypeStruct((m, w), BF16),
        grid=(bsz, nt),
        in_specs=[pl.BlockSpec((tm, w), lambda b, j: (b * nt + j, 0)),
                  pl.BlockSpec((tm, w), lambda b, j: (b * nt + j, 1)),
                  pl.BlockSpec((CONV_WIDTH, w), lambda b, j: (0, 0)), vec(),
                  pl.BlockSpec((LRU_HEADS, LANES, LANES), lambda b, j: (0, 0, 0)), vec(),
                  pl.BlockSpec((LRU_HEADS, LANES, LANES), lambda b, j: (0, 0, 0)), vec(), vec()],
        out_specs=pl.BlockSpec((tm, w), lambda b, j: (b * nt + j, 0)),
        scratch_shapes=[pltpu.VMEM((SUBLANES, w), F32), pltpu.VMEM((1, w), F32)],
        compiler_params=_cparams(("parallel", "arbitrary")),
    )(proj, proj, p["conv_w"], p["conv_b"], p["w_r"], p["b_r"], p["w_i"], p["b_i"], p["lam"])


def _mlstm_kernel(xb_ref, zb_ref, cw_ref, cb_ref, wq_ref, wk_ref, wv_ref, wg_ref, bg_ref,
                  nw_ref, sk_ref, o_ref, tail_ref, qkv_ref, xc_ref, caug_ref, m_ref):
    @pl.when(pl.program_id(1) == 0)
    def _():
        tail_ref[...] = jnp.zeros_like(tail_ref)
        caug_ref[...] = jnp.zeros_like(caug_ref)
        m_ref[...] = jnp.full(m_ref.shape, -jnp.inf, F32)

    for s in range(xb_ref.shape[0]):
        for c0 in range(0, xb_ref.shape[1], CHUNK):
            rows = pl.ds(c0, CHUNK)
            _mlstm_chunk(xb_ref.at[s, rows], zb_ref.at[s, rows], cw_ref, cb_ref, wq_ref, wk_ref, wv_ref,
                         wg_ref, bg_ref, nw_ref, sk_ref, o_ref.at[s, rows], tail_ref.at[s], qkv_ref.at[s],
                         xc_ref.at[s], caug_ref.at[s], m_ref.at[s])


def _mlstm_chunk(xb_ref, zb_ref, cw_ref, cb_ref, wq_ref, wk_ref, wv_ref, wg_ref, bg_ref,
                 nw_ref, sk_ref, o_ref, tail_ref, qkv_ref, xc_ref, caug_ref, m_ref):
    L = CHUNK
    width = ML_HEADS * LANES
    scale = LANES ** -0.5
    for hh in range(ML_HEADS):
        sl = slice(hh * LANES, (hh + 1) * LANES)
        xb = xb_ref[:, sl].astype(F32)
        xc = _silu(_causal_conv(xb, tail_ref, cw_ref, cb_ref, sl))
        xc_ref[:, sl] = xc
        xcb = xc.astype(BF16)
        qkv_ref[:, sl] = _dot(xcb, wq_ref[hh]).astype(BF16)
        qkv_ref[:, width + hh * LANES:width + (hh + 1) * LANES] = _dot(xcb, wk_ref[hh]).astype(BF16)
        qkv_ref[:, 2 * width + hh * LANES:2 * width + (hh + 1) * LANES] = (
            _dot(xb.astype(BF16), wv_ref[hh]).astype(BF16))

    gates = _dot(qkv_ref[...], wg_ref[...]) + bg_ref[...]
    rowi = lax.broadcasted_iota(I32, (L, L), 0)
    coli = lax.broadcasted_iota(I32, (L, L), 1)
    causal = rowi >= coli
    lf = jnp.where((coli >= ML_HEADS) & (coli < 2 * ML_HEADS), _log_sigmoid(gates), 0.0)
    gcum = _dot(causal.astype(F32), lf, precision=HIGHEST)
    x_col = jnp.where(coli < ML_HEADS, gates, gcum)
    x_row = x_col.T
    ones = jnp.ones((L, LANES), BF16)

    for hh in range(ML_HEADS):
        sl = slice(hh * LANES, (hh + 1) * LANES)
        q = qkv_ref[:, sl]
        k = qkv_ref[:, width + hh * LANES:width + (hh + 1) * LANES]
        v = qkv_ref[:, 2 * width + hh * LANES:2 * width + (hh + 1) * LANES]
        ic = x_col[:, hh:hh + 1]
        gc = x_col[:, ML_HEADS + hh:ML_HEADS + hh + 1]
        ir = x_row[hh:hh + 1, :]
        gr = x_row[ML_HEADS + hh:ML_HEADS + hh + 1, :]
        mp = m_ref[hh][:, 0:1]
        dmat = jnp.where(causal, gc - gr + ir, -jnp.inf)
        m_inter = mp + gc
        m_t = jnp.maximum(m_inter, jnp.max(dmat, axis=1, keepdims=True))
        w_intra = jnp.exp(dmat - m_t)
        w_inter = jnp.exp(m_inter - m_t)
        qk = (_dot_nt(q, k) * scale * w_intra).astype(BF16)
        v_aug = jnp.concatenate([v, ones], axis=1)
        caug = caug_ref[hh]
        nd = _dot(qk, v_aug) + w_inter * _dot(q, caug.astype(BF16))
        hval = nd[:, :LANES] / jnp.maximum(jnp.abs(nd[:, LANES:]), jnp.exp(-m_t))

        g_last = gc[L - 1:L, :]
        m_new = jnp.maximum(mp + g_last, jnp.max(g_last - gr + ir, axis=1, keepdims=True))
        ws = jnp.exp(g_last - gc + ic - m_new)
        wc = jnp.exp(mp + g_last - m_new)
        kw_t = (k.astype(F32) * (ws * scale)).T.astype(BF16)
        caug_ref[hh] = wc * caug + _dot(kw_t, v_aug)
        m_ref[hh] = jnp.broadcast_to(m_new, (1, LANES))

        mu = jnp.mean(hval, axis=1, keepdims=True)
        dv = hval - mu
        var = jnp.mean(dv * dv, axis=1, keepdims=True)
        hn = dv * lax.rsqrt(var + EPS) * nw_ref[:, sl]
        zb = zb_ref[:, sl].astype(F32)
        o_ref[:, sl] = ((hn + sk_ref[:, sl] * xc_ref[:, sl]) * _silu(zb)).astype(o_ref.dtype)


def _mlstm(proj, p, bsz, seq):
    m = proj.shape[0]
    w = ML_HEADS * LANES
    sps = 1
    cps = 2 if seq % (2 * CHUNK) == 0 else 1
    nt = seq // (cps * CHUNK)
    vec = lambda: pl.BlockSpec((1, w), lambda b, j: (0, 0))
    blk = lambda: pl.BlockSpec((ML_HEADS, LANES, LANES), lambda b, j: (0, 0, 0))
    tile = lambda col: pl.BlockSpec((sps, cps * CHUNK, w), lambda b, j: (b, j, col))
    proj3 = proj.reshape(bsz, seq, proj.shape[1])
    out = pl.pallas_call(
        _mlstm_kernel,
        out_shape=jax.ShapeDtypeStruct((bsz, seq, w), BF16),
        grid=(bsz // sps, nt),
        in_specs=[tile(2), tile(3),
                  pl.BlockSpec((CONV_WIDTH, w), lambda b, j: (0, 0)), vec(),
                  blk(), blk(), blk(),
                  pl.BlockSpec((3 * w, LANES), lambda b, j: (0, 0)),
                  pl.BlockSpec((1, LANES), lambda b, j: (0, 0)),
                  vec(), vec()],
        out_specs=tile(0),
        scratch_shapes=[pltpu.VMEM((sps, SUBLANES, w), F32),
                        pltpu.VMEM((sps, CHUNK, 3 * w), BF16),
                        pltpu.VMEM((sps, CHUNK, w), F32),
                        pltpu.VMEM((sps, ML_HEADS, LANES, 2 * LANES), F32),
                        pltpu.VMEM((sps, ML_HEADS, 1, LANES), F32)],
        compiler_params=_cparams(("parallel", "arbitrary")),
    )(proj3, proj3, p["conv_w"], p["conv_b"], p["w_q"], p["w_k"], p["w_v"], p["w_g"], p["b_g"],
      p["norm"], p["skip"])
    return out.reshape(m, w)


def _ssd_kernel(z_ref, xbc_ref, dt_ref, cw_ref, cb_ref, dtb_ref, alog_ref, dsk_ref, nw_ref,
                o_ref, tail_ref, act_ref, st_ref):
    @pl.when(pl.program_id(1) == 0)
    def _():
        tail_ref[...] = jnp.zeros_like(tail_ref)
        st_ref[...] = jnp.zeros_like(st_ref)

    for s in range(z_ref.shape[0]):
        _ssd_chunk(z_ref.at[s], xbc_ref.at[s], dt_ref.at[s], cw_ref, cb_ref, dtb_ref, alog_ref, dsk_ref,
                   nw_ref, o_ref.at[s], tail_ref.at[s], act_ref.at[s], st_ref.at[s])


def _ssd_chunk(z_ref, xbc_ref, dt_ref, cw_ref, cb_ref, dtb_ref, alog_ref, dsk_ref, nw_ref,
               o_ref, tail_ref, act_ref, st_ref):
    L = CHUNK
    inner = o_ref.shape[1]
    gw = inner // SSD_GROUPS
    hpg = gw // SSD_HEAD_DIM
    b_off = inner
    c_off = inner + SSD_GROUPS * SSD_STATE
    for cg in range(xbc_ref.shape[1] // LANES):
        sl = slice(cg * LANES, (cg + 1) * LANES)
        act_ref[:, sl] = _silu(_causal_conv(xbc_ref[:, sl].astype(F32), tail_ref, cw_ref, cb_ref, sl))

    rowi = lax.broadcasted_iota(I32, (L, L), 0)
    coli = lax.broadcasted_iota(I32, (L, L), 1)
    causal = rowi >= coli
    dt = _softplus(dt_ref[...] + dtb_ref[...])
    a = _dot(causal.astype(F32), dt * (-jnp.exp(alog_ref[...])), precision=HIGHEST)
    a_t = a.T
    dt_t = dt.T
    ea = jnp.exp(a)
    wsd = jnp.exp(a[L - 1:L, :] - a) * dt
    lane = lax.broadcasted_iota(I32, (L, gw), 1)

    def expand(cols, g):
        out = jnp.broadcast_to(cols[:, g * hpg + hpg - 1:g * hpg + hpg], (L, gw))
        for jj in range(hpg - 2, -1, -1):
            bc = jnp.broadcast_to(cols[:, g * hpg + jj:g * hpg + jj + 1], (L, gw))
            out = jnp.where(lane < (jj + 1) * SSD_HEAD_DIM, bc, out)
        return out

    for g in range(SSD_GROUPS):
        gsl = slice(g * gw, (g + 1) * gw)
        xg = act_ref[:, gsl]
        bg = act_ref[:, b_off + g * SSD_STATE:b_off + (g + 1) * SSD_STATE]
        cg_ = act_ref[:, c_off + g * SSD_STATE:c_off + (g + 1) * SSD_STATE].astype(BF16)
        cb = _dot_nt(cg_, bg.astype(BF16))
        ea_x = expand(ea, g)
        state = st_ref[g]
        acc = _dot(cg_, state.astype(BF16)) * ea_x
        for jj in range(hpg):
            hd = g * hpg + jj
            seg = jnp.where(causal, a[:, hd:hd + 1] - a_t[hd:hd + 1, :], -jnp.inf)
            w = (cb * jnp.exp(seg) * dt_t[hd:hd + 1, :]).astype(BF16)
            in_head = (lane >= jj * SSD_HEAD_DIM) & (lane < (jj + 1) * SSD_HEAD_DIM)
            acc = acc + _dot(w, jnp.where(in_head, xg, 0.0).astype(BF16))
        y = (acc + dsk_ref[:, gsl] * xg) * _silu(z_ref[:, gsl].astype(F32))
        y = y * lax.rsqrt(jnp.mean(y * y, axis=1, keepdims=True) + EPS) * nw_ref[:, gsl]
        o_ref[:, gsl] = y.astype(o_ref.dtype)
        xw = (xg * expand(wsd, g)).astype(BF16)
        st_ref[g] = ea_x[L - 1:L, :] * state + _dot(bg.T.astype(BF16), xw)


def _ssd(z, xbc, dt_raw, p, bsz, seq):
    m, inner = z.shape
    nt = seq // CHUNK
    conv_ch = xbc.shape[1]
    sps = SEQ_PER_STEP if bsz % SEQ_PER_STEP == 0 else 1
    vec = lambda n: pl.BlockSpec((1, n), lambda b, j: (0, 0))
    tile = lambda n: pl.BlockSpec((sps, CHUNK, n), lambda b, j: (b, j, 0))
    out = pl.pallas_call(
        _ssd_kernel,
        out_shape=jax.ShapeDtypeStruct((bsz, seq, inner), BF16),
        grid=(bsz // sps, nt),
        in_specs=[tile(inner), tile(conv_ch), tile(LANES),
                  pl.BlockSpec((CONV_WIDTH, conv_ch), lambda b, j: (0, 0)), vec(conv_ch),
                  vec(LANES), vec(LANES), vec(inner), vec(inner)],
        out_specs=tile(inner),
        scratch_shapes=[pltpu.VMEM((sps, SUBLANES, conv_ch), F32),
                        pltpu.VMEM((sps, CHUNK, conv_ch), F32),
                        pltpu.VMEM((sps, SSD_GROUPS, SSD_STATE, inner // SSD_GROUPS), F32)],
        compiler_params=_cparams(("parallel", "arbitrary")),
    )(z.reshape(bsz, seq, inner), xbc.reshape(bsz, seq, conv_ch), dt_raw.reshape(bsz, seq, LANES),
      p["conv_w"], p["conv_b"], p["dt_bias"], p["a_log"], p["d_skip"], p["norm"])
    return out.reshape(m, inner)


def _outproj_kernel(*refs, n_in):
    y_refs, w_refs = refs[:n_in], refs[n_in:2 * n_in]
    h_ref, g_ref, o_ref = refs[2 * n_in:]
    acc = _dot(y_refs[0][...], w_refs[0][...])
    for y_ref, w_ref in zip(y_refs[1:], w_refs[1:]):
        acc = acc + _dot(y_ref[...], w_ref[...])
    o_ref[...] = h_ref[...] + g_ref[0] * acc


def _outproj(ys, w, h, gate, seq, tm):
    m, d = h.shape
    tiles_per_seq = seq // tm
    in_specs, args, k0 = [], [], 0
    for y in ys:
        in_specs.append(pl.BlockSpec((tm, y.shape[1]), lambda i: (i, 0)))
        args.append(y)
    for y in ys:
        kk = y.shape[1]
        in_specs.append(pl.BlockSpec((kk, d), lambda i, kb=k0 // kk: (kb, 0)))
        args.append(w)
        k0 += kk
    in_specs += [pl.BlockSpec((tm, d), lambda i: (i, 0)),
                 pl.BlockSpec((1, 1, d), lambda i: (i // tiles_per_seq, 0, 0))]
    args += [h, gate]
    return pl.pallas_call(
        functools.partial(_outproj_kernel, n_in=len(ys)),
        out_shape=jax.ShapeDtypeStruct((m, d), F32),
        grid=(m // tm,), in_specs=in_specs,
        out_specs=pl.BlockSpec((tm, d), lambda i: (i, 0)),
        compiler_params=_cparams(("parallel",)),
    )(*args)


def _router_kernel(h_ref, g_ref, sh_ref, sc_ref, wr_ref, br_ref,
                   up_ref, topi_ref, gate_ref, rank_ref, cnt_ref, carry_ref):
    @pl.when(pl.program_id(0) == 0)
    def _():
        carry_ref[...] = jnp.zeros_like(carry_ref)

    tm = h_ref.shape[0]
    u = _norm_mod(h_ref[...], g_ref[...], sh_ref[0], sc_ref[0])
    up_ref[...] = _pack_pairs(u)
    u_hi = u.astype(BF16)
    u_lo = (u - u_hi.astype(F32)).astype(BF16)
    logits = (_dot(u_hi, wr_ref[0]) + (_dot(u_lo, wr_ref[0]) + _dot(u_hi, wr_ref[1]))
              + br_ref[...])
    lt = jnp.concatenate([logits[r0:r0 + LANES].T for r0 in range(0, tm, LANES)], axis=1)
    l = lt[:N_EXPERTS]
    e_iota = lax.broadcasted_iota(I32, (N_EXPERTS, tm), 0).astype(F32)
    vals, idxs, hots = [], [], []
    for _ in range(TOP_K):
        mx = jnp.max(l, axis=0, keepdims=True)
        idx = jnp.min(jnp.where(l == mx, e_iota, float(N_EXPERTS)), axis=0, keepdims=True)
        hot = e_iota == idx
        l = jnp.where(hot, -jnp.inf, l)
        vals.append(mx)
        idxs.append(idx)
        hots.append(hot)
    exps = [jnp.exp(v - vals[0]) for v in vals]
    den = exps[0] + exps[1] + exps[2] + exps[3]
    gate_ref[...] = jnp.concatenate([e / den for e in exps], axis=0)
    topi_ref[...] = jnp.concatenate(idxs, axis=0).astype(I32)

    sel = jnp.zeros((N_EXPERTS, tm), F32)
    for hot in hots:
        sel = jnp.where(hot, 1.0, sel)
    r_i = lax.broadcasted_iota(I32, (tm, tm), 0)
    c_i = lax.broadcasted_iota(I32, (tm, tm), 1)
    before = (r_i < c_i).astype(BF16)
    carry = carry_ref[:, 0:1]
    cum = _dot(sel.astype(BF16), before) + carry
    rank_ref[...] = jnp.concatenate(
        [jnp.sum(jnp.where(hot, cum, 0.0), axis=0, keepdims=True) for hot in hots], axis=0).astype(I32)
    total = carry + jnp.sum(sel, axis=1, keepdims=True)
    carry_ref[...] = jnp.broadcast_to(total, carry_ref.shape)
    cnt_ref[...] = jnp.broadcast_to(total, cnt_ref.shape)


def _router(h, g, shift, scale, wr, br, seq, tm):
    m, d = h.shape
    tiles_per_seq = seq // tm
    bmap = lambda i: (i // tiles_per_seq, 0, 0)
    row4 = lambda: pl.BlockSpec((TOP_K, tm), lambda i: (0, i))
    return pl.pallas_call(
        _router_kernel,
        out_shape=[jax.ShapeDtypeStruct((m, d // 2), I32),
                   jax.ShapeDtypeStruct((TOP_K, m), I32),
                   jax.ShapeDtypeStruct((TOP_K, m), F32),
                   jax.ShapeDtypeStruct((TOP_K, m), I32),
                   jax.ShapeDtypeStruct((N_EXPERTS, LANES), F32)],
        grid=(m // tm,),
        in_specs=[pl.BlockSpec((tm, d), lambda i: (i, 0)),
                  pl.BlockSpec((1, d), lambda i: (0, 0)),
                  pl.BlockSpec((1, 1, d), bmap), pl.BlockSpec((1, 1, d), bmap),
                  pl.BlockSpec((2, d, LANES), lambda i: (0, 0, 0)),
                  pl.BlockSpec((1, LANES), lambda i: (0, 0))],
        out_specs=[pl.BlockSpec((tm, d // 2), lambda i: (i, 0)), row4(), row4(), row4(),
                   pl.BlockSpec((N_EXPERTS, LANES), lambda i: (0, 0))],
        scratch_shapes=[pltpu.VMEM((N_EXPERTS, LANES), F32)],
        compiler_params=_cparams(("arbitrary",)),
    )(h, g.reshape(1, d), shift, scale, wr, br)


def _dest_kernel(ps_ref, topi_ref, rank_ref, o_ref):
    topi = topi_ref[...]
    acc = rank_ref[...]
    for e in range(N_EXPERTS):
        acc = acc + jnp.where(topi == e, ps_ref[e], 0)
    o_ref[...] = acc


def _dest_rows(pad_start, topi, rank, tw):
    k, m = topi.shape
    blk = lambda: pl.BlockSpec((k, tw), lambda i, ps: (0, i))
    return pl.pallas_call(
        _dest_kernel,
        out_shape=jax.ShapeDtypeStruct((k, m), I32),
        grid_spec=pltpu.PrefetchScalarGridSpec(
            num_scalar_prefetch=1, grid=(m // tw,), in_specs=[blk(), blk()], out_specs=blk()),
        compiler_params=_cparams(("parallel",)),
    )(pad_start, topi, rank)


def _dispatch_kernel(dest_ref, up_ref, xs_ref, buf_ref, lsem, ssem):
    i = pl.program_id(0)
    n = pl.num_programs(0)
    tm = buf_ref.shape[1]
    slot = i % 2

    def load(tile, s):
        return pltpu.make_async_copy(up_ref.at[pl.ds(tile * tm, tm)], buf_ref.at[s], lsem.at[s])

    def wait_rows(s):
        for _ in range(TOP_K):
            pltpu.make_async_copy(buf_ref.at[s], xs_ref.at[pl.ds(0, tm)], ssem.at[s]).wait()

    pl.when(i == 0)(lambda: load(0, 0).start())
    pl.when(i > 0)(lambda: wait_rows(1 - slot))
    pl.when(i + 1 < n)(lambda: load(i + 1, 1 - slot).start())
    load(i, slot).wait()

    def issue(t, c):
        for k in range(TOP_K):
            pltpu.make_async_copy(buf_ref.at[slot, pl.ds(t, 1)],
                                  xs_ref.at[pl.ds(dest_ref[k * tm + t], 1)], ssem.at[slot]).start()
        return c

    lax.fori_loop(0, tm, issue, 0)
    pl.when(i == n - 1)(lambda: wait_rows(slot))


def _dispatch(dest_tiles, up, n_rows, tm):
    m, wp = up.shape
    return pl.pallas_call(
        _dispatch_kernel,
        out_shape=jax.ShapeDtypeStruct((n_rows, wp), I32),
        grid=(m // tm,),
        in_specs=[pl.BlockSpec((TOP_K * tm,), lambda i: (i,), memory_space=pltpu.SMEM),
                  pl.BlockSpec(memory_space=pl.ANY)],
        out_specs=pl.BlockSpec(memory_space=pl.ANY),
        scratch_shapes=[pltpu.VMEM((2, tm, wp), I32), pltpu.SemaphoreType.DMA((2,)),
                        pltpu.SemaphoreType.DMA((2,))],
        compiler_params=_cparams(("arbitrary",), has_side_effects=True),
    )(dest_tiles, up)


def _combine_kernel(dcur_ref, dnext_ref, gate_ref, h_ref, gf_ref, fn_ref, y_ref, o_ref, buf_ref, sem,
                    *, final):
    i = pl.program_id(0)
    tm = h_ref.shape[0]
    slot = i % 2

    def issue(d_ref, s):
        def body(t, c):
            for k in range(TOP_K):
                pltpu.make_async_copy(y_ref.at[pl.ds(d_ref[k * tm + t], 1)],
                                      buf_ref.at[s, k, pl.ds(t, 1)], sem.at[s]).start()
            return c
        lax.fori_loop(0, tm, body, 0)

    pl.when(i == 0)(lambda: issue(dcur_ref, slot))
    pl.when(i + 1 < pl.num_programs(0))(lambda: issue(dnext_ref, 1 - slot))
    pltpu.make_async_copy(buf_ref.at[slot], buf_ref.at[slot], sem.at[slot]).wait()
    acc = gate_ref[:, 0:1] * _unpack_pairs(buf_ref[slot, 0])
    for k in range(1, TOP_K):
        acc = acc + gate_ref[:, k:k + 1] * _unpack_pairs(buf_ref[slot, k])
    hn = h_ref[...] + gf_ref[0] * acc
    if final:
        hn = hn * lax.rsqrt(jnp.mean(hn * hn, axis=-1, keepdims=True) + EPS) * fn_ref[...]
    o_ref[...] = hn


def _combine(dest_tiles, gates_col, h, gf, fnorm, y, seq, tm, final):
    m, d = h.shape
    tiles_per_seq = seq // tm
    nt = m // tm
    return pl.pallas_call(
        functools.partial(_combine_kernel, final=final),
        out_shape=jax.ShapeDtypeStruct((m, d), F32),
        grid=(nt,),
        in_specs=[pl.BlockSpec((TOP_K * tm,), lambda i: (i,), memory_space=pltpu.SMEM),
                  pl.BlockSpec((TOP_K * tm,), lambda i: (jnp.minimum(i + 1, nt - 1),),
                               memory_space=pltpu.SMEM),
                  pl.BlockSpec((tm, TOP_K), lambda i: (i, 0)),
                  pl.BlockSpec((tm, d), lambda i: (i, 0)),
                  pl.BlockSpec((1, 1, d), lambda i: (i // tiles_per_seq, 0, 0)),
                  pl.BlockSpec((1, d), lambda i: (0, 0)),
                  pl.BlockSpec(memory_space=pl.ANY)],
        out_specs=pl.BlockSpec((tm, d), lambda i: (i, 0)),
        scratch_shapes=[pltpu.VMEM((2, TOP_K, tm, d // 2), I32), pltpu.SemaphoreType.DMA((2,))],
        compiler_params=_cparams(("arbitrary",)),
    )(dest_tiles, dest_tiles, gates_col, h, gf, fnorm.reshape(1, d), y)


SC_CORES = 2
SC_SUBCORES = 16
SC_ROWS = 128


def _sc_gather_rows(table, idx):
    b = idx.shape[0]
    w = table.shape[1]
    workers = SC_CORES * SC_SUBCORES
    per_w = b // workers
    assert per_w * workers == b and per_w % SC_ROWS == 0
    mesh = plsc.VectorSubcoreMesh(core_axis_name="c", subcore_axis_name="s")

    @functools.partial(
        pl.kernel, mesh=mesh, out_type=jax.ShapeDtypeStruct((b, w), I32),
        scratch_types=[pltpu.VMEM((SC_ROWS,), I32), pltpu.VMEM((SC_ROWS, w), I32),
                       pltpu.SemaphoreType.DMA])
    def gather(table_hbm, idx_hbm, out_hbm, idx_v, rows_v, sem):
        base = (lax.axis_index("s") * SC_CORES + lax.axis_index("c")) * per_w

        @pl.loop(0, per_w // SC_ROWS)
        def _(c):
            off = base + c * SC_ROWS
            pltpu.sync_copy(idx_hbm.at[pl.ds(off, SC_ROWS)], idx_v)
            pltpu.async_copy(table_hbm.at[idx_v], rows_v, sem).wait()
            pltpu.sync_copy(rows_v, out_hbm.at[pl.ds(off, SC_ROWS)])

    return gather(table, idx)


def _sc_scatter_rows(rows, dest, n_rows):
    m, w = rows.shape
    kk = dest.shape[0]
    workers = SC_CORES * SC_SUBCORES
    per_w = m // workers
    assert per_w * workers == m and per_w % SC_ROWS == 0
    mesh = plsc.VectorSubcoreMesh(core_axis_name="c", subcore_axis_name="s")

    @functools.partial(
        pl.kernel, mesh=mesh, out_type=jax.ShapeDtypeStruct((n_rows, w), I32),
        scratch_types=[pltpu.VMEM((SC_ROWS,), I32), pltpu.VMEM((SC_ROWS, w), I32),
                       pltpu.SemaphoreType.DMA])
    def scatter(rows_hbm, dest_hbm, out_hbm, idx_v, rows_v, sem):
        base = (lax.axis_index("s") * SC_CORES + lax.axis_index("c")) * per_w

        @pl.loop(0, per_w // SC_ROWS)
        def _(c):
            off = base + c * SC_ROWS
            pltpu.sync_copy(rows_hbm.at[pl.ds(off, SC_ROWS)], rows_v)
            for k in range(kk):
                pltpu.sync_copy(dest_hbm.at[pl.ds(k * m + off, SC_ROWS)], idx_v)
                pltpu.async_copy(rows_v, out_hbm.at[idx_v], sem).wait()

    return scatter(rows, dest.reshape(-1))


def _combine_dense_kernel(y_ref, gate_ref, h_ref, gf_ref, fn_ref, o_ref, *, final):
    acc = gate_ref[:, 0:1] * _unpack_pairs(y_ref[0])
    for k in range(1, TOP_K):
        acc = acc + gate_ref[:, k:k + 1] * _unpack_pairs(y_ref[k])
    hn = h_ref[...] + gf_ref[0] * acc
    if final:
        hn = hn * lax.rsqrt(jnp.mean(hn * hn, axis=-1, keepdims=True) + EPS) * fn_ref[...]
    o_ref[...] = hn


def _combine_dense(y4, gates_col, h, gf, fnorm, seq, tm, final):
    m, d = h.shape
    tiles_per_seq = seq // tm
    return pl.pallas_call(
        functools.partial(_combine_dense_kernel, final=final),
        out_shape=jax.ShapeDtypeStruct((m, d), F32),
        grid=(m // tm,),
        in_specs=[pl.BlockSpec((TOP_K, tm, d // 2), lambda i: (0, i, 0)),
                  pl.BlockSpec((tm, TOP_K), lambda i: (i, 0)),
                  pl.BlockSpec((tm, d), lambda i: (i, 0)),
                  pl.BlockSpec((1, 1, d), lambda i: (i // tiles_per_seq, 0, 0)),
                  pl.BlockSpec((1, d), lambda i: (0, 0))],
        out_specs=pl.BlockSpec((tm, d), lambda i: (i, 0)),
        compiler_params=_cparams(("parallel",)),
    )(y4, gates_col, h, gf, fnorm.reshape(1, d))


def _expert_kernel(be_ref, nb_ref, first_ref, x_ref, wgu_ref, bgu_ref, wd_ref, bd_ref, y_ref,
                   wgu_bf, wd_bf):
    i = pl.program_id(0)

    @pl.when(i < nb_ref[0])
    def _():
        dff = wd_bf.shape[0]

        @pl.when(first_ref[i] == 1)
        def _():
            rows = 64

            def cast(r, c):
                r0 = pl.multiple_of(r * rows, rows)
                wgu_bf[pl.ds(r0, rows), :] = wgu_ref[0, 0, pl.ds(r0, rows), :].astype(BF16)
                wd_bf[pl.ds(r0, rows), :] = wd_ref[0, 0, pl.ds(r0, rows), :].astype(BF16)
                return c

            lax.fori_loop(0, dff // rows, cast, 0)

        x = _unpack_pairs(x_ref[...]).astype(BF16)
        hb = _dot(x, wgu_bf[...]) + bgu_ref[0, 0]
        h_glu = jnp.minimum(hb[:, :dff], SWIGLU_LIMIT)
        h_lin = jnp.clip(hb[:, dff:], -SWIGLU_LIMIT, SWIGLU_LIMIT)
        act = h_glu * jax.nn.sigmoid(SWIGLU_ALPHA * h_glu) * (h_lin + 1.0)
        y_ref[...] = _pack_pairs(_dot(act.astype(BF16), wd_bf[...]) + bd_ref[0, 0])


def _experts(block_e, n_used, first, xs, wgu, bgu, wd, bd, layer):
    n_rows, wp = xs.shape
    _, ne, d, ff2 = wgu.shape
    assert d == ff2 // 2
    nblk = n_rows // EXPERT_BLOCK

    def xmap(i, be, nb, fi):
        return (jnp.minimum(i, nb[0] - 1), 0)

    emap = lambda i, be, nb, fi: (layer, be[i], 0, 0)
    grid_spec = pltpu.PrefetchScalarGridSpec(
        num_scalar_prefetch=3, grid=(nblk,),
        in_specs=[pl.BlockSpec((EXPERT_BLOCK, wp), xmap),
                  pl.BlockSpec((1, 1, d, ff2), emap), pl.BlockSpec((1, 1, 1, ff2), emap),
                  pl.BlockSpec((1, 1, ff2 // 2, d), emap), pl.BlockSpec((1, 1, 1, d), emap)],
        out_specs=pl.BlockSpec((EXPERT_BLOCK, wp), xmap),
        scratch_shapes=[pltpu.VMEM((d, ff2), BF16), pltpu.VMEM((ff2 // 2, d), BF16)])
    depth = wgu.shape[0]
    return pl.pallas_call(
        _expert_kernel,
        out_shape=jax.ShapeDtypeStruct((n_rows, wp), I32),
        grid_spec=grid_spec,
        compiler_params=_cparams(("arbitrary",)),
    )(block_e, n_used, first, xs, wgu, bgu.reshape(depth, ne, 1, ff2), wd, bd.reshape(depth, ne, 1, d))


def _moe(h, g, shift, scale, gf, fnorm, wr, br, wgu, bgu, wd, bd, layer, seq, final):
    m, d = h.shape
    tm = 512
    wr_p =jnp.zeros((d, LANES), F32).at[:, :N_EXPERTS].set(wr)
    br_p = jnp.zeros((1, LANES), F32).at[0, :N_EXPERTS].set(br)
    wr_hi = wr_p.astype(BF16)
    wr_split = jnp.stack([wr_hi, (wr_p - wr_hi.astype(F32)).astype(BF16)])
    up, topi, gates, rank, cnt = _router(h, g, shift, scale, wr_split, br_p, seq, tm)

    counts = cnt[:, 0].astype(I32)
    padded = (counts + EXPERT_BLOCK - 1) // EXPERT_BLOCK * EXPERT_BLOCK
    pad_end = jnp.cumsum(padded)
    pad_start = pad_end - padded
    nblk = m * TOP_K // EXPERT_BLOCK + N_EXPERTS
    n_rows = nblk * EXPERT_BLOCK
    n_used = pad_end[-1:] // EXPERT_BLOCK
    blk = jnp.arange(nblk, dtype=I32)
    blk_c = jnp.minimum(blk, n_used - 1)
    block_e = jnp.minimum(jnp.sum(blk_c[:, None] * EXPERT_BLOCK >= pad_end[None, :], axis=1),
                          N_EXPERTS - 1).astype(I32)
    first = jnp.concatenate([jnp.ones((1,), I32), (block_e[1:] != block_e[:-1]).astype(I32)])

    dest = _dest_rows(pad_start, topi, rank, min(m, 8192))
    xs = _sc_scatter_rows(up, dest, n_rows)
    y = _experts(block_e, n_used.astype(I32), first, xs, wgu, bgu, wd, bd, layer)
    y4 = _sc_gather_rows(y, dest.reshape(-1)).reshape(TOP_K, m, d // 2)
    return _combine_dense(y4, gates.T, h, gf, fnorm, seq, tm, final)


def _block_diag(w, group):
    nb, b, _ = w.shape
    per = group // b
    wg = w.reshape(nb // per, per, b, b)
    dense = jnp.einsum("gnde,nm->gndme", wg, jnp.eye(per, dtype=w.dtype))
    return dense.reshape(nb // per, group, group)


def kernel(x, c, mod_w, mod_b, norm_mix, norm_ffn, ev_w_in, ev_lru_conv_w, ev_lru_conv_b, ev_lru_w_r, ev_lru_b_r, ev_lru_w_i, ev_lru_b_i, ev_lru_lambda, ev_ml_conv_w, ev_ml_conv_b, ev_ml_w_q, ev_ml_w_k, ev_ml_w_v, ev_ml_w_ig, ev_ml_b_ig, ev_ml_w_fg, ev_ml_b_fg, ev_ml_norm, ev_ml_skip, ev_w_out, od_w_in, od_conv_w, od_conv_b, od_dt_bias, od_a_log, od_d, od_norm, od_w_out, moe_router_w, moe_router_b, moe_w_gu, moe_b_gu, moe_w_down, moe_b_down, final_norm):
    bsz, seq, d = x.shape
    depth = mod_w.shape[0]
    m = bsz * seq
    mod = _modulation(c, mod_w, mod_b)
    h = x.reshape(m, d).astype(F32)
    for layer in range(depth):
        sh_m, sc_m, g_m, sh_f, sc_f, g_f = (mod[layer, i] for i in range(6))
        j = layer // 2
        if layer % 2 == 0:
            w = ev_lru_lambda.shape[1]
            w_in = ev_w_in[j].astype(BF16)
            proj = _inproj(h, norm_mix[layer], sh_m, sc_m, w_in, None, [w_in.shape[1]], seq, 512)[0]
            lru_p = dict(conv_w=ev_lru_conv_w[j], conv_b=ev_lru_conv_b[j].reshape(1, w),
                         w_r=ev_lru_w_r[j].astype(BF16), b_r=ev_lru_b_r[j].reshape(1, w),
                         w_i=ev_lru_w_i[j].astype(BF16), b_i=ev_lru_b_i[j].reshape(1, w),
                         lam=ev_lru_lambda[j].reshape(1, w))
            ya = _lru(proj, lru_p, bsz, seq, 256)
            wg = jnp.zeros((3 * w, LANES), F32)
            wg = wg.at[:, :ML_HEADS].set(ev_ml_w_ig[j]).at[:, ML_HEADS:2 * ML_HEADS].set(ev_ml_w_fg[j])
            bg = jnp.zeros((1, LANES), F32)
            bg = bg.at[0, :ML_HEADS].set(ev_ml_b_ig[j]).at[0, ML_HEADS:2 * ML_HEADS].set(ev_ml_b_fg[j])
            ml_p = dict(conv_w=ev_ml_conv_w[j], conv_b=ev_ml_conv_b[j].reshape(1, w),
                        w_q=_block_diag(ev_ml_w_q[j], LANES).astype(BF16),
                        w_k=_block_diag(ev_ml_w_k[j], LANES).astype(BF16),
                        w_v=_block_diag(ev_ml_w_v[j], LANES).astype(BF16),
                        w_g=wg.astype(BF16), b_g=bg,
                        norm=ev_ml_norm[j].reshape(1, w), skip=ev_ml_skip[j].reshape(1, w))
            yb = _mlstm(proj, ml_p, bsz, seq)
            h = _outproj([ya, yb], ev_w_out[j].astype(BF16), h, g_m, seq, 512)
        else:
            inner = od_norm.shape[1]
            heads = od_dt_bias.shape[1]
            conv_ch = od_conv_w.shape[2]
            w_in = od_w_in[j]
            wdt = jnp.zeros((d, LANES), F32).at[:, :heads].set(w_in[:, inner + conv_ch:])
            z, xbc, dt_raw = _inproj(h, norm_mix[layer], sh_m, sc_m, w_in[:, :inner + conv_ch].astype(BF16),
                                     wdt.astype(BF16), [inner, conv_ch], seq, 256)
            pad = lambda v: jnp.zeros((1, LANES), F32).at[0, :heads].set(v)
            ssd_p = dict(conv_w=od_conv_w[j], conv_b=od_conv_b[j].reshape(1, conv_ch),
                         dt_bias=pad(od_dt_bias[j]), a_log=pad(od_a_log[j]),
                         d_skip=jnp.repeat(od_d[j], SSD_HEAD_DIM).reshape(1, inner),
                         norm=od_norm[j].reshape(1, inner))
            y = _ssd(z, xbc, dt_raw, ssd_p, bsz, seq)
            h = _outproj([y], od_w_out[j].astype(BF16), h, g_m, seq, 512)
        h = _moe(h, norm_ffn[layer], sh_f, sc_f, g_f, final_norm,
                 moe_router_w[layer], moe_router_b[layer],
                 moe_w_gu, moe_b_gu, moe_w_down, moe_b_down, layer, seq, final=(layer == depth - 1))
    return h.reshape(bsz, seq, d)
```

```python
import functools

import jax
import jax.numpy as jnp
from jax import lax
from jax.experimental import pallas as pl
from jax.experimental.pallas import tpu as pltpu
from jax.experimental.pallas import tpu_sc as plsc

F32 = jnp.float32
BF16 = jnp.bfloat16
I32 = jnp.int32
HIGHEST = lax.Precision.HIGHEST

EPS = 1e-6
CONV_WIDTH = 4
LANES = 128
SUBLANES = 8
LRU_HEADS = 8
LRU_C = 8.0
ML_HEADS = 8
ML_QKV_BLOCK = 4
CHUNK = 128
SSD_HEAD_DIM = 64
SSD_GROUPS = 8
SSD_STATE = 128
N_EXPERTS = 32
TOP_K = 4
SWIGLU_ALPHA = 1.702
SWIGLU_LIMIT = 7.0
EXPERT_BLOCK = 512
SEQ_PER_STEP = 2
CONV_COLS = 512
VMEM_LIMIT = 56 * 1024 * 1024


def _cparams(sem, **kw):
    return pltpu.CompilerParams(dimension_semantics=sem, vmem_limit_bytes=VMEM_LIMIT, **kw)


def _silu(x):
    half = 0.5 * x
    return half + half * jnp.tanh(half)


def _log_sigmoid(x):
    return jnp.minimum(x, 0.0) - jnp.log1p(jnp.exp(-jnp.abs(x)))


def _softplus(x):
    return jnp.maximum(x, 0.0) + jnp.log1p(jnp.exp(-jnp.abs(x)))


def _dot(a, b, **kw):
    return jnp.dot(a, b, preferred_element_type=F32, **kw)


def _dot_nt(a, b):
    return lax.dot_general(a, b, (((1,), (1,)), ((), ())), preferred_element_type=F32)


def _pack_pairs(x):
    w = x.shape[1] // 2
    lo = lax.bitcast_convert_type(x[:, :w].astype(BF16).astype(F32), I32)
    hi = lax.bitcast_convert_type(x[:, w:].astype(BF16).astype(F32), I32)
    return lax.shift_right_logical(lo, 16) | (hi & jnp.int32(-65536))


def _unpack_pairs(p):
    lo = lax.bitcast_convert_type(lax.shift_left(p, 16), F32)
    hi = lax.bitcast_convert_type(p & jnp.int32(-65536), F32)
    return jnp.concatenate([lo, hi], axis=1)


def _norm_mod(h, g, shift, scale):
    y = h * lax.rsqrt(jnp.mean(h * h, axis=-1, keepdims=True) + EPS)
    return (y * g) * (1.0 + scale) + shift


def _causal_conv(x, tail_ref, w_ref, b_ref, sl):
    t = x.shape[0]
    tail = tail_ref[:, sl]
    row8 = lax.broadcasted_iota(I32, tail.shape, 0)
    out = b_ref[:, sl] + x * w_ref[CONV_WIDTH - 1:CONV_WIDTH, sl]
    for k in range(1, CONV_WIDTH):
        xs = pltpu.roll(x, k, axis=0)
        first = jnp.where(row8 < k, pltpu.roll(tail, k, axis=0), xs[:SUBLANES])
        xs = jnp.concatenate([first, xs[SUBLANES:]], axis=0)
        out = out + xs * w_ref[CONV_WIDTH - 1 - k:CONV_WIDTH - k, sl]
    tail_ref[:, sl] = x[t - SUBLANES:]
    return out


def _shift_matrix(t):
    r = lax.broadcasted_iota(I32, ((CONV_WIDTH - 1) * t, t), 0)
    c = lax.broadcasted_iota(I32, ((CONV_WIDTH - 1) * t, t), 1)
    src = (r & (t - 1)) - lax.shift_right_logical(r, t.bit_length() - 1) - 1
    return (src == c).astype(BF16)


def _causal_conv_shifted(x, shifted, tail_ref, w_ref, b_ref, sl):
    t = x.shape[0]
    tail = tail_ref[:, sl]
    row8 = lax.broadcasted_iota(I32, tail.shape, 0)
    out = b_ref[:, sl] + x * w_ref[CONV_WIDTH - 1:CONV_WIDTH, sl]
    head = jnp.zeros_like(tail)
    for k in range(1, CONV_WIDTH):
        wk = w_ref[CONV_WIDTH - 1 - k:CONV_WIDTH - k, sl]
        out = out + shifted[(k - 1) * t:k * t] * wk
        head = head + jnp.where(row8 < k, pltpu.roll(tail, k, axis=0), 0.0) * wk
    tail_ref[:, sl] = x[t - SUBLANES:]
    return jnp.concatenate([out[:SUBLANES] + head, out[SUBLANES:]], axis=0)


def _mod_kernel(c_ref, w_ref, b_ref, o_ref):
    cond = _silu(c_ref[...])
    o_ref[0, 0] = _dot(cond, w_ref[0], precision=HIGHEST) + b_ref[0, 0]


def _modulation(c, mod_w, mod_b):
    depth, d, _ = mod_w.shape
    bsz = c.shape[0]
    out = pl.pallas_call(
        _mod_kernel,
        out_shape=jax.ShapeDtypeStruct((depth, 6, bsz, d), F32),
        grid=(depth, 6),
        in_specs=[pl.BlockSpec((bsz, d), lambda l, j: (0, 0)),
                  pl.BlockSpec((1, d, d), lambda l, j: (l, 0, j)),
                  pl.BlockSpec((1, 1, 1, d), lambda l, j: (l, j, 0, 0))],
        out_specs=pl.BlockSpec((1, 1, bsz, d), lambda l, j: (l, j, 0, 0)),
        compiler_params=_cparams(("parallel", "parallel")),
    )(c.astype(F32), mod_w, mod_b.reshape(depth, 6, 1, d))
    return out.reshape(depth, 6, bsz, 1, d)


def _inproj_kernel(h_ref, g_ref, sh_ref, sc_ref, w_ref, *rest, n_chunk, with_dt):
    if with_dt:
        wdt_ref, *o_refs, odt_ref = rest
    else:
        o_refs = rest
    u = _norm_mod(h_ref[...], g_ref[...], sh_ref[0], sc_ref[0]).astype(BF16)
    off = 0
    for o_ref in o_refs:
        for n0 in range(0, o_ref.shape[1], n_chunk):
            o_ref[:, n0:n0 + n_chunk] = _dot(u, w_ref[:, off + n0:off + n0 + n_chunk]).astype(o_ref.dtype)
        off += o_ref.shape[1]
    if with_dt:
        odt_ref[...] = _dot(u, wdt_ref[...])


def _inproj(h, g, shift, scale, w, wdt, splits, seq, tm):
    m, d = h.shape
    n = w.shape[1]
    assert sum(splits) == n
    tiles_per_seq = seq // tm
    bmap = lambda i: (i // tiles_per_seq, 0, 0)
    in_specs = [pl.BlockSpec((tm, d), lambda i: (i, 0)),
                pl.BlockSpec((1, d), lambda i: (0, 0)),
                pl.BlockSpec((1, 1, d), bmap),
                pl.BlockSpec((1, 1, d), bmap),
                pl.BlockSpec((d, n), lambda i: (0, 0), pipeline_mode=pl.Buffered(1))]
    out_shape = [jax.ShapeDtypeStruct((m, s), BF16) for s in splits]
    out_specs = [pl.BlockSpec((tm, s), lambda i: (i, 0)) for s in splits]
    args = [h, g.reshape(1, d), shift, scale, w]
    if wdt is not None:
        in_specs.append(pl.BlockSpec((d, LANES), lambda i: (0, 0)))
        out_shape.append(jax.ShapeDtypeStruct((m, LANES), F32))
        out_specs.append(pl.BlockSpec((tm, LANES), lambda i: (i, 0)))
        args.append(wdt)
    return pl.pallas_call(
        functools.partial(_inproj_kernel, n_chunk=1024, with_dt=wdt is not None),
        out_shape=out_shape, grid=(m // tm,), in_specs=in_specs, out_specs=out_specs,
        compiler_params=_cparams(("parallel",)),
    )(*args)


def _lru_kernel(xa_ref, ga_ref, cw_ref, cb_ref, wr_ref, br_ref, wi_ref, bi_ref, lam_ref,
                o_ref, tail_ref, hc_ref):
    @pl.when(pl.program_id(1) == 0)
    def _():
        tail_ref[...] = jnp.zeros_like(tail_ref)
        hc_ref[...] = jnp.zeros_like(hc_ref)

    t = xa_ref.shape[0]
    row_in_group = lax.broadcasted_iota(I32, (t, LANES), 0) % SUBLANES
    for hh in range(LRU_HEADS):
        sl = slice(hh * LANES, (hh + 1) * LANES)
        xc = _causal_conv(xa_ref[:, sl].astype(F32), tail_ref, cw_ref, cb_ref, sl)
        xcb = xc.astype(BF16)
        r = jax.nn.sigmoid(_dot(xcb, wr_ref[hh]) + br_ref[:, sl])
        i = jax.nn.sigmoid(_dot(xcb, wi_ref[hh]) + bi_ref[:, sl])
        log_a = LRU_C * r * _log_sigmoid(lam_ref[:, sl])
        a = jnp.exp(log_a)
        th = jnp.tanh(log_a)
        u = jnp.sqrt(-2.0 * th / (1.0 - th)) * (i * xc)
        s = 1
        while s < SUBLANES:
            m = row_in_group >= s
            u = jnp.where(m, u + a * pltpu.roll(u, s, axis=0), u)
            a = jnp.where(m, a * pltpu.roll(a, s, axis=0), a)
            s *= 2
        carry = hc_ref[:, sl]
        groups = []
        for r0 in range(0, t, SUBLANES):
            hg = u[r0:r0 + SUBLANES] + a[r0:r0 + SUBLANES] * carry
            carry = hg[SUBLANES - 1:SUBLANES]
            groups.append(hg)
        hc_ref[:, sl] = carry
        h = jnp.concatenate(groups, axis=0)
        ga = ga_ref[:, sl].astype(F32)
        o_ref[:, sl] = (h * jax.nn.gelu(ga, approximate=True)).astype(o_ref.dtype)


def _lru(proj, p, bsz, seq, tm):
    m = proj.shape[0]
    w = LRU_HEADS * LANES
    nt = seq // tm
    vec = lambda: pl.BlockSpec((1, w), lambda b, j: (0, 0))
    return pl.pallas_call(
        _lru_kernel,
        out_shape=jax.ShapeDtypeStruct((m, w), BF16),
        grid=(bsz, nt),
        in_specs=[pl.BlockSpec((tm, w), lambda b, j: (b * nt + j, 0)),
                  pl.BlockSpec((tm, w), lambda b, j: (b * nt + j, 1)),
                  pl.BlockSpec((CONV_WIDTH, w), lambda b, j: (0, 0)), vec(),
                  pl.BlockSpec((LRU_HEADS, LANES, LANES), lambda b, j: (0, 0, 0)), vec(),
                  pl.BlockSpec((LRU_HEADS, LANES, LANES), lambda b, j: (0, 0, 0)), vec(), vec()],
        out_specs=pl.BlockSpec((tm, w), lambda b, j: (b * nt + j, 0)),
        scratch_shapes=[pltpu.VMEM((SUBLANES, w), F32), pltpu.VMEM((1, w), F32)],
        compiler_params=_cparams(("parallel", "arbitrary")),
    )(proj, proj, p["conv_w"], p["conv_b"], p["w_r"], p["b_r"], p["w_i"], p["b_i"], p["lam"])


def _mlstm_kernel(xb_ref, zb_ref, cw_ref, cb_ref, wq_ref, wk_ref, wv_ref, wg_ref, bg_ref,
                  nw_ref, sk_ref, o_ref, tail_ref, qkv_ref, xc_ref, caug_ref, m_ref):
    @pl.when(pl.program_id(1) == 0)
    def _():
        tail_ref[...] = jnp.zeros_like(tail_ref)
        caug_ref[...] = jnp.zeros_like(caug_ref)
        m_ref[...] = jnp.full(m_ref.shape, -jnp.inf, F32)

    for s in range(xb_ref.shape[0]):
        for c0 in range(0, xb_ref.shape[1], CHUNK):
            rows = pl.ds(c0, CHUNK)
            _mlstm_chunk(xb_ref.at[s, rows], zb_ref.at[s, rows], cw_ref, cb_ref, wq_ref, wk_ref, wv_ref,
                         wg_ref, bg_ref, nw_ref, sk_ref, o_ref.at[s, rows], tail_ref.at[s], qkv_ref.at[s],
                         xc_ref.at[s], caug_ref.at[s], m_ref.at[s])


def _mlstm_chunk(xb_ref, zb_ref, cw_ref, cb_ref, wq_ref, wk_ref, wv_ref, wg_ref, bg_ref,
                 nw_ref, sk_ref, o_ref, tail_ref, qkv_ref, xc_ref, caug_ref, m_ref):
    L = CHUNK
    width = ML_HEADS * LANES
    scale = LANES ** -0.5
    for hh in range(ML_HEADS):
        sl = slice(hh * LANES, (hh + 1) * LANES)
        xb = xb_ref[:, sl].astype(F32)
        xc = _silu(_causal_conv(xb, tail_ref, cw_ref, cb_ref, sl))
        xc_ref[:, sl] = xc
        xcb = xc.astype(BF16)
        qkv_ref[:, sl] = _dot(xcb, wq_ref[hh]).astype(BF16)
        qkv_ref[:, width + hh * LANES:width + (hh + 1) * LANES] = _dot(xcb, wk_ref[hh]).astype(BF16)
        qkv_ref[:, 2 * width + hh * LANES:2 * width + (hh + 1) * LANES] = (
            _dot(xb.astype(BF16), wv_ref[hh]).astype(BF16))

    gates = _dot(qkv_ref[...], wg_ref[...]) + bg_ref[...]
    rowi = lax.broadcasted_iota(I32, (L, L), 0)
    coli = lax.broadcasted_iota(I32, (L, L), 1)
    causal = rowi >= coli
    lf = jnp.where((coli >= ML_HEADS) & (coli < 2 * ML_HEADS), _log_sigmoid(gates), 0.0)
    tri = causal.astype(BF16)
    lf_hi = lf.astype(BF16)
    lf_mid = (lf - lf_hi.astype(F32)).astype(BF16)
    lf_lo = (lf - lf_hi.astype(F32) - lf_mid.astype(F32)).astype(BF16)
    gcum = _dot(tri, lf_hi) + (_dot(tri, lf_mid) + _dot(tri, lf_lo))
    x_col = jnp.where(coli < ML_HEADS, gates, gcum)
    x_row = x_col.T
    ones = jnp.ones((L, LANES), BF16)
    heads = range(ML_HEADS)
    hsl = [slice(hh * LANES, (hh + 1) * LANES) for hh in heads]

    qs = [qkv_ref[:, hsl[hh]] for hh in heads]
    ks = [qkv_ref[:, width + hh * LANES:width + (hh + 1) * LANES] for hh in heads]
    vaugs = [jnp.concatenate([qkv_ref[:, 2 * width + hh * LANES:2 * width + (hh + 1) * LANES], ones], axis=1)
             for hh in heads]
    scores = [_dot_nt(qs[hh], ks[hh]) * scale for hh in heads]
    ics = [x_col[:, hh:hh + 1] for hh in heads]
    gcs = [x_col[:, ML_HEADS + hh:ML_HEADS + hh + 1] for hh in heads]
    irs = [x_row[hh:hh + 1, :] for hh in heads]
    grs = [x_row[ML_HEADS + hh:ML_HEADS + hh + 1, :] for hh in heads]
    mps = [m_ref[hh][:, 0:1] for hh in heads]
    dmats = [jnp.where(causal, gcs[hh] - grs[hh] + irs[hh], -jnp.inf) for hh in heads]
    m_inters = [mps[hh] + gcs[hh] for hh in heads]
    m_ts = [jnp.maximum(m_inters[hh], jnp.max(dmats[hh], axis=1, keepdims=True)) for hh in heads]
    qks = [(scores[hh] * jnp.exp(dmats[hh] - m_ts[hh])).astype(BF16) for hh in heads]
    caugs = [caug_ref[hh] for hh in heads]
    nds = [_dot(qks[hh], vaugs[hh])
           + jnp.exp(m_inters[hh] - m_ts[hh]) * _dot(qs[hh], caugs[hh].astype(BF16)) for hh in heads]

    g_lasts = [gcs[hh][L - 1:L, :] for hh in heads]
    m_news = [jnp.maximum(mps[hh] + g_lasts[hh],
                          jnp.max(g_lasts[hh] - grs[hh] + irs[hh], axis=1, keepdims=True)) for hh in heads]
    for hh in heads:
        ws = jnp.exp(g_lasts[hh] - gcs[hh] + ics[hh] - m_news[hh])
        wc = jnp.exp(mps[hh] + g_lasts[hh] - m_news[hh])
        kw_t = (ks[hh].astype(F32) * (ws * scale)).T.astype(BF16)
        caug_ref[hh] = wc * caugs[hh] + _dot(kw_t, vaugs[hh])
        m_ref[hh] = jnp.broadcast_to(m_news[hh], (1, LANES))

    hvals = [nds[hh][:, :LANES] / jnp.maximum(jnp.abs(nds[hh][:, LANES:]), jnp.exp(-m_ts[hh]))
             for hh in heads]
    mus = [jnp.mean(hvals[hh], axis=1, keepdims=True) for hh in heads]
    dvs = [hvals[hh] - mus[hh] for hh in heads]
    variances = [jnp.mean(dvs[hh] * dvs[hh], axis=1, keepdims=True) for hh in heads]
    for hh in heads:
        sl = hsl[hh]
        hn = dvs[hh] * lax.rsqrt(variances[hh] + EPS) * nw_ref[:, sl]
        zb = zb_ref[:, sl].astype(F32)
        o_ref[:, sl] = ((hn + sk_ref[:, sl] * xc_ref[:, sl]) * _silu(zb)).astype(o_ref.dtype)


def _mlstm(proj, p, bsz, seq):
    m = proj.shape[0]
    w = ML_HEADS * LANES
    sps = 1
    cps = 2 if seq % (2 * CHUNK) == 0 else 1
    nt = seq // (cps * CHUNK)
    vec = lambda: pl.BlockSpec((1, w), lambda b, j: (0, 0))
    blk = lambda: pl.BlockSpec((ML_HEADS, LANES, LANES), lambda b, j: (0, 0, 0))
    tile = lambda col: pl.BlockSpec((sps, cps * CHUNK, w), lambda b, j: (b, j, col))
    proj3 = proj.reshape(bsz, seq, proj.shape[1])
    out = pl.pallas_call(
        _mlstm_kernel,
        out_shape=jax.ShapeDtypeStruct((bsz, seq, w), BF16),
        grid=(bsz // sps, nt),
        in_specs=[tile(2), tile(3),
                  pl.BlockSpec((CONV_WIDTH, w), lambda b, j: (0, 0)), vec(),
                  blk(), blk(), blk(),
                  pl.BlockSpec((3 * w, LANES), lambda b, j: (0, 0)),
                  pl.BlockSpec((1, LANES), lambda b, j: (0, 0)),
                  vec(), vec()],
        out_specs=tile(0),
        scratch_shapes=[pltpu.VMEM((sps, SUBLANES, w), F32),
                        pltpu.VMEM((sps, CHUNK, 3 * w), BF16),
                        pltpu.VMEM((sps, CHUNK, w), F32),
                        pltpu.VMEM((sps, ML_HEADS, LANES, 2 * LANES), F32),
                        pltpu.VMEM((sps, ML_HEADS, 1, LANES), F32)],
        compiler_params=_cparams(("parallel", "arbitrary")),
    )(proj3, proj3, p["conv_w"], p["conv_b"], p["w_q"], p["w_k"], p["w_v"], p["w_g"], p["b_g"],
      p["norm"], p["skip"])
    return out.reshape(m, w)


def _ssd_kernel(z_ref, xbc_ref, dt_ref, cw_ref, cb_ref, dtb_ref, alog_ref, dsk_ref, nw_ref,
                o_ref, tail_ref, act_ref, st_ref):
    @pl.when(pl.program_id(1) == 0)
    def _():
        tail_ref[...] = jnp.zeros_like(tail_ref)
        st_ref[...] = jnp.zeros_like(st_ref)

    for s in range(z_ref.shape[0]):
        _ssd_chunk(z_ref.at[s], xbc_ref.at[s], dt_ref.at[s], cw_ref, cb_ref, dtb_ref, alog_ref, dsk_ref,
                   nw_ref, o_ref.at[s], tail_ref.at[s], act_ref.at[s], st_ref.at[s])


def _ssd_chunk(z_ref, xbc_ref, dt_ref, cw_ref, cb_ref, dtb_ref, alog_ref, dsk_ref, nw_ref,
               o_ref, tail_ref, act_ref, st_ref):
    L = CHUNK
    inner = o_ref.shape[1]
    gw = inner // SSD_GROUPS
    hpg = gw // SSD_HEAD_DIM
    b_off = inner
    c_off = inner + SSD_GROUPS * SSD_STATE
    shift = _shift_matrix(L)
    for c0 in range(0, xbc_ref.shape[1], CONV_COLS):
        shifted = _dot(shift, xbc_ref[:, c0:c0 + CONV_COLS])
        for l0 in range(0, CONV_COLS, LANES):
            sl = slice(c0 + l0, c0 + l0 + LANES)
            act_ref[:, sl] = _silu(_causal_conv_shifted(
                xbc_ref[:, sl].astype(F32), shifted[:, l0:l0 + LANES], tail_ref, cw_ref, cb_ref, sl))

    rowi = lax.broadcasted_iota(I32, (L, L), 0)
    coli = lax.broadcasted_iota(I32, (L, L), 1)
    causal = rowi >= coli
    dt = _softplus(dt_ref[...] + dtb_ref[...])
    da = dt * (-jnp.exp(alog_ref[...]))
    tri = causal.astype(BF16)
    da_hi = da.astype(BF16)
    da_mid = (da - da_hi.astype(F32)).astype(BF16)
    da_lo = (da - da_hi.astype(F32) - da_mid.astype(F32)).astype(BF16)
    a = _dot(tri, da_hi) + (_dot(tri, da_mid) + _dot(tri, da_lo))
    a_t = a.T
    dt_t = dt.T
    ea = jnp.exp(a)
    wsd = jnp.exp(a[L - 1:L, :] - a) * dt
    lane = lax.broadcasted_iota(I32, (L, gw), 1)

    def expand(cols, g):
        out = jnp.broadcast_to(cols[:, g * hpg + hpg - 1:g * hpg + hpg], (L, gw))
        for jj in range(hpg - 2, -1, -1):
            bc = jnp.broadcast_to(cols[:, g * hpg + jj:g * hpg + jj + 1], (L, gw))
            out = jnp.where(lane < (jj + 1) * SSD_HEAD_DIM, bc, out)
        return out

    for g in range(SSD_GROUPS):
        gsl = slice(g * gw, (g + 1) * gw)
        xg = act_ref[:, gsl]
        bg = act_ref[:, b_off + g * SSD_STATE:b_off + (g + 1) * SSD_STATE]
        cg_ = act_ref[:, c_off + g * SSD_STATE:c_off + (g + 1) * SSD_STATE].astype(BF16)
        cb = _dot_nt(cg_, bg.astype(BF16))
        ea_x = expand(ea, g)
        state = st_ref[g]
        acc = _dot(cg_, state.astype(BF16)) * ea_x
        for jj in range(hpg):
            hd = g * hpg + jj
            seg = jnp.where(causal, a[:, hd:hd + 1] - a_t[hd:hd + 1, :], -jnp.inf)
            w = (cb * jnp.exp(seg) * dt_t[hd:hd + 1, :]).astype(BF16)
            in_head = (lane >= jj * SSD_HEAD_DIM) & (lane < (jj + 1) * SSD_HEAD_DIM)
            acc = acc + _dot(w, jnp.where(in_head, xg, 0.0).astype(BF16))
        y = (acc + dsk_ref[:, gsl] * xg) * _silu(z_ref[:, gsl].astype(F32))
        y = y * lax.rsqrt(jnp.mean(y * y, axis=1, keepdims=True) + EPS) * nw_ref[:, gsl]
        o_ref[:, gsl] = y.astype(o_ref.dtype)
        xw = (xg * expand(wsd, g)).astype(BF16)
        st_ref[g] = ea_x[L - 1:L, :] * state + _dot(bg.T.astype(BF16), xw)


def _ssd(z, xbc, dt_raw, p, bsz, seq):
    m, inner = z.shape
    nt = seq // CHUNK
    conv_ch = xbc.shape[1]
    sps = SEQ_PER_STEP if bsz % SEQ_PER_STEP == 0 else 1
    vec = lambda n: pl.BlockSpec((1, n), lambda b, j: (0, 0))
    tile = lambda n: pl.BlockSpec((sps, CHUNK, n), lambda b, j: (b, j, 0))
    out = pl.pallas_call(
        _ssd_kernel,
        out_shape=jax.ShapeDtypeStruct((bsz, seq, inner), BF16),
        grid=(bsz // sps, nt),
        in_specs=[tile(inner), tile(conv_ch), tile(LANES),
                  pl.BlockSpec((CONV_WIDTH, conv_ch), lambda b, j: (0, 0)), vec(conv_ch),
                  vec(LANES), vec(LANES), vec(inner), vec(inner)],
        out_specs=tile(inner),
        scratch_shapes=[pltpu.VMEM((sps, SUBLANES, conv_ch), F32),
                        pltpu.VMEM((sps, CHUNK, conv_ch), F32),
                        pltpu.VMEM((sps, SSD_GROUPS, SSD_STATE, inner // SSD_GROUPS), F32)],
        compiler_params=_cparams(("parallel", "arbitrary")),
    )(z.reshape(bsz, seq, inner), xbc.reshape(bsz, seq, conv_ch), dt_raw.reshape(bsz, seq, LANES),
      p["conv_w"], p["conv_b"], p["dt_bias"], p["a_log"], p["d_skip"], p["norm"])
    return out.reshape(m, inner)


def _outproj_kernel(*refs, n_in):
    y_refs, w_refs = refs[:n_in], refs[n_in:2 * n_in]
    h_ref, g_ref, o_ref = refs[2 * n_in:]
    acc = _dot(y_refs[0][...], w_refs[0][...])
    for y_ref, w_ref in zip(y_refs[1:], w_refs[1:]):
        acc = acc + _dot(y_ref[...], w_ref[...])
    o_ref[...] = h_ref[...] + g_ref[0] * acc


def _outproj(ys, w, h, gate, seq, tm):
    m, d = h.shape
    tiles_per_seq = seq // tm
    in_specs, args, k0 = [], [], 0
    for y in ys:
        in_specs.append(pl.BlockSpec((tm, y.shape[1]), lambda i: (i, 0)))
        args.append(y)
    for y in ys:
        kk = y.shape[1]
        in_specs.append(pl.BlockSpec((kk, d), lambda i, kb=k0 // kk: (kb, 0)))
        args.append(w)
        k0 += kk
    in_specs += [pl.BlockSpec((tm, d), lambda i: (i, 0)),
                 pl.BlockSpec((1, 1, d), lambda i: (i // tiles_per_seq, 0, 0))]
    args += [h, gate]
    return pl.pallas_call(
        functools.partial(_outproj_kernel, n_in=len(ys)),
        out_shape=jax.ShapeDtypeStruct((m, d), F32),
        grid=(m // tm,), in_specs=in_specs,
        out_specs=pl.BlockSpec((tm, d), lambda i: (i, 0)),
        compiler_params=_cparams(("parallel",)),
    )(*args)


def _router_kernel(h_ref, g_ref, sh_ref, sc_ref, wr_ref, br_ref,
                   up_ref, topi_ref, gate_ref, rank_ref, cnt_ref, carry_ref):
    @pl.when(pl.program_id(0) == 0)
    def _():
        carry_ref[...] = jnp.zeros_like(carry_ref)

    tm = h_ref.shape[0]
    u = _norm_mod(h_ref[...], g_ref[...], sh_ref[0], sc_ref[0])
    up_ref[...] = _pack_pairs(u)
    u_hi = u.astype(BF16)
    u_lo = (u - u_hi.astype(F32)).astype(BF16)
    logits = (_dot(u_hi, wr_ref[0]) + (_dot(u_lo, wr_ref[0]) + _dot(u_hi, wr_ref[1]))
              + br_ref[...])
    lt = jnp.concatenate([logits[r0:r0 + LANES].T for r0 in range(0, tm, LANES)], axis=1)
    l = lt[:N_EXPERTS]
    e_iota = lax.broadcasted_iota(I32, (N_EXPERTS, tm), 0).astype(F32)
    vals, idxs, hots = [], [], []
    for _ in range(TOP_K):
        mx = jnp.max(l, axis=0, keepdims=True)
        idx = jnp.min(jnp.where(l == mx, e_iota, float(N_EXPERTS)), axis=0, keepdims=True)
        hot = e_iota == idx
        l = jnp.where(hot, -jnp.inf, l)
        vals.append(mx)
        idxs.append(idx)
        hots.append(hot)
    exps = [jnp.exp(v - vals[0]) for v in vals]
    den = exps[0] + exps[1] + exps[2] + exps[3]
    gate_ref[...] = jnp.concatenate([e / den for e in exps], axis=0)
    topi_ref[...] = jnp.concatenate(idxs, axis=0).astype(I32)

    sel = jnp.zeros((N_EXPERTS, tm), F32)
    for hot in hots:
        sel = jnp.where(hot, 1.0, sel)
    r_i = lax.broadcasted_iota(I32, (tm, tm), 0)
    c_i = lax.broadcasted_iota(I32, (tm, tm), 1)
    before = (r_i < c_i).astype(BF16)
    carry = carry_ref[:, 0:1]
    cum = _dot(sel.astype(BF16), before) + carry
    rank_ref[...] = jnp.concatenate(
        [jnp.sum(jnp.where(hot, cum, 0.0), axis=0, keepdims=True) for hot in hots], axis=0).astype(I32)
    total = carry + jnp.sum(sel, axis=1, keepdims=True)
    carry_ref[...] = jnp.broadcast_to(total, carry_ref.shape)
    cnt_ref[...] = jnp.broadcast_to(total, cnt_ref.shape)


def _router(h, g, shift, scale, wr, br, seq, tm):
    m, d = h.shape
    tiles_per_seq = seq // tm
    bmap = lambda i: (i // tiles_per_seq, 0, 0)
    row4 = lambda: pl.BlockSpec((TOP_K, tm), lambda i: (0, i))
    return pl.pallas_call(
        _router_kernel,
        out_shape=[jax.ShapeDtypeStruct((m, d // 2), I32),
                   jax.ShapeDtypeStruct((TOP_K, m), I32),
                   jax.ShapeDtypeStruct((TOP_K, m), F32),
                   jax.ShapeDtypeStruct((TOP_K, m), I32),
                   jax.ShapeDtypeStruct((N_EXPERTS, LANES), F32)],
        grid=(m // tm,),
        in_specs=[pl.BlockSpec((tm, d), lambda i: (i, 0)),
                  pl.BlockSpec((1, d), lambda i: (0, 0)),
                  pl.BlockSpec((1, 1, d), bmap), pl.BlockSpec((1, 1, d), bmap),
                  pl.BlockSpec((2, d, LANES), lambda i: (0, 0, 0)),
                  pl.BlockSpec((1, LANES), lambda i: (0, 0))],
        out_specs=[pl.BlockSpec((tm, d // 2), lambda i: (i, 0)), row4(), row4(), row4(),
                   pl.BlockSpec((N_EXPERTS, LANES), lambda i: (0, 0))],
        scratch_shapes=[pltpu.VMEM((N_EXPERTS, LANES), F32)],
        compiler_params=_cparams(("arbitrary",)),
    )(h, g.reshape(1, d), shift, scale, wr, br)


def _dest_kernel(ps_ref, topi_ref, rank_ref, o_ref):
    topi = topi_ref[...]
    acc = rank_ref[...]
    for e in range(N_EXPERTS):
        acc = acc + jnp.where(topi == e, ps_ref[e], 0)
    o_ref[...] = acc


def _dest_rows(pad_start, topi, rank, tw):
    k, m = topi.shape
    blk = lambda: pl.BlockSpec((k, tw), lambda i, ps: (0, i))
    return pl.pallas_call(
        _dest_kernel,
        out_shape=jax.ShapeDtypeStruct((k, m), I32),
        grid_spec=pltpu.PrefetchScalarGridSpec(
            num_scalar_prefetch=1, grid=(m // tw,), in_specs=[blk(), blk()], out_specs=blk()),
        compiler_params=_cparams(("parallel",)),
    )(pad_start, topi, rank)


def _dispatch_kernel(dest_ref, up_ref, xs_ref, buf_ref, lsem, ssem):
    i = pl.program_id(0)
    n = pl.num_programs(0)
    tm = buf_ref.shape[1]
    slot = i % 2

    def load(tile, s):
        return pltpu.make_async_copy(up_ref.at[pl.ds(tile * tm, tm)], buf_ref.at[s], lsem.at[s])

    def wait_rows(s):
        for _ in range(TOP_K):
            pltpu.make_async_copy(buf_ref.at[s], xs_ref.at[pl.ds(0, tm)], ssem.at[s]).wait()

    pl.when(i == 0)(lambda: load(0, 0).start())
    pl.when(i > 0)(lambda: wait_rows(1 - slot))
    pl.when(i + 1 < n)(lambda: load(i + 1, 1 - slot).start())
    load(i, slot).wait()

    def issue(t, c):
        for k in range(TOP_K):
            pltpu.make_async_copy(buf_ref.at[slot, pl.ds(t, 1)],
                                  xs_ref.at[pl.ds(dest_ref[k * tm + t], 1)], ssem.at[slot]).start()
        return c

    lax.fori_loop(0, tm, issue, 0)
    pl.when(i == n - 1)(lambda: wait_rows(slot))


def _dispatch(dest_tiles, up, n_rows, tm):
    m, wp = up.shape
    return pl.pallas_call(
        _dispatch_kernel,
        out_shape=jax.ShapeDtypeStruct((n_rows, wp), I32),
        grid=(m // tm,),
        in_specs=[pl.BlockSpec((TOP_K * tm,), lambda i: (i,), memory_space=pltpu.SMEM),
                  pl.BlockSpec(memory_space=pl.ANY)],
        out_specs=pl.BlockSpec(memory_space=pl.ANY),
        scratch_shapes=[pltpu.VMEM((2, tm, wp), I32), pltpu.SemaphoreType.DMA((2,)),
                        pltpu.SemaphoreType.DMA((2,))],
        compiler_params=_cparams(("arbitrary",), has_side_effects=True),
    )(dest_tiles, up)


def _combine_kernel(dcur_ref, dnext_ref, gate_ref, h_ref, gf_ref, fn_ref, y_ref, o_ref, buf_ref, sem,
                    *, final):
    i = pl.program_id(0)
    tm = h_ref.shape[0]
    slot = i % 2

    def issue(d_ref, s):
        def body(t, c):
            for k in range(TOP_K):
                pltpu.make_async_copy(y_ref.at[pl.ds(d_ref[k * tm + t], 1)],
                                      buf_ref.at[s, k, pl.ds(t, 1)], sem.at[s]).start()
            return c
        lax.fori_loop(0, tm, body, 0)

    pl.when(i == 0)(lambda: issue(dcur_ref, slot))
    pl.when(i + 1 < pl.num_programs(0))(lambda: issue(dnext_ref, 1 - slot))
    pltpu.make_async_copy(buf_ref.at[slot], buf_ref.at[slot], sem.at[slot]).wait()
    acc = gate_ref[:, 0:1] * _unpack_pairs(buf_ref[slot, 0])
    for k in range(1, TOP_K):
        acc = acc + gate_ref[:, k:k + 1] * _unpack_pairs(buf_ref[slot, k])
    hn = h_ref[...] + gf_ref[0] * acc
    if final:
        hn = hn * lax.rsqrt(jnp.mean(hn * hn, axis=-1, keepdims=True) + EPS) * fn_ref[...]
    o_ref[...] = hn


def _combine(dest_tiles, gates_col, h, gf, fnorm, y, seq, tm, final):
    m, d = h.shape
    tiles_per_seq = seq // tm
    nt = m // tm
    return pl.pallas_call(
        functools.partial(_combine_kernel, final=final),
        out_shape=jax.ShapeDtypeStruct((m, d), F32),
        grid=(nt,),
        in_specs=[pl.BlockSpec((TOP_K * tm,), lambda i: (i,), memory_space=pltpu.SMEM),
                  pl.BlockSpec((TOP_K * tm,), lambda i: (jnp.minimum(i + 1, nt - 1),),
                               memory_space=pltpu.SMEM),
                  pl.BlockSpec((tm, TOP_K), lambda i: (i, 0)),
                  pl.BlockSpec((tm, d), lambda i: (i, 0)),
                  pl.BlockSpec((1, 1, d), lambda i: (i // tiles_per_seq, 0, 0)),
                  pl.BlockSpec((1, d), lambda i: (0, 0)),
                  pl.BlockSpec(memory_space=pl.ANY)],
        out_specs=pl.BlockSpec((tm, d), lambda i: (i, 0)),
        scratch_shapes=[pltpu.VMEM((2, TOP_K, tm, d // 2), I32), pltpu.SemaphoreType.DMA((2,))],
        compiler_params=_cparams(("arbitrary",)),
    )(dest_tiles, dest_tiles, gates_col, h, gf, fnorm.reshape(1, d), y)


SC_CORES = 2
SC_SUBCORES = 16
SC_ROWS = 128


def _sc_gather_rows(table, idx):
    b = idx.shape[0]
    w = table.shape[1]
    workers = SC_CORES * SC_SUBCORES
    per_w = b // workers
    assert per_w * workers == b and per_w % SC_ROWS == 0
    mesh = plsc.VectorSubcoreMesh(core_axis_name="c", subcore_axis_name="s")

    @functools.partial(
        pl.kernel, mesh=mesh, out_type=jax.ShapeDtypeStruct((b, w), I32),
        scratch_types=[pltpu.VMEM((SC_ROWS,), I32), pltpu.VMEM((SC_ROWS, w), I32),
                       pltpu.SemaphoreType.DMA])
    def gather(table_hbm, idx_hbm, out_hbm, idx_v, rows_v, sem):
        base = (lax.axis_index("s") * SC_CORES + lax.axis_index("c")) * per_w

        @pl.loop(0, per_w // SC_ROWS)
        def _(c):
            off = base + c * SC_ROWS
            pltpu.sync_copy(idx_hbm.at[pl.ds(off, SC_ROWS)], idx_v)
            pltpu.async_copy(table_hbm.at[idx_v], rows_v, sem).wait()
            pltpu.sync_copy(rows_v, out_hbm.at[pl.ds(off, SC_ROWS)])

    return gather(table, idx)


def _sc_scatter_rows(rows, dest, n_rows):
    m, w = rows.shape
    kk = dest.shape[0]
    workers = SC_CORES * SC_SUBCORES
    per_w = m // workers
    assert per_w * workers == m and per_w % SC_ROWS == 0
    mesh = plsc.VectorSubcoreMesh(core_axis_name="c", subcore_axis_name="s")

    @functools.partial(
        pl.kernel, mesh=mesh, out_type=jax.ShapeDtypeStruct((n_rows, w), I32),
        scratch_types=[pltpu.VMEM((SC_ROWS,), I32), pltpu.VMEM((SC_ROWS, w), I32),
                       pltpu.SemaphoreType.DMA])
    def scatter(rows_hbm, dest_hbm, out_hbm, idx_v, rows_v, sem):
        base = (lax.axis_index("s") * SC_CORES + lax.axis_index("c")) * per_w

        @pl.loop(0, per_w // SC_ROWS)
        def _(c):
            off = base + c * SC_ROWS
            pltpu.sync_copy(rows_hbm.at[pl.ds(off, SC_ROWS)], rows_v)
            for k in range(kk):
                pltpu.sync_copy(dest_hbm.at[pl.ds(k * m + off, SC_ROWS)], idx_v)
                pltpu.async_copy(rows_v, out_hbm.at[idx_v], sem).wait()

    return scatter(rows, dest.reshape(-1))


def _combine_dense_kernel(y_ref, gate_ref, h_ref, gf_ref, fn_ref, o_ref, *, final):
    acc = gate_ref[:, 0:1] * _unpack_pairs(y_ref[0])
    for k in range(1, TOP_K):
        acc = acc + gate_ref[:, k:k + 1] * _unpack_pairs(y_ref[k])
    hn = h_ref[...] + gf_ref[0] * acc
    if final:
        hn = hn * lax.rsqrt(jnp.mean(hn * hn, axis=-1, keepdims=True) + EPS) * fn_ref[...]
    o_ref[...] = hn


def _combine_dense(y4, gates_col, h, gf, fnorm, seq, tm, final):
    m, d = h.shape
    tiles_per_seq = seq // tm
    return pl.pallas_call(
        functools.partial(_combine_dense_kernel, final=final),
        out_shape=jax.ShapeDtypeStruct((m, d), F32),
        grid=(m // tm,),
        in_specs=[pl.BlockSpec((TOP_K, tm, d // 2), lambda i: (0, i, 0)),
                  pl.BlockSpec((tm, TOP_K), lambda i: (i, 0)),
                  pl.BlockSpec((tm, d), lambda i: (i, 0)),
                  pl.BlockSpec((1, 1, d), lambda i: (i // tiles_per_seq, 0, 0)),
                  pl.BlockSpec((1, d), lambda i: (0, 0))],
        out_specs=pl.BlockSpec((tm, d), lambda i: (i, 0)),
        compiler_params=_cparams(("parallel",)),
    )(y4, gates_col, h, gf, fnorm.reshape(1, d))


def _expert_kernel(be_ref, nb_ref, first_ref, x_ref, wgu_ref, bgu_ref, wd_ref, bd_ref, y_ref,
                   wgu_bf, wd_bf):
    i = pl.program_id(0)

    @pl.when(i < nb_ref[0])
    def _():
        dff = wd_bf.shape[0]

        @pl.when(first_ref[i] == 1)
        def _():
            rows = 64

            def cast(r, c):
                r0 = pl.multiple_of(r * rows, rows)
                wgu_bf[pl.ds(r0, rows), :] = wgu_ref[0, 0, pl.ds(r0, rows), :].astype(BF16)
                wd_bf[pl.ds(r0, rows), :] = wd_ref[0, 0, pl.ds(r0, rows), :].astype(BF16)
                return c

            lax.fori_loop(0, dff // rows, cast, 0)

        x = _unpack_pairs(x_ref[...]).astype(BF16)
        hb = _dot(x, wgu_bf[...]) + bgu_ref[0, 0]
        h_glu = jnp.minimum(hb[:, :dff], SWIGLU_LIMIT)
        h_lin = jnp.clip(hb[:, dff:], -SWIGLU_LIMIT, SWIGLU_LIMIT)
        act = h_glu * jax.nn.sigmoid(SWIGLU_ALPHA * h_glu) * (h_lin + 1.0)
        y_ref[...] = _pack_pairs(_dot(act.astype(BF16), wd_bf[...]) + bd_ref[0, 0])


def _experts(block_e, n_used, first, xs, wgu, bgu, wd, bd, layer):
    n_rows, wp = xs.shape
    _, ne, d, ff2 = wgu.shape
    assert d == ff2 // 2
    nblk = n_rows // EXPERT_BLOCK

    def xmap(i, be, nb, fi):
        return (jnp.minimum(i, nb[0] - 1), 0)

    emap = lambda i, be, nb, fi: (layer, be[i], 0, 0)
    grid_spec = pltpu.PrefetchScalarGridSpec(
        num_scalar_prefetch=3, grid=(nblk,),
        in_specs=[pl.BlockSpec((EXPERT_BLOCK, wp), xmap),
                  pl.BlockSpec((1, 1, d, ff2), emap), pl.BlockSpec((1, 1, 1, ff2), emap),
                  pl.BlockSpec((1, 1, ff2 // 2, d), emap), pl.BlockSpec((1, 1, 1, d), emap)],
        out_specs=pl.BlockSpec((EXPERT_BLOCK, wp), xmap),
        scratch_shapes=[pltpu.VMEM((d, ff2), BF16), pltpu.VMEM((ff2 // 2, d), BF16)])
    depth = wgu.shape[0]
    return pl.pallas_call(
        _expert_kernel,
        out_shape=jax.ShapeDtypeStruct((n_rows, wp), I32),
        grid_spec=grid_spec,
        compiler_params=_cparams(("arbitrary",)),
    )(block_e, n_used, first, xs, wgu, bgu.reshape(depth, ne, 1, ff2), wd, bd.reshape(depth, ne, 1, d))


def _moe(h, g, shift, scale, gf, fnorm, wr, br, wgu, bgu, wd, bd, layer, seq, final):
    m, d = h.shape
    tm = 512
    wr_p =jnp.zeros((d, LANES), F32).at[:, :N_EXPERTS].set(wr)
    br_p = jnp.zeros((1, LANES), F32).at[0, :N_EXPERTS].set(br)
    wr_hi = wr_p.astype(BF16)
    wr_split = jnp.stack([wr_hi, (wr_p - wr_hi.astype(F32)).astype(BF16)])
    up, topi, gates, rank, cnt = _router(h, g, shift, scale, wr_split, br_p, seq, tm)

    counts = cnt[:, 0].astype(I32)
    padded = (counts + EXPERT_BLOCK - 1) // EXPERT_BLOCK * EXPERT_BLOCK
    pad_end = jnp.cumsum(padded)
    pad_start = pad_end - padded
    nblk = m * TOP_K // EXPERT_BLOCK + N_EXPERTS
    n_rows = nblk * EXPERT_BLOCK
    n_used = pad_end[-1:] // EXPERT_BLOCK
    blk = jnp.arange(nblk, dtype=I32)
    blk_c = jnp.minimum(blk, n_used - 1)
    block_e = jnp.minimum(jnp.sum(blk_c[:, None] * EXPERT_BLOCK >= pad_end[None, :], axis=1),
                          N_EXPERTS - 1).astype(I32)
    first = jnp.concatenate([jnp.ones((1,), I32), (block_e[1:] != block_e[:-1]).astype(I32)])

    dest = _dest_rows(pad_start, topi, rank, min(m, 8192))
    xs = _sc_scatter_rows(up, dest, n_rows)
    y = _experts(block_e, n_used.astype(I32), first, xs, wgu, bgu, wd, bd, layer)
    y4 = _sc_gather_rows(y, dest.reshape(-1)).reshape(TOP_K, m, d // 2)
    return _combine_dense(y4, gates.T, h, gf, fnorm, seq, tm, final)


def _block_diag(w, group):
    nb, b, _ = w.shape
    per = group // b
    wg = w.reshape(nb // per, per, b, b)
    dense = jnp.einsum("gnde,nm->gndme", wg, jnp.eye(per, dtype=w.dtype))
    return dense.reshape(nb // per, group, group)


def kernel(x, c, mod_w, mod_b, norm_mix, norm_ffn, ev_w_in, ev_lru_conv_w, ev_lru_conv_b, ev_lru_w_r, ev_lru_b_r, ev_lru_w_i, ev_lru_b_i, ev_lru_lambda, ev_ml_conv_w, ev_ml_conv_b, ev_ml_w_q, ev_ml_w_k, ev_ml_w_v, ev_ml_w_ig, ev_ml_b_ig, ev_ml_w_fg, ev_ml_b_fg, ev_ml_norm, ev_ml_skip, ev_w_out, od_w_in, od_conv_w, od_conv_b, od_dt_bias, od_a_log, od_d, od_norm, od_w_out, moe_router_w, moe_router_b, moe_w_gu, moe_b_gu, moe_w_down, moe_b_down, final_norm):
    bsz, seq, d = x.shape
    depth = mod_w.shape[0]
    m = bsz * seq
    mod = _modulation(c, mod_w, mod_b)
    h = x.reshape(m, d).astype(F32)
    for layer in range(depth):
        sh_m, sc_m, g_m, sh_f, sc_f, g_f = (mod[layer, i] for i in range(6))
        j = layer // 2
        if layer % 2 == 0:
            w = ev_lru_lambda.shape[1]
            w_in = ev_w_in[j].astype(BF16)
            proj = _inproj(h, norm_mix[layer], sh_m, sc_m, w_in, None, [w_in.shape[1]], seq, 512)[0]
            lru_p = dict(conv_w=ev_lru_conv_w[j], conv_b=ev_lru_conv_b[j].reshape(1, w),
                         w_r=ev_lru_w_r[j].astype(BF16), b_r=ev_lru_b_r[j].reshape(1, w),
                         w_i=ev_lru_w_i[j].astype(BF16), b_i=ev_lru_b_i[j].reshape(1, w),
                         lam=ev_lru_lambda[j].reshape(1, w))
            ya = _lru(proj, lru_p, bsz, seq, 256)
            wg = jnp.zeros((3 * w, LANES), F32)
            wg = wg.at[:, :ML_HEADS].set(ev_ml_w_ig[j]).at[:, ML_HEADS:2 * ML_HEADS].set(ev_ml_w_fg[j])
            bg = jnp.zeros((1, LANES), F32)
            bg = bg.at[0, :ML_HEADS].set(ev_ml_b_ig[j]).at[0, ML_HEADS:2 * ML_HEADS].set(ev_ml_b_fg[j])
            ml_p = dict(conv_w=ev_ml_conv_w[j], conv_b=ev_ml_conv_b[j].reshape(1, w),
                        w_q=_block_diag(ev_ml_w_q[j], LANES).astype(BF16),
                        w_k=_block_diag(ev_ml_w_k[j], LANES).astype(BF16),
                        w_v=_block_diag(ev_ml_w_v[j], LANES).astype(BF16),
                        w_g=wg.astype(BF16), b_g=bg,
                        norm=ev_ml_norm[j].reshape(1, w), skip=ev_ml_skip[j].reshape(1, w))
            yb = _mlstm(proj, ml_p, bsz, seq)
            h = _outproj([ya, yb], ev_w_out[j].astype(BF16), h, g_m, seq, 512)
        else:
            inner = od_norm.shape[1]
            heads = od_dt_bias.shape[1]
            conv_ch = od_conv_w.shape[2]
            w_in = od_w_in[j]
            wdt = jnp.zeros((d, LANES), F32).at[:, :heads].set(w_in[:, inner + conv_ch:])
            z, xbc, dt_raw = _inproj(h, norm_mix[layer], sh_m, sc_m, w_in[:, :inner + conv_ch].astype(BF16),
                                     wdt.astype(BF16), [inner, conv_ch], seq, 256)
            pad = lambda v: jnp.zeros((1, LANES), F32).at[0, :heads].set(v)
            ssd_p = dict(conv_w=od_conv_w[j], conv_b=od_conv_b[j].reshape(1, conv_ch),
                         dt_bias=pad(od_dt_bias[j]), a_log=pad(od_a_log[j]),
                         d_skip=jnp.repeat(od_d[j], SSD_HEAD_DIM).reshape(1, inner),
                         norm=od_norm[j].reshape(1, inner))
            y = _ssd(z, xbc, dt_raw, ssd_p, bsz, seq)
            h = _outproj([y], od_w_out[j].astype(BF16), h, g_m, seq, 512)
        h = _moe(h, norm_ffn[layer], sh_f, sc_f, g_f, final_norm,
                 moe_router_w[layer], moe_router_b[layer],
                 moe_w_gu, moe_b_gu, moe_w_down, moe_b_down, layer, seq, final=(layer == depth - 1))
    return h.reshape(bsz, seq, d)
```

```python
import functools

import jax
import jax.numpy as jnp
from jax import lax
from jax.experimental import pallas as pl
from jax.experimental.pallas import tpu as pltpu
from jax.experimental.pallas import tpu_sc as plsc

F32 = jnp.float32
BF16 = jnp.bfloat16
I32 = jnp.int32
HIGHEST = lax.Precision.HIGHEST

EPS = 1e-6
CONV_WIDTH = 4
LANES = 128
SUBLANES = 8
LRU_HEADS = 8
LRU_C = 8.0
ML_HEADS = 8
ML_QKV_BLOCK = 4
CHUNK = 128
SSD_HEAD_DIM = 64
SSD_GROUPS = 8
SSD_STATE = 128
N_EXPERTS = 32
TOP_K = 4
SWIGLU_ALPHA = 1.702
SWIGLU_LIMIT = 7.0
EXPERT_BLOCK = 512
SEQ_PER_STEP = 2
CONV_COLS = 512
VMEM_LIMIT = 56 * 1024 * 1024


def _cparams(sem, **kw):
    return pltpu.CompilerParams(dimension_semantics=sem, vmem_limit_bytes=VMEM_LIMIT, **kw)


def _silu(x):
    half = 0.5 * x
    return half + half * jnp.tanh(half)


def _log_sigmoid(x):
    return jnp.minimum(x, 0.0) - jnp.log1p(jnp.exp(-jnp.abs(x)))


def _softplus(x):
    return jnp.maximum(x, 0.0) + jnp.log1p(jnp.exp(-jnp.abs(x)))


def _dot(a, b, **kw):
    return jnp.dot(a, b, preferred_element_type=F32, **kw)


def _dot_nt(a, b):
    return lax.dot_general(a, b, (((1,), (1,)), ((), ())), preferred_element_type=F32)


def _pack_pairs(x):
    w = x.shape[1] // 2
    lo = lax.bitcast_convert_type(x[:, :w].astype(BF16).astype(F32), I32)
    hi = lax.bitcast_convert_type(x[:, w:].astype(BF16).astype(F32), I32)
    return lax.shift_right_logical(lo, 16) | (hi & jnp.int32(-65536))


def _unpack_pairs(p):
    lo = lax.bitcast_convert_type(lax.shift_left(p, 16), F32)
    hi = lax.bitcast_convert_type(p & jnp.int32(-65536), F32)
    return jnp.concatenate([lo, hi], axis=1)


def _norm_mod(h, g, shift, scale):
    y = h * lax.rsqrt(jnp.mean(h * h, axis=-1, keepdims=True) + EPS)
    return (y * g) * (1.0 + scale) + shift


def _causal_conv(x, tail_ref, w_ref, b_ref, sl):
    t = x.shape[0]
    tail = tail_ref[:, sl]
    row8 = lax.broadcasted_iota(I32, tail.shape, 0)
    out = b_ref[:, sl] + x * w_ref[CONV_WIDTH - 1:CONV_WIDTH, sl]
    for k in range(1, CONV_WIDTH):
        xs = pltpu.roll(x, k, axis=0)
        first = jnp.where(row8 < k, pltpu.roll(tail, k, axis=0), xs[:SUBLANES])
        xs = jnp.concatenate([first, xs[SUBLANES:]], axis=0)
        out = out + xs * w_ref[CONV_WIDTH - 1 - k:CONV_WIDTH - k, sl]
    tail_ref[:, sl] = x[t - SUBLANES:]
    return out


def _shift_matrix(t):
    r = lax.broadcasted_iota(I32, ((CONV_WIDTH - 1) * t, t), 0)
    c = lax.broadcasted_iota(I32, ((CONV_WIDTH - 1) * t, t), 1)
    src = (r & (t - 1)) - lax.shift_right_logical(r, t.bit_length() - 1) - 1
    return (src == c).astype(BF16)


def _causal_conv_shifted(x, shifted, tail_ref, w_ref, b_ref, sl):
    t = x.shape[0]
    tail = tail_ref[:, sl]
    row8 = lax.broadcasted_iota(I32, tail.shape, 0)
    out = b_ref[:, sl] + x * w_ref[CONV_WIDTH - 1:CONV_WIDTH, sl]
    head = jnp.zeros_like(tail)
    for k in range(1, CONV_WIDTH):
        wk = w_ref[CONV_WIDTH - 1 - k:CONV_WIDTH - k, sl]
        out = out + shifted[(k - 1) * t:k * t] * wk
        head = head + jnp.where(row8 < k, pltpu.roll(tail, k, axis=0), 0.0) * wk
    tail_ref[:, sl] = x[t - SUBLANES:]
    return jnp.concatenate([out[:SUBLANES] + head, out[SUBLANES:]], axis=0)


def _mod_kernel(c_ref, w_ref, b_ref, o_ref):
    cond = _silu(c_ref[...])
    o_ref[0, 0] = _dot(cond, w_ref[0], precision=HIGHEST) + b_ref[0, 0]


def _modulation(c, mod_w, mod_b):
    depth, d, _ = mod_w.shape
    bsz = c.shape[0]
    out = pl.pallas_call(
        _mod_kernel,
        out_shape=jax.ShapeDtypeStruct((depth, 6, bsz, d), F32),
        grid=(depth, 6),
        in_specs=[pl.BlockSpec((bsz, d), lambda l, j: (0, 0)),
                  pl.BlockSpec((1, d, d), lambda l, j: (l, 0, j)),
                  pl.BlockSpec((1, 1, 1, d), lambda l, j: (l, j, 0, 0))],
        out_specs=pl.BlockSpec((1, 1, bsz, d), lambda l, j: (l, j, 0, 0)),
        compiler_params=_cparams(("parallel", "parallel")),
    )(c.astype(F32), mod_w, mod_b.reshape(depth, 6, 1, d))
    return out.reshape(depth, 6, bsz, 1, d)


def _inproj_kernel(h_ref, g_ref, sh_ref, sc_ref, w_ref, *rest, n_chunk, with_dt):
    if with_dt:
        wdt_ref, *o_refs, odt_ref = rest
    else:
        o_refs = rest
    u = _norm_mod(h_ref[...], g_ref[...], sh_ref[0], sc_ref[0]).astype(BF16)
    off = 0
    for o_ref in o_refs:
        for n0 in range(0, o_ref.shape[1], n_chunk):
            o_ref[:, n0:n0 + n_chunk] = _dot(u, w_ref[:, off + n0:off + n0 + n_chunk]).astype(o_ref.dtype)
        off += o_ref.shape[1]
    if with_dt:
        odt_ref[...] = _dot(u, wdt_ref[...])


def _inproj(h, g, shift, scale, w, wdt, splits, seq, tm):
    m, d = h.shape
    n = w.shape[1]
    assert sum(splits) == n
    tiles_per_seq = seq // tm
    bmap = lambda i: (i // tiles_per_seq, 0, 0)
    in_specs = [pl.BlockSpec((tm, d), lambda i: (i, 0)),
                pl.BlockSpec((1, d), lambda i: (0, 0)),
                pl.BlockSpec((1, 1, d), bmap),
                pl.BlockSpec((1, 1, d), bmap),
                pl.BlockSpec((d, n), lambda i: (0, 0), pipeline_mode=pl.Buffered(1))]
    out_shape = [jax.ShapeDtypeStruct((m, s), BF16) for s in splits]
    out_specs = [pl.BlockSpec((tm, s), lambda i: (i, 0)) for s in splits]
    args = [h, g.reshape(1, d), shift, scale, w]
    if wdt is not None:
        in_specs.append(pl.BlockSpec((d, LANES), lambda i: (0, 0)))
        out_shape.append(jax.ShapeDtypeStruct((m, LANES), F32))
        out_specs.append(pl.BlockSpec((tm, LANES), lambda i: (i, 0)))
        args.append(wdt)
    return pl.pallas_call(
        functools.partial(_inproj_kernel, n_chunk=1024, with_dt=wdt is not None),
        out_shape=out_shape, grid=(m // tm,), in_specs=in_specs, out_specs=out_specs,
        compiler_params=_cparams(("parallel",)),
    )(*args)


def _lru_kernel(xa_ref, ga_ref, cw_ref, cb_ref, wr_ref, br_ref, wi_ref, bi_ref, lam_ref,
                o_ref, tail_ref, hc_ref):
    @pl.when(pl.program_id(1) == 0)
    def _():
        tail_ref[...] = jnp.zeros_like(tail_ref)
        hc_ref[...] = jnp.zeros_like(hc_ref)

    t = xa_ref.shape[0]
    row_in_group = lax.broadcasted_iota(I32, (t, LANES), 0) % SUBLANES
    for hh in range(LRU_HEADS):
        sl = slice(hh * LANES, (hh + 1) * LANES)
        xc = _causal_conv(xa_ref[:, sl].astype(F32), tail_ref, cw_ref, cb_ref, sl)
        xcb = xc.astype(BF16)
        r = jax.nn.sigmoid(_dot(xcb, wr_ref[hh]) + br_ref[:, sl])
        i = jax.nn.sigmoid(_dot(xcb, wi_ref[hh]) + bi_ref[:, sl])
        log_a = LRU_C * r * _log_sigmoid(lam_ref[:, sl])
        a = jnp.exp(log_a)
        th = jnp.tanh(log_a)
        u = jnp.sqrt(-2.0 * th / (1.0 - th)) * (i * xc)
        s = 1
        while s < SUBLANES:
            m = row_in_group >= s
            u = jnp.where(m, u + a * pltpu.roll(u, s, axis=0), u)
            a = jnp.where(m, a * pltpu.roll(a, s, axis=0), a)
            s *= 2
        carry = hc_ref[:, sl]
        groups = []
        for r0 in range(0, t, SUBLANES):
            hg = u[r0:r0 + SUBLANES] + a[r0:r0 + SUBLANES] * carry
            carry = hg[SUBLANES - 1:SUBLANES]
            groups.append(hg)
        hc_ref[:, sl] = carry
        h = jnp.concatenate(groups, axis=0)
        ga = ga_ref[:, sl].astype(F32)
        o_ref[:, sl] = (h * jax.nn.gelu(ga, approximate=True)).astype(o_ref.dtype)


def _lru(proj, p, bsz, seq, tm):
    m = proj.shape[0]
    w = LRU_HEADS * LANES
    nt = seq // tm
    vec = lambda: pl.BlockSpec((1, w), lambda b, j: (0, 0))
    return pl.pallas_call(
        _lru_kernel,
        out_shape=jax.ShapeDtypeStruct((m, w), BF16),
        grid=(bsz, nt),
        in_specs=[pl.BlockSpec((tm, w), lambda b, j: (b * nt + j, 0)),
                  pl.BlockSpec((tm, w), lambda b, j: (b * nt + j, 1)),
                  pl.BlockSpec((CONV_WIDTH, w), lambda b, j: (0, 0)), vec(),
                  pl.BlockSpec((LRU_HEADS, LANES, LANES), lambda b, j: (0, 0, 0)), vec(),
                  pl.BlockSpec((LRU_HEADS, LANES, LANES), lambda b, j: (0, 0, 0)), vec(), vec()],
        out_specs=pl.BlockSpec((tm, w), lambda b, j: (b * nt + j, 0)),
        scratch_shapes=[pltpu.VMEM((SUBLANES, w), F32), pltpu.VMEM((1, w), F32)],
        compiler_params=_cparams(("parallel", "arbitrary")),
    )(proj, proj, p["conv_w"], p["conv_b"], p["w_r"], p["b_r"], p["w_i"], p["b_i"], p["lam"])


def _mlstm_kernel(xb_ref, zb_ref, cw_ref, cb_ref, wq_ref, wk_ref, wv_ref, wg_ref, bg_ref,
                  nw_ref, sk_ref, o_ref, tail_ref, qkv_ref, xc_ref, caug_ref, m_ref):
    @pl.when(pl.program_id(1) == 0)
    def _():
        tail_ref[...] = jnp.zeros_like(tail_ref)
        caug_ref[...] = jnp.zeros_like(caug_ref)
        m_ref[...] = jnp.full(m_ref.shape, -jnp.inf, F32)

    for s in range(xb_ref.shape[0]):
        for c0 in range(0, xb_ref.shape[1], CHUNK):
            rows = pl.ds(c0, CHUNK)
            _mlstm_chunk(xb_ref.at[s, rows], zb_ref.at[s, rows], cw_ref, cb_ref, wq_ref, wk_ref, wv_ref,
                         wg_ref, bg_ref, nw_ref, sk_ref, o_ref.at[s, rows], tail_ref.at[s], qkv_ref.at[s],
                         xc_ref.at[s], caug_ref.at[s], m_ref.at[s])


def _mlstm_chunk(xb_ref, zb_ref, cw_ref, cb_ref, wq_ref, wk_ref, wv_ref, wg_ref, bg_ref,
                 nw_ref, sk_ref, o_ref, tail_ref, qkv_ref, xc_ref, caug_ref, m_ref):
    L = CHUNK
    width = ML_HEADS * LANES
    scale = LANES ** -0.5
    for hh in range(ML_HEADS):
        sl = slice(hh * LANES, (hh + 1) * LANES)
        xb = xb_ref[:, sl].astype(F32)
        xc = _silu(_causal_conv(xb, tail_ref, cw_ref, cb_ref, sl))
        xc_ref[:, sl] = xc
        xcb = xc.astype(BF16)
        qkv_ref[:, sl] = _dot(xcb, wq_ref[hh]).astype(BF16)
        qkv_ref[:, width + hh * LANES:width + (hh + 1) * LANES] = _dot(xcb, wk_ref[hh]).astype(BF16)
        qkv_ref[:, 2 * width + hh * LANES:2 * width + (hh + 1) * LANES] = (
            _dot(xb.astype(BF16), wv_ref[hh]).astype(BF16))

    gates = _dot(qkv_ref[...], wg_ref[...]) + bg_ref[...]
    rowi = lax.broadcasted_iota(I32, (L, L), 0)
    coli = lax.broadcasted_iota(I32, (L, L), 1)
    causal = rowi >= coli
    lf = jnp.where((coli >= ML_HEADS) & (coli < 2 * ML_HEADS), _log_sigmoid(gates), 0.0)
    tri = causal.astype(BF16)
    lf_hi = lf.astype(BF16)
    lf_mid = (lf - lf_hi.astype(F32)).astype(BF16)
    lf_lo = (lf - lf_hi.astype(F32) - lf_mid.astype(F32)).astype(BF16)
    gcum = _dot(tri, lf_hi) + (_dot(tri, lf_mid) + _dot(tri, lf_lo))
    x_col = jnp.where(coli < ML_HEADS, gates, gcum)
    x_row = x_col.T
    ones = jnp.ones((L, LANES), BF16)
    heads = range(ML_HEADS)
    hsl = [slice(hh * LANES, (hh + 1) * LANES) for hh in heads]

    qs = [qkv_ref[:, hsl[hh]] for hh in heads]
    ks = [qkv_ref[:, width + hh * LANES:width + (hh + 1) * LANES] for hh in heads]
    vaugs = [jnp.concatenate([qkv_ref[:, 2 * width + hh * LANES:2 * width + (hh + 1) * LANES], ones], axis=1)
             for hh in heads]
    scores = [_dot_nt(qs[hh], ks[hh]) * scale for hh in heads]
    ics = [jnp.broadcast_to(x_col[:, hh:hh + 1], (L, LANES)) for hh in heads]
    gcs = [jnp.broadcast_to(x_col[:, ML_HEADS + hh:ML_HEADS + hh + 1], (L, LANES)) for hh in heads]
    irs = [x_row[hh:hh + 1, :] for hh in heads]
    grs = [x_row[ML_HEADS + hh:ML_HEADS + hh + 1, :] for hh in heads]
    mps = [m_ref[hh] for hh in heads]
    dmats = [jnp.where(causal, gcs[hh] - grs[hh] + irs[hh], -jnp.inf) for hh in heads]
    m_inters = [mps[hh] + gcs[hh] for hh in heads]
    m_ts = [jnp.maximum(m_inters[hh], jnp.max(dmats[hh], axis=1, keepdims=True)) for hh in heads]
    qks = [(scores[hh] * jnp.exp(dmats[hh] - m_ts[hh])).astype(BF16) for hh in heads]
    caugs = [caug_ref[hh] for hh in heads]
    w_inters = [jnp.exp(m_inters[hh] - m_ts[hh]) for hh in heads]
    nds = [_dot(qks[hh], vaugs[hh])
           + jnp.concatenate([w_inters[hh], w_inters[hh]], axis=1) * _dot(qs[hh], caugs[hh].astype(BF16))
           for hh in heads]

    g_lasts = [gcs[hh][L - 1:L, :] for hh in heads]
    m_news = [jnp.maximum(mps[hh] + g_lasts[hh],
                          jnp.max(g_lasts[hh] - grs[hh] + irs[hh], axis=1, keepdims=True)) for hh in heads]
    for hh in heads:
        ws = jnp.exp(g_lasts[hh] - gcs[hh] + ics[hh] - m_news[hh])
        wc = jnp.exp(mps[hh] + g_lasts[hh] - m_news[hh])
        kw_t = (ks[hh].astype(F32) * (ws * scale)).T.astype(BF16)
        caug_ref[hh] = jnp.concatenate([wc, wc], axis=1) * caugs[hh] + _dot(kw_t, vaugs[hh])
        m_ref[hh] = m_news[hh]

    hvals = [nds[hh][:, :LANES] / jnp.maximum(jnp.abs(nds[hh][:, LANES:]), jnp.exp(-m_ts[hh]))
             for hh in heads]
    mus = [jnp.mean(hvals[hh], axis=1, keepdims=True) for hh in heads]
    dvs = [hvals[hh] - mus[hh] for hh in heads]
    variances = [jnp.mean(dvs[hh] * dvs[hh], axis=1, keepdims=True) for hh in heads]
    for hh in heads:
        sl = hsl[hh]
        hn = dvs[hh] * lax.rsqrt(variances[hh] + EPS) * nw_ref[:, sl]
        zb = zb_ref[:, sl].astype(F32)
        o_ref[:, sl] = ((hn + sk_ref[:, sl] * xc_ref[:, sl]) * _silu(zb)).astype(o_ref.dtype)


def _mlstm(proj, p, bsz, seq):
    m = proj.shape[0]
    w = ML_HEADS * LANES
    sps = 1
    cps = 2 if seq % (2 * CHUNK) == 0 else 1
    nt = seq // (cps * CHUNK)
    vec = lambda: pl.BlockSpec((1, w), lambda b, j: (0, 0))
    blk = lambda: pl.BlockSpec((ML_HEADS, LANES, LANES), lambda b, j: (0, 0, 0))
    tile = lambda col: pl.BlockSpec((sps, cps * CHUNK, w), lambda b, j: (b, j, col))
    proj3 = proj.reshape(bsz, seq, proj.shape[1])
    out = pl.pallas_call(
        _mlstm_kernel,
        out_shape=jax.ShapeDtypeStruct((bsz, seq, w), BF16),
        grid=(bsz // sps, nt),
        in_specs=[tile(2), tile(3),
                  pl.BlockSpec((CONV_WIDTH, w), lambda b, j: (0, 0)), vec(),
                  blk(), blk(), blk(),
                  pl.BlockSpec((3 * w, LANES), lambda b, j: (0, 0)),
                  pl.BlockSpec((1, LANES), lambda b, j: (0, 0)),
                  vec(), vec()],
        out_specs=tile(0),
        scratch_shapes=[pltpu.VMEM((sps, SUBLANES, w), F32),
                        pltpu.VMEM((sps, CHUNK, 3 * w), BF16),
                        pltpu.VMEM((sps, CHUNK, w), F32),
                        pltpu.VMEM((sps, ML_HEADS, LANES, 2 * LANES), F32),
                        pltpu.VMEM((sps, ML_HEADS, 1, LANES), F32)],
        compiler_params=_cparams(("parallel", "arbitrary")),
    )(proj3, proj3, p["conv_w"], p["conv_b"], p["w_q"], p["w_k"], p["w_v"], p["w_g"], p["b_g"],
      p["norm"], p["skip"])
    return out.reshape(m, w)


def _ssd_kernel(z_ref, xbc_ref, dt_ref, cw_ref, cb_ref, dtb_ref, alog_ref, dsk_ref, nw_ref,
                o_ref, tail_ref, act_ref, st_ref):
    @pl.when(pl.program_id(1) == 0)
    def _():
        tail_ref[...] = jnp.zeros_like(tail_ref)
        st_ref[...] = jnp.zeros_like(st_ref)

    for s in range(z_ref.shape[0]):
        _ssd_chunk(z_ref.at[s], xbc_ref.at[s], dt_ref.at[s], cw_ref, cb_ref, dtb_ref, alog_ref, dsk_ref,
                   nw_ref, o_ref.at[s], tail_ref.at[s], act_ref.at[s], st_ref.at[s])


def _ssd_chunk(z_ref, xbc_ref, dt_ref, cw_ref, cb_ref, dtb_ref, alog_ref, dsk_ref, nw_ref,
               o_ref, tail_ref, act_ref, st_ref):
    L = CHUNK
    inner = o_ref.shape[1]
    gw = inner // SSD_GROUPS
    hpg = gw // SSD_HEAD_DIM
    b_off = inner
    c_off = inner + SSD_GROUPS * SSD_STATE
    shift = _shift_matrix(L)
    for c0 in range(0, xbc_ref.shape[1], CONV_COLS):
        shifted = _dot(shift, xbc_ref[:, c0:c0 + CONV_COLS])
        for l0 in range(0, CONV_COLS, LANES):
            sl = slice(c0 + l0, c0 + l0 + LANES)
            act_ref[:, sl] = _silu(_causal_conv_shifted(
                xbc_ref[:, sl].astype(F32), shifted[:, l0:l0 + LANES], tail_ref, cw_ref, cb_ref, sl))

    rowi = lax.broadcasted_iota(I32, (L, L), 0)
    coli = lax.broadcasted_iota(I32, (L, L), 1)
    causal = rowi >= coli
    dt = _softplus(dt_ref[...] + dtb_ref[...])
    da = dt * (-jnp.exp(alog_ref[...]))
    tri = causal.astype(BF16)
    da_hi = da.astype(BF16)
    da_mid = (da - da_hi.astype(F32)).astype(BF16)
    da_lo = (da - da_hi.astype(F32) - da_mid.astype(F32)).astype(BF16)
    a = _dot(tri, da_hi) + (_dot(tri, da_mid) + _dot(tri, da_lo))
    a_t = a.T
    lane = lax.broadcasted_iota(I32, (L, gw), 1)

    def over_heads(tiles):
        wide = [jnp.concatenate([t] * (gw // LANES), axis=1) for t in tiles]
        out = wide[hpg - 1]
        for jj in range(hpg - 2, -1, -1):
            out = jnp.where(lane < (jj + 1) * SSD_HEAD_DIM, wide[jj], out)
        return out

    for g in range(SSD_GROUPS):
        gsl = slice(g * gw, (g + 1) * gw)
        xg = act_ref[:, gsl]
        bg = act_ref[:, b_off + g * SSD_STATE:b_off + (g + 1) * SSD_STATE]
        cg_ = act_ref[:, c_off + g * SSD_STATE:c_off + (g + 1) * SSD_STATE].astype(BF16)
        cb = _dot_nt(cg_, bg.astype(BF16))
        state = st_ref[g]
        hds = [g * hpg + jj for jj in range(hpg)]
        a_bs = [jnp.broadcast_to(a[:, hd:hd + 1], (L, LANES)) for hd in hds]
        dt_bs = [jnp.broadcast_to(dt[:, hd:hd + 1], (L, LANES)) for hd in hds]
        ea_x = over_heads([jnp.exp(a_b) for a_b in a_bs])
        to_end_x = over_heads([jnp.exp(a_b[L - 1:L, :] - a_b) for a_b in a_bs])
        xdt = xg * over_heads(dt_bs)
        acc = _dot(cg_, state.astype(BF16)) * ea_x
        for jj in range(hpg):
            seg = jnp.where(causal, a_bs[jj] - a_t[hds[jj]:hds[jj] + 1, :], -jnp.inf)
            w = (cb * jnp.exp(seg)).astype(BF16)
            in_head = (lane >= jj * SSD_HEAD_DIM) & (lane < (jj + 1) * SSD_HEAD_DIM)
            acc = acc + _dot(w, jnp.where(in_head, xdt, 0.0).astype(BF16))
        y = (acc + dsk_ref[:, gsl] * xg) * _silu(z_ref[:, gsl].astype(F32))
        y = y * lax.rsqrt(jnp.mean(y * y, axis=1, keepdims=True) + EPS) * nw_ref[:, gsl]
        o_ref[:, gsl] = y.astype(o_ref.dtype)
        xw = (xdt * to_end_x).astype(BF16)
        st_ref[g] = ea_x[L - 1:L, :] * state + _dot(bg.T.astype(BF16), xw)


def _ssd(z, xbc, dt_raw, p, bsz, seq):
    m, inner = z.shape
    nt = seq // CHUNK
    conv_ch = xbc.shape[1]
    sps = SEQ_PER_STEP if bsz % SEQ_PER_STEP == 0 else 1
    vec = lambda n: pl.BlockSpec((1, n), lambda b, j: (0, 0))
    tile = lambda n: pl.BlockSpec((sps, CHUNK, n), lambda b, j: (b, j, 0))
    out = pl.pallas_call(
        _ssd_kernel,
        out_shape=jax.ShapeDtypeStruct((bsz, seq, inner), BF16),
        grid=(bsz // sps, nt),
        in_specs=[tile(inner), tile(conv_ch), tile(LANES),
                  pl.BlockSpec((CONV_WIDTH, conv_ch), lambda b, j: (0, 0)), vec(conv_ch),
                  vec(LANES), vec(LANES), vec(inner), vec(inner)],
        out_specs=tile(inner),
        scratch_shapes=[pltpu.VMEM((sps, SUBLANES, conv_ch), F32),
                        pltpu.VMEM((sps, CHUNK, conv_ch), F32),
                        pltpu.VMEM((sps, SSD_GROUPS, SSD_STATE, inner // SSD_GROUPS), F32)],
        compiler_params=_cparams(("parallel", "arbitrary")),
    )(z.reshape(bsz, seq, inner), xbc.reshape(bsz, seq, conv_ch), dt_raw.reshape(bsz, seq, LANES),
      p["conv_w"], p["conv_b"], p["dt_bias"], p["a_log"], p["d_skip"], p["norm"])
    return out.reshape(m, inner)


def _outproj_kernel(*refs, n_in):
    y_refs, w_refs = refs[:n_in], refs[n_in:2 * n_in]
    h_ref, g_ref, o_ref = refs[2 * n_in:]
    acc = _dot(y_refs[0][...], w_refs[0][...])
    for y_ref, w_ref in zip(y_refs[1:], w_refs[1:]):
        acc = acc + _dot(y_ref[...], w_ref[...])
    o_ref[...] = h_ref[...] + g_ref[0] * acc


def _outproj(ys, w, h, gate, seq, tm):
    m, d = h.shape
    tiles_per_seq = seq // tm
    in_specs, args, k0 = [], [], 0
    for y in ys:
        in_specs.append(pl.BlockSpec((tm, y.shape[1]), lambda i: (i, 0)))
        args.append(y)
    for y in ys:
        kk = y.shape[1]
        in_specs.append(pl.BlockSpec((kk, d), lambda i, kb=k0 // kk: (kb, 0)))
        args.append(w)
        k0 += kk
    in_specs += [pl.BlockSpec((tm, d), lambda i: (i, 0)),
                 pl.BlockSpec((1, 1, d), lambda i: (i // tiles_per_seq, 0, 0))]
    args += [h, gate]
    return pl.pallas_call(
        functools.partial(_outproj_kernel, n_in=len(ys)),
        out_shape=jax.ShapeDtypeStruct((m, d), F32),
        grid=(m // tm,), in_specs=in_specs,
        out_specs=pl.BlockSpec((tm, d), lambda i: (i, 0)),
        compiler_params=_cparams(("parallel",)),
    )(*args)


def _router_kernel(h_ref, g_ref, sh_ref, sc_ref, wr_ref, br_ref,
                   up_ref, topi_ref, gate_ref, rank_ref, cnt_ref, carry_ref):
    @pl.when(pl.program_id(0) == 0)
    def _():
        carry_ref[...] = jnp.zeros_like(carry_ref)

    tm = h_ref.shape[0]
    u = _norm_mod(h_ref[...], g_ref[...], sh_ref[0], sc_ref[0])
    up_ref[...] = _pack_pairs(u)
    u_hi = u.astype(BF16)
    u_lo = (u - u_hi.astype(F32)).astype(BF16)
    logits = (_dot(u_hi, wr_ref[0]) + (_dot(u_lo, wr_ref[0]) + _dot(u_hi, wr_ref[1]))
              + br_ref[...])
    lt = jnp.concatenate([logits[r0:r0 + LANES].T for r0 in range(0, tm, LANES)], axis=1)
    l = lt[:N_EXPERTS]
    e_iota = lax.broadcasted_iota(I32, (N_EXPERTS, tm), 0).astype(F32)
    vals, idxs, hots = [], [], []
    for _ in range(TOP_K):
        mx = jnp.max(l, axis=0, keepdims=True)
        idx = jnp.min(jnp.where(l == mx, e_iota, float(N_EXPERTS)), axis=0, keepdims=True)
        hot = e_iota == idx
        l = jnp.where(hot, -jnp.inf, l)
        vals.append(mx)
        idxs.append(idx)
        hots.append(hot)
    exps = [jnp.exp(v - vals[0]) for v in vals]
    den = exps[0] + exps[1] + exps[2] + exps[3]
    gate_ref[...] = jnp.concatenate([e / den for e in exps], axis=0)
    topi_ref[...] = jnp.concatenate(idxs, axis=0).astype(I32)

    sel = jnp.zeros((N_EXPERTS, tm), F32)
    for hot in hots:
        sel = jnp.where(hot, 1.0, sel)
    r_i = lax.broadcasted_iota(I32, (tm, tm), 0)
    c_i = lax.broadcasted_iota(I32, (tm, tm), 1)
    before = (r_i < c_i).astype(BF16)
    carry = carry_ref[:, 0:1]
    cum = _dot(sel.astype(BF16), before) + carry
    rank_ref[...] = jnp.concatenate(
        [jnp.sum(jnp.where(hot, cum, 0.0), axis=0, keepdims=True) for hot in hots], axis=0).astype(I32)
    total = carry + jnp.sum(sel, axis=1, keepdims=True)
    carry_ref[...] = jnp.broadcast_to(total, carry_ref.shape)
    cnt_ref[...] = jnp.broadcast_to(total, cnt_ref.shape)


def _router(h, g, shift, scale, wr, br, seq, tm):
    m, d = h.shape
    tiles_per_seq = seq // tm
    bmap = lambda i: (i // tiles_per_seq, 0, 0)
    row4 = lambda: pl.BlockSpec((TOP_K, tm), lambda i: (0, i))
    return pl.pallas_call(
        _router_kernel,
        out_shape=[jax.ShapeDtypeStruct((m, d // 2), I32),
                   jax.ShapeDtypeStruct((TOP_K, m), I32),
                   jax.ShapeDtypeStruct((TOP_K, m), F32),
                   jax.ShapeDtypeStruct((TOP_K, m), I32),
                   jax.ShapeDtypeStruct((N_EXPERTS, LANES), F32)],
        grid=(m // tm,),
        in_specs=[pl.BlockSpec((tm, d), lambda i: (i, 0)),
                  pl.BlockSpec((1, d), lambda i: (0, 0)),
                  pl.BlockSpec((1, 1, d), bmap), pl.BlockSpec((1, 1, d), bmap),
                  pl.BlockSpec((2, d, LANES), lambda i: (0, 0, 0)),
                  pl.BlockSpec((1, LANES), lambda i: (0, 0))],
        out_specs=[pl.BlockSpec((tm, d // 2), lambda i: (i, 0)), row4(), row4(), row4(),
                   pl.BlockSpec((N_EXPERTS, LANES), lambda i: (0, 0))],
        scratch_shapes=[pltpu.VMEM((N_EXPERTS, LANES), F32)],
        compiler_params=_cparams(("arbitrary",)),
    )(h, g.reshape(1, d), shift, scale, wr, br)


def _dest_kernel(ps_ref, topi_ref, rank_ref, o_ref):
    topi = topi_ref[...]
    acc = rank_ref[...]
    for e in range(N_EXPERTS):
        acc = acc + jnp.where(topi == e, ps_ref[e], 0)
    o_ref[...] = acc


def _dest_rows(pad_start, topi, rank, tw):
    k, m = topi.shape
    blk = lambda: pl.BlockSpec((k, tw), lambda i, ps: (0, i))
    return pl.pallas_call(
        _dest_kernel,
        out_shape=jax.ShapeDtypeStruct((k, m), I32),
        grid_spec=pltpu.PrefetchScalarGridSpec(
            num_scalar_prefetch=1, grid=(m // tw,), in_specs=[blk(), blk()], out_specs=blk()),
        compiler_params=_cparams(("parallel",)),
    )(pad_start, topi, rank)


def _dispatch_kernel(dest_ref, up_ref, xs_ref, buf_ref, lsem, ssem):
    i = pl.program_id(0)
    n = pl.num_programs(0)
    tm = buf_ref.shape[1]
    slot = i % 2

    def load(tile, s):
        return pltpu.make_async_copy(up_ref.at[pl.ds(tile * tm, tm)], buf_ref.at[s], lsem.at[s])

    def wait_rows(s):
        for _ in range(TOP_K):
            pltpu.make_async_copy(buf_ref.at[s], xs_ref.at[pl.ds(0, tm)], ssem.at[s]).wait()

    pl.when(i == 0)(lambda: load(0, 0).start())
    pl.when(i > 0)(lambda: wait_rows(1 - slot))
    pl.when(i + 1 < n)(lambda: load(i + 1, 1 - slot).start())
    load(i, slot).wait()

    def issue(t, c):
        for k in range(TOP_K):
            pltpu.make_async_copy(buf_ref.at[slot, pl.ds(t, 1)],
                                  xs_ref.at[pl.ds(dest_ref[k * tm + t], 1)], ssem.at[slot]).start()
        return c

    lax.fori_loop(0, tm, issue, 0)
    pl.when(i == n - 1)(lambda: wait_rows(slot))


def _dispatch(dest_tiles, up, n_rows, tm):
    m, wp = up.shape
    return pl.pallas_call(
        _dispatch_kernel,
        out_shape=jax.ShapeDtypeStruct((n_rows, wp), I32),
        grid=(m // tm,),
        in_specs=[pl.BlockSpec((TOP_K * tm,), lambda i: (i,), memory_space=pltpu.SMEM),
                  pl.BlockSpec(memory_space=pl.ANY)],
        out_specs=pl.BlockSpec(memory_space=pl.ANY),
        scratch_shapes=[pltpu.VMEM((2, tm, wp), I32), pltpu.SemaphoreType.DMA((2,)),
                        pltpu.SemaphoreType.DMA((2,))],
        compiler_params=_cparams(("arbitrary",), has_side_effects=True),
    )(dest_tiles, up)


def _combine_kernel(dcur_ref, dnext_ref, gate_ref, h_ref, gf_ref, fn_ref, y_ref, o_ref, buf_ref, sem,
                    *, final):
    i = pl.program_id(0)
    tm = h_ref.shape[0]
    slot = i % 2

    def issue(d_ref, s):
        def body(t, c):
            for k in range(TOP_K):
                pltpu.make_async_copy(y_ref.at[pl.ds(d_ref[k * tm + t], 1)],
                                      buf_ref.at[s, k, pl.ds(t, 1)], sem.at[s]).start()
            return c
        lax.fori_loop(0, tm, body, 0)

    pl.when(i == 0)(lambda: issue(dcur_ref, slot))
    pl.when(i + 1 < pl.num_programs(0))(lambda: issue(dnext_ref, 1 - slot))
    pltpu.make_async_copy(buf_ref.at[slot], buf_ref.at[slot], sem.at[slot]).wait()
    acc = gate_ref[:, 0:1] * _unpack_pairs(buf_ref[slot, 0])
    for k in range(1, TOP_K):
        acc = acc + gate_ref[:, k:k + 1] * _unpack_pairs(buf_ref[slot, k])
    hn = h_ref[...] + gf_ref[0] * acc
    if final:
        hn = hn * lax.rsqrt(jnp.mean(hn * hn, axis=-1, keepdims=True) + EPS) * fn_ref[...]
    o_ref[...] = hn


def _combine(dest_tiles, gates_col, h, gf, fnorm, y, seq, tm, final):
    m, d = h.shape
    tiles_per_seq = seq // tm
    nt = m // tm
    return pl.pallas_call(
        functools.partial(_combine_kernel, final=final),
        out_shape=jax.ShapeDtypeStruct((m, d), F32),
        grid=(nt,),
        in_specs=[pl.BlockSpec((TOP_K * tm,), lambda i: (i,), memory_space=pltpu.SMEM),
                  pl.BlockSpec((TOP_K * tm,), lambda i: (jnp.minimum(i + 1, nt - 1),),
                               memory_space=pltpu.SMEM),
                  pl.BlockSpec((tm, TOP_K), lambda i: (i, 0)),
                  pl.BlockSpec((tm, d), lambda i: (i, 0)),
                  pl.BlockSpec((1, 1, d), lambda i: (i // tiles_per_seq, 0, 0)),
                  pl.BlockSpec((1, d), lambda i: (0, 0)),
                  pl.BlockSpec(memory_space=pl.ANY)],
        out_specs=pl.BlockSpec((tm, d), lambda i: (i, 0)),
        scratch_shapes=[pltpu.VMEM((2, TOP_K, tm, d // 2), I32), pltpu.SemaphoreType.DMA((2,))],
        compiler_params=_cparams(("arbitrary",)),
    )(dest_tiles, dest_tiles, gates_col, h, gf, fnorm.reshape(1, d), y)


SC_CORES = 2
SC_SUBCORES = 16
SC_ROWS = 128


def _sc_gather_rows(table, idx):
    b = idx.shape[0]
    w = table.shape[1]
    workers = SC_CORES * SC_SUBCORES
    per_w = b // workers
    assert per_w * workers == b and per_w % SC_ROWS == 0
    mesh = plsc.VectorSubcoreMesh(core_axis_name="c", subcore_axis_name="s")

    @functools.partial(
        pl.kernel, mesh=mesh, out_type=jax.ShapeDtypeStruct((b, w), I32),
        scratch_types=[pltpu.VMEM((SC_ROWS,), I32), pltpu.VMEM((SC_ROWS, w), I32),
                       pltpu.SemaphoreType.DMA])
    def gather(table_hbm, idx_hbm, out_hbm, idx_v, rows_v, sem):
        base = (lax.axis_index("s") * SC_CORES + lax.axis_index("c")) * per_w

        @pl.loop(0, per_w // SC_ROWS)
        def _(c):
            off = base + c * SC_ROWS
            pltpu.sync_copy(idx_hbm.at[pl.ds(off, SC_ROWS)], idx_v)
            pltpu.async_copy(table_hbm.at[idx_v], rows_v, sem).wait()
            pltpu.sync_copy(rows_v, out_hbm.at[pl.ds(off, SC_ROWS)])

    return gather(table, idx)


def _sc_scatter_rows(rows, dest, n_rows):
    m, w = rows.shape
    kk = dest.shape[0]
    workers = SC_CORES * SC_SUBCORES
    per_w = m // workers
    assert per_w * workers == m and per_w % SC_ROWS == 0
    mesh = plsc.VectorSubcoreMesh(core_axis_name="c", subcore_axis_name="s")

    @functools.partial(
        pl.kernel, mesh=mesh, out_type=jax.ShapeDtypeStruct((n_rows, w), I32),
        scratch_types=[pltpu.VMEM((SC_ROWS,), I32), pltpu.VMEM((SC_ROWS, w), I32),
                       pltpu.SemaphoreType.DMA])
    def scatter(rows_hbm, dest_hbm, out_hbm, idx_v, rows_v, sem):
        base = (lax.axis_index("s") * SC_CORES + lax.axis_index("c")) * per_w

        @pl.loop(0, per_w // SC_ROWS)
        def _(c):
            off = base + c * SC_ROWS
            pltpu.sync_copy(rows_hbm.at[pl.ds(off, SC_ROWS)], rows_v)
            for k in range(kk):
                pltpu.sync_copy(dest_hbm.at[pl.ds(k * m + off, SC_ROWS)], idx_v)
                pltpu.async_copy(rows_v, out_hbm.at[idx_v], sem).wait()

    return scatter(rows, dest.reshape(-1))


def _combine_dense_kernel(y_ref, gate_ref, h_ref, gf_ref, fn_ref, o_ref, *, final):
    acc = gate_ref[:, 0:1] * _unpack_pairs(y_ref[0])
    for k in range(1, TOP_K):
        acc = acc + gate_ref[:, k:k + 1] * _unpack_pairs(y_ref[k])
    hn = h_ref[...] + gf_ref[0] * acc
    if final:
        hn = hn * lax.rsqrt(jnp.mean(hn * hn, axis=-1, keepdims=True) + EPS) * fn_ref[...]
    o_ref[...] = hn


def _combine_dense(y4, gates_col, h, gf, fnorm, seq, tm, final):
    m, d = h.shape
    tiles_per_seq = seq // tm
    return pl.pallas_call(
        functools.partial(_combine_dense_kernel, final=final),
        out_shape=jax.ShapeDtypeStruct((m, d), F32),
        grid=(m // tm,),
        in_specs=[pl.BlockSpec((TOP_K, tm, d // 2), lambda i: (0, i, 0)),
                  pl.BlockSpec((tm, TOP_K), lambda i: (i, 0)),
                  pl.BlockSpec((tm, d), lambda i: (i, 0)),
                  pl.BlockSpec((1, 1, d), lambda i: (i // tiles_per_seq, 0, 0)),
                  pl.BlockSpec((1, d), lambda i: (0, 0))],
        out_specs=pl.BlockSpec((tm, d), lambda i: (i, 0)),
        compiler_params=_cparams(("parallel",)),
    )(y4, gates_col, h, gf, fnorm.reshape(1, d))


def _expert_kernel(be_ref, nb_ref, first_ref, x_ref, wgu_ref, bgu_ref, wd_ref, bd_ref, y_ref,
                   wgu_bf, wd_bf):
    i = pl.program_id(0)

    @pl.when(i < nb_ref[0])
    def _():
        dff = wd_bf.shape[0]

        @pl.when(first_ref[i] == 1)
        def _():
            rows = 64

            def cast(r, c):
                r0 = pl.multiple_of(r * rows, rows)
                wgu_bf[pl.ds(r0, rows), :] = wgu_ref[0, 0, pl.ds(r0, rows), :].astype(BF16)
                wd_bf[pl.ds(r0, rows), :] = wd_ref[0, 0, pl.ds(r0, rows), :].astype(BF16)
                return c

            lax.fori_loop(0, dff // rows, cast, 0)

        x = _unpack_pairs(x_ref[...]).astype(BF16)
        hb = _dot(x, wgu_bf[...]) + bgu_ref[0, 0]
        h_glu = jnp.minimum(hb[:, :dff], SWIGLU_LIMIT)
        h_lin = jnp.clip(hb[:, dff:], -SWIGLU_LIMIT, SWIGLU_LIMIT)
        act = h_glu * jax.nn.sigmoid(SWIGLU_ALPHA * h_glu) * (h_lin + 1.0)
        y_ref[...] = _pack_pairs(_dot(act.astype(BF16), wd_bf[...]) + bd_ref[0, 0])


def _experts(block_e, n_used, first, xs, wgu, bgu, wd, bd, layer):
    n_rows, wp = xs.shape
    _, ne, d, ff2 = wgu.shape
    assert d == ff2 // 2
    nblk = n_rows // EXPERT_BLOCK

    def xmap(i, be, nb, fi):
        return (jnp.minimum(i, nb[0] - 1), 0)

    emap = lambda i, be, nb, fi: (layer, be[i], 0, 0)
    grid_spec = pltpu.PrefetchScalarGridSpec(
        num_scalar_prefetch=3, grid=(nblk,),
        in_specs=[pl.BlockSpec((EXPERT_BLOCK, wp), xmap),
                  pl.BlockSpec((1, 1, d, ff2), emap), pl.BlockSpec((1, 1, 1, ff2), emap),
                  pl.BlockSpec((1, 1, ff2 // 2, d), emap), pl.BlockSpec((1, 1, 1, d), emap)],
        out_specs=pl.BlockSpec((EXPERT_BLOCK, wp), xmap),
        scratch_shapes=[pltpu.VMEM((d, ff2), BF16), pltpu.VMEM((ff2 // 2, d), BF16)])
    depth = wgu.shape[0]
    return pl.pallas_call(
        _expert_kernel,
        out_shape=jax.ShapeDtypeStruct((n_rows, wp), I32),
        grid_spec=grid_spec,
        compiler_params=_cparams(("arbitrary",)),
    )(block_e, n_used, first, xs, wgu, bgu.reshape(depth, ne, 1, ff2), wd, bd.reshape(depth, ne, 1, d))


def _moe(h, g, shift, scale, gf, fnorm, wr, br, wgu, bgu, wd, bd, layer, seq, final):
    m, d = h.shape
    tm = 512
    wr_p =jnp.zeros((d, LANES), F32).at[:, :N_EXPERTS].set(wr)
    br_p = jnp.zeros((1, LANES), F32).at[0, :N_EXPERTS].set(br)
    wr_hi = wr_p.astype(BF16)
    wr_split = jnp.stack([wr_hi, (wr_p - wr_hi.astype(F32)).astype(BF16)])
    up, topi, gates, rank, cnt = _router(h, g, shift, scale, wr_split, br_p, seq, tm)

    counts = cnt[:, 0].astype(I32)
    padded = (counts + EXPERT_BLOCK - 1) // EXPERT_BLOCK * EXPERT_BLOCK
    pad_end = jnp.cumsum(padded)
    pad_start = pad_end - padded
    nblk = m * TOP_K // EXPERT_BLOCK + N_EXPERTS
    n_rows = nblk * EXPERT_BLOCK
    n_used = pad_end[-1:] // EXPERT_BLOCK
    blk = jnp.arange(nblk, dtype=I32)
    blk_c = jnp.minimum(blk, n_used - 1)
    block_e = jnp.minimum(jnp.sum(blk_c[:, None] * EXPERT_BLOCK >= pad_end[None, :], axis=1),
                          N_EXPERTS - 1).astype(I32)
    first = jnp.concatenate([jnp.ones((1,), I32), (block_e[1:] != block_e[:-1]).astype(I32)])

    dest = _dest_rows(pad_start, topi, rank, min(m, 8192))
    xs = _sc_scatter_rows(up, dest, n_rows)
    y = _experts(block_e, n_used.astype(I32), first, xs, wgu, bgu, wd, bd, layer)
    y4 = _sc_gather_rows(y, dest.reshape(-1)).reshape(TOP_K, m, d // 2)
    return _combine_dense(y4, gates.T, h, gf, fnorm, seq, tm, final)


def _block_diag(w, group):
    nb, b, _ = w.shape
    per = group // b
    wg = w.reshape(nb // per, per, b, b)
    dense = jnp.einsum("gnde,nm->gndme", wg, jnp.eye(per, dtype=w.dtype))
    return dense.reshape(nb // per, group, group)


def kernel(x, c, mod_w, mod_b, norm_mix, norm_ffn, ev_w_in, ev_lru_conv_w, ev_lru_conv_b, ev_lru_w_r, ev_lru_b_r, ev_lru_w_i, ev_lru_b_i, ev_lru_lambda, ev_ml_conv_w, ev_ml_conv_b, ev_ml_w_q, ev_ml_w_k, ev_ml_w_v, ev_ml_w_ig, ev_ml_b_ig, ev_ml_w_fg, ev_ml_b_fg, ev_ml_norm, ev_ml_skip, ev_w_out, od_w_in, od_conv_w, od_conv_b, od_dt_bias, od_a_log, od_d, od_norm, od_w_out, moe_router_w, moe_router_b, moe_w_gu, moe_b_gu, moe_w_down, moe_b_down, final_norm):
    bsz, seq, d = x.shape
    depth = mod_w.shape[0]
    m = bsz * seq
    mod = _modulation(c, mod_w, mod_b)
    h = x.reshape(m, d).astype(F32)
    for layer in range(depth):
        sh_m, sc_m, g_m, sh_f, sc_f, g_f = (mod[layer, i] for i in range(6))
        j = layer // 2
        if layer % 2 == 0:
            w = ev_lru_lambda.shape[1]
            w_in = ev_w_in[j].astype(BF16)
            proj = _inproj(h, norm_mix[layer], sh_m, sc_m, w_in, None, [w_in.shape[1]], seq, 512)[0]
            lru_p = dict(conv_w=ev_lru_conv_w[j], conv_b=ev_lru_conv_b[j].reshape(1, w),
                         w_r=ev_lru_w_r[j].astype(BF16), b_r=ev_lru_b_r[j].reshape(1, w),
                         w_i=ev_lru_w_i[j].astype(BF16), b_i=ev_lru_b_i[j].reshape(1, w),
                         lam=ev_lru_lambda[j].reshape(1, w))
            ya = _lru(proj, lru_p, bsz, seq, 256)
            wg = jnp.zeros((3 * w, LANES), F32)
            wg = wg.at[:, :ML_HEADS].set(ev_ml_w_ig[j]).at[:, ML_HEADS:2 * ML_HEADS].set(ev_ml_w_fg[j])
            bg = jnp.zeros((1, LANES), F32)
            bg = bg.at[0, :ML_HEADS].set(ev_ml_b_ig[j]).at[0, ML_HEADS:2 * ML_HEADS].set(ev_ml_b_fg[j])
            ml_p = dict(conv_w=ev_ml_conv_w[j], conv_b=ev_ml_conv_b[j].reshape(1, w),
                        w_q=_block_diag(ev_ml_w_q[j], LANES).astype(BF16),
                        w_k=_block_diag(ev_ml_w_k[j], LANES).astype(BF16),
                        w_v=_block_diag(ev_ml_w_v[j], LANES).astype(BF16),
                        w_g=wg.astype(BF16), b_g=bg,
                        norm=ev_ml_norm[j].reshape(1, w), skip=ev_ml_skip[j].reshape(1, w))
            yb = _mlstm(proj, ml_p, bsz, seq)
            h = _outproj([ya, yb], ev_w_out[j].astype(BF16), h, g_m, seq, 512)
        else:
            inner = od_norm.shape[1]
            heads = od_dt_bias.shape[1]
            conv_ch = od_conv_w.shape[2]
            w_in = od_w_in[j]
            wdt = jnp.zeros((d, LANES), F32).at[:, :heads].set(w_in[:, inner + conv_ch:])
            z, xbc, dt_raw = _inproj(h, norm_mix[layer], sh_m, sc_m, w_in[:, :inner + conv_ch].astype(BF16),
                                     wdt.astype(BF16), [inner, conv_ch], seq, 256)
            pad = lambda v: jnp.zeros((1, LANES), F32).at[0, :heads].set(v)
            ssd_p = dict(conv_w=od_conv_w[j], conv_b=od_conv_b[j].reshape(1, conv_ch),
                         dt_bias=pad(od_dt_bias[j]), a_log=pad(od_a_log[j]),
                         d_skip=jnp.repeat(od_d[j], SSD_HEAD_DIM).reshape(1, inner),
                         norm=od_norm[j].reshape(1, inner))
            y = _ssd(z, xbc, dt_raw, ssd_p, bsz, seq)
            h = _outproj([y], od_w_out[j].astype(BF16), h, g_m, seq, 512)
        h = _moe(h, norm_ffn[layer], sh_f, sc_f, g_f, final_norm,
                 moe_router_w[layer], moe_router_b[layer],
                 moe_w_gu, moe_b_gu, moe_w_down, moe_b_down, layer, seq, final=(layer == depth - 1))
    return h.reshape(bsz, seq, d)
```

```python
import functools

import jax
import jax.numpy as jnp
from jax import lax
from jax.experimental import pallas as pl
from jax.experimental.pallas import tpu as pltpu
from jax.experimental.pallas import tpu_sc as plsc

F32 = jnp.float32
BF16 = jnp.bfloat16
I32 = jnp.int32
HIGHEST = lax.Precision.HIGHEST

EPS = 1e-6
CONV_WIDTH = 4
LANES = 128
SUBLANES = 8
LRU_HEADS = 8
LRU_C = 8.0
ML_HEADS = 8
ML_QKV_BLOCK = 4
CHUNK = 128
SSD_HEAD_DIM = 64
SSD_GROUPS = 8
SSD_STATE = 128
N_EXPERTS = 32
TOP_K = 4
SWIGLU_ALPHA = 1.702
SWIGLU_LIMIT = 7.0
EXPERT_BLOCK = 512
SEQ_PER_STEP = 2
CONV_COLS = 512
VMEM_LIMIT = 56 * 1024 * 1024


def _cparams(sem, **kw):
    return pltpu.CompilerParams(dimension_semantics=sem, vmem_limit_bytes=VMEM_LIMIT, **kw)


def _silu(x):
    half = 0.5 * x
    return half + half * jnp.tanh(half)


def _log_sigmoid(x):
    return jnp.minimum(x, 0.0) - jnp.log1p(jnp.exp(-jnp.abs(x)))


def _softplus(x):
    return jnp.maximum(x, 0.0) + jnp.log1p(jnp.exp(-jnp.abs(x)))


def _dot(a, b, **kw):
    return jnp.dot(a, b, preferred_element_type=F32, **kw)


def _dot_nt(a, b):
    return lax.dot_general(a, b, (((1,), (1,)), ((), ())), preferred_element_type=F32)


def _pack_pairs(x):
    w = x.shape[1] // 2
    lo = lax.bitcast_convert_type(x[:, :w].astype(BF16).astype(F32), I32)
    hi = lax.bitcast_convert_type(x[:, w:].astype(BF16).astype(F32), I32)
    return lax.shift_right_logical(lo, 16) | (hi & jnp.int32(-65536))


def _unpack_pairs(p):
    lo = lax.bitcast_convert_type(lax.shift_left(p, 16), F32)
    hi = lax.bitcast_convert_type(p & jnp.int32(-65536), F32)
    return jnp.concatenate([lo, hi], axis=1)


def _norm_mod(h, g, shift, scale):
    y = h * lax.rsqrt(jnp.mean(h * h, axis=-1, keepdims=True) + EPS)
    return (y * g) * (1.0 + scale) + shift


def _causal_conv(x, tail_ref, w_ref, b_ref, sl):
    t = x.shape[0]
    tail = tail_ref[:, sl]
    row8 = lax.broadcasted_iota(I32, tail.shape, 0)
    out = b_ref[:, sl] + x * w_ref[CONV_WIDTH - 1:CONV_WIDTH, sl]
    for k in range(1, CONV_WIDTH):
        xs = pltpu.roll(x, k, axis=0)
        first = jnp.where(row8 < k, pltpu.roll(tail, k, axis=0), xs[:SUBLANES])
        xs = jnp.concatenate([first, xs[SUBLANES:]], axis=0)
        out = out + xs * w_ref[CONV_WIDTH - 1 - k:CONV_WIDTH - k, sl]
    tail_ref[:, sl] = x[t - SUBLANES:]
    return out


def _shift_matrix(t):
    r = lax.broadcasted_iota(I32, ((CONV_WIDTH - 1) * t, t), 0)
    c = lax.broadcasted_iota(I32, ((CONV_WIDTH - 1) * t, t), 1)
    src = (r & (t - 1)) - lax.shift_right_logical(r, t.bit_length() - 1) - 1
    return (src == c).astype(BF16)


def _causal_conv_shifted(x, shifted, tail_ref, w_ref, b_ref, sl):
    t = x.shape[0]
    tail = tail_ref[:, sl]
    row8 = lax.broadcasted_iota(I32, tail.shape, 0)
    out = b_ref[:, sl] + x * w_ref[CONV_WIDTH - 1:CONV_WIDTH, sl]
    head = jnp.zeros_like(tail)
    for k in range(1, CONV_WIDTH):
        wk = w_ref[CONV_WIDTH - 1 - k:CONV_WIDTH - k, sl]
        out = out + shifted[(k - 1) * t:k * t] * wk
        head = head + jnp.where(row8 < k, pltpu.roll(tail, k, axis=0), 0.0) * wk
    tail_ref[:, sl] = x[t - SUBLANES:]
    return jnp.concatenate([out[:SUBLANES] + head, out[SUBLANES:]], axis=0)


def _mod_kernel(c_ref, w_ref, b_ref, o_ref):
    cond = _silu(c_ref[...])
    o_ref[0, 0] = _dot(cond, w_ref[0], precision=HIGHEST) + b_ref[0, 0]


def _modulation(c, mod_w, mod_b):
    depth, d, _ = mod_w.shape
    bsz = c.shape[0]
    out = pl.pallas_call(
        _mod_kernel,
        out_shape=jax.ShapeDtypeStruct((depth, 6, bsz, d), F32),
        grid=(depth, 6),
        in_specs=[pl.BlockSpec((bsz, d), lambda l, j: (0, 0)),
                  pl.BlockSpec((1, d, d), lambda l, j: (l, 0, j)),
                  pl.BlockSpec((1, 1, 1, d), lambda l, j: (l, j, 0, 0))],
        out_specs=pl.BlockSpec((1, 1, bsz, d), lambda l, j: (l, j, 0, 0)),
        compiler_params=_cparams(("parallel", "parallel")),
    )(c.astype(F32), mod_w, mod_b.reshape(depth, 6, 1, d))
    return out.reshape(depth, 6, bsz, 1, d)


def _inproj_kernel(h_ref, g_ref, sh_ref, sc_ref, w_ref, *rest, n_chunk, with_dt):
    if with_dt:
        wdt_ref, *o_refs, odt_ref = rest
    else:
        o_refs = rest
    u = _norm_mod(h_ref[...], g_ref[...], sh_ref[0], sc_ref[0]).astype(BF16)
    off = 0
    for o_ref in o_refs:
        for n0 in range(0, o_ref.shape[1], n_chunk):
            o_ref[:, n0:n0 + n_chunk] = _dot(u, w_ref[:, off + n0:off + n0 + n_chunk]).astype(o_ref.dtype)
        off += o_ref.shape[1]
    if with_dt:
        odt_ref[...] = _dot(u, wdt_ref[...])


def _inproj(h, g, shift, scale, w, wdt, splits, seq, tm):
    m, d = h.shape
    n = w.shape[1]
    assert sum(splits) == n
    tiles_per_seq = seq // tm
    bmap = lambda i: (i // tiles_per_seq, 0, 0)
    in_specs = [pl.BlockSpec((tm, d), lambda i: (i, 0)),
                pl.BlockSpec((1, d), lambda i: (0, 0)),
                pl.BlockSpec((1, 1, d), bmap),
                pl.BlockSpec((1, 1, d), bmap),
                pl.BlockSpec((d, n), lambda i: (0, 0), pipeline_mode=pl.Buffered(1))]
    out_shape = [jax.ShapeDtypeStruct((m, s), BF16) for s in splits]
    out_specs = [pl.BlockSpec((tm, s), lambda i: (i, 0)) for s in splits]
    args = [h, g.reshape(1, d), shift, scale, w]
    if wdt is not None:
        in_specs.append(pl.BlockSpec((d, LANES), lambda i: (0, 0)))
        out_shape.append(jax.ShapeDtypeStruct((m, LANES), F32))
        out_specs.append(pl.BlockSpec((tm, LANES), lambda i: (i, 0)))
        args.append(wdt)
    return pl.pallas_call(
        functools.partial(_inproj_kernel, n_chunk=1024, with_dt=wdt is not None),
        out_shape=out_shape, grid=(m // tm,), in_specs=in_specs, out_specs=out_specs,
        compiler_params=_cparams(("parallel",)),
    )(*args)


def _lru_kernel(xa_ref, ga_ref, cw_ref, cb_ref, wr_ref, br_ref, wi_ref, bi_ref, lam_ref,
                o_ref, tail_ref, hc_ref):
    @pl.when(pl.program_id(1) == 0)
    def _():
        tail_ref[...] = jnp.zeros_like(tail_ref)
        hc_ref[...] = jnp.zeros_like(hc_ref)

    t = xa_ref.shape[0]
    row_in_group = lax.broadcasted_iota(I32, (t, LANES), 0) % SUBLANES
    steps = [s for s in (1, 2, 4) if s < SUBLANES]
    masks = [row_in_group >= s for s in steps]
    for hh in range(LRU_HEADS):
        sl = slice(hh * LANES, (hh + 1) * LANES)
        xc = _causal_conv(xa_ref[:, sl].astype(F32), tail_ref, cw_ref, cb_ref, sl)
        xcb = xc.astype(BF16)
        r = jax.nn.sigmoid(_dot(xcb, wr_ref[hh]) + br_ref[:, sl])
        i = jax.nn.sigmoid(_dot(xcb, wi_ref[hh]) + bi_ref[:, sl])
        log_a = LRU_C * r * _log_sigmoid(lam_ref[:, sl])
        a = jnp.exp(log_a)
        th = jnp.tanh(log_a)
        n2 = -2.0 * th
        root = jnp.where(n2 > 0.0, n2 * lax.rsqrt(n2), 0.0)
        u = (root * lax.rsqrt(1.0 - th)) * (i * xc)
        for s, m in zip(steps, masks):
            u = jnp.where(m, u + a * pltpu.roll(u, s, axis=0), u)
            a = jnp.where(m, a * pltpu.roll(a, s, axis=0), a)
        carry = hc_ref[:, sl]
        groups = []
        for r0 in range(0, t, SUBLANES):
            hg = u[r0:r0 + SUBLANES] + a[r0:r0 + SUBLANES] * carry
            carry = hg[SUBLANES - 1:SUBLANES]
            groups.append(hg)
        hc_ref[:, sl] = carry
        h = jnp.concatenate(groups, axis=0)
        ga = ga_ref[:, sl].astype(F32)
        o_ref[:, sl] = (h * jax.nn.gelu(ga, approximate=True)).astype(o_ref.dtype)


def _lru(proj, p, bsz, seq, tm):
    m = proj.shape[0]
    w = LRU_HEADS * LANES
    nt = seq // tm
    vec = lambda: pl.BlockSpec((1, w), lambda b, j: (0, 0))
    return pl.pallas_call(
        _lru_kernel,
        out_shape=jax.ShapeDtypeStruct((m, w), BF16),
        grid=(bsz, nt),
        in_specs=[pl.BlockSpec((tm, w), lambda b, j: (b * nt + j, 0)),
                  pl.BlockSpec((tm, w), lambda b, j: (b * nt + j, 1)),
                  pl.BlockSpec((CONV_WIDTH, w), lambda b, j: (0, 0)), vec(),
                  pl.BlockSpec((LRU_HEADS, LANES, LANES), lambda b, j: (0, 0, 0)), vec(),
                  pl.BlockSpec((LRU_HEADS, LANES, LANES), lambda b, j: (0, 0, 0)), vec(), vec()],
        out_specs=pl.BlockSpec((tm, w), lambda b, j: (b * nt + j, 0)),
        scratch_shapes=[pltpu.VMEM((SUBLANES, w), F32), pltpu.VMEM((1, w), F32)],
        compiler_params=_cparams(("parallel", "arbitrary")),
    )(proj, proj, p["conv_w"], p["conv_b"], p["w_r"], p["b_r"], p["w_i"], p["b_i"], p["lam"])


def _mlstm_kernel(xb_ref, zb_ref, cw_ref, cb_ref, wq_ref, wk_ref, wv_ref, wg_ref, bg_ref,
                  nw_ref, sk_ref, o_ref, tail_ref, qkv_ref, xc_ref, caug_ref, m_ref):
    @pl.when(pl.program_id(1) == 0)
    def _():
        tail_ref[...] = jnp.zeros_like(tail_ref)
        caug_ref[...] = jnp.zeros_like(caug_ref)
        m_ref[...] = jnp.full(m_ref.shape, -jnp.inf, F32)

    for s in range(xb_ref.shape[0]):
        for c0 in range(0, xb_ref.shape[1], CHUNK):
            rows = pl.ds(c0, CHUNK)
            _mlstm_chunk(xb_ref.at[s, rows], zb_ref.at[s, rows], cw_ref, cb_ref, wq_ref, wk_ref, wv_ref,
                         wg_ref, bg_ref, nw_ref, sk_ref, o_ref.at[s, rows], tail_ref.at[s], qkv_ref.at[s],
                         xc_ref.at[s], caug_ref.at[s], m_ref.at[s])


def _mlstm_chunk(xb_ref, zb_ref, cw_ref, cb_ref, wq_ref, wk_ref, wv_ref, wg_ref, bg_ref,
                 nw_ref, sk_ref, o_ref, tail_ref, qkv_ref, xc_ref, caug_ref, m_ref):
    L = CHUNK
    width = ML_HEADS * LANES
    scale = LANES ** -0.5
    for hh in range(ML_HEADS):
        sl = slice(hh * LANES, (hh + 1) * LANES)
        xb = xb_ref[:, sl].astype(F32)
        xc = _silu(_causal_conv(xb, tail_ref, cw_ref, cb_ref, sl))
        xc_ref[:, sl] = xc
        xcb = xc.astype(BF16)
        qkv_ref[:, sl] = _dot(xcb, wq_ref[hh]).astype(BF16)
        qkv_ref[:, width + hh * LANES:width + (hh + 1) * LANES] = _dot(xcb, wk_ref[hh]).astype(BF16)
        qkv_ref[:, 2 * width + hh * LANES:2 * width + (hh + 1) * LANES] = (
            _dot(xb.astype(BF16), wv_ref[hh]).astype(BF16))

    gates = _dot(qkv_ref[...], wg_ref[...]) + bg_ref[...]
    rowi = lax.broadcasted_iota(I32, (L, L), 0)
    coli = lax.broadcasted_iota(I32, (L, L), 1)
    causal = rowi >= coli
    lf = jnp.where((coli >= ML_HEADS) & (coli < 2 * ML_HEADS), _log_sigmoid(gates), 0.0)
    tri = causal.astype(BF16)
    lf_hi = lf.astype(BF16)
    lf_mid = (lf - lf_hi.astype(F32)).astype(BF16)
    lf_lo = (lf - lf_hi.astype(F32) - lf_mid.astype(F32)).astype(BF16)
    gcum = _dot(tri, lf_hi) + (_dot(tri, lf_mid) + _dot(tri, lf_lo))
    x_col = jnp.where(coli < ML_HEADS, gates, gcum)
    x_row = x_col.T
    ones = jnp.ones((L, LANES), BF16)
    heads = range(ML_HEADS)
    hsl = [slice(hh * LANES, (hh + 1) * LANES) for hh in heads]

    qs = [qkv_ref[:, hsl[hh]] for hh in heads]
    ks = [qkv_ref[:, width + hh * LANES:width + (hh + 1) * LANES] for hh in heads]
    vaugs = [jnp.concatenate([qkv_ref[:, 2 * width + hh * LANES:2 * width + (hh + 1) * LANES], ones], axis=1)
             for hh in heads]
    scores = [_dot_nt(qs[hh], ks[hh]) * scale for hh in heads]
    ics = [jnp.broadcast_to(x_col[:, hh:hh + 1], (L, LANES)) for hh in heads]
    gcs = [jnp.broadcast_to(x_col[:, ML_HEADS + hh:ML_HEADS + hh + 1], (L, LANES)) for hh in heads]
    irs = [x_row[hh:hh + 1, :] for hh in heads]
    grs = [x_row[ML_HEADS + hh:ML_HEADS + hh + 1, :] for hh in heads]
    mps = [m_ref[hh] for hh in heads]
    dmats = [jnp.where(causal, gcs[hh] - grs[hh] + irs[hh], -jnp.inf) for hh in heads]
    m_inters = [mps[hh] + gcs[hh] for hh in heads]
    m_ts = [jnp.maximum(m_inters[hh], jnp.max(dmats[hh], axis=1, keepdims=True)) for hh in heads]
    qks = [(scores[hh] * jnp.exp(dmats[hh] - m_ts[hh])).astype(BF16) for hh in heads]
    caugs = [caug_ref[hh] for hh in heads]
    w_inters = [jnp.exp(m_inters[hh] - m_ts[hh]) for hh in heads]
    nds = [_dot(qks[hh], vaugs[hh])
           + jnp.concatenate([w_inters[hh], w_inters[hh]], axis=1) * _dot(qs[hh], caugs[hh].astype(BF16))
           for hh in heads]

    g_lasts = [gcs[hh][L - 1:L, :] for hh in heads]
    m_news = [jnp.maximum(mps[hh] + g_lasts[hh],
                          jnp.max(g_lasts[hh] - grs[hh] + irs[hh], axis=1, keepdims=True)) for hh in heads]
    for hh in heads:
        ws = jnp.exp(g_lasts[hh] - gcs[hh] + ics[hh] - m_news[hh])
        wc = jnp.exp(mps[hh] + g_lasts[hh] - m_news[hh])
        kw_t = (ks[hh].astype(F32) * (ws * scale)).T.astype(BF16)
        caug_ref[hh] = jnp.concatenate([wc, wc], axis=1) * caugs[hh] + _dot(kw_t, vaugs[hh])
        m_ref[hh] = m_news[hh]

    hvals = [nds[hh][:, :LANES] / jnp.maximum(jnp.abs(nds[hh][:, LANES:]), jnp.exp(-m_ts[hh]))
             for hh in heads]
    mus = [jnp.mean(hvals[hh], axis=1, keepdims=True) for hh in heads]
    dvs = [hvals[hh] - mus[hh] for hh in heads]
    variances = [jnp.mean(dvs[hh] * dvs[hh], axis=1, keepdims=True) for hh in heads]
    for hh in heads:
        sl = hsl[hh]
        hn = dvs[hh] * lax.rsqrt(variances[hh] + EPS) * nw_ref[:, sl]
        zb = zb_ref[:, sl].astype(F32)
        o_ref[:, sl] = ((hn + sk_ref[:, sl] * xc_ref[:, sl]) * _silu(zb)).astype(o_ref.dtype)


def _mlstm(proj, p, bsz, seq):
    m = proj.shape[0]
    w = ML_HEADS * LANES
    sps = 1
    cps = 2 if seq % (2 * CHUNK) == 0 else 1
    nt = seq // (cps * CHUNK)
    vec = lambda: pl.BlockSpec((1, w), lambda b, j: (0, 0))
    blk = lambda: pl.BlockSpec((ML_HEADS, LANES, LANES), lambda b, j: (0, 0, 0))
    tile = lambda col: pl.BlockSpec((sps, cps * CHUNK, w), lambda b, j: (b, j, col))
    proj3 = proj.reshape(bsz, seq, proj.shape[1])
    out = pl.pallas_call(
        _mlstm_kernel,
        out_shape=jax.ShapeDtypeStruct((bsz, seq, w), BF16),
        grid=(bsz // sps, nt),
        in_specs=[tile(2), tile(3),
                  pl.BlockSpec((CONV_WIDTH, w), lambda b, j: (0, 0)), vec(),
                  blk(), blk(), blk(),
                  pl.BlockSpec((3 * w, LANES), lambda b, j: (0, 0)),
                  pl.BlockSpec((1, LANES), lambda b, j: (0, 0)),
                  vec(), vec()],
        out_specs=tile(0),
        scratch_shapes=[pltpu.VMEM((sps, SUBLANES, w), F32),
                        pltpu.VMEM((sps, CHUNK, 3 * w), BF16),
                        pltpu.VMEM((sps, CHUNK, w), F32),
                        pltpu.VMEM((sps, ML_HEADS, LANES, 2 * LANES), F32),
                        pltpu.VMEM((sps, ML_HEADS, 1, LANES), F32)],
        compiler_params=_cparams(("parallel", "arbitrary")),
    )(proj3, proj3, p["conv_w"], p["conv_b"], p["w_q"], p["w_k"], p["w_v"], p["w_g"], p["b_g"],
      p["norm"], p["skip"])
    return out.reshape(m, w)


def _ssd_kernel(z_ref, xbc_ref, dt_ref, cw_ref, cb_ref, dtb_ref, alog_ref, dsk_ref, nw_ref,
                o_ref, tail_ref, act_ref, st_ref):
    @pl.when(pl.program_id(1) == 0)
    def _():
        tail_ref[...] = jnp.zeros_like(tail_ref)
        st_ref[...] = jnp.zeros_like(st_ref)

    for s in range(z_ref.shape[0]):
        _ssd_chunk(z_ref.at[s], xbc_ref.at[s], dt_ref.at[s], cw_ref, cb_ref, dtb_ref, alog_ref, dsk_ref,
                   nw_ref, o_ref.at[s], tail_ref.at[s], act_ref.at[s], st_ref.at[s])


def _ssd_chunk(z_ref, xbc_ref, dt_ref, cw_ref, cb_ref, dtb_ref, alog_ref, dsk_ref, nw_ref,
               o_ref, tail_ref, act_ref, st_ref):
    L = CHUNK
    inner = o_ref.shape[1]
    gw = inner // SSD_GROUPS
    hpg = gw // SSD_HEAD_DIM
    b_off = inner
    c_off = inner + SSD_GROUPS * SSD_STATE
    shift = _shift_matrix(L)
    for c0 in range(0, xbc_ref.shape[1], CONV_COLS):
        shifted = _dot(shift, xbc_ref[:, c0:c0 + CONV_COLS])
        for l0 in range(0, CONV_COLS, LANES):
            sl = slice(c0 + l0, c0 + l0 + LANES)
            act_ref[:, sl] = _silu(_causal_conv_shifted(
                xbc_ref[:, sl].astype(F32), shifted[:, l0:l0 + LANES], tail_ref, cw_ref, cb_ref, sl))

    rowi = lax.broadcasted_iota(I32, (L, L), 0)
    coli = lax.broadcasted_iota(I32, (L, L), 1)
    causal = rowi >= coli
    dt = _softplus(dt_ref[...] + dtb_ref[...])
    da = dt * (-jnp.exp(alog_ref[...]))
    tri = causal.astype(BF16)
    da_hi = da.astype(BF16)
    da_mid = (da - da_hi.astype(F32)).astype(BF16)
    da_lo = (da - da_hi.astype(F32) - da_mid.astype(F32)).astype(BF16)
    a = _dot(tri, da_hi) + (_dot(tri, da_mid) + _dot(tri, da_lo))
    a_t = a.T
    hpt = LANES // SSD_HEAD_DIM
    lane = lax.broadcasted_iota(I32, (L, LANES), 1)

    def over_heads(tiles):
        cols = []
        for c0 in range(0, hpg, hpt):
            out = tiles[c0 + hpt - 1]
            for j in range(hpt - 2, -1, -1):
                out = jnp.where(lane < (j + 1) * SSD_HEAD_DIM, tiles[c0 + j], out)
            cols.append(out)
        return jnp.concatenate(cols, axis=1)

    for g in range(SSD_GROUPS):
        gsl = slice(g * gw, (g + 1) * gw)
        xg = act_ref[:, gsl]
        bg = act_ref[:, b_off + g * SSD_STATE:b_off + (g + 1) * SSD_STATE]
        cg_ = act_ref[:, c_off + g * SSD_STATE:c_off + (g + 1) * SSD_STATE].astype(BF16)
        cb = _dot_nt(cg_, bg.astype(BF16))
        state = st_ref[g]
        hds = [g * hpg + jj for jj in range(hpg)]
        a_bs = [jnp.broadcast_to(a[:, hd:hd + 1], (L, LANES)) for hd in hds]
        dt_bs = [jnp.broadcast_to(dt[:, hd:hd + 1], (L, LANES)) for hd in hds]
        ea_x = over_heads([jnp.exp(a_b) for a_b in a_bs])
        to_end_x = over_heads([jnp.exp(a_b[L - 1:L, :] - a_b) for a_b in a_bs])
        xdt = xg * over_heads(dt_bs)
        inter = _dot(cg_, state.astype(BF16)) * ea_x
        acc = [inter[:, c0:c0 + LANES] for c0 in range(0, gw, LANES)]
        for jj in range(hpg):
            seg = jnp.where(causal, a_bs[jj] - a_t[hds[jj]:hds[jj] + 1, :], -jnp.inf)
            w = (cb * jnp.exp(seg)).astype(BF16)
            c, j = divmod(jj, hpt)
            in_head = (lane >= j * SSD_HEAD_DIM) & (lane < (j + 1) * SSD_HEAD_DIM)
            x_head = jnp.where(in_head, xdt[:, c * LANES:(c + 1) * LANES], 0.0).astype(BF16)
            acc[c] = acc[c] + _dot(w, x_head)
        acc = jnp.concatenate(acc, axis=1)
        y = (acc + dsk_ref[:, gsl] * xg) * _silu(z_ref[:, gsl].astype(F32))
        y = y * lax.rsqrt(jnp.mean(y * y, axis=1, keepdims=True) + EPS) * nw_ref[:, gsl]
        o_ref[:, gsl] = y.astype(o_ref.dtype)
        xw = (xdt * to_end_x).astype(BF16)
        st_ref[g] = ea_x[L - 1:L, :] * state + _dot(bg.T.astype(BF16), xw)


def _ssd(z, xbc, dt_raw, p, bsz, seq):
    m, inner = z.shape
    nt = seq // CHUNK
    conv_ch = xbc.shape[1]
    sps = SEQ_PER_STEP if bsz % SEQ_PER_STEP == 0 else 1
    vec = lambda n: pl.BlockSpec((1, n), lambda b, j: (0, 0))
    tile = lambda n: pl.BlockSpec((sps, CHUNK, n), lambda b, j: (b, j, 0))
    out = pl.pallas_call(
        _ssd_kernel,
        out_shape=jax.ShapeDtypeStruct((bsz, seq, inner), BF16),
        grid=(bsz // sps, nt),
        in_specs=[tile(inner), tile(conv_ch), tile(LANES),
                  pl.BlockSpec((CONV_WIDTH, conv_ch), lambda b, j: (0, 0)), vec(conv_ch),
                  vec(LANES), vec(LANES), vec(inner), vec(inner)],
        out_specs=tile(inner),
        scratch_shapes=[pltpu.VMEM((sps, SUBLANES, conv_ch), F32),
                        pltpu.VMEM((sps, CHUNK, conv_ch), F32),
                        pltpu.VMEM((sps, SSD_GROUPS, SSD_STATE, inner // SSD_GROUPS), F32)],
        compiler_params=_cparams(("parallel", "arbitrary")),
    )(z.reshape(bsz, seq, inner), xbc.reshape(bsz, seq, conv_ch), dt_raw.reshape(bsz, seq, LANES),
      p["conv_w"], p["conv_b"], p["dt_bias"], p["a_log"], p["d_skip"], p["norm"])
    return out.reshape(m, inner)


def _router_kernel(*refs, n_in):
    y_refs, w_refs = refs[:n_in], refs[n_in:2 * n_in]
    (h_ref, gm_ref, g_ref, sh_ref, sc_ref, wr_ref, br_ref,
     hmid_ref, up_ref, topi_ref, gate_ref, rank_ref, cnt_ref, carry_ref) = refs[2 * n_in:]

    @pl.when(pl.program_id(0) == 0)
    def _():
        carry_ref[...] = jnp.zeros_like(carry_ref)

    tm = h_ref.shape[0]
    acc = _dot(y_refs[0][...], w_refs[0][...])
    for y_ref, w_ref in zip(y_refs[1:], w_refs[1:]):
        acc = acc + _dot(y_ref[...], w_ref[...])
    hmid = h_ref[...] + gm_ref[0] * acc
    hmid_ref[...] = hmid
    u = _norm_mod(hmid, g_ref[...], sh_ref[0], sc_ref[0])
    up_ref[...] = _pack_pairs(u)
    u_hi = u.astype(BF16)
    u_lo = (u - u_hi.astype(F32)).astype(BF16)
    logits = (_dot(u_hi, wr_ref[0]) + (_dot(u_lo, wr_ref[0]) + _dot(u_hi, wr_ref[1]))
              + br_ref[...])
    lt = jnp.concatenate([logits[r0:r0 + LANES].T for r0 in range(0, tm, LANES)], axis=1)
    l = lt[:N_EXPERTS]
    e_iota = lax.broadcasted_iota(I32, (N_EXPERTS, tm), 0).astype(F32)
    vals, idxs, hots = [], [], []
    for _ in range(TOP_K):
        mx = jnp.max(l, axis=0, keepdims=True)
        idx = jnp.min(jnp.where(l == mx, e_iota, float(N_EXPERTS)), axis=0, keepdims=True)
        hot = e_iota == idx
        l = jnp.where(hot, -jnp.inf, l)
        vals.append(mx)
        idxs.append(idx)
        hots.append(hot)
    exps = [jnp.exp(v - vals[0]) for v in vals]
    den = exps[0] + exps[1] + exps[2] + exps[3]
    gate_ref[...] = jnp.concatenate([e / den for e in exps], axis=0)
    topi_ref[...] = jnp.concatenate(idxs, axis=0).astype(I32)

    sel = jnp.zeros((N_EXPERTS, tm), F32)
    for hot in hots:
        sel = jnp.where(hot, 1.0, sel)
    r_i = lax.broadcasted_iota(I32, (tm, tm), 0)
    c_i = lax.broadcasted_iota(I32, (tm, tm), 1)
    before = (r_i < c_i).astype(BF16)
    carry = carry_ref[:, 0:1]
    cum = _dot(sel.astype(BF16), before) + carry
    rank_ref[...] = jnp.concatenate(
        [jnp.sum(jnp.where(hot, cum, 0.0), axis=0, keepdims=True) for hot in hots], axis=0).astype(I32)
    total = carry + jnp.sum(sel, axis=1, keepdims=True)
    carry_ref[...] = jnp.broadcast_to(total, carry_ref.shape)
    cnt_ref[...] = jnp.broadcast_to(total, cnt_ref.shape)


def _router(ys, w_out, h, g_m, g, shift, scale, wr, br, seq, tm):
    m, d = h.shape
    tiles_per_seq = seq // tm
    bmap = lambda i: (i // tiles_per_seq, 0, 0)
    row4 = lambda: pl.BlockSpec((TOP_K, tm), lambda i: (0, i))
    in_specs, args, k0 = [], [], 0
    for y in ys:
        in_specs.append(pl.BlockSpec((tm, y.shape[1]), lambda i: (i, 0)))
        args.append(y)
    for y in ys:
        kk = y.shape[1]
        in_specs.append(pl.BlockSpec((kk, d), lambda i, kb=k0 // kk: (kb, 0)))
        args.append(w_out)
        k0 += kk
    in_specs += [pl.BlockSpec((tm, d), lambda i: (i, 0)),
                 pl.BlockSpec((1, 1, d), bmap),
                 pl.BlockSpec((1, d), lambda i: (0, 0)),
                 pl.BlockSpec((1, 1, d), bmap), pl.BlockSpec((1, 1, d), bmap),
                 pl.BlockSpec((2, d, LANES), lambda i: (0, 0, 0)),
                 pl.BlockSpec((1, LANES), lambda i: (0, 0))]
    args += [h, g_m, g.reshape(1, d), shift, scale, wr, br]
    return pl.pallas_call(
        functools.partial(_router_kernel, n_in=len(ys)),
        out_shape=[jax.ShapeDtypeStruct((m, d), F32),
                   jax.ShapeDtypeStruct((m, d // 2), I32),
                   jax.ShapeDtypeStruct((TOP_K, m), I32),
                   jax.ShapeDtypeStruct((TOP_K, m), F32),
                   jax.ShapeDtypeStruct((TOP_K, m), I32),
                   jax.ShapeDtypeStruct((N_EXPERTS, LANES), F32)],
        grid=(m // tm,),
        in_specs=in_specs,
        out_specs=[pl.BlockSpec((tm, d), lambda i: (i, 0)),
                   pl.BlockSpec((tm, d // 2), lambda i: (i, 0)), row4(), row4(), row4(),
                   pl.BlockSpec((N_EXPERTS, LANES), lambda i: (0, 0))],
        scratch_shapes=[pltpu.VMEM((N_EXPERTS, LANES), F32)],
        compiler_params=_cparams(("arbitrary",)),
    )(*args)


def _dest_kernel(ps_ref, topi_ref, rank_ref, o_ref):
    topi = topi_ref[...]
    acc = rank_ref[...]
    for e in range(N_EXPERTS):
        acc = acc + jnp.where(topi == e, ps_ref[e], 0)
    o_ref[...] = acc


def _dest_rows(pad_start, topi, rank, tw):
    k, m = topi.shape
    blk = lambda: pl.BlockSpec((k, tw), lambda i, ps: (0, i))
    return pl.pallas_call(
        _dest_kernel,
        out_shape=jax.ShapeDtypeStruct((k, m), I32),
        grid_spec=pltpu.PrefetchScalarGridSpec(
            num_scalar_prefetch=1, grid=(m // tw,), in_specs=[blk(), blk()], out_specs=blk()),
        compiler_params=_cparams(("parallel",)),
    )(pad_start, topi, rank)


def _dispatch_kernel(dest_ref, up_ref, xs_ref, buf_ref, lsem, ssem):
    i = pl.program_id(0)
    n = pl.num_programs(0)
    tm = buf_ref.shape[1]
    slot = i % 2

    def load(tile, s):
        return pltpu.make_async_copy(up_ref.at[pl.ds(tile * tm, tm)], buf_ref.at[s], lsem.at[s])

    def wait_rows(s):
        for _ in range(TOP_K):
            pltpu.make_async_copy(buf_ref.at[s], xs_ref.at[pl.ds(0, tm)], ssem.at[s]).wait()

    pl.when(i == 0)(lambda: load(0, 0).start())
    pl.when(i > 0)(lambda: wait_rows(1 - slot))
    pl.when(i + 1 < n)(lambda: load(i + 1, 1 - slot).start())
    load(i, slot).wait()

    def issue(t, c):
        for k in range(TOP_K):
            pltpu.make_async_copy(buf_ref.at[slot, pl.ds(t, 1)],
                                  xs_ref.at[pl.ds(dest_ref[k * tm + t], 1)], ssem.at[slot]).start()
        return c

    lax.fori_loop(0, tm, issue, 0)
    pl.when(i == n - 1)(lambda: wait_rows(slot))


def _dispatch(dest_tiles, up, n_rows, tm):
    m, wp = up.shape
    return pl.pallas_call(
        _dispatch_kernel,
        out_shape=jax.ShapeDtypeStruct((n_rows, wp), I32),
        grid=(m // tm,),
        in_specs=[pl.BlockSpec((TOP_K * tm,), lambda i: (i,), memory_space=pltpu.SMEM),
                  pl.BlockSpec(memory_space=pl.ANY)],
        out_specs=pl.BlockSpec(memory_space=pl.ANY),
        scratch_shapes=[pltpu.VMEM((2, tm, wp), I32), pltpu.SemaphoreType.DMA((2,)),
                        pltpu.SemaphoreType.DMA((2,))],
        compiler_params=_cparams(("arbitrary",), has_side_effects=True),
    )(dest_tiles, up)


def _combine_kernel(dcur_ref, dnext_ref, gate_ref, h_ref, gf_ref, fn_ref, y_ref, o_ref, buf_ref, sem,
                    *, final):
    i = pl.program_id(0)
    tm = h_ref.shape[0]
    slot = i % 2

    def issue(d_ref, s):
        def body(t, c):
            for k in range(TOP_K):
                pltpu.make_async_copy(y_ref.at[pl.ds(d_ref[k * tm + t], 1)],
                                      buf_ref.at[s, k, pl.ds(t, 1)], sem.at[s]).start()
            return c
        lax.fori_loop(0, tm, body, 0)

    pl.when(i == 0)(lambda: issue(dcur_ref, slot))
    pl.when(i + 1 < pl.num_programs(0))(lambda: issue(dnext_ref, 1 - slot))
    pltpu.make_async_copy(buf_ref.at[slot], buf_ref.at[slot], sem.at[slot]).wait()
    acc = gate_ref[:, 0:1] * _unpack_pairs(buf_ref[slot, 0])
    for k in range(1, TOP_K):
        acc = acc + gate_ref[:, k:k + 1] * _unpack_pairs(buf_ref[slot, k])
    hn = h_ref[...] + gf_ref[0] * acc
    if final:
        hn = hn * lax.rsqrt(jnp.mean(hn * hn, axis=-1, keepdims=True) + EPS) * fn_ref[...]
    o_ref[...] = hn


def _combine(dest_tiles, gates_col, h, gf, fnorm, y, seq, tm, final):
    m, d = h.shape
    tiles_per_seq = seq // tm
    nt = m // tm
    return pl.pallas_call(
        functools.partial(_combine_kernel, final=final),
        out_shape=jax.ShapeDtypeStruct((m, d), F32),
        grid=(nt,),
        in_specs=[pl.BlockSpec((TOP_K * tm,), lambda i: (i,), memory_space=pltpu.SMEM),
                  pl.BlockSpec((TOP_K * tm,), lambda i: (jnp.minimum(i + 1, nt - 1),),
                               memory_space=pltpu.SMEM),
                  pl.BlockSpec((tm, TOP_K), lambda i: (i, 0)),
                  pl.BlockSpec((tm, d), lambda i: (i, 0)),
                  pl.BlockSpec((1, 1, d), lambda i: (i // tiles_per_seq, 0, 0)),
                  pl.BlockSpec((1, d), lambda i: (0, 0)),
                  pl.BlockSpec(memory_space=pl.ANY)],
        out_specs=pl.BlockSpec((tm, d), lambda i: (i, 0)),
        scratch_shapes=[pltpu.VMEM((2, TOP_K, tm, d // 2), I32), pltpu.SemaphoreType.DMA((2,))],
        compiler_params=_cparams(("arbitrary",)),
    )(dest_tiles, dest_tiles, gates_col, h, gf, fnorm.reshape(1, d), y)


SC_CORES = 2
SC_SUBCORES = 16
SC_ROWS = 128


def _sc_gather_rows(table, idx):
    b = idx.shape[0]
    w = table.shape[1]
    workers = SC_CORES * SC_SUBCORES
    per_w = b // workers
    assert per_w * workers == b and per_w % SC_ROWS == 0
    mesh = plsc.VectorSubcoreMesh(core_axis_name="c", subcore_axis_name="s")

    @functools.partial(
        pl.kernel, mesh=mesh, out_type=jax.ShapeDtypeStruct((b, w), I32),
        scratch_types=[pltpu.VMEM((SC_ROWS,), I32), pltpu.VMEM((SC_ROWS, w), I32),
                       pltpu.SemaphoreType.DMA])
    def gather(table_hbm, idx_hbm, out_hbm, idx_v, rows_v, sem):
        base = (lax.axis_index("s") * SC_CORES + lax.axis_index("c")) * per_w

        @pl.loop(0, per_w // SC_ROWS)
        def _(c):
            off = base + c * SC_ROWS
            pltpu.sync_copy(idx_hbm.at[pl.ds(off, SC_ROWS)], idx_v)
            pltpu.async_copy(table_hbm.at[idx_v], rows_v, sem).wait()
            pltpu.sync_copy(rows_v, out_hbm.at[pl.ds(off, SC_ROWS)])

    return gather(table, idx)


def _sc_scatter_rows(rows, dest, n_rows):
    m, w = rows.shape
    kk = dest.shape[0]
    workers = SC_CORES * SC_SUBCORES
    per_w = m // workers
    assert per_w * workers == m and per_w % SC_ROWS == 0
    mesh = plsc.VectorSubcoreMesh(core_axis_name="c", subcore_axis_name="s")

    @functools.partial(
        pl.kernel, mesh=mesh, out_type=jax.ShapeDtypeStruct((n_rows, w), I32),
        scratch_types=[pltpu.VMEM((SC_ROWS,), I32), pltpu.VMEM((SC_ROWS, w), I32),
                       pltpu.SemaphoreType.DMA])
    def scatter(rows_hbm, dest_hbm, out_hbm, idx_v, rows_v, sem):
        base = (lax.axis_index("s") * SC_CORES + lax.axis_index("c")) * per_w

        @pl.loop(0, per_w // SC_ROWS)
        def _(c):
            off = base + c * SC_ROWS
            pltpu.sync_copy(rows_hbm.at[pl.ds(off, SC_ROWS)], rows_v)
            for k in range(kk):
                pltpu.sync_copy(dest_hbm.at[pl.ds(k * m + off, SC_ROWS)], idx_v)
                pltpu.async_copy(rows_v, out_hbm.at[idx_v], sem).wait()

    return scatter(rows, dest.reshape(-1))


def _combine_dense_kernel(y_ref, gate_ref, h_ref, gf_ref, fn_ref, o_ref, *, final):
    acc = gate_ref[:, 0:1] * _unpack_pairs(y_ref[0])
    for k in range(1, TOP_K):
        acc = acc + gate_ref[:, k:k + 1] * _unpack_pairs(y_ref[k])
    hn = h_ref[...] + gf_ref[0] * acc
    if final:
        hn = hn * lax.rsqrt(jnp.mean(hn * hn, axis=-1, keepdims=True) + EPS) * fn_ref[...]
    o_ref[...] = hn


def _combine_dense(y4, gates_col, h, gf, fnorm, seq, tm, final):
    m, d = h.shape
    tiles_per_seq = seq // tm
    return pl.pallas_call(
        functools.partial(_combine_dense_kernel, final=final),
        out_shape=jax.ShapeDtypeStruct((m, d), F32),
        grid=(m // tm,),
        in_specs=[pl.BlockSpec((TOP_K, tm, d // 2), lambda i: (0, i, 0)),
                  pl.BlockSpec((tm, TOP_K), lambda i: (i, 0)),
                  pl.BlockSpec((tm, d), lambda i: (i, 0)),
                  pl.BlockSpec((1, 1, d), lambda i: (i // tiles_per_seq, 0, 0)),
                  pl.BlockSpec((1, d), lambda i: (0, 0))],
        out_specs=pl.BlockSpec((tm, d), lambda i: (i, 0)),
        compiler_params=_cparams(("parallel",)),
    )(y4, gates_col, h, gf, fnorm.reshape(1, d))


def _expert_kernel(be_ref, nb_ref, first_ref, x_ref, wgu_ref, bgu_ref, wd_ref, bd_ref, y_ref,
                   wgu_bf, wd_bf):
    i = pl.program_id(0)

    @pl.when(i < nb_ref[0])
    def _():
        dff = wd_bf.shape[0]

        @pl.when(first_ref[i] == 1)
        def _():
            rows = 64

            def cast(r, c):
                r0 = pl.multiple_of(r * rows, rows)
                wgu_bf[pl.ds(r0, rows), :] = wgu_ref[0, 0, pl.ds(r0, rows), :].astype(BF16)
                wd_bf[pl.ds(r0, rows), :] = wd_ref[0, 0, pl.ds(r0, rows), :].astype(BF16)
                return c

            lax.fori_loop(0, dff // rows, cast, 0)

        x = _unpack_pairs(x_ref[...]).astype(BF16)
        hb = _dot(x, wgu_bf[...]) + bgu_ref[0, 0]
        h_glu = jnp.minimum(hb[:, :dff], SWIGLU_LIMIT)
        h_lin = jnp.clip(hb[:, dff:], -SWIGLU_LIMIT, SWIGLU_LIMIT)
        act = h_glu * jax.nn.sigmoid(SWIGLU_ALPHA * h_glu) * (h_lin + 1.0)
        y_ref[...] = _pack_pairs(_dot(act.astype(BF16), wd_bf[...]) + bd_ref[0, 0])


def _experts(block_e, n_used, first, xs, wgu, bgu, wd, bd, layer):
    n_rows, wp = xs.shape
    _, ne, d, ff2 = wgu.shape
    assert d == ff2 // 2
    nblk = n_rows // EXPERT_BLOCK

    def xmap(i, be, nb, fi):
        return (jnp.minimum(i, nb[0] - 1), 0)

    emap = lambda i, be, nb, fi: (layer, be[i], 0, 0)
    grid_spec = pltpu.PrefetchScalarGridSpec(
        num_scalar_prefetch=3, grid=(nblk,),
        in_specs=[pl.BlockSpec((EXPERT_BLOCK, wp), xmap),
                  pl.BlockSpec((1, 1, d, ff2), emap), pl.BlockSpec((1, 1, 1, ff2), emap),
                  pl.BlockSpec((1, 1, ff2 // 2, d), emap), pl.BlockSpec((1, 1, 1, d), emap)],
        out_specs=pl.BlockSpec((EXPERT_BLOCK, wp), xmap),
        scratch_shapes=[pltpu.VMEM((d, ff2), BF16), pltpu.VMEM((ff2 // 2, d), BF16)])
    depth = wgu.shape[0]
    return pl.pallas_call(
        _expert_kernel,
        out_shape=jax.ShapeDtypeStruct((n_rows, wp), I32),
        grid_spec=grid_spec,
        compiler_params=_cparams(("arbitrary",)),
    )(block_e, n_used, first, xs, wgu, bgu.reshape(depth, ne, 1, ff2), wd, bd.reshape(depth, ne, 1, d))


def _moe(ys, w_out, g_m, h, g, shift, scale, gf, fnorm, wr, br, wgu, bgu, wd, bd, layer, seq, final):
    m, d = h.shape
    tm = 512
    wr_p = jnp.zeros((d, LANES), F32).at[:, :N_EXPERTS].set(wr)
    br_p = jnp.zeros((1, LANES), F32).at[0, :N_EXPERTS].set(br)
    wr_hi = wr_p.astype(BF16)
    wr_split = jnp.stack([wr_hi, (wr_p - wr_hi.astype(F32)).astype(BF16)])
    h, up, topi, gates, rank, cnt = _router(ys, w_out, h, g_m, g, shift, scale, wr_split, br_p, seq, tm)

    counts = cnt[:, 0].astype(I32)
    padded = (counts + EXPERT_BLOCK - 1) // EXPERT_BLOCK * EXPERT_BLOCK
    pad_end = jnp.cumsum(padded)
    pad_start = pad_end - padded
    nblk = m * TOP_K // EXPERT_BLOCK + N_EXPERTS
    n_rows = nblk * EXPERT_BLOCK
    n_used = pad_end[-1:] // EXPERT_BLOCK
    blk = jnp.arange(nblk, dtype=I32)
    blk_c = jnp.minimum(blk, n_used - 1)
    block_e = jnp.minimum(jnp.sum(blk_c[:, None] * EXPERT_BLOCK >= pad_end[None, :], axis=1),
                          N_EXPERTS - 1).astype(I32)
    first = jnp.concatenate([jnp.ones((1,), I32), (block_e[1:] != block_e[:-1]).astype(I32)])

    dest = _dest_rows(pad_start, topi, rank, min(m, 8192))
    xs = _sc_scatter_rows(up, dest, n_rows)
    y = _experts(block_e, n_used.astype(I32), first, xs, wgu, bgu, wd, bd, layer)
    y4 = _sc_gather_rows(y, dest.reshape(-1)).reshape(TOP_K, m, d // 2)
    return _combine_dense(y4, gates.T, h, gf, fnorm, seq, tm, final)


def _block_diag(w, group):
    nb, b, _ = w.shape
    per = group // b
    wg = w.reshape(nb // per, per, b, b)
    dense = jnp.einsum("gnde,nm->gndme", wg, jnp.eye(per, dtype=w.dtype))
    return dense.reshape(nb // per, group, group)


def kernel(x, c, mod_w, mod_b, norm_mix, norm_ffn, ev_w_in, ev_lru_conv_w, ev_lru_conv_b, ev_lru_w_r, ev_lru_b_r, ev_lru_w_i, ev_lru_b_i, ev_lru_lambda, ev_ml_conv_w, ev_ml_conv_b, ev_ml_w_q, ev_ml_w_k, ev_ml_w_v, ev_ml_w_ig, ev_ml_b_ig, ev_ml_w_fg, ev_ml_b_fg, ev_ml_norm, ev_ml_skip, ev_w_out, od_w_in, od_conv_w, od_conv_b, od_dt_bias, od_a_log, od_d, od_norm, od_w_out, moe_router_w, moe_router_b, moe_w_gu, moe_b_gu, moe_w_down, moe_b_down, final_norm):
    bsz, seq, d = x.shape
    depth = mod_w.shape[0]
    m = bsz * seq
    mod = _modulation(c, mod_w, mod_b)
    h = x.reshape(m, d).astype(F32)
    for layer in range(depth):
        sh_m, sc_m, g_m, sh_f, sc_f, g_f = (mod[layer, i] for i in range(6))
        j = layer // 2
        if layer % 2 == 0:
            w = ev_lru_lambda.shape[1]
            w_in = ev_w_in[j].astype(BF16)
            proj = _inproj(h, norm_mix[layer], sh_m, sc_m, w_in, None, [w_in.shape[1]], seq, 512)[0]
            lru_p = dict(conv_w=ev_lru_conv_w[j], conv_b=ev_lru_conv_b[j].reshape(1, w),
                         w_r=ev_lru_w_r[j].astype(BF16), b_r=ev_lru_b_r[j].reshape(1, w),
                         w_i=ev_lru_w_i[j].astype(BF16), b_i=ev_lru_b_i[j].reshape(1, w),
                         lam=ev_lru_lambda[j].reshape(1, w))
            ya = _lru(proj, lru_p, bsz, seq, 256)
            wg = jnp.zeros((3 * w, LANES), F32)
            wg = wg.at[:, :ML_HEADS].set(ev_ml_w_ig[j]).at[:, ML_HEADS:2 * ML_HEADS].set(ev_ml_w_fg[j])
            bg = jnp.zeros((1, LANES), F32)
            bg = bg.at[0, :ML_HEADS].set(ev_ml_b_ig[j]).at[0, ML_HEADS:2 * ML_HEADS].set(ev_ml_b_fg[j])
            ml_p = dict(conv_w=ev_ml_conv_w[j], conv_b=ev_ml_conv_b[j].reshape(1, w),
                        w_q=_block_diag(ev_ml_w_q[j], LANES).astype(BF16),
                        w_k=_block_diag(ev_ml_w_k[j], LANES).astype(BF16),
                        w_v=_block_diag(ev_ml_w_v[j], LANES).astype(BF16),
                        w_g=wg.astype(BF16), b_g=bg,
                        norm=ev_ml_norm[j].reshape(1, w), skip=ev_ml_skip[j].reshape(1, w))
            yb = _mlstm(proj, ml_p, bsz, seq)
            ys, w_out = [ya, yb], ev_w_out[j].astype(BF16)
        else:
            inner = od_norm.shape[1]
            heads = od_dt_bias.shape[1]
            conv_ch = od_conv_w.shape[2]
            w_in = od_w_in[j]
            wdt = jnp.zeros((d, LANES), F32).at[:, :heads].set(w_in[:, inner + conv_ch:])
            z, xbc, dt_raw = _inproj(h, norm_mix[layer], sh_m, sc_m, w_in[:, :inner + conv_ch].astype(BF16),
                                     wdt.astype(BF16), [inner, conv_ch], seq, 256)
            pad = lambda v: jnp.zeros((1, LANES), F32).at[0, :heads].set(v)
            ssd_p = dict(conv_w=od_conv_w[j], conv_b=od_conv_b[j].reshape(1, conv_ch),
                         dt_bias=pad(od_dt_bias[j]), a_log=pad(od_a_log[j]),
                         d_skip=jnp.repeat(od_d[j], SSD_HEAD_DIM).reshape(1, inner),
                         norm=od_norm[j].reshape(1, inner))
            y = _ssd(z, xbc, dt_raw, ssd_p, bsz, seq)
            ys, w_out = [y], od_w_out[j].astype(BF16)
        h = _moe(ys, w_out, g_m, h, norm_ffn[layer], sh_f, sc_f, g_f, final_norm,
                 moe_router_w[layer], moe_router_b[layer],
                 moe_w_gu, moe_b_gu, moe_w_down, moe_b_down, layer, seq, final=(layer == depth - 1))
    return h.reshape(bsz, seq, d)
```

```python
import functools

import jax
import jax.numpy as jnp
from jax import lax
from jax.experimental import pallas as pl
from jax.experimental.pallas import tpu as pltpu
from jax.experimental.pallas import tpu_sc as plsc

F32 = jnp.float32
BF16 = jnp.bfloat16
I32 = jnp.int32
HIGHEST = lax.Precision.HIGHEST

EPS = 1e-6
CONV_WIDTH = 4
LANES = 128
SUBLANES = 8
LRU_HEADS = 8
LRU_C = 8.0
ML_HEADS = 8
ML_QKV_BLOCK = 4
CHUNK = 128
SSD_HEAD_DIM = 64
SSD_GROUPS = 8
SSD_STATE = 128
N_EXPERTS = 32
TOP_K = 4
SWIGLU_ALPHA = 1.702
SWIGLU_LIMIT = 7.0
EXPERT_BLOCK = 512
SEQ_PER_STEP = 2
CONV_COLS = 512
VMEM_LIMIT = 56 * 1024 * 1024


def _cparams(sem, **kw):
    return pltpu.CompilerParams(dimension_semantics=sem, vmem_limit_bytes=VMEM_LIMIT, **kw)


def _silu(x):
    half = 0.5 * x
    return half + half * jnp.tanh(half)


def _log_sigmoid(x):
    return jnp.minimum(x, 0.0) - jnp.log1p(jnp.exp(-jnp.abs(x)))


def _softplus(x):
    return jnp.maximum(x, 0.0) + jnp.log1p(jnp.exp(-jnp.abs(x)))


def _dot(a, b, **kw):
    return jnp.dot(a, b, preferred_element_type=F32, **kw)


def _dot_nt(a, b):
    return lax.dot_general(a, b, (((1,), (1,)), ((), ())), preferred_element_type=F32)


def _pack_pairs(x):
    w = x.shape[1] // 2
    lo = lax.bitcast_convert_type(x[:, :w].astype(BF16).astype(F32), I32)
    hi = lax.bitcast_convert_type(x[:, w:].astype(BF16).astype(F32), I32)
    return lax.shift_right_logical(lo, 16) | (hi & jnp.int32(-65536))


def _unpack_pairs(p):
    lo = lax.bitcast_convert_type(lax.shift_left(p, 16), F32)
    hi = lax.bitcast_convert_type(p & jnp.int32(-65536), F32)
    return jnp.concatenate([lo, hi], axis=1)


def _norm_mod(h, g, shift, scale):
    y = h * lax.rsqrt(jnp.mean(h * h, axis=-1, keepdims=True) + EPS)
    return (y * g) * (1.0 + scale) + shift


def _causal_conv(x, tail_ref, w_ref, b_ref, sl):
    t = x.shape[0]
    tail = tail_ref[:, sl]
    row8 = lax.broadcasted_iota(I32, tail.shape, 0)
    out = b_ref[:, sl] + x * w_ref[CONV_WIDTH - 1:CONV_WIDTH, sl]
    for k in range(1, CONV_WIDTH):
        xs = pltpu.roll(x, k, axis=0)
        first = jnp.where(row8 < k, pltpu.roll(tail, k, axis=0), xs[:SUBLANES])
        xs = jnp.concatenate([first, xs[SUBLANES:]], axis=0)
        out = out + xs * w_ref[CONV_WIDTH - 1 - k:CONV_WIDTH - k, sl]
    tail_ref[:, sl] = x[t - SUBLANES:]
    return out


def _shift_matrix(t):
    r = lax.broadcasted_iota(I32, ((CONV_WIDTH - 1) * t, t), 0)
    c = lax.broadcasted_iota(I32, ((CONV_WIDTH - 1) * t, t), 1)
    src = (r & (t - 1)) - lax.shift_right_logical(r, t.bit_length() - 1) - 1
    return (src == c).astype(BF16)


def _causal_conv_shifted(x, shifted, tail_ref, w_ref, b_ref, sl):
    t = x.shape[0]
    tail = tail_ref[:, sl]
    row8 = lax.broadcasted_iota(I32, tail.shape, 0)
    out = b_ref[:, sl] + x * w_ref[CONV_WIDTH - 1:CONV_WIDTH, sl]
    head = jnp.zeros_like(tail)
    for k in range(1, CONV_WIDTH):
        wk = w_ref[CONV_WIDTH - 1 - k:CONV_WIDTH - k, sl]
        out = out + shifted[(k - 1) * t:k * t] * wk
        head = head + jnp.where(row8 < k, pltpu.roll(tail, k, axis=0), 0.0) * wk
    tail_ref[:, sl] = x[t - SUBLANES:]
    return jnp.concatenate([out[:SUBLANES] + head, out[SUBLANES:]], axis=0)


def _mod_kernel(c_ref, w_ref, b_ref, o_ref):
    cond = _silu(c_ref[...])
    o_ref[0, 0] = _dot(cond, w_ref[0], precision=HIGHEST) + b_ref[0, 0]


def _modulation(c, mod_w, mod_b):
    depth, d, _ = mod_w.shape
    bsz = c.shape[0]
    out = pl.pallas_call(
        _mod_kernel,
        out_shape=jax.ShapeDtypeStruct((depth, 6, bsz, d), F32),
        grid=(depth, 6),
        in_specs=[pl.BlockSpec((bsz, d), lambda l, j: (0, 0)),
                  pl.BlockSpec((1, d, d), lambda l, j: (l, 0, j)),
                  pl.BlockSpec((1, 1, 1, d), lambda l, j: (l, j, 0, 0))],
        out_specs=pl.BlockSpec((1, 1, bsz, d), lambda l, j: (l, j, 0, 0)),
        compiler_params=_cparams(("parallel", "parallel")),
    )(c.astype(F32), mod_w, mod_b.reshape(depth, 6, 1, d))
    return out.reshape(depth, 6, bsz, 1, d)


def _inproj_kernel(h_ref, g_ref, sh_ref, sc_ref, w_ref, *rest, n_chunk, with_dt):
    if with_dt:
        wdt_ref, *o_refs, odt_ref = rest
    else:
        o_refs = rest
    u = _norm_mod(h_ref[...], g_ref[...], sh_ref[0], sc_ref[0]).astype(BF16)
    off = 0
    for o_ref in o_refs:
        for n0 in range(0, o_ref.shape[1], n_chunk):
            o_ref[:, n0:n0 + n_chunk] = _dot(u, w_ref[:, off + n0:off + n0 + n_chunk]).astype(o_ref.dtype)
        off += o_ref.shape[1]
    if with_dt:
        odt_ref[...] = _dot(u, wdt_ref[...])


def _inproj(h, g, shift, scale, w, wdt, splits, seq, tm):
    m, d = h.shape
    n = w.shape[1]
    assert sum(splits) == n
    tiles_per_seq = seq // tm
    bmap = lambda i: (i // tiles_per_seq, 0, 0)
    in_specs = [pl.BlockSpec((tm, d), lambda i: (i, 0)),
                pl.BlockSpec((1, d), lambda i: (0, 0)),
                pl.BlockSpec((1, 1, d), bmap),
                pl.BlockSpec((1, 1, d), bmap),
                pl.BlockSpec((d, n), lambda i: (0, 0), pipeline_mode=pl.Buffered(1))]
    out_shape = [jax.ShapeDtypeStruct((m, s), BF16) for s in splits]
    out_specs = [pl.BlockSpec((tm, s), lambda i: (i, 0)) for s in splits]
    args = [h, g.reshape(1, d), shift, scale, w]
    if wdt is not None:
        in_specs.append(pl.BlockSpec((d, LANES), lambda i: (0, 0)))
        out_shape.append(jax.ShapeDtypeStruct((m, LANES), F32))
        out_specs.append(pl.BlockSpec((tm, LANES), lambda i: (i, 0)))
        args.append(wdt)
    return pl.pallas_call(
        functools.partial(_inproj_kernel, n_chunk=1024, with_dt=wdt is not None),
        out_shape=out_shape, grid=(m // tm,), in_specs=in_specs, out_specs=out_specs,
        compiler_params=_cparams(("parallel",)),
    )(*args)


def _lru_kernel(xa_ref, ga_ref, cw_ref, cb_ref, wr_ref, br_ref, wi_ref, bi_ref, lam_ref,
                o_ref, tail_ref, hc_ref):
    @pl.when(pl.program_id(1) == 0)
    def _():
        tail_ref[...] = jnp.zeros_like(tail_ref)
        hc_ref[...] = jnp.zeros_like(hc_ref)

    t = xa_ref.shape[0]
    row_in_group = lax.broadcasted_iota(I32, (t, LANES), 0) % SUBLANES
    steps = [s for s in (1, 2, 4) if s < SUBLANES]
    masks = [row_in_group >= s for s in steps]
    for hh in range(LRU_HEADS):
        sl = slice(hh * LANES, (hh + 1) * LANES)
        xc = _causal_conv(xa_ref[:, sl].astype(F32), tail_ref, cw_ref, cb_ref, sl)
        xcb = xc.astype(BF16)
        r = jax.nn.sigmoid(_dot(xcb, wr_ref[hh]) + br_ref[:, sl])
        i = jax.nn.sigmoid(_dot(xcb, wi_ref[hh]) + bi_ref[:, sl])
        log_a = LRU_C * r * _log_sigmoid(lam_ref[:, sl])
        a = jnp.exp(log_a)
        th = jnp.tanh(log_a)
        n2 = -2.0 * th
        root = jnp.where(n2 > 0.0, n2 * lax.rsqrt(n2), 0.0)
        u = (root * lax.rsqrt(1.0 - th)) * (i * xc)
        for s, m in zip(steps, masks):
            u = jnp.where(m, u + a * pltpu.roll(u, s, axis=0), u)
            a = jnp.where(m, a * pltpu.roll(a, s, axis=0), a)
        carry = hc_ref[:, sl]
        groups = []
        for r0 in range(0, t, SUBLANES):
            hg = u[r0:r0 + SUBLANES] + a[r0:r0 + SUBLANES] * carry
            carry = hg[SUBLANES - 1:SUBLANES]
            groups.append(hg)
        hc_ref[:, sl] = carry
        h = jnp.concatenate(groups, axis=0)
        ga = ga_ref[:, sl].astype(F32)
        o_ref[:, sl] = (h * jax.nn.gelu(ga, approximate=True)).astype(o_ref.dtype)


def _lru(proj, p, bsz, seq, tm):
    m = proj.shape[0]
    w = LRU_HEADS * LANES
    nt = seq // tm
    vec = lambda: pl.BlockSpec((1, w), lambda b, j: (0, 0))
    return pl.pallas_call(
        _lru_kernel,
        out_shape=jax.ShapeDtypeStruct((m, w), BF16),
        grid=(bsz, nt),
        in_specs=[pl.BlockSpec((tm, w), lambda b, j: (b * nt + j, 0)),
                  pl.BlockSpec((tm, w), lambda b, j: (b * nt + j, 1)),
                  pl.BlockSpec((CONV_WIDTH, w), lambda b, j: (0, 0)), vec(),
                  pl.BlockSpec((LRU_HEADS, LANES, LANES), lambda b, j: (0, 0, 0)), vec(),
                  pl.BlockSpec((LRU_HEADS, LANES, LANES), lambda b, j: (0, 0, 0)), vec(), vec()],
        out_specs=pl.BlockSpec((tm, w), lambda b, j: (b * nt + j, 0)),
        scratch_shapes=[pltpu.VMEM((SUBLANES, w), F32), pltpu.VMEM((1, w), F32)],
        compiler_params=_cparams(("parallel", "arbitrary")),
    )(proj, proj, p["conv_w"], p["conv_b"], p["w_r"], p["b_r"], p["w_i"], p["b_i"], p["lam"])


def _mlstm_kernel(xb_ref, zb_ref, cw_ref, cb_ref, wq_ref, wk_ref, wv_ref, wg_ref, bg_ref,
                  nw_ref, sk_ref, o_ref, tail_ref, qkv_ref, xc_ref, caug_ref, m_ref):
    @pl.when(pl.program_id(1) == 0)
    def _():
        tail_ref[...] = jnp.zeros_like(tail_ref)
        caug_ref[...] = jnp.zeros_like(caug_ref)
        m_ref[...] = jnp.full(m_ref.shape, -jnp.inf, F32)

    for s in range(xb_ref.shape[0]):
        for c0 in range(0, xb_ref.shape[1], CHUNK):
            rows = pl.ds(c0, CHUNK)
            _mlstm_chunk(xb_ref.at[s, rows], zb_ref.at[s, rows], cw_ref, cb_ref, wq_ref, wk_ref, wv_ref,
                         wg_ref, bg_ref, nw_ref, sk_ref, o_ref.at[s, rows], tail_ref.at[s], qkv_ref.at[s],
                         xc_ref.at[s], caug_ref.at[s], m_ref.at[s])


def _mlstm_chunk(xb_ref, zb_ref, cw_ref, cb_ref, wq_ref, wk_ref, wv_ref, wg_ref, bg_ref,
                 nw_ref, sk_ref, o_ref, tail_ref, qkv_ref, xc_ref, caug_ref, m_ref):
    L = CHUNK
    width = ML_HEADS * LANES
    scale = LANES ** -0.5
    for hh in range(ML_HEADS):
        sl = slice(hh * LANES, (hh + 1) * LANES)
        xb = xb_ref[:, sl].astype(F32)
        xc = _silu(_causal_conv(xb, tail_ref, cw_ref, cb_ref, sl))
        xc_ref[:, sl] = xc
        xcb = xc.astype(BF16)
        qkv_ref[:, sl] = _dot(xcb, wq_ref[hh]).astype(BF16)
        qkv_ref[:, width + hh * LANES:width + (hh + 1) * LANES] = _dot(xcb, wk_ref[hh]).astype(BF16)
        qkv_ref[:, 2 * width + hh * LANES:2 * width + (hh + 1) * LANES] = (
            _dot(xb.astype(BF16), wv_ref[hh]).astype(BF16))

    gates = _dot(qkv_ref[...], wg_ref[...]) + bg_ref[...]
    rowi = lax.broadcasted_iota(I32, (L, L), 0)
    coli = lax.broadcasted_iota(I32, (L, L), 1)
    causal = rowi >= coli
    lf = jnp.where((coli >= ML_HEADS) & (coli < 2 * ML_HEADS), _log_sigmoid(gates), 0.0)
    tri = causal.astype(BF16)
    lf_hi = lf.astype(BF16)
    lf_mid = (lf - lf_hi.astype(F32)).astype(BF16)
    lf_lo = (lf - lf_hi.astype(F32) - lf_mid.astype(F32)).astype(BF16)
    gcum = _dot(tri, lf_hi) + (_dot(tri, lf_mid) + _dot(tri, lf_lo))
    x_col = jnp.where(coli < ML_HEADS, gates, gcum)
    x_row = x_col.T
    ones = jnp.ones((L, LANES), BF16)
    heads = range(ML_HEADS)
    hsl = [slice(hh * LANES, (hh + 1) * LANES) for hh in heads]

    qs = [qkv_ref[:, hsl[hh]] for hh in heads]
    ks = [qkv_ref[:, width + hh * LANES:width + (hh + 1) * LANES] for hh in heads]
    vaugs = [jnp.concatenate([qkv_ref[:, 2 * width + hh * LANES:2 * width + (hh + 1) * LANES], ones], axis=1)
             for hh in heads]
    scores = [_dot_nt(qs[hh], ks[hh]) * scale for hh in heads]
    ics = [jnp.broadcast_to(x_col[:, hh:hh + 1], (L, LANES)) for hh in heads]
    gcs = [jnp.broadcast_to(x_col[:, ML_HEADS + hh:ML_HEADS + hh + 1], (L, LANES)) for hh in heads]
    irs = [x_row[hh:hh + 1, :] for hh in heads]
    grs = [x_row[ML_HEADS + hh:ML_HEADS + hh + 1, :] for hh in heads]
    mps = [m_ref[hh] for hh in heads]
    dmats = [jnp.where(causal, gcs[hh] - grs[hh] + irs[hh], -jnp.inf) for hh in heads]
    m_inters = [mps[hh] + gcs[hh] for hh in heads]
    m_ts = [jnp.maximum(m_inters[hh], jnp.max(dmats[hh], axis=1, keepdims=True)) for hh in heads]
    qks = [(scores[hh] * jnp.exp(dmats[hh] - m_ts[hh])).astype(BF16) for hh in heads]
    caugs = [caug_ref[hh] for hh in heads]
    w_inters = [jnp.exp(m_inters[hh] - m_ts[hh]) for hh in heads]
    nds = [_dot(qks[hh], vaugs[hh])
           + jnp.concatenate([w_inters[hh], w_inters[hh]], axis=1) * _dot(qs[hh], caugs[hh].astype(BF16))
           for hh in heads]

    g_lasts = [gcs[hh][L - 1:L, :] for hh in heads]
    m_news = [jnp.maximum(mps[hh] + g_lasts[hh],
                          jnp.max(g_lasts[hh] - grs[hh] + irs[hh], axis=1, keepdims=True)) for hh in heads]
    for hh in heads:
        ws = jnp.exp(g_lasts[hh] - gcs[hh] + ics[hh] - m_news[hh])
        wc = jnp.exp(mps[hh] + g_lasts[hh] - m_news[hh])
        kw_t = (ks[hh].astype(F32) * (ws * scale)).T.astype(BF16)
        caug_ref[hh] = jnp.concatenate([wc, wc], axis=1) * caugs[hh] + _dot(kw_t, vaugs[hh])
        m_ref[hh] = m_news[hh]

    hvals = [nds[hh][:, :LANES] / jnp.maximum(jnp.abs(nds[hh][:, LANES:]), jnp.exp(-m_ts[hh]))
             for hh in heads]
    mus = [jnp.mean(hvals[hh], axis=1, keepdims=True) for hh in heads]
    dvs = [hvals[hh] - mus[hh] for hh in heads]
    variances = [jnp.mean(dvs[hh] * dvs[hh], axis=1, keepdims=True) for hh in heads]
    for hh in heads:
        sl = hsl[hh]
        hn = dvs[hh] * lax.rsqrt(variances[hh] + EPS) * nw_ref[:, sl]
        zb = zb_ref[:, sl].astype(F32)
        o_ref[:, sl] = ((hn + sk_ref[:, sl] * xc_ref[:, sl]) * _silu(zb)).astype(o_ref.dtype)


def _mlstm(proj, p, bsz, seq):
    m = proj.shape[0]
    w = ML_HEADS * LANES
    sps = 1
    cps = 2 if seq % (2 * CHUNK) == 0 else 1
    nt = seq // (cps * CHUNK)
    vec = lambda: pl.BlockSpec((1, w), lambda b, j: (0, 0))
    blk = lambda: pl.BlockSpec((ML_HEADS, LANES, LANES), lambda b, j: (0, 0, 0))
    tile = lambda col: pl.BlockSpec((sps, cps * CHUNK, w), lambda b, j: (b, j, col))
    proj3 = proj.reshape(bsz, seq, proj.shape[1])
    out = pl.pallas_call(
        _mlstm_kernel,
        out_shape=jax.ShapeDtypeStruct((bsz, seq, w), BF16),
        grid=(bsz // sps, nt),
        in_specs=[tile(2), tile(3),
                  pl.BlockSpec((CONV_WIDTH, w), lambda b, j: (0, 0)), vec(),
                  blk(), blk(), blk(),
                  pl.BlockSpec((3 * w, LANES), lambda b, j: (0, 0)),
                  pl.BlockSpec((1, LANES), lambda b, j: (0, 0)),
                  vec(), vec()],
        out_specs=tile(0),
        scratch_shapes=[pltpu.VMEM((sps, SUBLANES, w), F32),
                        pltpu.VMEM((sps, CHUNK, 3 * w), BF16),
                        pltpu.VMEM((sps, CHUNK, w), F32),
                        pltpu.VMEM((sps, ML_HEADS, LANES, 2 * LANES), F32),
                        pltpu.VMEM((sps, ML_HEADS, 1, LANES), F32)],
        compiler_params=_cparams(("parallel", "arbitrary")),
    )(proj3, proj3, p["conv_w"], p["conv_b"], p["w_q"], p["w_k"], p["w_v"], p["w_g"], p["b_g"],
      p["norm"], p["skip"])
    return out.reshape(m, w)


def _ssd_kernel(z_ref, xbc_ref, dt_ref, cw_ref, cb_ref, dtb_ref, alog_ref, dsk_ref, nw_ref,
                o_ref, tail_ref, act_ref, st_ref):
    @pl.when(pl.program_id(1) == 0)
    def _():
        tail_ref[...] = jnp.zeros_like(tail_ref)
        st_ref[...] = jnp.zeros_like(st_ref)

    for s in range(z_ref.shape[0]):
        _ssd_chunk(z_ref.at[s], xbc_ref.at[s], dt_ref.at[s], cw_ref, cb_ref, dtb_ref, alog_ref, dsk_ref,
                   nw_ref, o_ref.at[s], tail_ref.at[s], act_ref.at[s], st_ref.at[s])


def _ssd_chunk(z_ref, xbc_ref, dt_ref, cw_ref, cb_ref, dtb_ref, alog_ref, dsk_ref, nw_ref,
               o_ref, tail_ref, act_ref, st_ref):
    L = CHUNK
    inner = o_ref.shape[1]
    gw = inner // SSD_GROUPS
    hpg = gw // SSD_HEAD_DIM
    b_off = inner
    c_off = inner + SSD_GROUPS * SSD_STATE
    shift = _shift_matrix(L)
    for c0 in range(0, xbc_ref.shape[1], CONV_COLS):
        shifted = _dot(shift, xbc_ref[:, c0:c0 + CONV_COLS])
        for l0 in range(0, CONV_COLS, LANES):
            sl = slice(c0 + l0, c0 + l0 + LANES)
            act_ref[:, sl] = _silu(_causal_conv_shifted(
                xbc_ref[:, sl].astype(F32), shifted[:, l0:l0 + LANES], tail_ref, cw_ref, cb_ref, sl))

    rowi = lax.broadcasted_iota(I32, (L, L), 0)
    coli = lax.broadcasted_iota(I32, (L, L), 1)
    causal = rowi >= coli
    dt = _softplus(dt_ref[...] + dtb_ref[...])
    da = dt * (-jnp.exp(alog_ref[...]))
    tri = causal.astype(BF16)
    da_hi = da.astype(BF16)
    da_mid = (da - da_hi.astype(F32)).astype(BF16)
    da_lo = (da - da_hi.astype(F32) - da_mid.astype(F32)).astype(BF16)
    a = _dot(tri, da_hi) + (_dot(tri, da_mid) + _dot(tri, da_lo))
    a_t = a.T
    hpt = LANES // SSD_HEAD_DIM
    lane = lax.broadcasted_iota(I32, (L, LANES), 1)

    def over_heads(tiles):
        cols = []
        for c0 in range(0, hpg, hpt):
            out = tiles[c0 + hpt - 1]
            for j in range(hpt - 2, -1, -1):
                out = jnp.where(lane < (j + 1) * SSD_HEAD_DIM, tiles[c0 + j], out)
            cols.append(out)
        return jnp.concatenate(cols, axis=1)

    for g in range(SSD_GROUPS):
        gsl = slice(g * gw, (g + 1) * gw)
        xg = act_ref[:, gsl]
        bg = act_ref[:, b_off + g * SSD_STATE:b_off + (g + 1) * SSD_STATE]
        cg_ = act_ref[:, c_off + g * SSD_STATE:c_off + (g + 1) * SSD_STATE].astype(BF16)
        cb = _dot_nt(cg_, bg.astype(BF16))
        state = st_ref[g]
        hds = [g * hpg + jj for jj in range(hpg)]
        a_bs = [jnp.broadcast_to(a[:, hd:hd + 1], (L, LANES)) for hd in hds]
        dt_bs = [jnp.broadcast_to(dt[:, hd:hd + 1], (L, LANES)) for hd in hds]
        ea_x = over_heads([jnp.exp(a_b) for a_b in a_bs])
        to_end_x = over_heads([jnp.exp(a_b[L - 1:L, :] - a_b) for a_b in a_bs])
        xdt = xg * over_heads(dt_bs)
        inter = _dot(cg_, state.astype(BF16)) * ea_x
        acc = [inter[:, c0:c0 + LANES] for c0 in range(0, gw, LANES)]
        for jj in range(hpg):
            seg = jnp.where(causal, a_bs[jj] - a_t[hds[jj]:hds[jj] + 1, :], -jnp.inf)
            w = (cb * jnp.exp(seg)).astype(BF16)
            c, j = divmod(jj, hpt)
            in_head = (lane >= j * SSD_HEAD_DIM) & (lane < (j + 1) * SSD_HEAD_DIM)
            x_head = jnp.where(in_head, xdt[:, c * LANES:(c + 1) * LANES], 0.0).astype(BF16)
            acc[c] = acc[c] + _dot(w, x_head)
        acc = jnp.concatenate(acc, axis=1)
        y = (acc + dsk_ref[:, gsl] * xg) * _silu(z_ref[:, gsl].astype(F32))
        y = y * lax.rsqrt(jnp.mean(y * y, axis=1, keepdims=True) + EPS) * nw_ref[:, gsl]
        o_ref[:, gsl] = y.astype(o_ref.dtype)
        xw = (xdt * to_end_x).astype(BF16)
        st_ref[g] = ea_x[L - 1:L, :] * state + _dot(bg.T.astype(BF16), xw)


def _ssd(z, xbc, dt_raw, p, bsz, seq):
    m, inner = z.shape
    nt = seq // CHUNK
    conv_ch = xbc.shape[1]
    sps = SEQ_PER_STEP if bsz % SEQ_PER_STEP == 0 else 1
    vec = lambda n: pl.BlockSpec((1, n), lambda b, j: (0, 0))
    tile = lambda n: pl.BlockSpec((sps, CHUNK, n), lambda b, j: (b, j, 0))
    out = pl.pallas_call(
        _ssd_kernel,
        out_shape=jax.ShapeDtypeStruct((bsz, seq, inner), BF16),
        grid=(bsz // sps, nt),
        in_specs=[tile(inner), tile(conv_ch), tile(LANES),
                  pl.BlockSpec((CONV_WIDTH, conv_ch), lambda b, j: (0, 0)), vec(conv_ch),
                  vec(LANES), vec(LANES), vec(inner), vec(inner)],
        out_specs=tile(inner),
        scratch_shapes=[pltpu.VMEM((sps, SUBLANES, conv_ch), F32),
                        pltpu.VMEM((sps, CHUNK, conv_ch), F32),
                        pltpu.VMEM((sps, SSD_GROUPS, SSD_STATE, inner // SSD_GROUPS), F32)],
        compiler_params=_cparams(("parallel", "arbitrary")),
    )(z.reshape(bsz, seq, inner), xbc.reshape(bsz, seq, conv_ch), dt_raw.reshape(bsz, seq, LANES),
      p["conv_w"], p["conv_b"], p["dt_bias"], p["a_log"], p["d_skip"], p["norm"])
    return out.reshape(m, inner)


def _router_kernel(*refs, n_in):
    y_refs, w_refs = refs[:n_in], refs[n_in:2 * n_in]
    (h_ref, gm_ref, g_ref, sh_ref, sc_ref, wr_ref, br_ref,
     hmid_ref, up_ref, topi_ref, gate_ref, rank_ref, cnt_ref, carry_ref) = refs[2 * n_in:]

    @pl.when(pl.program_id(0) == 0)
    def _():
        carry_ref[...] = jnp.zeros_like(carry_ref)

    tm = h_ref.shape[0]
    acc = _dot(y_refs[0][...], w_refs[0][...])
    for y_ref, w_ref in zip(y_refs[1:], w_refs[1:]):
        acc = acc + _dot(y_ref[...], w_ref[...])
    hmid = h_ref[...] + gm_ref[0] * acc
    hmid_ref[...] = hmid
    u = _norm_mod(hmid, g_ref[...], sh_ref[0], sc_ref[0])
    up_ref[...] = _pack_pairs(u)
    u_hi = u.astype(BF16)
    u_lo = (u - u_hi.astype(F32)).astype(BF16)
    logits = (_dot(u_hi, wr_ref[0]) + (_dot(u_lo, wr_ref[0]) + _dot(u_hi, wr_ref[1]))
              + br_ref[...])
    lt = jnp.concatenate([logits[r0:r0 + LANES].T for r0 in range(0, tm, LANES)], axis=1)
    l = lt[:N_EXPERTS]
    e_iota = lax.broadcasted_iota(I32, (N_EXPERTS, tm), 0).astype(F32)
    vals, idxs, hots = [], [], []
    for _ in range(TOP_K):
        mx = jnp.max(l, axis=0, keepdims=True)
        idx = jnp.min(jnp.where(l == mx, e_iota, float(N_EXPERTS)), axis=0, keepdims=True)
        hot = e_iota == idx
        l = jnp.where(hot, -jnp.inf, l)
        vals.append(mx)
        idxs.append(idx)
        hots.append(hot)
    exps = [jnp.exp(v - vals[0]) for v in vals]
    den = exps[0] + exps[1] + exps[2] + exps[3]
    gate_ref[...] = jnp.concatenate([e / den for e in exps], axis=0)
    topi_ref[...] = jnp.concatenate(idxs, axis=0).astype(I32)

    sel = jnp.zeros((N_EXPERTS, tm), F32)
    for hot in hots:
        sel = jnp.where(hot, 1.0, sel)
    r_i = lax.broadcasted_iota(I32, (tm, tm), 0)
    c_i = lax.broadcasted_iota(I32, (tm, tm), 1)
    before = (r_i < c_i).astype(BF16)
    carry = carry_ref[:, 0:1]
    cum = _dot(sel.astype(BF16), before) + carry
    rank_ref[...] = jnp.concatenate(
        [jnp.sum(jnp.where(hot, cum, 0.0), axis=0, keepdims=True) for hot in hots], axis=0).astype(I32)
    total = carry + jnp.sum(sel, axis=1, keepdims=True)
    carry_ref[...] = jnp.broadcast_to(total, carry_ref.shape)
    cnt_ref[...] = jnp.broadcast_to(total, cnt_ref.shape)


def _router(ys, w_out, h, g_m, g, shift, scale, wr, br, seq, tm):
    m, d = h.shape
    tiles_per_seq = seq // tm
    bmap = lambda i: (i // tiles_per_seq, 0, 0)
    row4 = lambda: pl.BlockSpec((TOP_K, tm), lambda i: (0, i))
    in_specs, args, k0 = [], [], 0
    for y in ys:
        in_specs.append(pl.BlockSpec((tm, y.shape[1]), lambda i: (i, 0)))
        args.append(y)
    for y in ys:
        kk = y.shape[1]
        in_specs.append(pl.BlockSpec((kk, d), lambda i, kb=k0 // kk: (kb, 0)))
        args.append(w_out)
        k0 += kk
    in_specs += [pl.BlockSpec((tm, d), lambda i: (i, 0)),
                 pl.BlockSpec((1, 1, d), bmap),
                 pl.BlockSpec((1, d), lambda i: (0, 0)),
                 pl.BlockSpec((1, 1, d), bmap), pl.BlockSpec((1, 1, d), bmap),
                 pl.BlockSpec((2, d, LANES), lambda i: (0, 0, 0)),
                 pl.BlockSpec((1, LANES), lambda i: (0, 0))]
    args += [h, g_m, g.reshape(1, d), shift, scale, wr, br]
    return pl.pallas_call(
        functools.partial(_router_kernel, n_in=len(ys)),
        out_shape=[jax.ShapeDtypeStruct((m, d), F32),
                   jax.ShapeDtypeStruct((m, d // 2), I32),
                   jax.ShapeDtypeStruct((TOP_K, m), I32),
                   jax.ShapeDtypeStruct((TOP_K, m), F32),
                   jax.ShapeDtypeStruct((TOP_K, m), I32),
                   jax.ShapeDtypeStruct((N_EXPERTS, LANES), F32)],
        grid=(m // tm,),
        in_specs=in_specs,
        out_specs=[pl.BlockSpec((tm, d), lambda i: (i, 0)),
                   pl.BlockSpec((tm, d // 2), lambda i: (i, 0)), row4(), row4(), row4(),
                   pl.BlockSpec((N_EXPERTS, LANES), lambda i: (0, 0))],
        scratch_shapes=[pltpu.VMEM((N_EXPERTS, LANES), F32)],
        compiler_params=_cparams(("arbitrary",)),
    )(*args)


def _dest_kernel(ps_ref, topi_ref, rank_ref, o_ref):
    topi = topi_ref[...]
    acc = rank_ref[...]
    for e in range(N_EXPERTS):
        acc = acc + jnp.where(topi == e, ps_ref[e], 0)
    o_ref[...] = acc


def _dest_rows(pad_start, topi, rank, tw):
    k, m = topi.shape
    blk = lambda: pl.BlockSpec((k, tw), lambda i, ps: (0, i))
    return pl.pallas_call(
        _dest_kernel,
        out_shape=jax.ShapeDtypeStruct((k, m), I32),
        grid_spec=pltpu.PrefetchScalarGridSpec(
            num_scalar_prefetch=1, grid=(m // tw,), in_specs=[blk(), blk()], out_specs=blk()),
        compiler_params=_cparams(("parallel",)),
    )(pad_start, topi, rank)


SC_CORES = 2
SC_SUBCORES = 16
SC_ROWS = 128


def _sc_gather_rows(table, idx):
    b = idx.shape[0]
    w = table.shape[1]
    workers = SC_CORES * SC_SUBCORES
    per_w = b // workers
    assert per_w * workers == b and per_w % SC_ROWS == 0
    mesh = plsc.VectorSubcoreMesh(core_axis_name="c", subcore_axis_name="s")

    @functools.partial(
        pl.kernel, mesh=mesh, out_type=jax.ShapeDtypeStruct((b, w), I32),
        scratch_types=[pltpu.VMEM((SC_ROWS,), I32), pltpu.VMEM((SC_ROWS, w), I32),
                       pltpu.SemaphoreType.DMA])
    def gather(table_hbm, idx_hbm, out_hbm, idx_v, rows_v, sem):
        base = (lax.axis_index("s") * SC_CORES + lax.axis_index("c")) * per_w

        @pl.loop(0, per_w // SC_ROWS)
        def _(c):
            off = base + c * SC_ROWS
            pltpu.sync_copy(idx_hbm.at[pl.ds(off, SC_ROWS)], idx_v)
            pltpu.async_copy(table_hbm.at[idx_v], rows_v, sem).wait()
            pltpu.sync_copy(rows_v, out_hbm.at[pl.ds(off, SC_ROWS)])

    return gather(table, idx)


def _sc_scatter_rows(rows, dest, n_rows):
    m, w = rows.shape
    kk = dest.shape[0]
    workers = SC_CORES * SC_SUBCORES
    per_w = m // workers
    assert per_w * workers == m and per_w % SC_ROWS == 0
    mesh = plsc.VectorSubcoreMesh(core_axis_name="c", subcore_axis_name="s")

    @functools.partial(
        pl.kernel, mesh=mesh, out_type=jax.ShapeDtypeStruct((n_rows, w), I32),
        scratch_types=[pltpu.VMEM((SC_ROWS,), I32), pltpu.VMEM((SC_ROWS, w), I32),
                       pltpu.SemaphoreType.DMA])
    def scatter(rows_hbm, dest_hbm, out_hbm, idx_v, rows_v, sem):
        base = (lax.axis_index("s") * SC_CORES + lax.axis_index("c")) * per_w

        @pl.loop(0, per_w // SC_ROWS)
        def _(c):
            off = base + c * SC_ROWS
            pltpu.sync_copy(rows_hbm.at[pl.ds(off, SC_ROWS)], rows_v)
            for k in range(kk):
                pltpu.sync_copy(dest_hbm.at[pl.ds(k * m + off, SC_ROWS)], idx_v)
                pltpu.async_copy(rows_v, out_hbm.at[idx_v], sem).wait()

    return scatter(rows, dest.reshape(-1))


def _combine_dense_kernel(y_ref, gate_ref, h_ref, gf_ref, fn_ref, o_ref, *, final):
    acc = gate_ref[:, 0:1] * _unpack_pairs(y_ref[0])
    for k in range(1, TOP_K):
        acc = acc + gate_ref[:, k:k + 1] * _unpack_pairs(y_ref[k])
    hn = h_ref[...] + gf_ref[0] * acc
    if final:
        hn = hn * lax.rsqrt(jnp.mean(hn * hn, axis=-1, keepdims=True) + EPS) * fn_ref[...]
    o_ref[...] = hn


def _combine_dense(y4, gates_col, h, gf, fnorm, seq, tm, final):
    m, d = h.shape
    tiles_per_seq = seq // tm
    return pl.pallas_call(
        functools.partial(_combine_dense_kernel, final=final),
        out_shape=jax.ShapeDtypeStruct((m, d), F32),
        grid=(m // tm,),
        in_specs=[pl.BlockSpec((TOP_K, tm, d // 2), lambda i: (0, i, 0)),
                  pl.BlockSpec((tm, TOP_K), lambda i: (i, 0)),
                  pl.BlockSpec((tm, d), lambda i: (i, 0)),
                  pl.BlockSpec((1, 1, d), lambda i: (i // tiles_per_seq, 0, 0)),
                  pl.BlockSpec((1, d), lambda i: (0, 0))],
        out_specs=pl.BlockSpec((tm, d), lambda i: (i, 0)),
        compiler_params=_cparams(("parallel",)),
    )(y4, gates_col, h, gf, fnorm.reshape(1, d))


def _expert_kernel(be_ref, nb_ref, first_ref, nv_ref, x_ref, wgu_ref, bgu_ref, wd_ref, bd_ref, y_ref,
                   wgu_bf, wd_bf):
    i = pl.program_id(0)

    @pl.when(i < nb_ref[0])
    def _():
        dff = wd_bf.shape[0]

        @pl.when(first_ref[i] == 1)
        def _():
            rows = 64

            def cast(r, c):
                r0 = pl.multiple_of(r * rows, rows)
                wgu_bf[pl.ds(r0, rows), :] = wgu_ref[0, 0, pl.ds(r0, rows), :].astype(BF16)
                wd_bf[pl.ds(r0, rows), :] = wd_ref[0, 0, pl.ds(r0, rows), :].astype(BF16)
                return c

            lax.fori_loop(0, dff // rows, cast, 0)

        def ffn(rows):
            x = _unpack_pairs(x_ref[0:rows, :]).astype(BF16)
            hb = _dot(x, wgu_bf[...]) + bgu_ref[0, 0]
            h_glu = jnp.minimum(hb[:, :dff], SWIGLU_LIMIT)
            h_lin = jnp.clip(hb[:, dff:], -SWIGLU_LIMIT, SWIGLU_LIMIT)
            act = h_glu * jax.nn.sigmoid(SWIGLU_ALPHA * h_glu) * (h_lin + 1.0)
            y_ref[0:rows, :] = _pack_pairs(_dot(act.astype(BF16), wd_bf[...]) + bd_ref[0, 0])

        half = x_ref.shape[0] // 2
        pl.when(nv_ref[i] > half)(lambda: ffn(x_ref.shape[0]))
        pl.when(nv_ref[i] <= half)(lambda: ffn(half))


def _experts(block_e, n_used, first, n_valid, xs, wgu, bgu, wd, bd, layer):
    n_rows, wp = xs.shape
    _, ne, d, ff2 = wgu.shape
    assert d == ff2 // 2
    nblk = n_rows // EXPERT_BLOCK

    def xmap(i, be, nb, fi, nv):
        return (jnp.minimum(i, nb[0] - 1), 0)

    emap = lambda i, be, nb, fi, nv: (layer, be[i], 0, 0)
    grid_spec = pltpu.PrefetchScalarGridSpec(
        num_scalar_prefetch=4, grid=(nblk,),
        in_specs=[pl.BlockSpec((EXPERT_BLOCK, wp), xmap),
                  pl.BlockSpec((1, 1, d, ff2), emap), pl.BlockSpec((1, 1, 1, ff2), emap),
                  pl.BlockSpec((1, 1, ff2 // 2, d), emap), pl.BlockSpec((1, 1, 1, d), emap)],
        out_specs=pl.BlockSpec((EXPERT_BLOCK, wp), xmap),
        scratch_shapes=[pltpu.VMEM((d, ff2), BF16), pltpu.VMEM((ff2 // 2, d), BF16)])
    depth = wgu.shape[0]
    return pl.pallas_call(
        _expert_kernel,
        out_shape=jax.ShapeDtypeStruct((n_rows, wp), I32),
        grid_spec=grid_spec,
        compiler_params=_cparams(("arbitrary",)),
    )(block_e, n_used, first, n_valid, xs, wgu, bgu.reshape(depth, ne, 1, ff2), wd,
      bd.reshape(depth, ne, 1, d))


def _moe(ys, w_out, g_m, h, g, shift, scale, gf, fnorm, wr, br, wgu, bgu, wd, bd, layer, seq, final):
    m, d = h.shape
    tm = 512
    wr_p = jnp.zeros((d, LANES), F32).at[:, :N_EXPERTS].set(wr)
    br_p = jnp.zeros((1, LANES), F32).at[0, :N_EXPERTS].set(br)
    wr_hi = wr_p.astype(BF16)
    wr_split = jnp.stack([wr_hi, (wr_p - wr_hi.astype(F32)).astype(BF16)])
    h, up, topi, gates, rank, cnt = _router(ys, w_out, h, g_m, g, shift, scale, wr_split, br_p, seq, tm)

    counts = cnt[:, 0].astype(I32)
    padded = (counts + EXPERT_BLOCK - 1) // EXPERT_BLOCK * EXPERT_BLOCK
    pad_end = jnp.cumsum(padded)
    pad_start = pad_end - padded
    nblk = m * TOP_K // EXPERT_BLOCK + N_EXPERTS
    n_rows = nblk * EXPERT_BLOCK
    n_used = pad_end[-1:] // EXPERT_BLOCK
    blk = jnp.arange(nblk, dtype=I32)
    blk_c = jnp.minimum(blk, n_used - 1)
    block_e = jnp.minimum(jnp.sum(blk_c[:, None] * EXPERT_BLOCK >= pad_end[None, :], axis=1),
                          N_EXPERTS - 1).astype(I32)
    first = jnp.concatenate([jnp.ones((1,), I32), (block_e[1:] != block_e[:-1]).astype(I32)])
    n_valid = jnp.clip((pad_start + counts)[block_e] - blk_c * EXPERT_BLOCK, 0, EXPERT_BLOCK).astype(I32)

    dest = _dest_rows(pad_start, topi, rank, min(m, 8192))
    xs = _sc_scatter_rows(up, dest, n_rows)
    y = _experts(block_e, n_used.astype(I32), first, n_valid, xs, wgu, bgu, wd, bd, layer)
    y4 = _sc_gather_rows(y, dest.reshape(-1)).reshape(TOP_K, m, d // 2)
    return _combine_dense(y4, gates.T, h, gf, fnorm, seq, tm, final)


def _block_diag(w, group):
    nb, b, _ = w.shape
    per = group // b
    wg = w.reshape(nb // per, per, b, b)
    dense = jnp.einsum("gnde,nm->gndme", wg, jnp.eye(per, dtype=w.dtype))
    return dense.reshape(nb // per, group, group)


def kernel(x, c, mod_w, mod_b, norm_mix, norm_ffn, ev_w_in, ev_lru_conv_w, ev_lru_conv_b, ev_lru_w_r, ev_lru_b_r, ev_lru_w_i, ev_lru_b_i, ev_lru_lambda, ev_ml_conv_w, ev_ml_conv_b, ev_ml_w_q, ev_ml_w_k, ev_ml_w_v, ev_ml_w_ig, ev_ml_b_ig, ev_ml_w_fg, ev_ml_b_fg, ev_ml_norm, ev_ml_skip, ev_w_out, od_w_in, od_conv_w, od_conv_b, od_dt_bias, od_a_log, od_d, od_norm, od_w_out, moe_router_w, moe_router_b, moe_w_gu, moe_b_gu, moe_w_down, moe_b_down, final_norm):
    bsz, seq, d = x.shape
    depth = mod_w.shape[0]
    m = bsz * seq
    mod = _modulation(c, mod_w, mod_b)
    h = x.reshape(m, d).astype(F32)
    for layer in range(depth):
        sh_m, sc_m, g_m, sh_f, sc_f, g_f = (mod[layer, i] for i in range(6))
        j = layer // 2
        if layer % 2 == 0:
            w = ev_lru_lambda.shape[1]
            w_in = ev_w_in[j].astype(BF16)
            proj = _inproj(h, norm_mix[layer], sh_m, sc_m, w_in, None, [w_in.shape[1]], seq, 512)[0]
            lru_p = dict(conv_w=ev_lru_conv_w[j], conv_b=ev_lru_conv_b[j].reshape(1, w),
                         w_r=ev_lru_w_r[j].astype(BF16), b_r=ev_lru_b_r[j].reshape(1, w),
                         w_i=ev_lru_w_i[j].astype(BF16), b_i=ev_lru_b_i[j].reshape(1, w),
                         lam=ev_lru_lambda[j].reshape(1, w))
            ya = _lru(proj, lru_p, bsz, seq, 256)
            wg = jnp.zeros((3 * w, LANES), F32)
            wg = wg.at[:, :ML_HEADS].set(ev_ml_w_ig[j]).at[:, ML_HEADS:2 * ML_HEADS].set(ev_ml_w_fg[j])
            bg = jnp.zeros((1, LANES), F32)
            bg = bg.at[0, :ML_HEADS].set(ev_ml_b_ig[j]).at[0, ML_HEADS:2 * ML_HEADS].set(ev_ml_b_fg[j])
            ml_p = dict(conv_w=ev_ml_conv_w[j], conv_b=ev_ml_conv_b[j].reshape(1, w),
                        w_q=_block_diag(ev_ml_w_q[j], LANES).astype(BF16),
                        w_k=_block_diag(ev_ml_w_k[j], LANES).astype(BF16),
                        w_v=_block_diag(ev_ml_w_v[j], LANES).astype(BF16),
                        w_g=wg.astype(BF16), b_g=bg,
                        norm=ev_ml_norm[j].reshape(1, w), skip=ev_ml_skip[j].reshape(1, w))
            yb = _mlstm(proj, ml_p, bsz, seq)
            ys, w_out = [ya, yb], ev_w_out[j].astype(BF16)
        else:
            inner = od_norm.shape[1]
            heads = od_dt_bias.shape[1]
            conv_ch = od_conv_w.shape[2]
            w_in = od_w_in[j]
            wdt = jnp.zeros((d, LANES), F32).at[:, :heads].set(w_in[:, inner + conv_ch:])
            z, xbc, dt_raw = _inproj(h, norm_mix[layer], sh_m, sc_m, w_in[:, :inner + conv_ch].astype(BF16),
                                     wdt.astype(BF16), [inner, conv_ch], seq, 512)
            pad = lambda v: jnp.zeros((1, LANES), F32).at[0, :heads].set(v)
            ssd_p = dict(conv_w=od_conv_w[j], conv_b=od_conv_b[j].reshape(1, conv_ch),
                         dt_bias=pad(od_dt_bias[j]), a_log=pad(od_a_log[j]),
                         d_skip=jnp.repeat(od_d[j], SSD_HEAD_DIM).reshape(1, inner),
                         norm=od_norm[j].reshape(1, inner))
            y = _ssd(z, xbc, dt_raw, ssd_p, bsz, seq)
            ys, w_out = [y], od_w_out[j].astype(BF16)
        h = _moe(ys, w_out, g_m, h, norm_ffn[layer], sh_f, sc_f, g_f, final_norm,
                 moe_router_w[layer], moe_router_b[layer],
                 moe_w_gu, moe_b_gu, moe_w_down, moe_b_down, layer, seq, final=(layer == depth - 1))
    return h.reshape(bsz, seq, d)
```

```python
import functools

import jax
import jax.numpy as jnp
from jax import lax
from jax.experimental import pallas as pl
from jax.experimental.pallas import tpu as pltpu
from jax.experimental.pallas import tpu_sc as plsc

F32 = jnp.float32
BF16 = jnp.bfloat16
I32 = jnp.int32
HIGHEST = lax.Precision.HIGHEST

EPS = 1e-6
CONV_WIDTH = 4
LANES = 128
SUBLANES = 8
LRU_HEADS = 8
LRU_C = 8.0
ML_HEADS = 8
ML_QKV_BLOCK = 4
CHUNK = 128
SSD_HEAD_DIM = 64
SSD_GROUPS = 8
SSD_STATE = 128
N_EXPERTS = 32
TOP_K = 4
SWIGLU_ALPHA = 1.702
SWIGLU_LIMIT = 7.0
EXPERT_BLOCK = 512
SEQ_PER_STEP = 2
CONV_COLS = 512
VMEM_LIMIT = 56 * 1024 * 1024


def _cparams(sem, **kw):
    return pltpu.CompilerParams(dimension_semantics=sem, vmem_limit_bytes=VMEM_LIMIT, **kw)


def _silu(x):
    half = 0.5 * x
    return half + half * jnp.tanh(half)


def _log_sigmoid(x):
    return jnp.minimum(x, 0.0) - jnp.log1p(jnp.exp(-jnp.abs(x)))


def _softplus(x):
    return jnp.maximum(x, 0.0) + jnp.log1p(jnp.exp(-jnp.abs(x)))


def _dot(a, b, **kw):
    return jnp.dot(a, b, preferred_element_type=F32, **kw)


def _dot_nt(a, b):
    return lax.dot_general(a, b, (((1,), (1,)), ((), ())), preferred_element_type=F32)


def _pack_pairs(x):
    w = x.shape[1] // 2
    lo = lax.bitcast_convert_type(x[:, :w].astype(BF16).astype(F32), I32)
    hi = lax.bitcast_convert_type(x[:, w:].astype(BF16).astype(F32), I32)
    return lax.shift_right_logical(lo, 16) | (hi & jnp.int32(-65536))


def _unpack_pairs(p):
    lo = lax.bitcast_convert_type(lax.shift_left(p, 16), F32)
    hi = lax.bitcast_convert_type(p & jnp.int32(-65536), F32)
    return jnp.concatenate([lo, hi], axis=1)


def _norm_mod(h, g, shift, scale):
    y = h * lax.rsqrt(jnp.mean(h * h, axis=-1, keepdims=True) + EPS)
    return (y * g) * (1.0 + scale) + shift


def _causal_conv(x, tail_ref, w_ref, b_ref, sl):
    t = x.shape[0]
    tail = tail_ref[:, sl]
    row8 = lax.broadcasted_iota(I32, tail.shape, 0)
    out = b_ref[:, sl] + x * w_ref[CONV_WIDTH - 1:CONV_WIDTH, sl]
    for k in range(1, CONV_WIDTH):
        xs = pltpu.roll(x, k, axis=0)
        first = jnp.where(row8 < k, pltpu.roll(tail, k, axis=0), xs[:SUBLANES])
        xs = jnp.concatenate([first, xs[SUBLANES:]], axis=0)
        out = out + xs * w_ref[CONV_WIDTH - 1 - k:CONV_WIDTH - k, sl]
    tail_ref[:, sl] = x[t - SUBLANES:]
    return out


def _shift_matrix(t):
    r = lax.broadcasted_iota(I32, ((CONV_WIDTH - 1) * t, t), 0)
    c = lax.broadcasted_iota(I32, ((CONV_WIDTH - 1) * t, t), 1)
    src = (r & (t - 1)) - lax.shift_right_logical(r, t.bit_length() - 1) - 1
    return (src == c).astype(BF16)


def _causal_conv_shifted(x, shifted, tail_ref, w_ref, b_ref, sl):
    t = x.shape[0]
    tail = tail_ref[:, sl]
    row8 = lax.broadcasted_iota(I32, tail.shape, 0)
    out = b_ref[:, sl] + x * w_ref[CONV_WIDTH - 1:CONV_WIDTH, sl]
    head = jnp.zeros_like(tail)
    for k in range(1, CONV_WIDTH):
        wk = w_ref[CONV_WIDTH - 1 - k:CONV_WIDTH - k, sl]
        out = out + shifted[(k - 1) * t:k * t] * wk
        head = head + jnp.where(row8 < k, pltpu.roll(tail, k, axis=0), 0.0) * wk
    tail_ref[:, sl] = x[t - SUBLANES:]
    return jnp.concatenate([out[:SUBLANES] + head, out[SUBLANES:]], axis=0)


def _mod_kernel(c_ref, w_ref, b_ref, o_ref):
    cond = _silu(c_ref[...])
    o_ref[0, 0] = _dot(cond, w_ref[0], precision=HIGHEST) + b_ref[0, 0]


def _modulation(c, mod_w, mod_b):
    depth, d, _ = mod_w.shape
    bsz = c.shape[0]
    out = pl.pallas_call(
        _mod_kernel,
        out_shape=jax.ShapeDtypeStruct((depth, 6, bsz, d), F32),
        grid=(depth, 6),
        in_specs=[pl.BlockSpec((bsz, d), lambda l, j: (0, 0)),
                  pl.BlockSpec((1, d, d), lambda l, j: (l, 0, j)),
                  pl.BlockSpec((1, 1, 1, d), lambda l, j: (l, j, 0, 0))],
        out_specs=pl.BlockSpec((1, 1, bsz, d), lambda l, j: (l, j, 0, 0)),
        compiler_params=_cparams(("parallel", "parallel")),
    )(c.astype(F32), mod_w, mod_b.reshape(depth, 6, 1, d))
    return out.reshape(depth, 6, bsz, 1, d)


def _inproj_kernel(h_ref, g_ref, sh_ref, sc_ref, w_ref, *rest, n_chunk, with_dt):
    if with_dt:
        wdt_ref, *o_refs, odt_ref = rest
    else:
        o_refs = rest
    u = _norm_mod(h_ref[...], g_ref[...], sh_ref[0], sc_ref[0]).astype(BF16)
    off = 0
    for o_ref in o_refs:
        for n0 in range(0, o_ref.shape[1], n_chunk):
            o_ref[:, n0:n0 + n_chunk] = _dot(u, w_ref[:, off + n0:off + n0 + n_chunk]).astype(o_ref.dtype)
        off += o_ref.shape[1]
    if with_dt:
        odt_ref[...] = _dot(u, wdt_ref[...])


def _inproj(h, g, shift, scale, w, wdt, splits, seq, tm):
    m, d = h.shape
    n = sum(splits)
    assert n <= w.shape[1] and n % LANES == 0
    tiles_per_seq = seq // tm
    bmap = lambda i: (i // tiles_per_seq, 0, 0)
    in_specs = [pl.BlockSpec((tm, d), lambda i: (i, 0)),
                pl.BlockSpec((1, d), lambda i: (0, 0)),
                pl.BlockSpec((1, 1, d), bmap),
                pl.BlockSpec((1, 1, d), bmap),
                pl.BlockSpec((d, n), lambda i: (0, 0), pipeline_mode=pl.Buffered(1))]
    out_shape = [jax.ShapeDtypeStruct((m, s), BF16) for s in splits]
    out_specs = [pl.BlockSpec((tm, s), lambda i: (i, 0)) for s in splits]
    args = [h, g.reshape(1, d), shift, scale, w]
    if wdt is not None:
        in_specs.append(pl.BlockSpec((d, LANES), lambda i: (0, 0)))
        out_shape.append(jax.ShapeDtypeStruct((m, LANES), F32))
        out_specs.append(pl.BlockSpec((tm, LANES), lambda i: (i, 0)))
        args.append(wdt)
    return pl.pallas_call(
        functools.partial(_inproj_kernel, n_chunk=1024, with_dt=wdt is not None),
        out_shape=out_shape, grid=(m // tm,), in_specs=in_specs, out_specs=out_specs,
        compiler_params=_cparams(("parallel",)),
    )(*args)


def _lru_kernel(xa_ref, ga_ref, cw_ref, cb_ref, wr_ref, br_ref, wi_ref, bi_ref, lam_ref,
                o_ref, tail_ref, hc_ref):
    @pl.when(pl.program_id(1) == 0)
    def _():
        tail_ref[...] = jnp.zeros_like(tail_ref)
        hc_ref[...] = jnp.zeros_like(hc_ref)

    t = xa_ref.shape[0]
    row_in_group = lax.broadcasted_iota(I32, (t, LANES), 0) % SUBLANES
    steps = [s for s in (1, 2, 4) if s < SUBLANES]
    masks = [row_in_group >= s for s in steps]
    for hh in range(LRU_HEADS):
        sl = slice(hh * LANES, (hh + 1) * LANES)
        xc = _causal_conv(xa_ref[:, sl].astype(F32), tail_ref, cw_ref, cb_ref, sl)
        xcb = xc.astype(BF16)
        r = jax.nn.sigmoid(_dot(xcb, wr_ref[hh]) + br_ref[:, sl])
        i = jax.nn.sigmoid(_dot(xcb, wi_ref[hh]) + bi_ref[:, sl])
        log_a = LRU_C * r * _log_sigmoid(lam_ref[:, sl])
        a = jnp.exp(log_a)
        th = jnp.tanh(log_a)
        n2 = -2.0 * th
        root = jnp.where(n2 > 0.0, n2 * lax.rsqrt(n2), 0.0)
        u = (root * lax.rsqrt(1.0 - th)) * (i * xc)
        def roll_in_groups(v, s):
            v3 = v.reshape(t // SUBLANES, SUBLANES, LANES)
            return pltpu.roll(v3, s, axis=1).reshape(t, LANES)

        for s, m in zip(steps, masks):
            u = jnp.where(m, u + a * roll_in_groups(u, s), u)
            a = jnp.where(m, a * roll_in_groups(a, s), a)
        carry = hc_ref[:, sl]
        groups = []
        for r0 in range(0, t, SUBLANES):
            hg = u[r0:r0 + SUBLANES] + a[r0:r0 + SUBLANES] * carry
            carry = hg[SUBLANES - 1:SUBLANES]
            groups.append(hg)
        hc_ref[:, sl] = carry
        h = jnp.concatenate(groups, axis=0)
        ga = ga_ref[:, sl].astype(F32)
        o_ref[:, sl] = (h * jax.nn.gelu(ga, approximate=True)).astype(o_ref.dtype)


def _lru(proj, p, bsz, seq, tm):
    m = proj.shape[0]
    w = LRU_HEADS * LANES
    nt = seq // tm
    vec = lambda: pl.BlockSpec((1, w), lambda b, j: (0, 0))
    return pl.pallas_call(
        _lru_kernel,
        out_shape=jax.ShapeDtypeStruct((m, w), BF16),
        grid=(bsz, nt),
        in_specs=[pl.BlockSpec((tm, w), lambda b, j: (b * nt + j, 0)),
                  pl.BlockSpec((tm, w), lambda b, j: (b * nt + j, 1)),
                  pl.BlockSpec((CONV_WIDTH, w), lambda b, j: (0, 0)), vec(),
                  pl.BlockSpec((LRU_HEADS, LANES, LANES), lambda b, j: (0, 0, 0)), vec(),
                  pl.BlockSpec((LRU_HEADS, LANES, LANES), lambda b, j: (0, 0, 0)), vec(), vec()],
        out_specs=pl.BlockSpec((tm, w), lambda b, j: (b * nt + j, 0)),
        scratch_shapes=[pltpu.VMEM((SUBLANES, w), F32), pltpu.VMEM((1, w), F32)],
        compiler_params=_cparams(("parallel", "arbitrary")),
    )(proj, proj, p["conv_w"], p["conv_b"], p["w_r"], p["b_r"], p["w_i"], p["b_i"], p["lam"])


def _mlstm_kernel(xb_ref, zb_ref, cw_ref, cb_ref, wq_ref, wk_ref, wv_ref, wg_ref, bg_ref,
                  nw_ref, sk_ref, o_ref, tail_ref, qkv_ref, xc_ref, caug_ref, m_ref):
    @pl.when(pl.program_id(1) == 0)
    def _():
        tail_ref[...] = jnp.zeros_like(tail_ref)
        caug_ref[...] = jnp.zeros_like(caug_ref)
        m_ref[...] = jnp.full(m_ref.shape, -jnp.inf, F32)

    for s in range(xb_ref.shape[0]):
        for c0 in range(0, xb_ref.shape[1], CHUNK):
            rows = pl.ds(c0, CHUNK)
            _mlstm_chunk(xb_ref.at[s, rows], zb_ref.at[s, rows], cw_ref, cb_ref, wq_ref, wk_ref, wv_ref,
                         wg_ref, bg_ref, nw_ref, sk_ref, o_ref.at[s, rows], tail_ref.at[s], qkv_ref.at[s],
                         xc_ref.at[s], caug_ref.at[s], m_ref.at[s])


def _mlstm_chunk(xb_ref, zb_ref, cw_ref, cb_ref, wq_ref, wk_ref, wv_ref, wg_ref, bg_ref,
                 nw_ref, sk_ref, o_ref, tail_ref, qkv_ref, xc_ref, caug_ref, m_ref):
    L = CHUNK
    width = ML_HEADS * LANES
    scale = LANES ** -0.5
    for hh in range(ML_HEADS):
        sl = slice(hh * LANES, (hh + 1) * LANES)
        xb = xb_ref[:, sl].astype(F32)
        xc = _silu(_causal_conv(xb, tail_ref, cw_ref, cb_ref, sl))
        xc_ref[:, sl] = xc
        xcb = xc.astype(BF16)
        qkv_ref[:, sl] = _dot(xcb, wq_ref[hh]).astype(BF16)
        qkv_ref[:, width + hh * LANES:width + (hh + 1) * LANES] = _dot(xcb, wk_ref[hh]).astype(BF16)
        qkv_ref[:, 2 * width + hh * LANES:2 * width + (hh + 1) * LANES] = (
            _dot(xb.astype(BF16), wv_ref[hh]).astype(BF16))

    gates = _dot(qkv_ref[...], wg_ref[...]) + bg_ref[...]
    rowi = lax.broadcasted_iota(I32, (L, L), 0)
    coli = lax.broadcasted_iota(I32, (L, L), 1)
    causal = rowi >= coli
    lf = jnp.where((coli >= ML_HEADS) & (coli < 2 * ML_HEADS), _log_sigmoid(gates), 0.0)
    tri = causal.astype(BF16)
    lf_hi = lf.astype(BF16)
    lf_mid = (lf - lf_hi.astype(F32)).astype(BF16)
    lf_lo = (lf - lf_hi.astype(F32) - lf_mid.astype(F32)).astype(BF16)
    gcum = _dot(tri, lf_hi) + (_dot(tri, lf_mid) + _dot(tri, lf_lo))
    x_col = jnp.where(coli < ML_HEADS, gates, gcum)
    x_row = x_col.T
    ones = jnp.ones((L, LANES), BF16)
    heads = range(ML_HEADS)
    hsl = [slice(hh * LANES, (hh + 1) * LANES) for hh in heads]

    qs = [qkv_ref[:, hsl[hh]] for hh in heads]
    ks = [qkv_ref[:, width + hh * LANES:width + (hh + 1) * LANES] for hh in heads]
    vaugs = [jnp.concatenate([qkv_ref[:, 2 * width + hh * LANES:2 * width + (hh + 1) * LANES], ones], axis=1)
             for hh in heads]
    scores = [_dot_nt(qs[hh], ks[hh]) * scale for hh in heads]
    ics = [jnp.broadcast_to(x_col[:, hh:hh + 1], (L, LANES)) for hh in heads]
    gcs = [jnp.broadcast_to(x_col[:, ML_HEADS + hh:ML_HEADS + hh + 1], (L, LANES)) for hh in heads]
    irs = [x_row[hh:hh + 1, :] for hh in heads]
    grs = [x_row[ML_HEADS + hh:ML_HEADS + hh + 1, :] for hh in heads]
    mps = [m_ref[hh] for hh in heads]
    dmats = [jnp.where(causal, gcs[hh] - grs[hh] + irs[hh], -jnp.inf) for hh in heads]
    m_inters = [mps[hh] + gcs[hh] for hh in heads]
    m_ts = [jnp.maximum(m_inters[hh], jnp.max(dmats[hh], axis=1, keepdims=True)) for hh in heads]
    qks = [(scores[hh] * jnp.exp(dmats[hh] - m_ts[hh])).astype(BF16) for hh in heads]
    caugs = [caug_ref[hh] for hh in heads]
    w_inters = [jnp.exp(m_inters[hh] - m_ts[hh]) for hh in heads]
    nds = [_dot(qks[hh], vaugs[hh])
           + jnp.concatenate([w_inters[hh], w_inters[hh]], axis=1) * _dot(qs[hh], caugs[hh].astype(BF16))
           for hh in heads]

    g_lasts = [gcs[hh][L - 1:L, :] for hh in heads]
    m_news = [jnp.maximum(mps[hh] + g_lasts[hh],
                          jnp.max(g_lasts[hh] - grs[hh] + irs[hh], axis=1, keepdims=True)) for hh in heads]
    for hh in heads:
        ws = jnp.exp(g_lasts[hh] - gcs[hh] + ics[hh] - m_news[hh])
        wc = jnp.exp(mps[hh] + g_lasts[hh] - m_news[hh])
        kw_t = (ks[hh].astype(F32) * (ws * scale)).T.astype(BF16)
        caug_ref[hh] = jnp.concatenate([wc, wc], axis=1) * caugs[hh] + _dot(kw_t, vaugs[hh])
        m_ref[hh] = m_news[hh]

    hvals = [nds[hh][:, :LANES] / jnp.maximum(jnp.abs(nds[hh][:, LANES:]), jnp.exp(-m_ts[hh]))
             for hh in heads]
    mus = [jnp.mean(hvals[hh], axis=1, keepdims=True) for hh in heads]
    dvs = [hvals[hh] - mus[hh] for hh in heads]
    variances = [jnp.mean(dvs[hh] * dvs[hh], axis=1, keepdims=True) for hh in heads]
    for hh in heads:
        sl = hsl[hh]
        hn = dvs[hh] * lax.rsqrt(variances[hh] + EPS) * nw_ref[:, sl]
        zb = zb_ref[:, sl].astype(F32)
        o_ref[:, sl] = ((hn + sk_ref[:, sl] * xc_ref[:, sl]) * _silu(zb)).astype(o_ref.dtype)


def _mlstm(proj, p, bsz, seq):
    m = proj.shape[0]
    w = ML_HEADS * LANES
    sps = 1
    cps = 2 if seq % (2 * CHUNK) == 0 else 1
    nt = seq // (cps * CHUNK)
    vec = lambda: pl.BlockSpec((1, w), lambda b, j: (0, 0))
    blk = lambda: pl.BlockSpec((ML_HEADS, LANES, LANES), lambda b, j: (0, 0, 0))
    tile = lambda col: pl.BlockSpec((sps, cps * CHUNK, w), lambda b, j: (b, j, col))
    proj3 = proj.reshape(bsz, seq, proj.shape[1])
    out = pl.pallas_call(
        _mlstm_kernel,
        out_shape=jax.ShapeDtypeStruct((bsz, seq, w), BF16),
        grid=(bsz // sps, nt),
        in_specs=[tile(2), tile(3),
                  pl.BlockSpec((CONV_WIDTH, w), lambda b, j: (0, 0)), vec(),
                  blk(), blk(), blk(),
                  pl.BlockSpec((3 * w, LANES), lambda b, j: (0, 0)),
                  pl.BlockSpec((1, LANES), lambda b, j: (0, 0)),
                  vec(), vec()],
        out_specs=tile(0),
        scratch_shapes=[pltpu.VMEM((sps, SUBLANES, w), F32),
                        pltpu.VMEM((sps, CHUNK, 3 * w), BF16),
                        pltpu.VMEM((sps, CHUNK, w), F32),
                        pltpu.VMEM((sps, ML_HEADS, LANES, 2 * LANES), F32),
                        pltpu.VMEM((sps, ML_HEADS, 1, LANES), F32)],
        compiler_params=_cparams(("parallel", "arbitrary")),
    )(proj3, proj3, p["conv_w"], p["conv_b"], p["w_q"], p["w_k"], p["w_v"], p["w_g"], p["b_g"],
      p["norm"], p["skip"])
    return out.reshape(m, w)


def _ssd_kernel(z_ref, xbc_ref, dt_ref, cw_ref, cb_ref, dtb_ref, alog_ref, dsk_ref, nw_ref,
                o_ref, tail_ref, act_ref, st_ref):
    @pl.when(pl.program_id(1) == 0)
    def _():
        tail_ref[...] = jnp.zeros_like(tail_ref)
        st_ref[...] = jnp.zeros_like(st_ref)

    for s in range(z_ref.shape[0]):
        _ssd_chunk(z_ref.at[s], xbc_ref.at[s], dt_ref.at[s], cw_ref, cb_ref, dtb_ref, alog_ref, dsk_ref,
                   nw_ref, o_ref.at[s], tail_ref.at[s], act_ref.at[s], st_ref.at[s])


def _ssd_chunk(z_ref, xbc_ref, dt_ref, cw_ref, cb_ref, dtb_ref, alog_ref, dsk_ref, nw_ref,
               o_ref, tail_ref, act_ref, st_ref):
    L = CHUNK
    inner = o_ref.shape[1]
    gw = inner // SSD_GROUPS
    hpg = gw // SSD_HEAD_DIM
    b_off = inner
    c_off = inner + SSD_GROUPS * SSD_STATE
    shift = _shift_matrix(L)
    for c0 in range(0, xbc_ref.shape[1], CONV_COLS):
        shifted = _dot(shift, xbc_ref[:, c0:c0 + CONV_COLS])
        for l0 in range(0, CONV_COLS, LANES):
            sl = slice(c0 + l0, c0 + l0 + LANES)
            act_ref[:, sl] = _silu(_causal_conv_shifted(
                xbc_ref[:, sl].astype(F32), shifted[:, l0:l0 + LANES], tail_ref, cw_ref, cb_ref, sl))

    rowi = lax.broadcasted_iota(I32, (L, L), 0)
    coli = lax.broadcasted_iota(I32, (L, L), 1)
    causal = rowi >= coli
    dt = _softplus(dt_ref[...] + dtb_ref[...])
    da = dt * (-jnp.exp(alog_ref[...]))
    tri = causal.astype(BF16)
    da_hi = da.astype(BF16)
    da_mid = (da - da_hi.astype(F32)).astype(BF16)
    da_lo = (da - da_hi.astype(F32) - da_mid.astype(F32)).astype(BF16)
    a = _dot(tri, da_hi) + (_dot(tri, da_mid) + _dot(tri, da_lo))
    a_t = a.T
    hpt = LANES // SSD_HEAD_DIM
    lane = lax.broadcasted_iota(I32, (L, LANES), 1)

    def over_heads(tiles):
        cols = []
        for c0 in range(0, hpg, hpt):
            out = tiles[c0 + hpt - 1]
            for j in range(hpt - 2, -1, -1):
                out = jnp.where(lane < (j + 1) * SSD_HEAD_DIM, tiles[c0 + j], out)
            cols.append(out)
        return jnp.concatenate(cols, axis=1)

    for g in range(SSD_GROUPS):
        gsl = slice(g * gw, (g + 1) * gw)
        xg = act_ref[:, gsl]
        bg = act_ref[:, b_off + g * SSD_STATE:b_off + (g + 1) * SSD_STATE]
        cg_ = act_ref[:, c_off + g * SSD_STATE:c_off + (g + 1) * SSD_STATE].astype(BF16)
        cb = _dot_nt(cg_, bg.astype(BF16))
        state = st_ref[g]
        hds = [g * hpg + jj for jj in range(hpg)]
        a_bs = [jnp.broadcast_to(a[:, hd:hd + 1], (L, LANES)) for hd in hds]
        dt_bs = [jnp.broadcast_to(dt[:, hd:hd + 1], (L, LANES)) for hd in hds]
        ea_x = over_heads([jnp.exp(a_b) for a_b in a_bs])
        to_end_x = over_heads([jnp.exp(a_b[L - 1:L, :] - a_b) for a_b in a_bs])
        xdt = xg * over_heads(dt_bs)
        inter = _dot(cg_, state.astype(BF16)) * ea_x
        acc = [inter[:, c0:c0 + LANES] for c0 in range(0, gw, LANES)]
        for jj in range(hpg):
            seg = jnp.where(causal, a_bs[jj] - a_t[hds[jj]:hds[jj] + 1, :], -jnp.inf)
            w = (cb * jnp.exp(seg)).astype(BF16)
            c, j = divmod(jj, hpt)
            in_head = (lane >= j * SSD_HEAD_DIM) & (lane < (j + 1) * SSD_HEAD_DIM)
            x_head = jnp.where(in_head, xdt[:, c * LANES:(c + 1) * LANES], 0.0).astype(BF16)
            acc[c] = acc[c] + _dot(w, x_head)
        acc = jnp.concatenate(acc, axis=1)
        y = (acc + dsk_ref[:, gsl] * xg) * _silu(z_ref[:, gsl].astype(F32))
        y = y * lax.rsqrt(jnp.mean(y * y, axis=1, keepdims=True) + EPS) * nw_ref[:, gsl]
        o_ref[:, gsl] = y.astype(o_ref.dtype)
        xw = (xdt * to_end_x).astype(BF16)
        st_ref[g] = ea_x[L - 1:L, :] * state + _dot(bg.T.astype(BF16), xw)


def _ssd(z, xbc, dt_raw, p, bsz, seq):
    m, inner = z.shape
    nt = seq // CHUNK
    conv_ch = xbc.shape[1]
    sps = SEQ_PER_STEP if bsz % SEQ_PER_STEP == 0 else 1
    vec = lambda n: pl.BlockSpec((1, n), lambda b, j: (0, 0))
    tile = lambda n: pl.BlockSpec((sps, CHUNK, n), lambda b, j: (b, j, 0))
    out = pl.pallas_call(
        _ssd_kernel,
        out_shape=jax.ShapeDtypeStruct((bsz, seq, inner), BF16),
        grid=(bsz // sps, nt),
        in_specs=[tile(inner), tile(conv_ch), tile(LANES),
                  pl.BlockSpec((CONV_WIDTH, conv_ch), lambda b, j: (0, 0)), vec(conv_ch),
                  vec(LANES), vec(LANES), vec(inner), vec(inner)],
        out_specs=tile(inner),
        scratch_shapes=[pltpu.VMEM((sps, SUBLANES, conv_ch), F32),
                        pltpu.VMEM((sps, CHUNK, conv_ch), F32),
                        pltpu.VMEM((sps, SSD_GROUPS, SSD_STATE, inner // SSD_GROUPS), F32)],
        compiler_params=_cparams(("parallel", "arbitrary")),
    )(z.reshape(bsz, seq, inner), xbc.reshape(bsz, seq, conv_ch), dt_raw.reshape(bsz, seq, LANES),
      p["conv_w"], p["conv_b"], p["dt_bias"], p["a_log"], p["d_skip"], p["norm"])
    return out.reshape(m, inner)


def _router_kernel(*refs, n_in):
    y_refs, w_refs = refs[:n_in], refs[n_in:2 * n_in]
    (h_ref, gm_ref, g_ref, sh_ref, sc_ref, wr_ref, br_ref,
     hmid_ref, up_ref, topi_ref, gate_ref, rank_ref, cnt_ref, carry_ref) = refs[2 * n_in:]

    @pl.when(pl.program_id(0) == 0)
    def _():
        carry_ref[...] = jnp.zeros_like(carry_ref)

    tm = h_ref.shape[0]
    acc = _dot(y_refs[0][...], w_refs[0][...])
    for y_ref, w_ref in zip(y_refs[1:], w_refs[1:]):
        acc = acc + _dot(y_ref[...], w_ref[...])
    hmid = h_ref[...] + gm_ref[0] * acc
    hmid_ref[...] = hmid
    u = _norm_mod(hmid, g_ref[...], sh_ref[0], sc_ref[0])
    up_ref[...] = _pack_pairs(u)
    u_hi = u.astype(BF16)
    u_lo = (u - u_hi.astype(F32)).astype(BF16)
    logits = (_dot(u_hi, wr_ref[0]) + (_dot(u_lo, wr_ref[0]) + _dot(u_hi, wr_ref[1]))
              + br_ref[...])
    lt = jnp.concatenate([logits[r0:r0 + LANES].T for r0 in range(0, tm, LANES)], axis=1)
    l = lt[:N_EXPERTS]
    e_iota = lax.broadcasted_iota(I32, (N_EXPERTS, tm), 0).astype(F32)
    vals, idxs, hots = [], [], []
    for _ in range(TOP_K):
        mx = jnp.max(l, axis=0, keepdims=True)
        idx = jnp.min(jnp.where(l == mx, e_iota, float(N_EXPERTS)), axis=0, keepdims=True)
        hot = e_iota == idx
        l = jnp.where(hot, -jnp.inf, l)
        vals.append(mx)
        idxs.append(idx)
        hots.append(hot)
    exps = [jnp.exp(v - vals[0]) for v in vals]
    den = exps[0] + exps[1] + exps[2] + exps[3]
    gate_ref[...] = jnp.concatenate([e / den for e in exps], axis=0)
    topi_ref[...] = jnp.concatenate(idxs, axis=0).astype(I32)

    sel = jnp.zeros((N_EXPERTS, tm), F32)
    for hot in hots:
        sel = jnp.where(hot, 1.0, sel)
    r_i = lax.broadcasted_iota(I32, (tm, tm), 0)
    c_i = lax.broadcasted_iota(I32, (tm, tm), 1)
    before = (r_i < c_i).astype(BF16)
    carry = carry_ref[:, 0:1]
    cum = _dot(sel.astype(BF16), before) + carry
    rank_ref[...] = jnp.concatenate(
        [jnp.sum(jnp.where(hot, cum, 0.0), axis=0, keepdims=True) for hot in hots], axis=0).astype(I32)
    total = carry + jnp.sum(sel, axis=1, keepdims=True)
    carry_ref[...] = jnp.broadcast_to(total, carry_ref.shape)
    cnt_ref[...] = jnp.broadcast_to(total, cnt_ref.shape)


def _router(ys, w_out, h, g_m, g, shift, scale, wr, br, seq, tm):
    m, d = h.shape
    tiles_per_seq = seq // tm
    bmap = lambda i: (i // tiles_per_seq, 0, 0)
    row4 = lambda: pl.BlockSpec((TOP_K, tm), lambda i: (0, i))
    in_specs, args, k0 = [], [], 0
    for y in ys:
        in_specs.append(pl.BlockSpec((tm, y.shape[1]), lambda i: (i, 0)))
        args.append(y)
    for y in ys:
        kk = y.shape[1]
        in_specs.append(pl.BlockSpec((kk, d), lambda i, kb=k0 // kk: (kb, 0)))
        args.append(w_out)
        k0 += kk
    in_specs += [pl.BlockSpec((tm, d), lambda i: (i, 0)),
                 pl.BlockSpec((1, 1, d), bmap),
                 pl.BlockSpec((1, d), lambda i: (0, 0)),
                 pl.BlockSpec((1, 1, d), bmap), pl.BlockSpec((1, 1, d), bmap),
                 pl.BlockSpec((2, d, LANES), lambda i: (0, 0, 0)),
                 pl.BlockSpec((1, LANES), lambda i: (0, 0))]
    args += [h, g_m, g.reshape(1, d), shift, scale, wr, br]
    return pl.pallas_call(
        functools.partial(_router_kernel, n_in=len(ys)),
        out_shape=[jax.ShapeDtypeStruct((m, d), F32),
                   jax.ShapeDtypeStruct((m, d // 2), I32),
                   jax.ShapeDtypeStruct((TOP_K, m), I32),
                   jax.ShapeDtypeStruct((TOP_K, m), F32),
                   jax.ShapeDtypeStruct((TOP_K, m), I32),
                   jax.ShapeDtypeStruct((N_EXPERTS, LANES), F32)],
        grid=(m // tm,),
        in_specs=in_specs,
        out_specs=[pl.BlockSpec((tm, d), lambda i: (i, 0)),
                   pl.BlockSpec((tm, d // 2), lambda i: (i, 0)), row4(), row4(), row4(),
                   pl.BlockSpec((N_EXPERTS, LANES), lambda i: (0, 0))],
        scratch_shapes=[pltpu.VMEM((N_EXPERTS, LANES), F32)],
        compiler_params=_cparams(("arbitrary",)),
    )(*args)


def _dest_kernel(ps_ref, topi_ref, rank_ref, o_ref):
    topi = topi_ref[...]
    acc = rank_ref[...]
    for e in range(N_EXPERTS):
        acc = acc + jnp.where(topi == e, ps_ref[e], 0)
    o_ref[...] = acc


def _dest_rows(pad_start, topi, rank, tw):
    k, m = topi.shape
    blk = lambda: pl.BlockSpec((k, tw), lambda i, ps: (0, i))
    return pl.pallas_call(
        _dest_kernel,
        out_shape=jax.ShapeDtypeStruct((k, m), I32),
        grid_spec=pltpu.PrefetchScalarGridSpec(
            num_scalar_prefetch=1, grid=(m // tw,), in_specs=[blk(), blk()], out_specs=blk()),
        compiler_params=_cparams(("parallel",)),
    )(pad_start, topi, rank)


SC_CORES = 2
SC_SUBCORES = 16
SC_ROWS = 128


def _sc_gather_rows(table, idx):
    b = idx.shape[0]
    w = table.shape[1]
    workers = SC_CORES * SC_SUBCORES
    per_w = b // workers
    assert per_w * workers == b and per_w % SC_ROWS == 0
    mesh = plsc.VectorSubcoreMesh(core_axis_name="c", subcore_axis_name="s")

    @functools.partial(
        pl.kernel, mesh=mesh, out_type=jax.ShapeDtypeStruct((b, w), I32),
        scratch_types=[pltpu.VMEM((SC_ROWS,), I32), pltpu.VMEM((SC_ROWS, w), I32),
                       pltpu.SemaphoreType.DMA])
    def gather(table_hbm, idx_hbm, out_hbm, idx_v, rows_v, sem):
        base = (lax.axis_index("s") * SC_CORES + lax.axis_index("c")) * per_w

        @pl.loop(0, per_w // SC_ROWS)
        def _(c):
            off = base + c * SC_ROWS
            pltpu.sync_copy(idx_hbm.at[pl.ds(off, SC_ROWS)], idx_v)
            pltpu.async_copy(table_hbm.at[idx_v], rows_v, sem).wait()
            pltpu.sync_copy(rows_v, out_hbm.at[pl.ds(off, SC_ROWS)])

    return gather(table, idx)


def _sc_scatter_rows(rows, dest, n_rows):
    m, w = rows.shape
    kk = dest.shape[0]
    workers = SC_CORES * SC_SUBCORES
    per_w = m // workers
    assert per_w * workers == m and per_w % SC_ROWS == 0
    mesh = plsc.VectorSubcoreMesh(core_axis_name="c", subcore_axis_name="s")

    @functools.partial(
        pl.kernel, mesh=mesh, out_type=jax.ShapeDtypeStruct((n_rows, w), I32),
        scratch_types=[pltpu.VMEM((SC_ROWS,), I32), pltpu.VMEM((SC_ROWS, w), I32),
                       pltpu.SemaphoreType.DMA])
    def scatter(rows_hbm, dest_hbm, out_hbm, idx_v, rows_v, sem):
        base = (lax.axis_index("s") * SC_CORES + lax.axis_index("c")) * per_w

        @pl.loop(0, per_w // SC_ROWS)
        def _(c):
            off = base + c * SC_ROWS
            pltpu.sync_copy(rows_hbm.at[pl.ds(off, SC_ROWS)], rows_v)
            for k in range(kk):
                pltpu.sync_copy(dest_hbm.at[pl.ds(k * m + off, SC_ROWS)], idx_v)
                pltpu.async_copy(rows_v, out_hbm.at[idx_v], sem).wait()

    return scatter(rows, dest.reshape(-1))


def _combine_dense_kernel(y_ref, gate_ref, h_ref, gf_ref, fn_ref, o_ref, *, final):
    acc = gate_ref[:, 0:1] * _unpack_pairs(y_ref[0])
    for k in range(1, TOP_K):
        acc = acc + gate_ref[:, k:k + 1] * _unpack_pairs(y_ref[k])
    hn = h_ref[...] + gf_ref[0] * acc
    if final:
        hn = hn * lax.rsqrt(jnp.mean(hn * hn, axis=-1, keepdims=True) + EPS) * fn_ref[...]
    o_ref[...] = hn


def _combine_dense(y4, gates_col, h, gf, fnorm, seq, tm, final):
    m, d = h.shape
    tiles_per_seq = seq // tm
    return pl.pallas_call(
        functools.partial(_combine_dense_kernel, final=final),
        out_shape=jax.ShapeDtypeStruct((m, d), F32),
        grid=(m // tm,),
        in_specs=[pl.BlockSpec((TOP_K, tm, d // 2), lambda i: (0, i, 0)),
                  pl.BlockSpec((tm, TOP_K), lambda i: (i, 0)),
                  pl.BlockSpec((tm, d), lambda i: (i, 0)),
                  pl.BlockSpec((1, 1, d), lambda i: (i // tiles_per_seq, 0, 0)),
                  pl.BlockSpec((1, d), lambda i: (0, 0))],
        out_specs=pl.BlockSpec((tm, d), lambda i: (i, 0)),
        compiler_params=_cparams(("parallel",)),
    )(y4, gates_col, h, gf, fnorm.reshape(1, d))


def _expert_kernel(be_ref, nb_ref, first_ref, x_ref, wgu_ref, bgu_ref, wd_ref, bd_ref, y_ref,
                   wgu_bf, wd_bf):
    i = pl.program_id(0)

    @pl.when(i < nb_ref[0])
    def _():
        dff = wd_bf.shape[0]

        @pl.when(first_ref[i] == 1)
        def _():
            rows = 64

            def cast(r, c):
                r0 = pl.multiple_of(r * rows, rows)
                wgu_bf[pl.ds(r0, rows), :] = wgu_ref[0, 0, pl.ds(r0, rows), :].astype(BF16)
                wd_bf[pl.ds(r0, rows), :] = wd_ref[0, 0, pl.ds(r0, rows), :].astype(BF16)
                return c

            lax.fori_loop(0, dff // rows, cast, 0)

        x = _unpack_pairs(x_ref[...]).astype(BF16)
        hb = _dot(x, wgu_bf[...]) + bgu_ref[0, 0]
        h_glu = jnp.minimum(hb[:, :dff], SWIGLU_LIMIT)
        h_lin = jnp.clip(hb[:, dff:], -SWIGLU_LIMIT, SWIGLU_LIMIT)
        act = h_glu * jax.nn.sigmoid(SWIGLU_ALPHA * h_glu) * (h_lin + 1.0)
        y_ref[...] = _pack_pairs(_dot(act.astype(BF16), wd_bf[...]) + bd_ref[0, 0])


def _experts(block_e, n_used, first, xs, wgu, bgu, wd, bd, layer):
    n_rows, wp = xs.shape
    _, ne, d, ff2 = wgu.shape
    assert d == ff2 // 2
    nblk = n_rows // EXPERT_BLOCK

    def xmap(i, be, nb, fi):
        return (jnp.minimum(i, nb[0] - 1), 0)

    emap = lambda i, be, nb, fi: (layer, be[i], 0, 0)
    grid_spec = pltpu.PrefetchScalarGridSpec(
        num_scalar_prefetch=3, grid=(nblk,),
        in_specs=[pl.BlockSpec((EXPERT_BLOCK, wp), xmap),
                  pl.BlockSpec((1, 1, d, ff2), emap), pl.BlockSpec((1, 1, 1, ff2), emap),
                  pl.BlockSpec((1, 1, ff2 // 2, d), emap), pl.BlockSpec((1, 1, 1, d), emap)],
        out_specs=pl.BlockSpec((EXPERT_BLOCK, wp), xmap),
        scratch_shapes=[pltpu.VMEM((d, ff2), BF16), pltpu.VMEM((ff2 // 2, d), BF16)])
    depth = wgu.shape[0]
    return pl.pallas_call(
        _expert_kernel,
        out_shape=jax.ShapeDtypeStruct((n_rows, wp), I32),
        grid_spec=grid_spec,
        compiler_params=_cparams(("arbitrary",)),
    )(block_e, n_used, first, xs, wgu, bgu.reshape(depth, ne, 1, ff2), wd, bd.reshape(depth, ne, 1, d))


def _moe(ys, w_out, g_m, h, g, shift, scale, gf, fnorm, wr, br, wgu, bgu, wd, bd, layer, seq, final):
    m, d = h.shape
    tm = 512
    wr_p = jnp.zeros((d, LANES), F32).at[:, :N_EXPERTS].set(wr)
    br_p = jnp.zeros((1, LANES), F32).at[0, :N_EXPERTS].set(br)
    wr_hi = wr_p.astype(BF16)
    wr_split = jnp.stack([wr_hi, (wr_p - wr_hi.astype(F32)).astype(BF16)])
    h, up, topi, gates, rank, cnt = _router(ys, w_out, h, g_m, g, shift, scale, wr_split, br_p, seq, tm)

    counts = cnt[:, 0].astype(I32)
    padded = (counts + EXPERT_BLOCK - 1) // EXPERT_BLOCK * EXPERT_BLOCK
    pad_end = jnp.cumsum(padded)
    pad_start = pad_end - padded
    nblk = m * TOP_K // EXPERT_BLOCK + N_EXPERTS
    n_rows = nblk * EXPERT_BLOCK
    n_used = pad_end[-1:] // EXPERT_BLOCK
    blk = jnp.arange(nblk, dtype=I32)
    blk_c = jnp.minimum(blk, n_used - 1)
    block_e = jnp.minimum(jnp.sum(blk_c[:, None] * EXPERT_BLOCK >= pad_end[None, :], axis=1),
                          N_EXPERTS - 1).astype(I32)
    first = jnp.concatenate([jnp.ones((1,), I32), (block_e[1:] != block_e[:-1]).astype(I32)])

    dest = _dest_rows(pad_start, topi, rank, min(m, 8192))
    xs = _sc_scatter_rows(up, dest, n_rows)
    y = _experts(block_e, n_used.astype(I32), first, xs, wgu, bgu, wd, bd, layer)
    y4 = _sc_gather_rows(y, dest.reshape(-1)).reshape(TOP_K, m, d // 2)
    return _combine_dense(y4, gates.T, h, gf, fnorm, seq, tm, final)


def _block_diag(w, group):
    nb, b, _ = w.shape
    per = group // b
    wg = w.reshape(nb // per, per, b, b)
    dense = jnp.einsum("gnde,nm->gndme", wg, jnp.eye(per, dtype=w.dtype))
    return dense.reshape(nb // per, group, group)


def kernel(x, c, mod_w, mod_b, norm_mix, norm_ffn, ev_w_in, ev_lru_conv_w, ev_lru_conv_b, ev_lru_w_r, ev_lru_b_r, ev_lru_w_i, ev_lru_b_i, ev_lru_lambda, ev_ml_conv_w, ev_ml_conv_b, ev_ml_w_q, ev_ml_w_k, ev_ml_w_v, ev_ml_w_ig, ev_ml_b_ig, ev_ml_w_fg, ev_ml_b_fg, ev_ml_norm, ev_ml_skip, ev_w_out, od_w_in, od_conv_w, od_conv_b, od_dt_bias, od_a_log, od_d, od_norm, od_w_out, moe_router_w, moe_router_b, moe_w_gu, moe_b_gu, moe_w_down, moe_b_down, final_norm):
    bsz, seq, d = x.shape
    depth = mod_w.shape[0]
    m = bsz * seq
    mod = _modulation(c, mod_w, mod_b)
    h = x.reshape(m, d).astype(F32)
    for layer in range(depth):
        sh_m, sc_m, g_m, sh_f, sc_f, g_f = (mod[layer, i] for i in range(6))
        j = layer // 2
        if layer % 2 == 0:
            w = ev_lru_lambda.shape[1]
            w_in = ev_w_in[j].astype(BF16)
            proj = _inproj(h, norm_mix[layer], sh_m, sc_m, w_in, None, [w_in.shape[1]], seq, 512)[0]
            lru_p = dict(conv_w=ev_lru_conv_w[j], conv_b=ev_lru_conv_b[j].reshape(1, w),
                         w_r=ev_lru_w_r[j].astype(BF16), b_r=ev_lru_b_r[j].reshape(1, w),
                         w_i=ev_lru_w_i[j].astype(BF16), b_i=ev_lru_b_i[j].reshape(1, w),
                         lam=ev_lru_lambda[j].reshape(1, w))
            ya = _lru(proj, lru_p, bsz, seq, 256)
            wg = jnp.zeros((3 * w, LANES), F32)
            wg = wg.at[:, :ML_HEADS].set(ev_ml_w_ig[j]).at[:, ML_HEADS:2 * ML_HEADS].set(ev_ml_w_fg[j])
            bg = jnp.zeros((1, LANES), F32)
            bg = bg.at[0, :ML_HEADS].set(ev_ml_b_ig[j]).at[0, ML_HEADS:2 * ML_HEADS].set(ev_ml_b_fg[j])
            ml_p = dict(conv_w=ev_ml_conv_w[j], conv_b=ev_ml_conv_b[j].reshape(1, w),
                        w_q=_block_diag(ev_ml_w_q[j], LANES).astype(BF16),
                        w_k=_block_diag(ev_ml_w_k[j], LANES).astype(BF16),
                        w_v=_block_diag(ev_ml_w_v[j], LANES).astype(BF16),
                        w_g=wg.astype(BF16), b_g=bg,
                        norm=ev_ml_norm[j].reshape(1, w), skip=ev_ml_skip[j].reshape(1, w))
            yb = _mlstm(proj, ml_p, bsz, seq)
            ys, w_out = [ya, yb], ev_w_out[j].astype(BF16)
        else:
            inner = od_norm.shape[1]
            heads = od_dt_bias.shape[1]
            conv_ch = od_conv_w.shape[2]
            w_in = od_w_in[j]
            wdt = jnp.zeros((d, LANES), F32).at[:, :heads].set(w_in[:, inner + conv_ch:])
            z, xbc, dt_raw = _inproj(h, norm_mix[layer], sh_m, sc_m, w_in.astype(BF16),
                                     wdt.astype(BF16), [inner, conv_ch], seq, 512)
            pad = lambda v: jnp.zeros((1, LANES), F32).at[0, :heads].set(v)
            ssd_p = dict(conv_w=od_conv_w[j], conv_b=od_conv_b[j].reshape(1, conv_ch),
                         dt_bias=pad(od_dt_bias[j]), a_log=pad(od_a_log[j]),
                         d_skip=jnp.repeat(od_d[j], SSD_HEAD_DIM).reshape(1, inner),
                         norm=od_norm[j].reshape(1, inner))
            y = _ssd(z, xbc, dt_raw, ssd_p, bsz, seq)
            ys, w_out = [y], od_w_out[j].astype(BF16)
        h = _moe(ys, w_out, g_m, h, norm_ffn[layer], sh_f, sc_f, g_f, final_norm,
                 moe_router_w[layer], moe_router_b[layer],
                 moe_w_gu, moe_b_gu, moe_w_down, moe_b_down, layer, seq, final=(layer == depth - 1))
    return h.reshape(bsz, seq, d)
```

```python
import functools

import jax
import jax.numpy as jnp
from jax import lax
from jax.experimental import pallas as pl
from jax.experimental.pallas import tpu as pltpu
from jax.experimental.pallas import tpu_sc as plsc

F32 = jnp.float32
BF16 = jnp.bfloat16
I32 = jnp.int32
HIGHEST = lax.Precision.HIGHEST

EPS = 1e-6
CONV_WIDTH = 4
LANES = 128
SUBLANES = 8
LRU_HEADS = 8
LRU_C = 8.0
ML_HEADS = 8
ML_QKV_BLOCK = 4
CHUNK = 128
SSD_HEAD_DIM = 64
SSD_GROUPS = 8
SSD_STATE = 128
N_EXPERTS = 32
TOP_K = 4
SWIGLU_ALPHA = 1.702
SWIGLU_LIMIT = 7.0
EXPERT_BLOCK = 512
SEQ_PER_STEP = 2
CONV_COLS = 512
VMEM_LIMIT = 56 * 1024 * 1024


def _cparams(sem, **kw):
    return pltpu.CompilerParams(dimension_semantics=sem, vmem_limit_bytes=VMEM_LIMIT, **kw)


def _silu(x):
    half = 0.5 * x
    return half + half * jnp.tanh(half)


def _log_sigmoid(x):
    return jnp.minimum(x, 0.0) - jnp.log1p(jnp.exp(-jnp.abs(x)))


def _softplus(x):
    return jnp.maximum(x, 0.0) + jnp.log1p(jnp.exp(-jnp.abs(x)))


def _dot(a, b, **kw):
    return jnp.dot(a, b, preferred_element_type=F32, **kw)


def _dot_nt(a, b):
    return lax.dot_general(a, b, (((1,), (1,)), ((), ())), preferred_element_type=F32)


def _pack_pairs(x):
    w = x.shape[1] // 2
    lo = lax.bitcast_convert_type(x[:, :w].astype(BF16).astype(F32), I32)
    hi = lax.bitcast_convert_type(x[:, w:].astype(BF16).astype(F32), I32)
    return lax.shift_right_logical(lo, 16) | (hi & jnp.int32(-65536))


def _unpack_pairs(p):
    lo = lax.bitcast_convert_type(lax.shift_left(p, 16), F32)
    hi = lax.bitcast_convert_type(p & jnp.int32(-65536), F32)
    return jnp.concatenate([lo, hi], axis=1)


def _norm_mod(h, g, shift, scale):
    y = h * lax.rsqrt(jnp.mean(h * h, axis=-1, keepdims=True) + EPS)
    return (y * g) * (1.0 + scale) + shift


def _causal_conv(x, tail_ref, w_ref, b_ref, sl):
    t = x.shape[0]
    tail = tail_ref[:, sl]
    row8 = lax.broadcasted_iota(I32, tail.shape, 0)
    out = b_ref[:, sl] + x * w_ref[CONV_WIDTH - 1:CONV_WIDTH, sl]
    for k in range(1, CONV_WIDTH):
        xs = pltpu.roll(x, k, axis=0)
        first = jnp.where(row8 < k, pltpu.roll(tail, k, axis=0), xs[:SUBLANES])
        xs = jnp.concatenate([first, xs[SUBLANES:]], axis=0)
        out = out + xs * w_ref[CONV_WIDTH - 1 - k:CONV_WIDTH - k, sl]
    tail_ref[:, sl] = x[t - SUBLANES:]
    return out


def _shift_matrix(t):
    r = lax.broadcasted_iota(I32, ((CONV_WIDTH - 1) * t, t), 0)
    c = lax.broadcasted_iota(I32, ((CONV_WIDTH - 1) * t, t), 1)
    src = (r & (t - 1)) - lax.shift_right_logical(r, t.bit_length() - 1) - 1
    return (src == c).astype(BF16)


def _causal_conv_shifted(x, shifted, tail_ref, w_ref, b_ref, sl):
    t = x.shape[0]
    tail = tail_ref[:, sl]
    row8 = lax.broadcasted_iota(I32, tail.shape, 0)
    out = b_ref[:, sl] + x * w_ref[CONV_WIDTH - 1:CONV_WIDTH, sl]
    head = jnp.zeros_like(tail)
    for k in range(1, CONV_WIDTH):
        wk = w_ref[CONV_WIDTH - 1 - k:CONV_WIDTH - k, sl]
        out = out + shifted[(k - 1) * t:k * t] * wk
        head = head + jnp.where(row8 < k, pltpu.roll(tail, k, axis=0), 0.0) * wk
    tail_ref[:, sl] = x[t - SUBLANES:]
    return jnp.concatenate([out[:SUBLANES] + head, out[SUBLANES:]], axis=0)


def _mod_kernel(c_ref, w_ref, b_ref, o_ref):
    cond = _silu(c_ref[...])
    o_ref[0, 0] = _dot(cond, w_ref[0], precision=HIGHEST) + b_ref[0, 0]


def _modulation(c, mod_w, mod_b):
    depth, d, _ = mod_w.shape
    bsz = c.shape[0]
    out = pl.pallas_call(
        _mod_kernel,
        out_shape=jax.ShapeDtypeStruct((depth, 6, bsz, d), F32),
        grid=(depth, 6),
        in_specs=[pl.BlockSpec((bsz, d), lambda l, j: (0, 0)),
                  pl.BlockSpec((1, d, d), lambda l, j: (l, 0, j)),
                  pl.BlockSpec((1, 1, 1, d), lambda l, j: (l, j, 0, 0))],
        out_specs=pl.BlockSpec((1, 1, bsz, d), lambda l, j: (l, j, 0, 0)),
        compiler_params=_cparams(("parallel", "parallel")),
    )(c.astype(F32), mod_w, mod_b.reshape(depth, 6, 1, d))
    return out.reshape(depth, 6, bsz, 1, d)


def _inproj_kernel(h_ref, g_ref, sh_ref, sc_ref, w_ref, *rest, n_chunk, with_dt):
    if with_dt:
        wdt_ref, *o_refs, odt_ref = rest
    else:
        o_refs = rest
    u = _norm_mod(h_ref[...], g_ref[...], sh_ref[0], sc_ref[0]).astype(BF16)
    off = 0
    for o_ref in o_refs:
        for n0 in range(0, o_ref.shape[1], n_chunk):
            o_ref[:, n0:n0 + n_chunk] = _dot(u, w_ref[:, off + n0:off + n0 + n_chunk]).astype(o_ref.dtype)
        off += o_ref.shape[1]
    if with_dt:
        odt_ref[...] = _dot(u, wdt_ref[...])


def _inproj(h, g, shift, scale, w, wdt, splits, seq, tm):
    m, d = h.shape
    n = sum(splits)
    assert n <= w.shape[1] and n % LANES == 0
    tiles_per_seq = seq // tm
    bmap = lambda i: (i // tiles_per_seq, 0, 0)
    in_specs = [pl.BlockSpec((tm, d), lambda i: (i, 0)),
                pl.BlockSpec((1, d), lambda i: (0, 0)),
                pl.BlockSpec((1, 1, d), bmap),
                pl.BlockSpec((1, 1, d), bmap),
                pl.BlockSpec((d, n), lambda i: (0, 0), pipeline_mode=pl.Buffered(1))]
    out_shape = [jax.ShapeDtypeStruct((m, s), BF16) for s in splits]
    out_specs = [pl.BlockSpec((tm, s), lambda i: (i, 0)) for s in splits]
    args = [h, g.reshape(1, d), shift, scale, w]
    if wdt is not None:
        in_specs.append(pl.BlockSpec((d, LANES), lambda i: (0, 0)))
        out_shape.append(jax.ShapeDtypeStruct((m, LANES), F32))
        out_specs.append(pl.BlockSpec((tm, LANES), lambda i: (i, 0)))
        args.append(wdt)
    return pl.pallas_call(
        functools.partial(_inproj_kernel, n_chunk=1024, with_dt=wdt is not None),
        out_shape=out_shape, grid=(m // tm,), in_specs=in_specs, out_specs=out_specs,
        compiler_params=_cparams(("parallel",)),
    )(*args)


def _lru_kernel(xa_ref, ga_ref, cw_ref, cb_ref, wr_ref, br_ref, wi_ref, bi_ref, lam_ref,
                o_ref, tail_ref, hc_ref):
    @pl.when(pl.program_id(1) == 0)
    def _():
        tail_ref[...] = jnp.zeros_like(tail_ref)
        hc_ref[...] = jnp.zeros_like(hc_ref)

    t = xa_ref.shape[0]
    row_in_group = lax.broadcasted_iota(I32, (t, LANES), 0) % SUBLANES
    steps = [s for s in (1, 2, 4) if s < SUBLANES]
    masks = [row_in_group >= s for s in steps]
    for hh in range(LRU_HEADS):
        sl = slice(hh * LANES, (hh + 1) * LANES)
        xc = _causal_conv(xa_ref[:, sl].astype(F32), tail_ref, cw_ref, cb_ref, sl)
        xcb = xc.astype(BF16)
        r = jax.nn.sigmoid(_dot(xcb, wr_ref[hh]) + br_ref[:, sl])
        i = jax.nn.sigmoid(_dot(xcb, wi_ref[hh]) + bi_ref[:, sl])
        log_a = LRU_C * r * _log_sigmoid(lam_ref[:, sl])
        a = jnp.exp(log_a)
        th = jnp.tanh(log_a)
        n2 = -2.0 * th
        root = jnp.where(n2 > 0.0, n2 * lax.rsqrt(n2), 0.0)
        u = (root * lax.rsqrt(1.0 - th)) * (i * xc)
        def roll_in_groups(v, s):
            v3 = v.reshape(t // SUBLANES, SUBLANES, LANES)
            return pltpu.roll(v3, s, axis=1).reshape(t, LANES)

        for s, m in zip(steps, masks):
            u = jnp.where(m, u + a * roll_in_groups(u, s), u)
            a = jnp.where(m, a * roll_in_groups(a, s), a)
        carry = hc_ref[:, sl]
        groups = []
        for r0 in range(0, t, SUBLANES):
            hg = u[r0:r0 + SUBLANES] + a[r0:r0 + SUBLANES] * carry
            carry = hg[SUBLANES - 1:SUBLANES]
            groups.append(hg)
        hc_ref[:, sl] = carry
        h = jnp.concatenate(groups, axis=0)
        ga = ga_ref[:, sl].astype(F32)
        o_ref[:, sl] = (h * jax.nn.gelu(ga, approximate=True)).astype(o_ref.dtype)


def _lru(proj, p, bsz, seq, tm):
    m = proj.shape[0]
    w = LRU_HEADS * LANES
    nt = seq // tm
    vec = lambda: pl.BlockSpec((1, w), lambda b, j: (0, 0))
    return pl.pallas_call(
        _lru_kernel,
        out_shape=jax.ShapeDtypeStruct((m, w), BF16),
        grid=(bsz, nt),
        in_specs=[pl.BlockSpec((tm, w), lambda b, j: (b * nt + j, 0)),
                  pl.BlockSpec((tm, w), lambda b, j: (b * nt + j, 1)),
                  pl.BlockSpec((CONV_WIDTH, w), lambda b, j: (0, 0)), vec(),
                  pl.BlockSpec((LRU_HEADS, LANES, LANES), lambda b, j: (0, 0, 0)), vec(),
                  pl.BlockSpec((LRU_HEADS, LANES, LANES), lambda b, j: (0, 0, 0)), vec(), vec()],
        out_specs=pl.BlockSpec((tm, w), lambda b, j: (b * nt + j, 0)),
        scratch_shapes=[pltpu.VMEM((SUBLANES, w), F32), pltpu.VMEM((1, w), F32)],
        compiler_params=_cparams(("parallel", "arbitrary")),
    )(proj, proj, p["conv_w"], p["conv_b"], p["w_r"], p["b_r"], p["w_i"], p["b_i"], p["lam"])


def _mlstm_kernel(xb_ref, zb_ref, cw_ref, cb_ref, wq_ref, wk_ref, wv_ref, wg_ref, bg_ref,
                  nw_ref, sk_ref, o_ref, tail_ref, qkv_ref, xc_ref, caug_ref, m_ref):
    @pl.when(pl.program_id(1) == 0)
    def _():
        tail_ref[...] = jnp.zeros_like(tail_ref)
        caug_ref[...] = jnp.zeros_like(caug_ref)
        m_ref[...] = jnp.full(m_ref.shape, -jnp.inf, F32)

    for s in range(xb_ref.shape[0]):
        for c0 in range(0, xb_ref.shape[1], CHUNK):
            rows = pl.ds(c0, CHUNK)
            _mlstm_chunk(xb_ref.at[s, rows], zb_ref.at[s, rows], cw_ref, cb_ref, wq_ref, wk_ref, wv_ref,
                         wg_ref, bg_ref, nw_ref, sk_ref, o_ref.at[s, rows], tail_ref.at[s], qkv_ref.at[s],
                         xc_ref.at[s], caug_ref.at[s], m_ref.at[s])


def _mlstm_chunk(xb_ref, zb_ref, cw_ref, cb_ref, wq_ref, wk_ref, wv_ref, wg_ref, bg_ref,
                 nw_ref, sk_ref, o_ref, tail_ref, qkv_ref, xc_ref, caug_ref, m_ref):
    L = CHUNK
    width = ML_HEADS * LANES
    scale = LANES ** -0.5
    for hh in range(ML_HEADS):
        sl = slice(hh * LANES, (hh + 1) * LANES)
        xb = xb_ref[:, sl].astype(F32)
        xc = _silu(_causal_conv(xb, tail_ref, cw_ref, cb_ref, sl))
        xc_ref[:, sl] = xc
        xcb = xc.astype(BF16)
        qkv_ref[:, sl] = _dot(xcb, wq_ref[hh]).astype(BF16)
        qkv_ref[:, width + hh * LANES:width + (hh + 1) * LANES] = _dot(xcb, wk_ref[hh]).astype(BF16)
        qkv_ref[:, 2 * width + hh * LANES:2 * width + (hh + 1) * LANES] = (
            _dot(xb.astype(BF16), wv_ref[hh]).astype(BF16))

    gates = _dot(qkv_ref[...], wg_ref[...]) + bg_ref[...]
    rowi = lax.broadcasted_iota(I32, (L, L), 0)
    coli = lax.broadcasted_iota(I32, (L, L), 1)
    causal = rowi >= coli
    lf = jnp.where((coli >= ML_HEADS) & (coli < 2 * ML_HEADS), _log_sigmoid(gates), 0.0)
    tri = causal.astype(BF16)
    lf_hi = lf.astype(BF16)
    lf_mid = (lf - lf_hi.astype(F32)).astype(BF16)
    lf_lo = (lf - lf_hi.astype(F32) - lf_mid.astype(F32)).astype(BF16)
    gcum = _dot(tri, lf_hi) + (_dot(tri, lf_mid) + _dot(tri, lf_lo))
    x_col = jnp.where(coli < ML_HEADS, gates, gcum)
    x_row = x_col.T
    ones = jnp.ones((L, LANES), BF16)
    heads = range(ML_HEADS)
    hsl = [slice(hh * LANES, (hh + 1) * LANES) for hh in heads]

    qs = [qkv_ref[:, hsl[hh]] for hh in heads]
    ks = [qkv_ref[:, width + hh * LANES:width + (hh + 1) * LANES] for hh in heads]
    vaugs = [jnp.concatenate([qkv_ref[:, 2 * width + hh * LANES:2 * width + (hh + 1) * LANES], ones], axis=1)
             for hh in heads]
    scores = [_dot_nt(qs[hh], ks[hh]) * scale for hh in heads]
    ics = [jnp.broadcast_to(x_col[:, hh:hh + 1], (L, LANES)) for hh in heads]
    gcs = [jnp.broadcast_to(x_col[:, ML_HEADS + hh:ML_HEADS + hh + 1], (L, LANES)) for hh in heads]
    irs = [x_row[hh:hh + 1, :] for hh in heads]
    grs = [x_row[ML_HEADS + hh:ML_HEADS + hh + 1, :] for hh in heads]
    mps = [m_ref[hh] for hh in heads]
    dmats = [jnp.where(causal, gcs[hh] - grs[hh] + irs[hh], -jnp.inf) for hh in heads]
    m_inters = [mps[hh] + gcs[hh] for hh in heads]
    m_ts = [jnp.maximum(m_inters[hh], jnp.max(dmats[hh], axis=1, keepdims=True)) for hh in heads]
    qks = [(scores[hh] * jnp.exp(dmats[hh] - m_ts[hh])).astype(BF16) for hh in heads]
    caugs = [caug_ref[hh] for hh in heads]
    w_inters = [jnp.exp(m_inters[hh] - m_ts[hh]) for hh in heads]
    nds = [_dot(qks[hh], vaugs[hh])
           + jnp.concatenate([w_inters[hh], w_inters[hh]], axis=1) * _dot(qs[hh], caugs[hh].astype(BF16))
           for hh in heads]

    g_lasts = [gcs[hh][L - 1:L, :] for hh in heads]
    m_news = [jnp.maximum(mps[hh] + g_lasts[hh],
                          jnp.max(g_lasts[hh] - grs[hh] + irs[hh], axis=1, keepdims=True)) for hh in heads]
    for hh in heads:
        ws = jnp.exp(g_lasts[hh] - gcs[hh] + ics[hh] - m_news[hh])
        wc = jnp.exp(mps[hh] + g_lasts[hh] - m_news[hh])
        kw_t = (ks[hh].astype(F32) * (ws * scale)).T.astype(BF16)
        caug_ref[hh] = jnp.concatenate([wc, wc], axis=1) * caugs[hh] + _dot(kw_t, vaugs[hh])
        m_ref[hh] = m_news[hh]

    hvals = [nds[hh][:, :LANES] / jnp.maximum(jnp.abs(nds[hh][:, LANES:]), jnp.exp(-m_ts[hh]))
             for hh in heads]
    mus = [jnp.mean(hvals[hh], axis=1, keepdims=True) for hh in heads]
    dvs = [hvals[hh] - mus[hh] for hh in heads]
    variances = [jnp.mean(dvs[hh] * dvs[hh], axis=1, keepdims=True) for hh in heads]
    for hh in heads:
        sl = hsl[hh]
        hn = dvs[hh] * lax.rsqrt(variances[hh] + EPS) * nw_ref[:, sl]
        zb = zb_ref[:, sl].astype(F32)
        o_ref[:, sl] = ((hn + sk_ref[:, sl] * xc_ref[:, sl]) * _silu(zb)).astype(o_ref.dtype)


def _mlstm(proj, p, bsz, seq):
    m = proj.shape[0]
    w = ML_HEADS * LANES
    sps = 1
    cps = 2 if seq % (2 * CHUNK) == 0 else 1
    nt = seq // (cps * CHUNK)
    vec = lambda: pl.BlockSpec((1, w), lambda b, j: (0, 0))
    blk = lambda: pl.BlockSpec((ML_HEADS, LANES, LANES), lambda b, j: (0, 0, 0))
    tile = lambda col: pl.BlockSpec((sps, cps * CHUNK, w), lambda b, j: (b, j, col))
    proj3 = proj.reshape(bsz, seq, proj.shape[1])
    out = pl.pallas_call(
        _mlstm_kernel,
        out_shape=jax.ShapeDtypeStruct((bsz, seq, w), BF16),
        grid=(bsz // sps, nt),
        in_specs=[tile(2), tile(3),
                  pl.BlockSpec((CONV_WIDTH, w), lambda b, j: (0, 0)), vec(),
                  blk(), blk(), blk(),
                  pl.BlockSpec((3 * w, LANES), lambda b, j: (0, 0)),
                  pl.BlockSpec((1, LANES), lambda b, j: (0, 0)),
                  vec(), vec()],
        out_specs=tile(0),
        scratch_shapes=[pltpu.VMEM((sps, SUBLANES, w), F32),
                        pltpu.VMEM((sps, CHUNK, 3 * w), BF16),
                        pltpu.VMEM((sps, CHUNK, w), F32),
                        pltpu.VMEM((sps, ML_HEADS, LANES, 2 * LANES), F32),
                        pltpu.VMEM((sps, ML_HEADS, 1, LANES), F32)],
        compiler_params=_cparams(("parallel", "arbitrary")),
    )(proj3, proj3, p["conv_w"], p["conv_b"], p["w_q"], p["w_k"], p["w_v"], p["w_g"], p["b_g"],
      p["norm"], p["skip"])
    return out.reshape(m, w)


def _ssd_kernel(z_ref, xbc_ref, dt_ref, cw_ref, cb_ref, dtb_ref, alog_ref, dsk_ref, nw_ref,
                o_ref, tail_ref, act_ref, st_ref):
    @pl.when(pl.program_id(1) == 0)
    def _():
        tail_ref[...] = jnp.zeros_like(tail_ref)
        st_ref[...] = jnp.zeros_like(st_ref)

    for s in range(z_ref.shape[0]):
        _ssd_chunk(z_ref.at[s], xbc_ref.at[s], dt_ref.at[s], cw_ref, cb_ref, dtb_ref, alog_ref, dsk_ref,
                   nw_ref, o_ref.at[s], tail_ref.at[s], act_ref.at[s], st_ref.at[s])


def _ssd_chunk(z_ref, xbc_ref, dt_ref, cw_ref, cb_ref, dtb_ref, alog_ref, dsk_ref, nw_ref,
               o_ref, tail_ref, act_ref, st_ref):
    L = CHUNK
    inner = o_ref.shape[1]
    gw = inner // SSD_GROUPS
    hpg = gw // SSD_HEAD_DIM
    b_off = inner
    c_off = inner + SSD_GROUPS * SSD_STATE
    shift = _shift_matrix(L)
    for c0 in range(0, xbc_ref.shape[1], CONV_COLS):
        shifted = _dot(shift, xbc_ref[:, c0:c0 + CONV_COLS])
        for l0 in range(0, CONV_COLS, LANES):
            sl = slice(c0 + l0, c0 + l0 + LANES)
            act_ref[:, sl] = _silu(_causal_conv_shifted(
                xbc_ref[:, sl].astype(F32), shifted[:, l0:l0 + LANES], tail_ref, cw_ref, cb_ref, sl))

    rowi = lax.broadcasted_iota(I32, (L, L), 0)
    coli = lax.broadcasted_iota(I32, (L, L), 1)
    causal = rowi >= coli
    dt = _softplus(dt_ref[...] + dtb_ref[...])
    da = dt * (-jnp.exp(alog_ref[...]))
    tri = causal.astype(BF16)
    da_hi = da.astype(BF16)
    da_mid = (da - da_hi.astype(F32)).astype(BF16)
    da_lo = (da - da_hi.astype(F32) - da_mid.astype(F32)).astype(BF16)
    a = _dot(tri, da_hi) + (_dot(tri, da_mid) + _dot(tri, da_lo))
    a_t = a.T
    hpt = LANES // SSD_HEAD_DIM
    lane = lax.broadcasted_iota(I32, (L, LANES), 1)

    def over_heads(tiles):
        cols = []
        for c0 in range(0, hpg, hpt):
            out = tiles[c0 + hpt - 1]
            for j in range(hpt - 2, -1, -1):
                out = jnp.where(lane < (j + 1) * SSD_HEAD_DIM, tiles[c0 + j], out)
            cols.append(out)
        return jnp.concatenate(cols, axis=1)

    for g in range(SSD_GROUPS):
        gsl = slice(g * gw, (g + 1) * gw)
        xg = act_ref[:, gsl]
        bg = act_ref[:, b_off + g * SSD_STATE:b_off + (g + 1) * SSD_STATE]
        cg_ = act_ref[:, c_off + g * SSD_STATE:c_off + (g + 1) * SSD_STATE].astype(BF16)
        cb = _dot_nt(cg_, bg.astype(BF16))
        state = st_ref[g]
        hds = [g * hpg + jj for jj in range(hpg)]
        a_bs = [jnp.broadcast_to(a[:, hd:hd + 1], (L, LANES)) for hd in hds]
        dt_bs = [jnp.broadcast_to(dt[:, hd:hd + 1], (L, LANES)) for hd in hds]
        ea_x = over_heads([jnp.exp(a_b) for a_b in a_bs])
        to_end_x = over_heads([jnp.exp(a_b[L - 1:L, :] - a_b) for a_b in a_bs])
        xdt = xg * over_heads(dt_bs)
        inter = _dot(cg_, state.astype(BF16)) * ea_x
        acc = [inter[:, c0:c0 + LANES] for c0 in range(0, gw, LANES)]
        for jj in range(hpg):
            seg = jnp.where(causal, a_bs[jj] - a_t[hds[jj]:hds[jj] + 1, :], -jnp.inf)
            w = (cb * jnp.exp(seg)).astype(BF16)
            c, j = divmod(jj, hpt)
            in_head = (lane >= j * SSD_HEAD_DIM) & (lane < (j + 1) * SSD_HEAD_DIM)
            x_head = jnp.where(in_head, xdt[:, c * LANES:(c + 1) * LANES], 0.0).astype(BF16)
            acc[c] = acc[c] + _dot(w, x_head)
        acc = jnp.concatenate(acc, axis=1)
        y = (acc + dsk_ref[:, gsl] * xg) * _silu(z_ref[:, gsl].astype(F32))
        y = y * lax.rsqrt(jnp.mean(y * y, axis=1, keepdims=True) + EPS) * nw_ref[:, gsl]
        o_ref[:, gsl] = y.astype(o_ref.dtype)
        xw = (xdt * to_end_x).astype(BF16)
        st_ref[g] = ea_x[L - 1:L, :] * state + _dot(bg.T.astype(BF16), xw)


def _ssd(z, xbc, dt_raw, p, bsz, seq):
    m, inner = z.shape
    nt = seq // CHUNK
    conv_ch = xbc.shape[1]
    sps = SEQ_PER_STEP if bsz % SEQ_PER_STEP == 0 else 1
    vec = lambda n: pl.BlockSpec((1, n), lambda b, j: (0, 0))
    tile = lambda n: pl.BlockSpec((sps, CHUNK, n), lambda b, j: (b, j, 0))
    out = pl.pallas_call(
        _ssd_kernel,
        out_shape=jax.ShapeDtypeStruct((bsz, seq, inner), BF16),
        grid=(bsz // sps, nt),
        in_specs=[tile(inner), tile(conv_ch), tile(LANES),
                  pl.BlockSpec((CONV_WIDTH, conv_ch), lambda b, j: (0, 0)), vec(conv_ch),
                  vec(LANES), vec(LANES), vec(inner), vec(inner)],
        out_specs=tile(inner),
        scratch_shapes=[pltpu.VMEM((sps, SUBLANES, conv_ch), F32),
                        pltpu.VMEM((sps, CHUNK, conv_ch), F32),
                        pltpu.VMEM((sps, SSD_GROUPS, SSD_STATE, inner // SSD_GROUPS), F32)],
        compiler_params=_cparams(("parallel", "arbitrary")),
    )(z.reshape(bsz, seq, inner), xbc.reshape(bsz, seq, conv_ch), dt_raw.reshape(bsz, seq, LANES),
      p["conv_w"], p["conv_b"], p["dt_bias"], p["a_log"], p["d_skip"], p["norm"])
    return out.reshape(m, inner)


def _router_kernel(*refs, n_in):
    y_refs, w_refs = refs[:n_in], refs[n_in:2 * n_in]
    (h_ref, gm_ref, g_ref, sh_ref, sc_ref, wr_ref, br_ref,
     hmid_ref, up_ref, topi_ref, gate_ref, rank_ref, cnt_ref, carry_ref) = refs[2 * n_in:]

    @pl.when(pl.program_id(0) == 0)
    def _():
        carry_ref[...] = jnp.zeros_like(carry_ref)

    tm = h_ref.shape[0]
    acc = _dot(y_refs[0][...], w_refs[0][...])
    for y_ref, w_ref in zip(y_refs[1:], w_refs[1:]):
        acc = acc + _dot(y_ref[...], w_ref[...])
    hmid = h_ref[...] + gm_ref[0] * acc
    hmid_ref[...] = hmid
    u = _norm_mod(hmid, g_ref[...], sh_ref[0], sc_ref[0])
    up_ref[...] = _pack_pairs(u)
    u_hi = u.astype(BF16)
    u_lo = (u - u_hi.astype(F32)).astype(BF16)
    logits = (_dot(u_hi, wr_ref[0]) + (_dot(u_lo, wr_ref[0]) + _dot(u_hi, wr_ref[1]))
              + br_ref[...])
    lt = jnp.concatenate([logits[r0:r0 + LANES].T for r0 in range(0, tm, LANES)], axis=1)
    l = lt[:N_EXPERTS]
    e_iota = lax.broadcasted_iota(I32, (N_EXPERTS, tm), 0).astype(F32)
    vals, idxs, hots = [], [], []
    for _ in range(TOP_K):
        mx = jnp.max(l, axis=0, keepdims=True)
        idx = jnp.min(jnp.where(l == mx, e_iota, float(N_EXPERTS)), axis=0, keepdims=True)
        hot = e_iota == idx
        l = jnp.where(hot, -jnp.inf, l)
        vals.append(mx)
        idxs.append(idx)
        hots.append(hot)
    exps = [jnp.exp(v - vals[0]) for v in vals]
    den = exps[0] + exps[1] + exps[2] + exps[3]
    gate_ref[...] = jnp.concatenate([e / den for e in exps], axis=0)
    topi_ref[...] = jnp.concatenate(idxs, axis=0).astype(I32)

    sel = jnp.zeros((N_EXPERTS, tm), F32)
    for hot in hots:
        sel = jnp.where(hot, 1.0, sel)
    r_i = lax.broadcasted_iota(I32, (tm, tm), 0)
    c_i = lax.broadcasted_iota(I32, (tm, tm), 1)
    before = (r_i < c_i).astype(BF16)
    carry = carry_ref[:, 0:1]
    cum = _dot(sel.astype(BF16), before) + carry
    rank_ref[...] = jnp.concatenate(
        [jnp.sum(jnp.where(hot, cum, 0.0), axis=0, keepdims=True) for hot in hots], axis=0).astype(I32)
    total = carry + jnp.sum(sel, axis=1, keepdims=True)
    carry_ref[...] = jnp.broadcast_to(total, carry_ref.shape)
    cnt_ref[...] = jnp.broadcast_to(total, cnt_ref.shape)


def _router(ys, w_out, h, g_m, g, shift, scale, wr, br, seq, tm):
    m, d = h.shape
    tiles_per_seq = seq // tm
    bmap = lambda i: (i // tiles_per_seq, 0, 0)
    row4 = lambda: pl.BlockSpec((TOP_K, tm), lambda i: (0, i))
    in_specs, args, k0 = [], [], 0
    for y in ys:
        in_specs.append(pl.BlockSpec((tm, y.shape[1]), lambda i: (i, 0)))
        args.append(y)
    for y in ys:
        kk = y.shape[1]
        in_specs.append(pl.BlockSpec((kk, d), lambda i, kb=k0 // kk: (kb, 0)))
        args.append(w_out)
        k0 += kk
    in_specs += [pl.BlockSpec((tm, d), lambda i: (i, 0)),
                 pl.BlockSpec((1, 1, d), bmap),
                 pl.BlockSpec((1, d), lambda i: (0, 0)),
                 pl.BlockSpec((1, 1, d), bmap), pl.BlockSpec((1, 1, d), bmap),
                 pl.BlockSpec((2, d, LANES), lambda i: (0, 0, 0)),
                 pl.BlockSpec((1, LANES), lambda i: (0, 0))]
    args += [h, g_m, g.reshape(1, d), shift, scale, wr, br]
    return pl.pallas_call(
        functools.partial(_router_kernel, n_in=len(ys)),
        out_shape=[jax.ShapeDtypeStruct((m, d), F32),
                   jax.ShapeDtypeStruct((m, d // 2), I32),
                   jax.ShapeDtypeStruct((TOP_K, m), I32),
                   jax.ShapeDtypeStruct((TOP_K, m), F32),
                   jax.ShapeDtypeStruct((TOP_K, m), I32),
                   jax.ShapeDtypeStruct((N_EXPERTS, LANES), F32)],
        grid=(m // tm,),
        in_specs=in_specs,
        out_specs=[pl.BlockSpec((tm, d), lambda i: (i, 0)),
                   pl.BlockSpec((tm, d // 2), lambda i: (i, 0)), row4(), row4(), row4(),
                   pl.BlockSpec((N_EXPERTS, LANES), lambda i: (0, 0))],
        scratch_shapes=[pltpu.VMEM((N_EXPERTS, LANES), F32)],
        compiler_params=_cparams(("arbitrary",)),
    )(*args)


def _dest_kernel(ps_ref, topi_ref, rank_ref, o_ref):
    topi = topi_ref[...]
    acc = rank_ref[...]
    for e in range(N_EXPERTS):
        acc = acc + jnp.where(topi == e, ps_ref[e], 0)
    o_ref[...] = acc


def _dest_rows(pad_start, topi, rank, tw):
    k, m = topi.shape
    blk = lambda: pl.BlockSpec((k, tw), lambda i, ps: (0, i))
    return pl.pallas_call(
        _dest_kernel,
        out_shape=jax.ShapeDtypeStruct((k, m), I32),
        grid_spec=pltpu.PrefetchScalarGridSpec(
            num_scalar_prefetch=1, grid=(m // tw,), in_specs=[blk(), blk()], out_specs=blk()),
        compiler_params=_cparams(("parallel",)),
    )(pad_start, topi, rank)


SC_CORES = 2
SC_SUBCORES = 16
SC_ROWS = 128


def _sc_gather_rows(table, idx):
    b = idx.shape[0]
    w = table.shape[1]
    workers = SC_CORES * SC_SUBCORES
    per_w = b // workers
    assert per_w * workers == b and per_w % SC_ROWS == 0
    mesh = plsc.VectorSubcoreMesh(core_axis_name="c", subcore_axis_name="s")

    @functools.partial(
        pl.kernel, mesh=mesh, out_type=jax.ShapeDtypeStruct((b, w), I32),
        scratch_types=[pltpu.VMEM((SC_ROWS,), I32), pltpu.VMEM((SC_ROWS, w), I32),
                       pltpu.SemaphoreType.DMA])
    def gather(table_hbm, idx_hbm, out_hbm, idx_v, rows_v, sem):
        base = (lax.axis_index("s") * SC_CORES + lax.axis_index("c")) * per_w

        @pl.loop(0, per_w // SC_ROWS)
        def _(c):
            off = base + c * SC_ROWS
            pltpu.sync_copy(idx_hbm.at[pl.ds(off, SC_ROWS)], idx_v)
            pltpu.async_copy(table_hbm.at[idx_v], rows_v, sem).wait()
            pltpu.sync_copy(rows_v, out_hbm.at[pl.ds(off, SC_ROWS)])

    return gather(table, idx)


def _sc_scatter_rows(rows, dest, n_rows):
    m, w = rows.shape
    kk = dest.shape[0]
    workers = SC_CORES * SC_SUBCORES
    per_w = m // workers
    assert per_w * workers == m and per_w % SC_ROWS == 0
    mesh = plsc.VectorSubcoreMesh(core_axis_name="c", subcore_axis_name="s")

    @functools.partial(
        pl.kernel, mesh=mesh, out_type=jax.ShapeDtypeStruct((n_rows, w), I32),
        scratch_types=[pltpu.VMEM((SC_ROWS,), I32), pltpu.VMEM((SC_ROWS, w), I32),
                       pltpu.SemaphoreType.DMA])
    def scatter(rows_hbm, dest_hbm, out_hbm, idx_v, rows_v, sem):
        base = (lax.axis_index("s") * SC_CORES + lax.axis_index("c")) * per_w

        @pl.loop(0, per_w // SC_ROWS)
        def _(c):
            off = base + c * SC_ROWS
            pltpu.sync_copy(rows_hbm.at[pl.ds(off, SC_ROWS)], rows_v)
            for k in range(kk):
                pltpu.sync_copy(dest_hbm.at[pl.ds(k * m + off, SC_ROWS)], idx_v)
                pltpu.async_copy(rows_v, out_hbm.at[idx_v], sem).wait()

    return scatter(rows, dest.reshape(-1))


def _combine_dense_kernel(y_ref, gate_ref, h_ref, gf_ref, fn_ref, o_ref, *, final):
    acc = gate_ref[:, 0:1] * _unpack_pairs(y_ref[0])
    for k in range(1, TOP_K):
        acc = acc + gate_ref[:, k:k + 1] * _unpack_pairs(y_ref[k])
    hn = h_ref[...] + gf_ref[0] * acc
    if final:
        hn = hn * lax.rsqrt(jnp.mean(hn * hn, axis=-1, keepdims=True) + EPS) * fn_ref[...]
    o_ref[...] = hn


def _combine_dense(y4, gates_col, h, gf, fnorm, seq, tm, final):
    m, d = h.shape
    tiles_per_seq = seq // tm
    return pl.pallas_call(
        functools.partial(_combine_dense_kernel, final=final),
        out_shape=jax.ShapeDtypeStruct((m, d), F32),
        grid=(m // tm,),
        in_specs=[pl.BlockSpec((TOP_K, tm, d // 2), lambda i: (0, i, 0)),
                  pl.BlockSpec((tm, TOP_K), lambda i: (i, 0)),
                  pl.BlockSpec((tm, d), lambda i: (i, 0)),
                  pl.BlockSpec((1, 1, d), lambda i: (i // tiles_per_seq, 0, 0)),
                  pl.BlockSpec((1, d), lambda i: (0, 0))],
        out_specs=pl.BlockSpec((tm, d), lambda i: (i, 0)),
        compiler_params=_cparams(("parallel",)),
    )(y4, gates_col, h, gf, fnorm.reshape(1, d))


def _expert_kernel(be_ref, nb_ref, first_ref, x_ref, wgu_ref, bgu_ref, wd_ref, bd_ref, y_ref,
                   wgu_bf, wd_bf):
    i = pl.program_id(0)

    @pl.when(i < nb_ref[0])
    def _():
        dff = wd_bf.shape[0]

        @pl.when(first_ref[i] == 1)
        def _():
            rows = 64

            def cast(r, c):
                r0 = pl.multiple_of(r * rows, rows)
                wgu_bf[pl.ds(r0, rows), :] = wgu_ref[0, 0, pl.ds(r0, rows), :].astype(BF16)
                wd_bf[pl.ds(r0, rows), :] = wd_ref[0, 0, pl.ds(r0, rows), :].astype(BF16)
                return c

            lax.fori_loop(0, dff // rows, cast, 0)

        x = _unpack_pairs(x_ref[...]).astype(BF16)
        hb = _dot(x, wgu_bf[...]) + bgu_ref[0, 0]
        h_glu = jnp.minimum(hb[:, :dff], SWIGLU_LIMIT)
        h_lin = jnp.clip(hb[:, dff:], -SWIGLU_LIMIT, SWIGLU_LIMIT)
        half = 0.5 * h_glu
        act = (half + half * jnp.tanh(SWIGLU_ALPHA * half)) * (h_lin + 1.0)
        y_ref[...] = _pack_pairs(_dot(act.astype(BF16), wd_bf[...]) + bd_ref[0, 0])


def _experts(block_e, n_used, first, xs, wgu, bgu, wd, bd, layer):
    n_rows, wp = xs.shape
    _, ne, d, ff2 = wgu.shape
    assert d == ff2 // 2
    nblk = n_rows // EXPERT_BLOCK

    def xmap(i, be, nb, fi):
        return (jnp.minimum(i, nb[0] - 1), 0)

    emap = lambda i, be, nb, fi: (layer, be[i], 0, 0)
    grid_spec = pltpu.PrefetchScalarGridSpec(
        num_scalar_prefetch=3, grid=(nblk,),
        in_specs=[pl.BlockSpec((EXPERT_BLOCK, wp), xmap),
                  pl.BlockSpec((1, 1, d, ff2), emap), pl.BlockSpec((1, 1, 1, ff2), emap),
                  pl.BlockSpec((1, 1, ff2 // 2, d), emap), pl.BlockSpec((1, 1, 1, d), emap)],
        out_specs=pl.BlockSpec((EXPERT_BLOCK, wp), xmap),
        scratch_shapes=[pltpu.VMEM((d, ff2), BF16), pltpu.VMEM((ff2 // 2, d), BF16)])
    depth = wgu.shape[0]
    return pl.pallas_call(
        _expert_kernel,
        out_shape=jax.ShapeDtypeStruct((n_rows, wp), I32),
        grid_spec=grid_spec,
        compiler_params=_cparams(("arbitrary",)),
    )(block_e, n_used, first, xs, wgu, bgu.reshape(depth, ne, 1, ff2), wd, bd.reshape(depth, ne, 1, d))


def _moe(ys, w_out, g_m, h, g, shift, scale, gf, fnorm, wr, br, wgu, bgu, wd, bd, layer, seq, final):
    m, d = h.shape
    tm = 512
    wr_p = jnp.zeros((d, LANES), F32).at[:, :N_EXPERTS].set(wr)
    br_p = jnp.zeros((1, LANES), F32).at[0, :N_EXPERTS].set(br)
    wr_hi = wr_p.astype(BF16)
    wr_split = jnp.stack([wr_hi, (wr_p - wr_hi.astype(F32)).astype(BF16)])
    h, up, topi, gates, rank, cnt = _router(ys, w_out, h, g_m, g, shift, scale, wr_split, br_p, seq, tm)

    counts = cnt[:, 0].astype(I32)
    padded = (counts + EXPERT_BLOCK - 1) // EXPERT_BLOCK * EXPERT_BLOCK
    pad_end = jnp.cumsum(padded)
    pad_start = pad_end - padded
    nblk = m * TOP_K // EXPERT_BLOCK + N_EXPERTS
    n_rows = nblk * EXPERT_BLOCK
    n_used = pad_end[-1:] // EXPERT_BLOCK
    blk = jnp.arange(nblk, dtype=I32)
    blk_c = jnp.minimum(blk, n_used - 1)
    block_e = jnp.minimum(jnp.sum(blk_c[:, None] * EXPERT_BLOCK >= pad_end[None, :], axis=1),
                          N_EXPERTS - 1).astype(I32)
    first = jnp.concatenate([jnp.ones((1,), I32), (block_e[1:] != block_e[:-1]).astype(I32)])

    dest = _dest_rows(pad_start, topi, rank, min(m, 8192))
    xs = _sc_scatter_rows(up, dest, n_rows)
    y = _experts(block_e, n_used.astype(I32), first, xs, wgu, bgu, wd, bd, layer)
    y4 = _sc_gather_rows(y, dest.reshape(-1)).reshape(TOP_K, m, d // 2)
    return _combine_dense(y4, gates.T, h, gf, fnorm, seq, tm, final)


def _block_diag(w, group):
    nb, b, _ = w.shape
    per = group // b
    wg = w.reshape(nb // per, per, b, b)
    dense = jnp.einsum("gnde,nm->gndme", wg, jnp.eye(per, dtype=w.dtype))
    return dense.reshape(nb // per, group, group)


def kernel(x, c, mod_w, mod_b, norm_mix, norm_ffn, ev_w_in, ev_lru_conv_w, ev_lru_conv_b, ev_lru_w_r, ev_lru_b_r, ev_lru_w_i, ev_lru_b_i, ev_lru_lambda, ev_ml_conv_w, ev_ml_conv_b, ev_ml_w_q, ev_ml_w_k, ev_ml_w_v, ev_ml_w_ig, ev_ml_b_ig, ev_ml_w_fg, ev_ml_b_fg, ev_ml_norm, ev_ml_skip, ev_w_out, od_w_in, od_conv_w, od_conv_b, od_dt_bias, od_a_log, od_d, od_norm, od_w_out, moe_router_w, moe_router_b, moe_w_gu, moe_b_gu, moe_w_down, moe_b_down, final_norm):
    bsz, seq, d = x.shape
    depth = mod_w.shape[0]
    m = bsz * seq
    mod = _modulation(c, mod_w, mod_b)
    h = x.reshape(m, d).astype(F32)
    for layer in range(depth):
        sh_m, sc_m, g_m, sh_f, sc_f, g_f = (mod[layer, i] for i in range(6))
        j = layer // 2
        if layer % 2 == 0:
            w = ev_lru_lambda.shape[1]
            w_in = ev_w_in[j].astype(BF16)
            proj = _inproj(h, norm_mix[layer], sh_m, sc_m, w_in, None, [w_in.shape[1]], seq, 512)[0]
            lru_p = dict(conv_w=ev_lru_conv_w[j], conv_b=ev_lru_conv_b[j].reshape(1, w),
                         w_r=ev_lru_w_r[j].astype(BF16), b_r=ev_lru_b_r[j].reshape(1, w),
                         w_i=ev_lru_w_i[j].astype(BF16), b_i=ev_lru_b_i[j].reshape(1, w),
                         lam=ev_lru_lambda[j].reshape(1, w))
            ya = _lru(proj, lru_p, bsz, seq, 256)
            wg = jnp.zeros((3 * w, LANES), F32)
            wg = wg.at[:, :ML_HEADS].set(ev_ml_w_ig[j]).at[:, ML_HEADS:2 * ML_HEADS].set(ev_ml_w_fg[j])
            bg = jnp.zeros((1, LANES), F32)
            bg = bg.at[0, :ML_HEADS].set(ev_ml_b_ig[j]).at[0, ML_HEADS:2 * ML_HEADS].set(ev_ml_b_fg[j])
            ml_p = dict(conv_w=ev_ml_conv_w[j], conv_b=ev_ml_conv_b[j].reshape(1, w),
                        w_q=_block_diag(ev_ml_w_q[j], LANES).astype(BF16),
                        w_k=_block_diag(ev_ml_w_k[j], LANES).astype(BF16),
                        w_v=_block_diag(ev_ml_w_v[j], LANES).astype(BF16),
                        w_g=wg.astype(BF16), b_g=bg,
                        norm=ev_ml_norm[j].reshape(1, w), skip=ev_ml_skip[j].reshape(1, w))
            yb = _mlstm(proj, ml_p, bsz, seq)
            ys, w_out = [ya, yb], ev_w_out[j].astype(BF16)
        else:
            inner = od_norm.shape[1]
            heads = od_dt_bias.shape[1]
            conv_ch = od_conv_w.shape[2]
            w_in = od_w_in[j]
            wdt = jnp.zeros((d, LANES), F32).at[:, :heads].set(w_in[:, inner + conv_ch:])
            z, xbc, dt_raw = _inproj(h, norm_mix[layer], sh_m, sc_m, w_in.astype(BF16),
                                     wdt.astype(BF16), [inner, conv_ch], seq, 512)
            pad = lambda v: jnp.zeros((1, LANES), F32).at[0, :heads].set(v)
            ssd_p = dict(conv_w=od_conv_w[j], conv_b=od_conv_b[j].reshape(1, conv_ch),
                         dt_bias=pad(od_dt_bias[j]), a_log=pad(od_a_log[j]),
                         d_skip=jnp.repeat(od_d[j], SSD_HEAD_DIM).reshape(1, inner),
                         norm=od_norm[j].reshape(1, inner))
            y = _ssd(z, xbc, dt_raw, ssd_p, bsz, seq)
            ys, w_out = [y], od_w_out[j].astype(BF16)
        h = _moe(ys, w_out, g_m, h, norm_ffn[layer], sh_f, sc_f, g_f, final_norm,
                 moe_router_w[layer], moe_router_b[layer],
                 moe_w_gu, moe_b_gu, moe_w_down, moe_b_down, layer, seq, final=(layer == depth - 1))
    return h.reshape(bsz, seq, d)
```

```python
import functools

import jax
import jax.numpy as jnp
from jax import lax
from jax.experimental import pallas as pl
from jax.experimental.pallas import tpu as pltpu
from jax.experimental.pallas import tpu_sc as plsc

F32 = jnp.float32
BF16 = jnp.bfloat16
I32 = jnp.int32
HIGHEST = lax.Precision.HIGHEST

EPS = 1e-6
CONV_WIDTH = 4
LANES = 128
SUBLANES = 8
LRU_HEADS = 8
LRU_C = 8.0
ML_HEADS = 8
CHUNK = 128
PROJ_TILE = 512
LRU_TILE = 512
MOE_TILE = 512
DEST_TILE = 8192
SSD_HEAD_DIM = 64
SSD_GROUPS = 8
SSD_STATE = 128
N_EXPERTS = 32
TOP_K = 4
SWIGLU_ALPHA = 1.702
SWIGLU_LIMIT = 7.0
EXPERT_BLOCK = 512
SEQ_PER_STEP = 2
CONV_COLS = 512
VMEM_LIMIT = 56 * 1024 * 1024


def _cparams(sem, **kw):
    return pltpu.CompilerParams(dimension_semantics=sem, vmem_limit_bytes=VMEM_LIMIT, **kw)


def _silu(x):
    half = 0.5 * x
    return half + half * jnp.tanh(half)


def _log_sigmoid(x):
    return jnp.minimum(x, 0.0) - jnp.log1p(jnp.exp(-jnp.abs(x)))


def _softplus(x):
    return jnp.maximum(x, 0.0) + jnp.log1p(jnp.exp(-jnp.abs(x)))


def _dot(a, b, **kw):
    return jnp.dot(a, b, preferred_element_type=F32, **kw)


def _dot_nt(a, b):
    return lax.dot_general(a, b, (((1,), (1,)), ((), ())), preferred_element_type=F32)


def _pack_pairs(x):
    w = x.shape[1] // 2
    lo = lax.bitcast_convert_type(x[:, :w].astype(BF16).astype(F32), I32)
    hi = lax.bitcast_convert_type(x[:, w:].astype(BF16).astype(F32), I32)
    return lax.shift_right_logical(lo, 16) | (hi & jnp.int32(-65536))


def _unpack_pairs(p):
    lo = lax.bitcast_convert_type(lax.shift_left(p, 16), F32)
    hi = lax.bitcast_convert_type(p & jnp.int32(-65536), F32)
    return jnp.concatenate([lo, hi], axis=1)


def _norm_mod(h, g, shift, scale):
    y = h * lax.rsqrt(jnp.mean(h * h, axis=-1, keepdims=True) + EPS)
    return (y * g) * (1.0 + scale) + shift


def _causal_conv(x, tail_ref, w_ref, b_ref, sl):
    t = x.shape[0]
    tail = tail_ref[:, sl]
    row8 = lax.broadcasted_iota(I32, tail.shape, 0)
    out = b_ref[:, sl] + x * w_ref[CONV_WIDTH - 1:CONV_WIDTH, sl]
    for k in range(1, CONV_WIDTH):
        xs = pltpu.roll(x, k, axis=0)
        first = jnp.where(row8 < k, pltpu.roll(tail, k, axis=0), xs[:SUBLANES])
        xs = jnp.concatenate([first, xs[SUBLANES:]], axis=0)
        out = out + xs * w_ref[CONV_WIDTH - 1 - k:CONV_WIDTH - k, sl]
    tail_ref[:, sl] = x[t - SUBLANES:]
    return out


def _shift_matrix(t):
    r = lax.broadcasted_iota(I32, ((CONV_WIDTH - 1) * t, t), 0)
    c = lax.broadcasted_iota(I32, ((CONV_WIDTH - 1) * t, t), 1)
    src = (r & (t - 1)) - lax.shift_right_logical(r, t.bit_length() - 1) - 1
    return (src == c).astype(BF16)


def _causal_conv_shifted(x, shifted, tail_ref, w_ref, b_ref, sl):
    t = x.shape[0]
    tail = tail_ref[:, sl]
    row8 = lax.broadcasted_iota(I32, tail.shape, 0)
    out = b_ref[:, sl] + x * w_ref[CONV_WIDTH - 1:CONV_WIDTH, sl]
    head = jnp.zeros_like(tail)
    for k in range(1, CONV_WIDTH):
        wk = w_ref[CONV_WIDTH - 1 - k:CONV_WIDTH - k, sl]
        out = out + shifted[(k - 1) * t:k * t] * wk
        head = head + jnp.where(row8 < k, pltpu.roll(tail, k, axis=0), 0.0) * wk
    tail_ref[:, sl] = x[t - SUBLANES:]
    return jnp.concatenate([out[:SUBLANES] + head, out[SUBLANES:]], axis=0)


def _mod_kernel(c_ref, w_ref, b_ref, o_ref):
    cond = _silu(c_ref[...])
    o_ref[0, 0] = _dot(cond, w_ref[0], precision=HIGHEST) + b_ref[0, 0]


def _modulation(c, mod_w, mod_b):
    depth, d, _ = mod_w.shape
    bsz = c.shape[0]
    out = pl.pallas_call(
        _mod_kernel,
        out_shape=jax.ShapeDtypeStruct((depth, 6, bsz, d), F32),
        grid=(depth, 6),
        in_specs=[pl.BlockSpec((bsz, d), lambda l, j: (0, 0)),
                  pl.BlockSpec((1, d, d), lambda l, j: (l, 0, j)),
                  pl.BlockSpec((1, 1, 1, d), lambda l, j: (l, j, 0, 0))],
        out_specs=pl.BlockSpec((1, 1, bsz, d), lambda l, j: (l, j, 0, 0)),
        compiler_params=_cparams(("parallel", "parallel")),
    )(c.astype(F32), mod_w, mod_b.reshape(depth, 6, 1, d))
    return out.reshape(depth, 6, bsz, 1, d)


def _inproj_kernel(h_ref, g_ref, sh_ref, sc_ref, w_ref, *rest, n_chunk, with_dt):
    if with_dt:
        wdt_ref, *o_refs, odt_ref = rest
    else:
        o_refs = rest
    u = _norm_mod(h_ref[...], g_ref[...], sh_ref[0], sc_ref[0]).astype(BF16)
    off = 0
    for o_ref in o_refs:
        for n0 in range(0, o_ref.shape[1], n_chunk):
            o_ref[:, n0:n0 + n_chunk] = _dot(u, w_ref[:, off + n0:off + n0 + n_chunk]).astype(o_ref.dtype)
        off += o_ref.shape[1]
    if with_dt:
        odt_ref[...] = _dot(u, wdt_ref[...])


def _inproj(h, g, shift, scale, w, wdt, splits, seq, tm):
    m, d = h.shape
    n = sum(splits)
    assert n <= w.shape[1] and n % LANES == 0
    tiles_per_seq = seq // tm
    bmap = lambda i: (i // tiles_per_seq, 0, 0)
    in_specs = [pl.BlockSpec((tm, d), lambda i: (i, 0)),
                pl.BlockSpec((1, d), lambda i: (0, 0)),
                pl.BlockSpec((1, 1, d), bmap),
                pl.BlockSpec((1, 1, d), bmap),
                pl.BlockSpec((d, n), lambda i: (0, 0), pipeline_mode=pl.Buffered(1))]
    out_shape = [jax.ShapeDtypeStruct((m, s), BF16) for s in splits]
    out_specs = [pl.BlockSpec((tm, s), lambda i: (i, 0)) for s in splits]
    args = [h, g.reshape(1, d), shift, scale, w]
    if wdt is not None:
        in_specs.append(pl.BlockSpec((d, LANES), lambda i: (0, 0)))
        out_shape.append(jax.ShapeDtypeStruct((m, LANES), F32))
        out_specs.append(pl.BlockSpec((tm, LANES), lambda i: (i, 0)))
        args.append(wdt)
    return pl.pallas_call(
        functools.partial(_inproj_kernel, n_chunk=1024, with_dt=wdt is not None),
        out_shape=out_shape, grid=(m // tm,), in_specs=in_specs, out_specs=out_specs,
        compiler_params=_cparams(("parallel",)),
    )(*args)


def _lru_kernel(xa_ref, ga_ref, cw_ref, cb_ref, wr_ref, br_ref, wi_ref, bi_ref, lam_ref,
                o_ref, tail_ref, hc_ref):
    @pl.when(pl.program_id(1) == 0)
    def _():
        tail_ref[...] = jnp.zeros_like(tail_ref)
        hc_ref[...] = jnp.zeros_like(hc_ref)

    t = xa_ref.shape[0]
    row_in_group = lax.broadcasted_iota(I32, (t, LANES), 0) % SUBLANES
    steps = [s for s in (1, 2, 4) if s < SUBLANES]
    masks = [row_in_group >= s for s in steps]
    for hh in range(LRU_HEADS):
        sl = slice(hh * LANES, (hh + 1) * LANES)
        xc = _causal_conv(xa_ref[:, sl].astype(F32), tail_ref, cw_ref, cb_ref, sl)
        xcb = xc.astype(BF16)
        r = jax.nn.sigmoid(_dot(xcb, wr_ref[hh]) + br_ref[:, sl])
        i = jax.nn.sigmoid(_dot(xcb, wi_ref[hh]) + bi_ref[:, sl])
        log_a = LRU_C * r * _log_sigmoid(lam_ref[:, sl])
        a = jnp.exp(log_a)
        th = jnp.tanh(log_a)
        n2 = -2.0 * th
        root = jnp.where(n2 > 0.0, n2 * lax.rsqrt(n2), 0.0)
        u = (root * lax.rsqrt(1.0 - th)) * (i * xc)
        def roll_in_groups(v, s):
            v3 = v.reshape(t // SUBLANES, SUBLANES, LANES)
            return pltpu.roll(v3, s, axis=1).reshape(t, LANES)

        for s, m in zip(steps, masks):
            u = jnp.where(m, u + a * roll_in_groups(u, s), u)
            a = jnp.where(m, a * roll_in_groups(a, s), a)
        carry = hc_ref[:, sl]
        groups = []
        for r0 in range(0, t, SUBLANES):
            hg = u[r0:r0 + SUBLANES] + a[r0:r0 + SUBLANES] * carry
            carry = hg[SUBLANES - 1:SUBLANES]
            groups.append(hg)
        hc_ref[:, sl] = carry
        h = jnp.concatenate(groups, axis=0)
        ga = ga_ref[:, sl].astype(F32)
        o_ref[:, sl] = (h * jax.nn.gelu(ga, approximate=True)).astype(o_ref.dtype)


def _lru(proj, p, bsz, seq, tm):
    m = proj.shape[0]
    w = LRU_HEADS * LANES
    nt = seq // tm
    vec = lambda: pl.BlockSpec((1, w), lambda b, j: (0, 0))
    return pl.pallas_call(
        _lru_kernel,
        out_shape=jax.ShapeDtypeStruct((m, w), BF16),
        grid=(bsz, nt),
        in_specs=[pl.BlockSpec((tm, w), lambda b, j: (b * nt + j, 0)),
                  pl.BlockSpec((tm, w), lambda b, j: (b * nt + j, 1)),
                  pl.BlockSpec((CONV_WIDTH, w), lambda b, j: (0, 0)), vec(),
                  pl.BlockSpec((LRU_HEADS, LANES, LANES), lambda b, j: (0, 0, 0)), vec(),
                  pl.BlockSpec((LRU_HEADS, LANES, LANES), lambda b, j: (0, 0, 0)), vec(), vec()],
        out_specs=pl.BlockSpec((tm, w), lambda b, j: (b * nt + j, 0)),
        scratch_shapes=[pltpu.VMEM((SUBLANES, w), F32), pltpu.VMEM((1, w), F32)],
        compiler_params=_cparams(("parallel", "arbitrary")),
    )(proj, proj, p["conv_w"], p["conv_b"], p["w_r"], p["b_r"], p["w_i"], p["b_i"], p["lam"])


def _mlstm_kernel(xb_ref, zb_ref, cw_ref, cb_ref, wq_ref, wk_ref, wv_ref, wg_ref, bg_ref,
                  nw_ref, sk_ref, o_ref, tail_ref, qkv_ref, xc_ref, caug_ref, m_ref):
    @pl.when(pl.program_id(1) == 0)
    def _():
        tail_ref[...] = jnp.zeros_like(tail_ref)
        caug_ref[...] = jnp.zeros_like(caug_ref)
        m_ref[...] = jnp.full(m_ref.shape, -jnp.inf, F32)

    for s in range(xb_ref.shape[0]):
        for c0 in range(0, xb_ref.shape[1], CHUNK):
            rows = pl.ds(c0, CHUNK)
            _mlstm_chunk(xb_ref.at[s, rows], zb_ref.at[s, rows], cw_ref, cb_ref, wq_ref, wk_ref, wv_ref,
                         wg_ref, bg_ref, nw_ref, sk_ref, o_ref.at[s, rows], tail_ref.at[s], qkv_ref.at[s],
                         xc_ref.at[s], caug_ref.at[s], m_ref.at[s])


def _mlstm_chunk(xb_ref, zb_ref, cw_ref, cb_ref, wq_ref, wk_ref, wv_ref, wg_ref, bg_ref,
                 nw_ref, sk_ref, o_ref, tail_ref, qkv_ref, xc_ref, caug_ref, m_ref):
    L = CHUNK
    width = ML_HEADS * LANES
    scale = LANES ** -0.5
    for hh in range(ML_HEADS):
        sl = slice(hh * LANES, (hh + 1) * LANES)
        xb = xb_ref[:, sl].astype(F32)
        xc = _silu(_causal_conv(xb, tail_ref, cw_ref, cb_ref, sl))
        xc_ref[:, sl] = xc
        xcb = xc.astype(BF16)
        qkv_ref[:, sl] = _dot(xcb, wq_ref[hh]).astype(BF16)
        qkv_ref[:, width + hh * LANES:width + (hh + 1) * LANES] = _dot(xcb, wk_ref[hh]).astype(BF16)
        qkv_ref[:, 2 * width + hh * LANES:2 * width + (hh + 1) * LANES] = (
            _dot(xb.astype(BF16), wv_ref[hh]).astype(BF16))

    gates = _dot(qkv_ref[...], wg_ref[...]) + bg_ref[...]
    rowi = lax.broadcasted_iota(I32, (L, L), 0)
    coli = lax.broadcasted_iota(I32, (L, L), 1)
    causal = rowi >= coli
    lf = jnp.where((coli >= ML_HEADS) & (coli < 2 * ML_HEADS), _log_sigmoid(gates), 0.0)
    tri = causal.astype(BF16)
    lf_hi = lf.astype(BF16)
    lf_mid = (lf - lf_hi.astype(F32)).astype(BF16)
    lf_lo = (lf - lf_hi.astype(F32) - lf_mid.astype(F32)).astype(BF16)
    gcum = _dot(tri, lf_hi) + (_dot(tri, lf_mid) + _dot(tri, lf_lo))
    x_col = jnp.where(coli < ML_HEADS, gates, gcum)
    x_row = x_col.T
    ones = jnp.ones((L, LANES), BF16)
    heads = range(ML_HEADS)
    hsl = [slice(hh * LANES, (hh + 1) * LANES) for hh in heads]

    qs = [qkv_ref[:, hsl[hh]] for hh in heads]
    ks = [qkv_ref[:, width + hh * LANES:width + (hh + 1) * LANES] for hh in heads]
    vaugs = [jnp.concatenate([qkv_ref[:, 2 * width + hh * LANES:2 * width + (hh + 1) * LANES], ones], axis=1)
             for hh in heads]
    scores = [_dot_nt(qs[hh], ks[hh]) * scale for hh in heads]
    ics = [jnp.broadcast_to(x_col[:, hh:hh + 1], (L, LANES)) for hh in heads]
    gcs = [jnp.broadcast_to(x_col[:, ML_HEADS + hh:ML_HEADS + hh + 1], (L, LANES)) for hh in heads]
    irs = [x_row[hh:hh + 1, :] for hh in heads]
    grs = [x_row[ML_HEADS + hh:ML_HEADS + hh + 1, :] for hh in heads]
    mps = [m_ref[hh] for hh in heads]
    dmats = [jnp.where(causal, gcs[hh] - grs[hh] + irs[hh], -jnp.inf) for hh in heads]
    m_inters = [mps[hh] + gcs[hh] for hh in heads]
    m_ts = [jnp.maximum(m_inters[hh], jnp.max(dmats[hh], axis=1, keepdims=True)) for hh in heads]
    qks = [(scores[hh] * jnp.exp(dmats[hh] - m_ts[hh])).astype(BF16) for hh in heads]
    caugs = [caug_ref[hh] for hh in heads]
    w_inters = [jnp.exp(m_inters[hh] - m_ts[hh]) for hh in heads]
    nds = [_dot(qks[hh], vaugs[hh])
           + jnp.concatenate([w_inters[hh], w_inters[hh]], axis=1) * _dot(qs[hh], caugs[hh].astype(BF16))
           for hh in heads]

    g_lasts = [gcs[hh][L - 1:L, :] for hh in heads]
    m_news = [jnp.maximum(mps[hh] + g_lasts[hh],
                          jnp.max(g_lasts[hh] - grs[hh] + irs[hh], axis=1, keepdims=True)) for hh in heads]
    for hh in heads:
        ws = jnp.exp(g_lasts[hh] - gcs[hh] + ics[hh] - m_news[hh])
        wc = jnp.exp(mps[hh] + g_lasts[hh] - m_news[hh])
        kw_t = (ks[hh].astype(F32) * (ws * scale)).T.astype(BF16)
        caug_ref[hh] = jnp.concatenate([wc, wc], axis=1) * caugs[hh] + _dot(kw_t, vaugs[hh])
        m_ref[hh] = m_news[hh]

    hvals = [nds[hh][:, :LANES] / jnp.maximum(jnp.abs(nds[hh][:, LANES:]), jnp.exp(-m_ts[hh]))
             for hh in heads]
    mus = [jnp.mean(hvals[hh], axis=1, keepdims=True) for hh in heads]
    dvs = [hvals[hh] - mus[hh] for hh in heads]
    variances = [jnp.mean(dvs[hh] * dvs[hh], axis=1, keepdims=True) for hh in heads]
    for hh in heads:
        sl = hsl[hh]
        hn = dvs[hh] * lax.rsqrt(variances[hh] + EPS) * nw_ref[:, sl]
        zb = zb_ref[:, sl].astype(F32)
        o_ref[:, sl] = ((hn + sk_ref[:, sl] * xc_ref[:, sl]) * _silu(zb)).astype(o_ref.dtype)


def _mlstm(proj, p, bsz, seq):
    m = proj.shape[0]
    w = ML_HEADS * LANES
    sps = 1
    cps = next(c for c in (4, 2, 1) if seq % (c * CHUNK) == 0)
    nt = seq // (cps * CHUNK)
    vec = lambda: pl.BlockSpec((1, w), lambda b, j: (0, 0))
    blk = lambda: pl.BlockSpec((ML_HEADS, LANES, LANES), lambda b, j: (0, 0, 0))
    tile = lambda col: pl.BlockSpec((sps, cps * CHUNK, w), lambda b, j: (b, j, col))
    proj3 = proj.reshape(bsz, seq, proj.shape[1])
    out = pl.pallas_call(
        _mlstm_kernel,
        out_shape=jax.ShapeDtypeStruct((bsz, seq, w), BF16),
        grid=(bsz // sps, nt),
        in_specs=[tile(2), tile(3),
                  pl.BlockSpec((CONV_WIDTH, w), lambda b, j: (0, 0)), vec(),
                  blk(), blk(), blk(),
                  pl.BlockSpec((3 * w, LANES), lambda b, j: (0, 0)),
                  pl.BlockSpec((1, LANES), lambda b, j: (0, 0)),
                  vec(), vec()],
        out_specs=tile(0),
        scratch_shapes=[pltpu.VMEM((sps, SUBLANES, w), F32),
                        pltpu.VMEM((sps, CHUNK, 3 * w), BF16),
                        pltpu.VMEM((sps, CHUNK, w), F32),
                        pltpu.VMEM((sps, ML_HEADS, LANES, 2 * LANES), F32),
                        pltpu.VMEM((sps, ML_HEADS, 1, LANES), F32)],
        compiler_params=_cparams(("parallel", "arbitrary")),
    )(proj3, proj3, p["conv_w"], p["conv_b"], p["w_q"], p["w_k"], p["w_v"], p["w_g"], p["b_g"],
      p["norm"], p["skip"])
    return out.reshape(m, w)


def _ssd_kernel(z_ref, xbc_ref, dt_ref, cw_ref, cb_ref, dtb_ref, alog_ref, dsk_ref, nw_ref,
                o_ref, tail_ref, act_ref, st_ref):
    @pl.when(pl.program_id(1) == 0)
    def _():
        tail_ref[...] = jnp.zeros_like(tail_ref)
        st_ref[...] = jnp.zeros_like(st_ref)

    for s in range(z_ref.shape[0]):
        _ssd_chunk(z_ref.at[s], xbc_ref.at[s], dt_ref.at[s], cw_ref, cb_ref, dtb_ref, alog_ref, dsk_ref,
                   nw_ref, o_ref.at[s], tail_ref.at[s], act_ref.at[s], st_ref.at[s])


def _ssd_chunk(z_ref, xbc_ref, dt_ref, cw_ref, cb_ref, dtb_ref, alog_ref, dsk_ref, nw_ref,
               o_ref, tail_ref, act_ref, st_ref):
    L = CHUNK
    inner = o_ref.shape[1]
    gw = inner // SSD_GROUPS
    hpg = gw // SSD_HEAD_DIM
    b_off = inner
    c_off = inner + SSD_GROUPS * SSD_STATE
    shift = _shift_matrix(L)
    for c0 in range(0, xbc_ref.shape[1], CONV_COLS):
        shifted = _dot(shift, xbc_ref[:, c0:c0 + CONV_COLS])
        for l0 in range(0, CONV_COLS, LANES):
            sl = slice(c0 + l0, c0 + l0 + LANES)
            act_ref[:, sl] = _silu(_causal_conv_shifted(
                xbc_ref[:, sl].astype(F32), shifted[:, l0:l0 + LANES], tail_ref, cw_ref, cb_ref, sl))

    rowi = lax.broadcasted_iota(I32, (L, L), 0)
    coli = lax.broadcasted_iota(I32, (L, L), 1)
    causal = rowi >= coli
    dt = _softplus(dt_ref[...] + dtb_ref[...])
    da = dt * (-jnp.exp(alog_ref[...]))
    tri = causal.astype(BF16)
    da_hi = da.astype(BF16)
    da_mid = (da - da_hi.astype(F32)).astype(BF16)
    da_lo = (da - da_hi.astype(F32) - da_mid.astype(F32)).astype(BF16)
    a = _dot(tri, da_hi) + (_dot(tri, da_mid) + _dot(tri, da_lo))
    a_t = a.T
    hpt = LANES // SSD_HEAD_DIM
    lane = lax.broadcasted_iota(I32, (L, LANES), 1)

    def over_heads(tiles):
        cols = []
        for c0 in range(0, hpg, hpt):
            out = tiles[c0 + hpt - 1]
            for j in range(hpt - 2, -1, -1):
                out = jnp.where(lane < (j + 1) * SSD_HEAD_DIM, tiles[c0 + j], out)
            cols.append(out)
        return jnp.concatenate(cols, axis=1)

    for g in range(SSD_GROUPS):
        gsl = slice(g * gw, (g + 1) * gw)
        xg = act_ref[:, gsl]
        bg = act_ref[:, b_off + g * SSD_STATE:b_off + (g + 1) * SSD_STATE]
        cg_ = act_ref[:, c_off + g * SSD_STATE:c_off + (g + 1) * SSD_STATE].astype(BF16)
        cb = _dot_nt(cg_, bg.astype(BF16))
        state = st_ref[g]
        hds = [g * hpg + jj for jj in range(hpg)]
        a_bs = [jnp.broadcast_to(a[:, hd:hd + 1], (L, LANES)) for hd in hds]
        dt_bs = [jnp.broadcast_to(dt[:, hd:hd + 1], (L, LANES)) for hd in hds]
        ea_x = over_heads([jnp.exp(a_b) for a_b in a_bs])
        to_end_x = over_heads([jnp.exp(a_b[L - 1:L, :] - a_b) for a_b in a_bs])
        xdt = xg * over_heads(dt_bs)
        inter = _dot(cg_, state.astype(BF16)) * ea_x
        acc = [inter[:, c0:c0 + LANES] for c0 in range(0, gw, LANES)]
        for jj in range(hpg):
            seg = jnp.where(causal, a_bs[jj] - a_t[hds[jj]:hds[jj] + 1, :], -jnp.inf)
            w = (cb * jnp.exp(seg)).astype(BF16)
            c, j = divmod(jj, hpt)
            in_head = (lane >= j * SSD_HEAD_DIM) & (lane < (j + 1) * SSD_HEAD_DIM)
            x_head = jnp.where(in_head, xdt[:, c * LANES:(c + 1) * LANES], 0.0).astype(BF16)
            acc[c] = acc[c] + _dot(w, x_head)
        acc = jnp.concatenate(acc, axis=1)
        y = (acc + dsk_ref[:, gsl] * xg) * _silu(z_ref[:, gsl].astype(F32))
        y = y * lax.rsqrt(jnp.mean(y * y, axis=1, keepdims=True) + EPS) * nw_ref[:, gsl]
        o_ref[:, gsl] = y.astype(o_ref.dtype)
        xw = (xdt * to_end_x).astype(BF16)
        st_ref[g] = ea_x[L - 1:L, :] * state + _dot(bg.T.astype(BF16), xw)


def _ssd(z, xbc, dt_raw, p, bsz, seq):
    m, inner = z.shape
    nt = seq // CHUNK
    conv_ch = xbc.shape[1]
    sps = SEQ_PER_STEP if bsz % SEQ_PER_STEP == 0 else 1
    vec = lambda n: pl.BlockSpec((1, n), lambda b, j: (0, 0))
    tile = lambda n: pl.BlockSpec((sps, CHUNK, n), lambda b, j: (b, j, 0))
    out = pl.pallas_call(
        _ssd_kernel,
        out_shape=jax.ShapeDtypeStruct((bsz, seq, inner), BF16),
        grid=(bsz // sps, nt),
        in_specs=[tile(inner), tile(conv_ch), tile(LANES),
                  pl.BlockSpec((CONV_WIDTH, conv_ch), lambda b, j: (0, 0)), vec(conv_ch),
                  vec(LANES), vec(LANES), vec(inner), vec(inner)],
        out_specs=tile(inner),
        scratch_shapes=[pltpu.VMEM((sps, SUBLANES, conv_ch), F32),
                        pltpu.VMEM((sps, CHUNK, conv_ch), F32),
                        pltpu.VMEM((sps, SSD_GROUPS, SSD_STATE, inner // SSD_GROUPS), F32)],
        compiler_params=_cparams(("parallel", "arbitrary")),
    )(z.reshape(bsz, seq, inner), xbc.reshape(bsz, seq, conv_ch), dt_raw.reshape(bsz, seq, LANES),
      p["conv_w"], p["conv_b"], p["dt_bias"], p["a_log"], p["d_skip"], p["norm"])
    return out.reshape(m, inner)


def _router_kernel(*refs, n_in):
    y_refs, w_refs = refs[:n_in], refs[n_in:2 * n_in]
    (h_ref, gm_ref, g_ref, sh_ref, sc_ref, wr_ref, br_ref,
     hmid_ref, up_ref, topi_ref, gate_ref, rank_ref, cnt_ref, carry_ref) = refs[2 * n_in:]

    @pl.when(pl.program_id(0) == 0)
    def _():
        carry_ref[...] = jnp.zeros_like(carry_ref)

    tm = h_ref.shape[0]
    acc = _dot(y_refs[0][...], w_refs[0][...])
    for y_ref, w_ref in zip(y_refs[1:], w_refs[1:]):
        acc = acc + _dot(y_ref[...], w_ref[...])
    hmid = h_ref[...] + gm_ref[0] * acc
    hmid_ref[...] = hmid
    u = _norm_mod(hmid, g_ref[...], sh_ref[0], sc_ref[0])
    up_ref[...] = _pack_pairs(u)
    u_hi = u.astype(BF16)
    u_lo = (u - u_hi.astype(F32)).astype(BF16)
    logits = (_dot(u_hi, wr_ref[0]) + (_dot(u_lo, wr_ref[0]) + _dot(u_hi, wr_ref[1]))
              + br_ref[...])
    lt = jnp.concatenate([logits[r0:r0 + LANES].T for r0 in range(0, tm, LANES)], axis=1)
    l = lt[:N_EXPERTS]
    e_iota = lax.broadcasted_iota(I32, (N_EXPERTS, tm), 0).astype(F32)
    vals, idxs, hots = [], [], []
    for _ in range(TOP_K):
        mx = jnp.max(l, axis=0, keepdims=True)
        idx = jnp.min(jnp.where(l == mx, e_iota, float(N_EXPERTS)), axis=0, keepdims=True)
        hot = e_iota == idx
        l = jnp.where(hot, -jnp.inf, l)
        vals.append(mx)
        idxs.append(idx)
        hots.append(hot)
    exps = [jnp.exp(v - vals[0]) for v in vals]
    den = exps[0] + exps[1] + exps[2] + exps[3]
    gate_ref[...] = jnp.concatenate([e / den for e in exps], axis=0)
    topi_ref[...] = jnp.concatenate(idxs, axis=0).astype(I32)

    sel = jnp.zeros((N_EXPERTS, tm), F32)
    for hot in hots:
        sel = jnp.where(hot, 1.0, sel)
    r_i = lax.broadcasted_iota(I32, (tm, tm), 0)
    c_i = lax.broadcasted_iota(I32, (tm, tm), 1)
    before = (r_i < c_i).astype(BF16)
    carry = carry_ref[:, 0:1]
    cum = _dot(sel.astype(BF16), before) + carry
    rank_ref[...] = jnp.concatenate(
        [jnp.sum(jnp.where(hot, cum, 0.0), axis=0, keepdims=True) for hot in hots], axis=0).astype(I32)
    total = carry + jnp.sum(sel, axis=1, keepdims=True)
    carry_ref[...] = jnp.broadcast_to(total, carry_ref.shape)
    cnt_ref[...] = jnp.broadcast_to(total, cnt_ref.shape)


def _router(ys, w_out, h, g_m, g, shift, scale, wr, br, seq, tm):
    m, d = h.shape
    tiles_per_seq = seq // tm
    bmap = lambda i: (i // tiles_per_seq, 0, 0)
    row4 = lambda: pl.BlockSpec((TOP_K, tm), lambda i: (0, i))
    in_specs, args, k0 = [], [], 0
    for y in ys:
        in_specs.append(pl.BlockSpec((tm, y.shape[1]), lambda i: (i, 0)))
        args.append(y)
    for y in ys:
        kk = y.shape[1]
        in_specs.append(pl.BlockSpec((kk, d), lambda i, kb=k0 // kk: (kb, 0)))
        args.append(w_out)
        k0 += kk
    in_specs += [pl.BlockSpec((tm, d), lambda i: (i, 0)),
                 pl.BlockSpec((1, 1, d), bmap),
                 pl.BlockSpec((1, d), lambda i: (0, 0)),
                 pl.BlockSpec((1, 1, d), bmap), pl.BlockSpec((1, 1, d), bmap),
                 pl.BlockSpec((2, d, LANES), lambda i: (0, 0, 0)),
                 pl.BlockSpec((1, LANES), lambda i: (0, 0))]
    args += [h, g_m, g.reshape(1, d), shift, scale, wr, br]
    return pl.pallas_call(
        functools.partial(_router_kernel, n_in=len(ys)),
        out_shape=[jax.ShapeDtypeStruct((m, d), F32),
                   jax.ShapeDtypeStruct((m, d // 2), I32),
                   jax.ShapeDtypeStruct((TOP_K, m), I32),
                   jax.ShapeDtypeStruct((TOP_K, m), F32),
                   jax.ShapeDtypeStruct((TOP_K, m), I32),
                   jax.ShapeDtypeStruct((N_EXPERTS, LANES), F32)],
        grid=(m // tm,),
        in_specs=in_specs,
        out_specs=[pl.BlockSpec((tm, d), lambda i: (i, 0)),
                   pl.BlockSpec((tm, d // 2), lambda i: (i, 0)), row4(), row4(), row4(),
                   pl.BlockSpec((N_EXPERTS, LANES), lambda i: (0, 0))],
        scratch_shapes=[pltpu.VMEM((N_EXPERTS, LANES), F32)],
        compiler_params=_cparams(("arbitrary",)),
    )(*args)


def _dest_kernel(ps_ref, topi_ref, rank_ref, o_ref):
    topi = topi_ref[...]
    acc = rank_ref[...]
    for e in range(N_EXPERTS):
        acc = acc + jnp.where(topi == e, ps_ref[e], 0)
    o_ref[...] = acc


def _dest_rows(pad_start, topi, rank, tw):
    k, m = topi.shape
    blk = lambda: pl.BlockSpec((k, tw), lambda i, ps: (0, i))
    return pl.pallas_call(
        _dest_kernel,
        out_shape=jax.ShapeDtypeStruct((k, m), I32),
        grid_spec=pltpu.PrefetchScalarGridSpec(
            num_scalar_prefetch=1, grid=(m // tw,), in_specs=[blk(), blk()], out_specs=blk()),
        compiler_params=_cparams(("parallel",)),
    )(pad_start, topi, rank)


SC_CORES = 2
SC_SUBCORES = 16
SC_ROWS = 128


def _sc_gather_rows(table, idx):
    b = idx.shape[0]
    w = table.shape[1]
    workers = SC_CORES * SC_SUBCORES
    per_w = b // workers
    assert per_w * workers == b and per_w % SC_ROWS == 0
    mesh = plsc.VectorSubcoreMesh(core_axis_name="c", subcore_axis_name="s")

    @functools.partial(
        pl.kernel, mesh=mesh, out_type=jax.ShapeDtypeStruct((b, w), I32),
        scratch_types=[pltpu.VMEM((SC_ROWS,), I32), pltpu.VMEM((SC_ROWS, w), I32),
                       pltpu.SemaphoreType.DMA])
    def gather(table_hbm, idx_hbm, out_hbm, idx_v, rows_v, sem):
        base = (lax.axis_index("s") * SC_CORES + lax.axis_index("c")) * per_w

        @pl.loop(0, per_w // SC_ROWS)
        def _(c):
            off = base + c * SC_ROWS
            pltpu.sync_copy(idx_hbm.at[pl.ds(off, SC_ROWS)], idx_v)
            pltpu.async_copy(table_hbm.at[idx_v], rows_v, sem).wait()
            pltpu.sync_copy(rows_v, out_hbm.at[pl.ds(off, SC_ROWS)])

    return gather(table, idx)


def _sc_scatter_rows(rows, dest, n_rows):
    m, w = rows.shape
    kk = dest.shape[0]
    workers = SC_CORES * SC_SUBCORES
    per_w = m // workers
    assert per_w * workers == m and per_w % SC_ROWS == 0
    mesh = plsc.VectorSubcoreMesh(core_axis_name="c", subcore_axis_name="s")

    @functools.partial(
        pl.kernel, mesh=mesh, out_type=jax.ShapeDtypeStruct((n_rows, w), I32),
        scratch_types=[pltpu.VMEM((SC_ROWS,), I32), pltpu.VMEM((SC_ROWS, w), I32),
                       pltpu.SemaphoreType.DMA])
    def scatter(rows_hbm, dest_hbm, out_hbm, idx_v, rows_v, sem):
        base = (lax.axis_index("s") * SC_CORES + lax.axis_index("c")) * per_w

        @pl.loop(0, per_w // SC_ROWS)
        def _(c):
            off = base + c * SC_ROWS
            pltpu.sync_copy(rows_hbm.at[pl.ds(off, SC_ROWS)], rows_v)
            for k in range(kk):
                pltpu.sync_copy(dest_hbm.at[pl.ds(k * m + off, SC_ROWS)], idx_v)
                pltpu.async_copy(rows_v, out_hbm.at[idx_v], sem).wait()

    return scatter(rows, dest.reshape(-1))


def _combine_dense_kernel(y_ref, gate_ref, h_ref, gf_ref, fn_ref, o_ref, *, final):
    acc = gate_ref[:, 0:1] * _unpack_pairs(y_ref[0])
    for k in range(1, TOP_K):
        acc = acc + gate_ref[:, k:k + 1] * _unpack_pairs(y_ref[k])
    hn = h_ref[...] + gf_ref[0] * acc
    if final:
        hn = hn * lax.rsqrt(jnp.mean(hn * hn, axis=-1, keepdims=True) + EPS) * fn_ref[...]
    o_ref[...] = hn


def _combine_dense(y4, gates_col, h, gf, fnorm, seq, tm, final):
    m, d = h.shape
    tiles_per_seq = seq // tm
    return pl.pallas_call(
        functools.partial(_combine_dense_kernel, final=final),
        out_shape=jax.ShapeDtypeStruct((m, d), F32),
        grid=(m // tm,),
        in_specs=[pl.BlockSpec((TOP_K, tm, d // 2), lambda i: (0, i, 0)),
                  pl.BlockSpec((tm, TOP_K), lambda i: (i, 0)),
                  pl.BlockSpec((tm, d), lambda i: (i, 0)),
                  pl.BlockSpec((1, 1, d), lambda i: (i // tiles_per_seq, 0, 0)),
                  pl.BlockSpec((1, d), lambda i: (0, 0))],
        out_specs=pl.BlockSpec((tm, d), lambda i: (i, 0)),
        compiler_params=_cparams(("parallel",)),
    )(y4, gates_col, h, gf, fnorm.reshape(1, d))


def _expert_kernel(be_ref, nb_ref, first_ref, x_ref, wgu_ref, bgu_ref, wd_ref, bd_ref, y_ref,
                   wgu_bf, wd_bf):
    i = pl.program_id(0)

    @pl.when(i < nb_ref[0])
    def _():
        dff = wd_bf.shape[0]

        @pl.when(first_ref[i] == 1)
        def _():
            rows = 64

            def cast(r, c):
                r0 = pl.multiple_of(r * rows, rows)
                wgu_bf[pl.ds(r0, rows), :] = wgu_ref[0, 0, pl.ds(r0, rows), :].astype(BF16)
                wd_bf[pl.ds(r0, rows), :] = wd_ref[0, 0, pl.ds(r0, rows), :].astype(BF16)
                return c

            lax.fori_loop(0, dff // rows, cast, 0)

        x = _unpack_pairs(x_ref[...]).astype(BF16)
        hb = _dot(x, wgu_bf[...]) + bgu_ref[0, 0]
        h_glu = jnp.minimum(hb[:, :dff], SWIGLU_LIMIT)
        h_lin = jnp.clip(hb[:, dff:], -SWIGLU_LIMIT, SWIGLU_LIMIT)
        half = 0.5 * h_glu
        act = (half + half * jnp.tanh(SWIGLU_ALPHA * half)) * (h_lin + 1.0)
        y_ref[...] = _pack_pairs(_dot(act.astype(BF16), wd_bf[...]) + bd_ref[0, 0])


def _experts(block_e, n_used, first, xs, wgu, bgu, wd, bd, layer):
    n_rows, wp = xs.shape
    _, ne, d, ff2 = wgu.shape
    assert d == ff2 // 2
    nblk = n_rows // EXPERT_BLOCK

    def xmap(i, be, nb, fi):
        return (jnp.minimum(i, nb[0] - 1), 0)

    emap = lambda i, be, nb, fi: (layer, be[i], 0, 0)
    grid_spec = pltpu.PrefetchScalarGridSpec(
        num_scalar_prefetch=3, grid=(nblk,),
        in_specs=[pl.BlockSpec((EXPERT_BLOCK, wp), xmap),
                  pl.BlockSpec((1, 1, d, ff2), emap), pl.BlockSpec((1, 1, 1, ff2), emap),
                  pl.BlockSpec((1, 1, ff2 // 2, d), emap), pl.BlockSpec((1, 1, 1, d), emap)],
        out_specs=pl.BlockSpec((EXPERT_BLOCK, wp), xmap),
        scratch_shapes=[pltpu.VMEM((d, ff2), BF16), pltpu.VMEM((ff2 // 2, d), BF16)])
    depth = wgu.shape[0]
    return pl.pallas_call(
        _expert_kernel,
        out_shape=jax.ShapeDtypeStruct((n_rows, wp), I32),
        grid_spec=grid_spec,
        compiler_params=_cparams(("arbitrary",)),
    )(block_e, n_used, first, xs, wgu, bgu.reshape(depth, ne, 1, ff2), wd, bd.reshape(depth, ne, 1, d))


def _moe(ys, w_out, g_m, h, g, shift, scale, gf, fnorm, wr, br, wgu, bgu, wd, bd, layer, seq, final):
    m, d = h.shape
    tm = MOE_TILE
    wr_p = jnp.zeros((d, LANES), F32).at[:, :N_EXPERTS].set(wr)
    br_p = jnp.zeros((1, LANES), F32).at[0, :N_EXPERTS].set(br)
    wr_hi = wr_p.astype(BF16)
    wr_split = jnp.stack([wr_hi, (wr_p - wr_hi.astype(F32)).astype(BF16)])
    h, up, topi, gates, rank, cnt = _router(ys, w_out, h, g_m, g, shift, scale, wr_split, br_p, seq, tm)

    counts = cnt[:, 0].astype(I32)
    padded = (counts + EXPERT_BLOCK - 1) // EXPERT_BLOCK * EXPERT_BLOCK
    pad_end = jnp.cumsum(padded)
    pad_start = pad_end - padded
    nblk = m * TOP_K // EXPERT_BLOCK + N_EXPERTS
    n_rows = nblk * EXPERT_BLOCK
    n_used = pad_end[-1:] // EXPERT_BLOCK
    blk = jnp.arange(nblk, dtype=I32)
    blk_c = jnp.minimum(blk, n_used - 1)
    block_e = jnp.minimum(jnp.sum(blk_c[:, None] * EXPERT_BLOCK >= pad_end[None, :], axis=1),
                          N_EXPERTS - 1).astype(I32)
    first = jnp.concatenate([jnp.ones((1,), I32), (block_e[1:] != block_e[:-1]).astype(I32)])

    dest = _dest_rows(pad_start, topi, rank, min(m, DEST_TILE))
    xs = _sc_scatter_rows(up, dest, n_rows)
    y = _experts(block_e, n_used.astype(I32), first, xs, wgu, bgu, wd, bd, layer)
    y4 = _sc_gather_rows(y, dest.reshape(-1)).reshape(TOP_K, m, d // 2)
    return _combine_dense(y4, gates.T, h, gf, fnorm, seq, tm, final)


def _block_diag(w, group):
    nb, b, _ = w.shape
    per = group // b
    wg = w.reshape(nb // per, per, b, b)
    dense = jnp.einsum("gnde,nm->gndme", wg, jnp.eye(per, dtype=w.dtype))
    return dense.reshape(nb // per, group, group)


def kernel(x, c, mod_w, mod_b, norm_mix, norm_ffn, ev_w_in, ev_lru_conv_w, ev_lru_conv_b, ev_lru_w_r, ev_lru_b_r, ev_lru_w_i, ev_lru_b_i, ev_lru_lambda, ev_ml_conv_w, ev_ml_conv_b, ev_ml_w_q, ev_ml_w_k, ev_ml_w_v, ev_ml_w_ig, ev_ml_b_ig, ev_ml_w_fg, ev_ml_b_fg, ev_ml_norm, ev_ml_skip, ev_w_out, od_w_in, od_conv_w, od_conv_b, od_dt_bias, od_a_log, od_d, od_norm, od_w_out, moe_router_w, moe_router_b, moe_w_gu, moe_b_gu, moe_w_down, moe_b_down, final_norm):
    bsz, seq, d = x.shape
    depth = mod_w.shape[0]
    m = bsz * seq
    mod = _modulation(c, mod_w, mod_b)
    h = x.reshape(m, d).astype(F32)
    for layer in range(depth):
        sh_m, sc_m, g_m, sh_f, sc_f, g_f = (mod[layer, i] for i in range(6))
        j = layer // 2
        if layer % 2 == 0:
            w = ev_lru_lambda.shape[1]
            w_in = ev_w_in[j].astype(BF16)
            proj = _inproj(h, norm_mix[layer], sh_m, sc_m, w_in, None, [w_in.shape[1]], seq, PROJ_TILE)[0]
            lru_p = dict(conv_w=ev_lru_conv_w[j], conv_b=ev_lru_conv_b[j].reshape(1, w),
                         w_r=ev_lru_w_r[j].astype(BF16), b_r=ev_lru_b_r[j].reshape(1, w),
                         w_i=ev_lru_w_i[j].astype(BF16), b_i=ev_lru_b_i[j].reshape(1, w),
                         lam=ev_lru_lambda[j].reshape(1, w))
            ya = _lru(proj, lru_p, bsz, seq, LRU_TILE)
            wg = jnp.zeros((3 * w, LANES), F32)
            wg = wg.at[:, :ML_HEADS].set(ev_ml_w_ig[j]).at[:, ML_HEADS:2 * ML_HEADS].set(ev_ml_w_fg[j])
            bg = jnp.zeros((1, LANES), F32)
            bg = bg.at[0, :ML_HEADS].set(ev_ml_b_ig[j]).at[0, ML_HEADS:2 * ML_HEADS].set(ev_ml_b_fg[j])
            ml_p = dict(conv_w=ev_ml_conv_w[j], conv_b=ev_ml_conv_b[j].reshape(1, w),
                        w_q=_block_diag(ev_ml_w_q[j], LANES).astype(BF16),
                        w_k=_block_diag(ev_ml_w_k[j], LANES).astype(BF16),
                        w_v=_block_diag(ev_ml_w_v[j], LANES).astype(BF16),
                        w_g=wg.astype(BF16), b_g=bg,
                        norm=ev_ml_norm[j].reshape(1, w), skip=ev_ml_skip[j].reshape(1, w))
            yb = _mlstm(proj, ml_p, bsz, seq)
            ys, w_out = [ya, yb], ev_w_out[j].astype(BF16)
        else:
            inner = od_norm.shape[1]
            heads = od_dt_bias.shape[1]
            conv_ch = od_conv_w.shape[2]
            w_in = od_w_in[j]
            wdt = jnp.zeros((d, LANES), F32).at[:, :heads].set(w_in[:, inner + conv_ch:])
            z, xbc, dt_raw = _inproj(h, norm_mix[layer], sh_m, sc_m, w_in.astype(BF16),
                                     wdt.astype(BF16), [inner, conv_ch], seq, PROJ_TILE)
            pad = lambda v: jnp.zeros((1, LANES), F32).at[0, :heads].set(v)
            ssd_p = dict(conv_w=od_conv_w[j], conv_b=od_conv_b[j].reshape(1, conv_ch),
                         dt_bias=pad(od_dt_bias[j]), a_log=pad(od_a_log[j]),
                         d_skip=jnp.repeat(od_d[j], SSD_HEAD_DIM).reshape(1, inner),
                         norm=od_norm[j].reshape(1, inner))
            y = _ssd(z, xbc, dt_raw, ssd_p, bsz, seq)
            ys, w_out = [y], od_w_out[j].astype(BF16)
        h = _moe(ys, w_out, g_m, h, norm_ffn[layer], sh_f, sc_f, g_f, final_norm,
                 moe_router_w[layer], moe_router_b[layer],
                 moe_w_gu, moe_b_gu, moe_w_down, moe_b_down, layer, seq, final=(layer == depth - 1))
    return h.reshape(bsz, seq, d)
```

```python
import functools

import jax
import jax.numpy as jnp
from jax import lax
from jax.experimental import pallas as pl
from jax.experimental.pallas import tpu as pltpu
from jax.experimental.pallas import tpu_sc as plsc

F32 = jnp.float32
BF16 = jnp.bfloat16
I32 = jnp.int32
HIGHEST = lax.Precision.HIGHEST

EPS = 1e-6
CONV_WIDTH = 4
LANES = 128
SUBLANES = 8
LRU_HEADS = 8
LRU_C = 8.0
ML_HEADS = 8
CHUNK = 128
PROJ_TILE = 512
LRU_TILE = 512
MOE_TILE = 512
DEST_TILE = 8192
SSD_HEAD_DIM = 64
SSD_GROUPS = 8
SSD_STATE = 128
N_EXPERTS = 32
TOP_K = 4
SWIGLU_ALPHA = 1.702
SWIGLU_LIMIT = 7.0
EXPERT_BLOCK = 512
SEQ_PER_STEP = 2
CONV_COLS = 512
VMEM_LIMIT = 56 * 1024 * 1024


def _cparams(sem, **kw):
    return pltpu.CompilerParams(dimension_semantics=sem, vmem_limit_bytes=VMEM_LIMIT, **kw)


def _silu(x):
    half = 0.5 * x
    return half + half * jnp.tanh(half)


def _log_sigmoid(x):
    return jnp.minimum(x, 0.0) - jnp.log1p(jnp.exp(-jnp.abs(x)))


def _softplus(x):
    return jnp.maximum(x, 0.0) + jnp.log1p(jnp.exp(-jnp.abs(x)))


def _dot(a, b, **kw):
    return jnp.dot(a, b, preferred_element_type=F32, **kw)


def _dot_nt(a, b):
    return lax.dot_general(a, b, (((1,), (1,)), ((), ())), preferred_element_type=F32)


def _pack_pairs(x):
    w = x.shape[1] // 2
    lo = lax.bitcast_convert_type(x[:, :w].astype(BF16).astype(F32), I32)
    hi = lax.bitcast_convert_type(x[:, w:].astype(BF16).astype(F32), I32)
    return lax.shift_right_logical(lo, 16) | (hi & jnp.int32(-65536))


def _unpack_pairs(p):
    lo = lax.bitcast_convert_type(lax.shift_left(p, 16), F32)
    hi = lax.bitcast_convert_type(p & jnp.int32(-65536), F32)
    return jnp.concatenate([lo, hi], axis=1)


def _norm_mod(h, g, shift, scale):
    y = h * lax.rsqrt(jnp.mean(h * h, axis=-1, keepdims=True) + EPS)
    return (y * g) * (1.0 + scale) + shift


def _causal_conv(x, tail_ref, w_ref, b_ref, sl):
    t = x.shape[0]
    tail = tail_ref[:, sl]
    row8 = lax.broadcasted_iota(I32, tail.shape, 0)
    out = b_ref[:, sl] + x * w_ref[CONV_WIDTH - 1:CONV_WIDTH, sl]
    for k in range(1, CONV_WIDTH):
        xs = pltpu.roll(x, k, axis=0)
        first = jnp.where(row8 < k, pltpu.roll(tail, k, axis=0), xs[:SUBLANES])
        xs = jnp.concatenate([first, xs[SUBLANES:]], axis=0)
        out = out + xs * w_ref[CONV_WIDTH - 1 - k:CONV_WIDTH - k, sl]
    tail_ref[:, sl] = x[t - SUBLANES:]
    return out


def _shift_matrix(t):
    r = lax.broadcasted_iota(I32, ((CONV_WIDTH - 1) * t, t), 0)
    c = lax.broadcasted_iota(I32, ((CONV_WIDTH - 1) * t, t), 1)
    src = (r & (t - 1)) - lax.shift_right_logical(r, t.bit_length() - 1) - 1
    return (src == c).astype(BF16)


def _causal_conv_shifted(x, shifted, tail_ref, w_ref, b_ref, sl):
    t = x.shape[0]
    tail = tail_ref[:, sl]
    row8 = lax.broadcasted_iota(I32, tail.shape, 0)
    out = b_ref[:, sl] + x * w_ref[CONV_WIDTH - 1:CONV_WIDTH, sl]
    head = jnp.zeros_like(tail)
    for k in range(1, CONV_WIDTH):
        wk = w_ref[CONV_WIDTH - 1 - k:CONV_WIDTH - k, sl]
        out = out + shifted[(k - 1) * t:k * t] * wk
        head = head + jnp.where(row8 < k, pltpu.roll(tail, k, axis=0), 0.0) * wk
    tail_ref[:, sl] = x[t - SUBLANES:]
    return jnp.concatenate([out[:SUBLANES] + head, out[SUBLANES:]], axis=0)


def _mod_kernel(c_ref, w_ref, b_ref, o_ref):
    cond = _silu(c_ref[...])
    o_ref[0, 0] = _dot(cond, w_ref[0], precision=HIGHEST) + b_ref[0, 0]


def _modulation(c, mod_w, mod_b):
    depth, d, _ = mod_w.shape
    bsz = c.shape[0]
    out = pl.pallas_call(
        _mod_kernel,
        out_shape=jax.ShapeDtypeStruct((depth, 6, bsz, d), F32),
        grid=(depth, 6),
        in_specs=[pl.BlockSpec((bsz, d), lambda l, j: (0, 0)),
                  pl.BlockSpec((1, d, d), lambda l, j: (l, 0, j)),
                  pl.BlockSpec((1, 1, 1, d), lambda l, j: (l, j, 0, 0))],
        out_specs=pl.BlockSpec((1, 1, bsz, d), lambda l, j: (l, j, 0, 0)),
        compiler_params=_cparams(("parallel", "parallel")),
    )(c.astype(F32), mod_w, mod_b.reshape(depth, 6, 1, d))
    return out.reshape(depth, 6, bsz, 1, d)


def _inproj_kernel(h_ref, g_ref, sh_ref, sc_ref, w_ref, *rest, n_chunk, with_dt):
    if with_dt:
        wdt_ref, *o_refs, odt_ref = rest
    else:
        o_refs = rest
    u = _norm_mod(h_ref[...], g_ref[...], sh_ref[0], sc_ref[0]).astype(BF16)
    off = 0
    for o_ref in o_refs:
        for n0 in range(0, o_ref.shape[1], n_chunk):
            o_ref[:, n0:n0 + n_chunk] = _dot(u, w_ref[:, off + n0:off + n0 + n_chunk]).astype(o_ref.dtype)
        off += o_ref.shape[1]
    if with_dt:
        odt_ref[...] = _dot(u, wdt_ref[...])


def _inproj(h, g, shift, scale, w, wdt, splits, seq, tm):
    m, d = h.shape
    n = sum(splits)
    assert n <= w.shape[1] and n % LANES == 0
    tiles_per_seq = seq // tm
    bmap = lambda i: (i // tiles_per_seq, 0, 0)
    in_specs = [pl.BlockSpec((tm, d), lambda i: (i, 0)),
                pl.BlockSpec((1, d), lambda i: (0, 0)),
                pl.BlockSpec((1, 1, d), bmap),
                pl.BlockSpec((1, 1, d), bmap),
                pl.BlockSpec((d, n), lambda i: (0, 0), pipeline_mode=pl.Buffered(1))]
    out_shape = [jax.ShapeDtypeStruct((m, s), BF16) for s in splits]
    out_specs = [pl.BlockSpec((tm, s), lambda i: (i, 0)) for s in splits]
    args = [h, g.reshape(1, d), shift, scale, w]
    if wdt is not None:
        in_specs.append(pl.BlockSpec((d, LANES), lambda i: (0, 0)))
        out_shape.append(jax.ShapeDtypeStruct((m, LANES), F32))
        out_specs.append(pl.BlockSpec((tm, LANES), lambda i: (i, 0)))
        args.append(wdt)
    return pl.pallas_call(
        functools.partial(_inproj_kernel, n_chunk=1024, with_dt=wdt is not None),
        out_shape=out_shape, grid=(m // tm,), in_specs=in_specs, out_specs=out_specs,
        compiler_params=_cparams(("parallel",)),
    )(*args)


def _lru_kernel(xa_ref, ga_ref, cw_ref, cb_ref, wr_ref, br_ref, wi_ref, bi_ref, lam_ref,
                o_ref, tail_ref, hc_ref):
    @pl.when(pl.program_id(1) == 0)
    def _():
        tail_ref[...] = jnp.zeros_like(tail_ref)
        hc_ref[...] = jnp.zeros_like(hc_ref)

    t = xa_ref.shape[0]
    row_in_group = lax.broadcasted_iota(I32, (t, LANES), 0) % SUBLANES
    steps = [s for s in (1, 2, 4) if s < SUBLANES]
    masks = [row_in_group >= s for s in steps]
    for hh in range(LRU_HEADS):
        sl = slice(hh * LANES, (hh + 1) * LANES)
        xc = _causal_conv(xa_ref[:, sl].astype(F32), tail_ref, cw_ref, cb_ref, sl)
        xcb = xc.astype(BF16)
        r = jax.nn.sigmoid(_dot(xcb, wr_ref[hh]) + br_ref[:, sl])
        i = jax.nn.sigmoid(_dot(xcb, wi_ref[hh]) + bi_ref[:, sl])
        log_a = LRU_C * r * _log_sigmoid(lam_ref[:, sl])
        a = jnp.exp(log_a)
        th = jnp.tanh(log_a)
        n2 = -2.0 * th
        root = jnp.where(n2 > 0.0, n2 * lax.rsqrt(n2), 0.0)
        u = (root * lax.rsqrt(1.0 - th)) * (i * xc)
        def roll_in_groups(v, s):
            v3 = v.reshape(t // SUBLANES, SUBLANES, LANES)
            return pltpu.roll(v3, s, axis=1).reshape(t, LANES)

        for s, m in zip(steps, masks):
            u = jnp.where(m, u + a * roll_in_groups(u, s), u)
            a = jnp.where(m, a * roll_in_groups(a, s), a)
        carry = hc_ref[:, sl]
        groups = []
        for r0 in range(0, t, SUBLANES):
            hg = u[r0:r0 + SUBLANES] + a[r0:r0 + SUBLANES] * carry
            carry = hg[SUBLANES - 1:SUBLANES]
            groups.append(hg)
        hc_ref[:, sl] = carry
        h = jnp.concatenate(groups, axis=0)
        ga = ga_ref[:, sl].astype(F32)
        o_ref[:, sl] = (h * jax.nn.gelu(ga, approximate=True)).astype(o_ref.dtype)


def _lru(proj, p, bsz, seq, tm):
    m = proj.shape[0]
    w = LRU_HEADS * LANES
    nt = seq // tm
    vec = lambda: pl.BlockSpec((1, w), lambda b, j: (0, 0))
    return pl.pallas_call(
        _lru_kernel,
        out_shape=jax.ShapeDtypeStruct((m, w), BF16),
        grid=(bsz, nt),
        in_specs=[pl.BlockSpec((tm, w), lambda b, j: (b * nt + j, 0)),
                  pl.BlockSpec((tm, w), lambda b, j: (b * nt + j, 1)),
                  pl.BlockSpec((CONV_WIDTH, w), lambda b, j: (0, 0)), vec(),
                  pl.BlockSpec((LRU_HEADS, LANES, LANES), lambda b, j: (0, 0, 0)), vec(),
                  pl.BlockSpec((LRU_HEADS, LANES, LANES), lambda b, j: (0, 0, 0)), vec(), vec()],
        out_specs=pl.BlockSpec((tm, w), lambda b, j: (b * nt + j, 0)),
        scratch_shapes=[pltpu.VMEM((SUBLANES, w), F32), pltpu.VMEM((1, w), F32)],
        compiler_params=_cparams(("parallel", "arbitrary")),
    )(proj, proj, p["conv_w"], p["conv_b"], p["w_r"], p["b_r"], p["w_i"], p["b_i"], p["lam"])


def _mlstm_kernel(xb_ref, zb_ref, cw_ref, cb_ref, wq_ref, wk_ref, wv_ref, wg_ref, bg_ref,
                  nw_ref, sk_ref, o_ref, tail_ref, qkv_ref, xc_ref, caug_ref, m_ref):
    @pl.when(pl.program_id(1) == 0)
    def _():
        tail_ref[...] = jnp.zeros_like(tail_ref)
        caug_ref[...] = jnp.zeros_like(caug_ref)
        m_ref[...] = jnp.full(m_ref.shape, -jnp.inf, F32)

    for s in range(xb_ref.shape[0]):
        for c0 in range(0, xb_ref.shape[1], CHUNK):
            rows = pl.ds(c0, CHUNK)
            _mlstm_chunk(xb_ref.at[s, rows], zb_ref.at[s, rows], cw_ref, cb_ref, wq_ref, wk_ref, wv_ref,
                         wg_ref, bg_ref, nw_ref, sk_ref, o_ref.at[s, rows], tail_ref.at[s], qkv_ref.at[s],
                         xc_ref.at[s], caug_ref.at[s], m_ref.at[s])


def _mlstm_chunk(xb_ref, zb_ref, cw_ref, cb_ref, wq_ref, wk_ref, wv_ref, wg_ref, bg_ref,
                 nw_ref, sk_ref, o_ref, tail_ref, qkv_ref, xc_ref, caug_ref, m_ref):
    L = CHUNK
    width = ML_HEADS * LANES
    scale = LANES ** -0.5
    for hh in range(ML_HEADS):
        sl = slice(hh * LANES, (hh + 1) * LANES)
        xb = xb_ref[:, sl].astype(F32)
        xc = _silu(_causal_conv(xb, tail_ref, cw_ref, cb_ref, sl))
        xc_ref[:, sl] = xc
        xcb = xc.astype(BF16)
        qkv_ref[:, sl] = _dot(xcb, wq_ref[hh]).astype(BF16)
        qkv_ref[:, width + hh * LANES:width + (hh + 1) * LANES] = _dot(xcb, wk_ref[hh]).astype(BF16)
        qkv_ref[:, 2 * width + hh * LANES:2 * width + (hh + 1) * LANES] = (
            _dot(xb.astype(BF16), wv_ref[hh]).astype(BF16))

    gates = _dot(qkv_ref[...], wg_ref[...]) + bg_ref[...]
    rowi = lax.broadcasted_iota(I32, (L, L), 0)
    coli = lax.broadcasted_iota(I32, (L, L), 1)
    causal = rowi >= coli
    lf = jnp.where((coli >= ML_HEADS) & (coli < 2 * ML_HEADS), _log_sigmoid(gates), 0.0)
    tri = causal.astype(BF16)
    lf_hi = lf.astype(BF16)
    lf_mid = (lf - lf_hi.astype(F32)).astype(BF16)
    lf_lo = (lf - lf_hi.astype(F32) - lf_mid.astype(F32)).astype(BF16)
    gcum = _dot(tri, lf_hi) + (_dot(tri, lf_mid) + _dot(tri, lf_lo))
    x_col = jnp.where(coli < ML_HEADS, gates, gcum)
    x_row = x_col.T
    ones = jnp.ones((L, LANES), BF16)
    heads = range(ML_HEADS)
    hsl = [slice(hh * LANES, (hh + 1) * LANES) for hh in heads]

    qs = [qkv_ref[:, hsl[hh]] for hh in heads]
    ks = [qkv_ref[:, width + hh * LANES:width + (hh + 1) * LANES] for hh in heads]
    vaugs = [jnp.concatenate([qkv_ref[:, 2 * width + hh * LANES:2 * width + (hh + 1) * LANES], ones], axis=1)
             for hh in heads]
    scores = [_dot_nt(qs[hh], ks[hh]) * scale for hh in heads]
    ics = [jnp.broadcast_to(x_col[:, hh:hh + 1], (L, LANES)) for hh in heads]
    gcs = [jnp.broadcast_to(x_col[:, ML_HEADS + hh:ML_HEADS + hh + 1], (L, LANES)) for hh in heads]
    irs = [x_row[hh:hh + 1, :] for hh in heads]
    grs = [x_row[ML_HEADS + hh:ML_HEADS + hh + 1, :] for hh in heads]
    mps = [m_ref[hh] for hh in heads]
    dmats = [jnp.where(causal, gcs[hh] - grs[hh] + irs[hh], -jnp.inf) for hh in heads]
    m_inters = [mps[hh] + gcs[hh] for hh in heads]
    m_ts = [jnp.maximum(m_inters[hh], jnp.max(dmats[hh], axis=1, keepdims=True)) for hh in heads]
    qks = [(scores[hh] * jnp.exp(dmats[hh] - m_ts[hh])).astype(BF16) for hh in heads]
    caugs = [caug_ref[hh] for hh in heads]
    w_inters = [jnp.exp(m_inters[hh] - m_ts[hh]) for hh in heads]
    nds = [_dot(qks[hh], vaugs[hh])
           + jnp.concatenate([w_inters[hh], w_inters[hh]], axis=1) * _dot(qs[hh], caugs[hh].astype(BF16))
           for hh in heads]

    g_lasts = [gcs[hh][L - 1:L, :] for hh in heads]
    m_news = [jnp.maximum(mps[hh] + g_lasts[hh],
                          jnp.max(g_lasts[hh] - grs[hh] + irs[hh], axis=1, keepdims=True)) for hh in heads]
    for hh in heads:
        ws = jnp.exp(g_lasts[hh] - gcs[hh] + ics[hh] - m_news[hh])
        wc = jnp.exp(mps[hh] + g_lasts[hh] - m_news[hh])
        kw_t = (ks[hh].astype(F32) * (ws * scale)).T.astype(BF16)
        caug_ref[hh] = jnp.concatenate([wc, wc], axis=1) * caugs[hh] + _dot(kw_t, vaugs[hh])
        m_ref[hh] = m_news[hh]

    hvals = [nds[hh][:, :LANES] / jnp.maximum(jnp.abs(nds[hh][:, LANES:]), jnp.exp(-m_ts[hh]))
             for hh in heads]
    mus = [jnp.mean(hvals[hh], axis=1, keepdims=True) for hh in heads]
    dvs = [hvals[hh] - mus[hh] for hh in heads]
    variances = [jnp.mean(dvs[hh] * dvs[hh], axis=1, keepdims=True) for hh in heads]
    for hh in heads:
        sl = hsl[hh]
        hn = dvs[hh] * lax.rsqrt(variances[hh] + EPS) * nw_ref[:, sl]
        zb = zb_ref[:, sl].astype(F32)
        o_ref[:, sl] = ((hn + sk_ref[:, sl] * xc_ref[:, sl]) * _silu(zb)).astype(o_ref.dtype)


def _mlstm(proj, p, bsz, seq):
    m = proj.shape[0]
    w = ML_HEADS * LANES
    sps = 1
    cps = next(c for c in (4, 2, 1) if seq % (c * CHUNK) == 0)
    nt = seq // (cps * CHUNK)
    vec = lambda: pl.BlockSpec((1, w), lambda b, j: (0, 0))
    blk = lambda: pl.BlockSpec((ML_HEADS, LANES, LANES), lambda b, j: (0, 0, 0))
    tile = lambda col: pl.BlockSpec((sps, cps * CHUNK, w), lambda b, j: (b, j, col))
    proj3 = proj.reshape(bsz, seq, proj.shape[1])
    out = pl.pallas_call(
        _mlstm_kernel,
        out_shape=jax.ShapeDtypeStruct((bsz, seq, w), BF16),
        grid=(bsz // sps, nt),
        in_specs=[tile(2), tile(3),
                  pl.BlockSpec((CONV_WIDTH, w), lambda b, j: (0, 0)), vec(),
                  blk(), blk(), blk(),
                  pl.BlockSpec((3 * w, LANES), lambda b, j: (0, 0)),
                  pl.BlockSpec((1, LANES), lambda b, j: (0, 0)),
                  vec(), vec()],
        out_specs=tile(0),
        scratch_shapes=[pltpu.VMEM((sps, SUBLANES, w), F32),
                        pltpu.VMEM((sps, CHUNK, 3 * w), BF16),
                        pltpu.VMEM((sps, CHUNK, w), F32),
                        pltpu.VMEM((sps, ML_HEADS, LANES, 2 * LANES), F32),
                        pltpu.VMEM((sps, ML_HEADS, 1, LANES), F32)],
        compiler_params=_cparams(("parallel", "arbitrary")),
    )(proj3, proj3, p["conv_w"], p["conv_b"], p["w_q"], p["w_k"], p["w_v"], p["w_g"], p["b_g"],
      p["norm"], p["skip"])
    return out.reshape(m, w)


def _ssd_kernel(z_ref, xbc_ref, dt_ref, cw_ref, cb_ref, dtb_ref, alog_ref, dsk_ref, nw_ref,
                o_ref, tail_ref, act_ref, st_ref):
    @pl.when(pl.program_id(1) == 0)
    def _():
        tail_ref[...] = jnp.zeros_like(tail_ref)
        st_ref[...] = jnp.zeros_like(st_ref)

    for c0 in range(0, z_ref.shape[1], CHUNK):
        rows = pl.ds(c0, CHUNK)
        for s in range(z_ref.shape[0]):
            _ssd_chunk(z_ref.at[s, rows], xbc_ref.at[s, rows], dt_ref.at[s, rows], cw_ref, cb_ref, dtb_ref,
                       alog_ref, dsk_ref, nw_ref, o_ref.at[s, rows], tail_ref.at[s], act_ref.at[s],
                       st_ref.at[s])


def _ssd_chunk(z_ref, xbc_ref, dt_ref, cw_ref, cb_ref, dtb_ref, alog_ref, dsk_ref, nw_ref,
               o_ref, tail_ref, act_ref, st_ref):
    L = CHUNK
    inner = o_ref.shape[1]
    gw = inner // SSD_GROUPS
    hpg = gw // SSD_HEAD_DIM
    b_off = inner
    c_off = inner + SSD_GROUPS * SSD_STATE
    shift = _shift_matrix(L)
    for c0 in range(0, xbc_ref.shape[1], CONV_COLS):
        shifted = _dot(shift, xbc_ref[:, c0:c0 + CONV_COLS])
        for l0 in range(0, CONV_COLS, LANES):
            sl = slice(c0 + l0, c0 + l0 + LANES)
            act_ref[:, sl] = _silu(_causal_conv_shifted(
                xbc_ref[:, sl].astype(F32), shifted[:, l0:l0 + LANES], tail_ref, cw_ref, cb_ref, sl))

    rowi = lax.broadcasted_iota(I32, (L, L), 0)
    coli = lax.broadcasted_iota(I32, (L, L), 1)
    causal = rowi >= coli
    dt = _softplus(dt_ref[...] + dtb_ref[...])
    da = dt * (-jnp.exp(alog_ref[...]))
    tri = causal.astype(BF16)
    da_hi = da.astype(BF16)
    da_mid = (da - da_hi.astype(F32)).astype(BF16)
    da_lo = (da - da_hi.astype(F32) - da_mid.astype(F32)).astype(BF16)
    a = _dot(tri, da_hi) + (_dot(tri, da_mid) + _dot(tri, da_lo))
    a_t = a.T
    hpt = LANES // SSD_HEAD_DIM
    lane = lax.broadcasted_iota(I32, (L, LANES), 1)

    def over_heads(tiles):
        cols = []
        for c0 in range(0, hpg, hpt):
            out = tiles[c0 + hpt - 1]
            for j in range(hpt - 2, -1, -1):
                out = jnp.where(lane < (j + 1) * SSD_HEAD_DIM, tiles[c0 + j], out)
            cols.append(out)
        return jnp.concatenate(cols, axis=1)

    for g in range(SSD_GROUPS):
        gsl = slice(g * gw, (g + 1) * gw)
        xg = act_ref[:, gsl]
        bg = act_ref[:, b_off + g * SSD_STATE:b_off + (g + 1) * SSD_STATE]
        cg_ = act_ref[:, c_off + g * SSD_STATE:c_off + (g + 1) * SSD_STATE].astype(BF16)
        cb = _dot_nt(cg_, bg.astype(BF16))
        state = st_ref[g]
        hds = [g * hpg + jj for jj in range(hpg)]
        a_bs = [jnp.broadcast_to(a[:, hd:hd + 1], (L, LANES)) for hd in hds]
        dt_bs = [jnp.broadcast_to(dt[:, hd:hd + 1], (L, LANES)) for hd in hds]
        ea_x = over_heads([jnp.exp(a_b) for a_b in a_bs])
        to_end_x = over_heads([jnp.exp(a_b[L - 1:L, :] - a_b) for a_b in a_bs])
        xdt = xg * over_heads(dt_bs)
        inter = _dot(cg_, state.astype(BF16)) * ea_x
        acc = [inter[:, c0:c0 + LANES] for c0 in range(0, gw, LANES)]
        for jj in range(hpg):
            seg = jnp.where(causal, a_bs[jj] - a_t[hds[jj]:hds[jj] + 1, :], -jnp.inf)
            w = (cb * jnp.exp(seg)).astype(BF16)
            c, j = divmod(jj, hpt)
            in_head = (lane >= j * SSD_HEAD_DIM) & (lane < (j + 1) * SSD_HEAD_DIM)
            x_head = jnp.where(in_head, xdt[:, c * LANES:(c + 1) * LANES], 0.0).astype(BF16)
            acc[c] = acc[c] + _dot(w, x_head)
        acc = jnp.concatenate(acc, axis=1)
        y = (acc + dsk_ref[:, gsl] * xg) * _silu(z_ref[:, gsl].astype(F32))
        y = y * lax.rsqrt(jnp.mean(y * y, axis=1, keepdims=True) + EPS) * nw_ref[:, gsl]
        o_ref[:, gsl] = y.astype(o_ref.dtype)
        xw = (xdt * to_end_x).astype(BF16)
        st_ref[g] = ea_x[L - 1:L, :] * state + _dot(bg.T.astype(BF16), xw)


def _ssd(z, xbc, dt_raw, p, bsz, seq):
    m, inner = z.shape
    conv_ch = xbc.shape[1]
    sps = SEQ_PER_STEP if bsz % SEQ_PER_STEP == 0 else 1
    cps = 2 if seq % (2 * CHUNK) == 0 else 1
    nt = seq // (cps * CHUNK)
    vec = lambda n: pl.BlockSpec((1, n), lambda b, j: (0, 0))
    tile = lambda n: pl.BlockSpec((sps, cps * CHUNK, n), lambda b, j: (b, j, 0))
    out = pl.pallas_call(
        _ssd_kernel,
        out_shape=jax.ShapeDtypeStruct((bsz, seq, inner), BF16),
        grid=(bsz // sps, nt),
        in_specs=[tile(inner), tile(conv_ch), tile(LANES),
                  pl.BlockSpec((CONV_WIDTH, conv_ch), lambda b, j: (0, 0)), vec(conv_ch),
                  vec(LANES), vec(LANES), vec(inner), vec(inner)],
        out_specs=tile(inner),
        scratch_shapes=[pltpu.VMEM((sps, SUBLANES, conv_ch), F32),
                        pltpu.VMEM((sps, CHUNK, conv_ch), F32),
                        pltpu.VMEM((sps, SSD_GROUPS, SSD_STATE, inner // SSD_GROUPS), F32)],
        compiler_params=_cparams(("parallel", "arbitrary")),
    )(z.reshape(bsz, seq, inner), xbc.reshape(bsz, seq, conv_ch), dt_raw.reshape(bsz, seq, LANES),
      p["conv_w"], p["conv_b"], p["dt_bias"], p["a_log"], p["d_skip"], p["norm"])
    return out.reshape(m, inner)


def _router_kernel(*refs, n_in):
    y_refs, w_refs = refs[:n_in], refs[n_in:2 * n_in]
    (h_ref, gm_ref, g_ref, sh_ref, sc_ref, wr_ref, br_ref,
     hmid_ref, up_ref, topi_ref, gate_ref, rank_ref, cnt_ref, carry_ref) = refs[2 * n_in:]

    @pl.when(pl.program_id(0) == 0)
    def _():
        carry_ref[...] = jnp.zeros_like(carry_ref)

    tm = h_ref.shape[0]
    acc = _dot(y_refs[0][...], w_refs[0][...])
    for y_ref, w_ref in zip(y_refs[1:], w_refs[1:]):
        acc = acc + _dot(y_ref[...], w_ref[...])
    hmid = h_ref[...] + gm_ref[0] * acc
    hmid_ref[...] = hmid
    u = _norm_mod(hmid, g_ref[...], sh_ref[0], sc_ref[0])
    up_ref[...] = _pack_pairs(u)
    u_hi = u.astype(BF16)
    u_lo = (u - u_hi.astype(F32)).astype(BF16)
    hi_both = _dot(u_hi, wr_ref[...])
    logits = (hi_both[:, :LANES] + (_dot(u_lo, wr_ref[:, :LANES]) + hi_both[:, LANES:])
              + br_ref[...])
    lt = jnp.concatenate([logits[r0:r0 + LANES].T for r0 in range(0, tm, LANES)], axis=1)
    l = lt[:N_EXPERTS]
    e_iota = lax.broadcasted_iota(I32, (N_EXPERTS, tm), 0).astype(F32)
    vals, idxs, hots = [], [], []
    for _ in range(TOP_K):
        mx = jnp.max(l, axis=0, keepdims=True)
        idx = jnp.min(jnp.where(l == mx, e_iota, float(N_EXPERTS)), axis=0, keepdims=True)
        hot = e_iota == idx
        l = jnp.where(hot, -jnp.inf, l)
        vals.append(mx)
        idxs.append(idx)
        hots.append(hot)
    exps = [jnp.exp(v - vals[0]) for v in vals]
    den = exps[0] + exps[1] + exps[2] + exps[3]
    gate_ref[...] = jnp.concatenate([e / den for e in exps], axis=0)
    topi_ref[...] = jnp.concatenate(idxs, axis=0).astype(I32)

    sel = jnp.zeros((N_EXPERTS, tm), F32)
    for hot in hots:
        sel = jnp.where(hot, 1.0, sel)
    r_i = lax.broadcasted_iota(I32, (tm, tm), 0)
    c_i = lax.broadcasted_iota(I32, (tm, tm), 1)
    before = (r_i < c_i).astype(BF16)
    carry = carry_ref[:, 0:1]
    cum = _dot(sel.astype(BF16), before) + carry
    rank_ref[...] = jnp.concatenate(
        [jnp.sum(jnp.where(hot, cum, 0.0), axis=0, keepdims=True) for hot in hots], axis=0).astype(I32)
    total = carry + jnp.sum(sel, axis=1, keepdims=True)
    carry_ref[...] = jnp.broadcast_to(total, carry_ref.shape)
    cnt_ref[...] = jnp.broadcast_to(total, cnt_ref.shape)


def _router(ys, w_out, h, g_m, g, shift, scale, wr, br, seq, tm):
    m, d = h.shape
    tiles_per_seq = seq // tm
    bmap = lambda i: (i // tiles_per_seq, 0, 0)
    row4 = lambda: pl.BlockSpec((TOP_K, tm), lambda i: (0, i))
    in_specs, args, k0 = [], [], 0
    for y in ys:
        in_specs.append(pl.BlockSpec((tm, y.shape[1]), lambda i: (i, 0)))
        args.append(y)
    for y in ys:
        kk = y.shape[1]
        in_specs.append(pl.BlockSpec((kk, d), lambda i, kb=k0 // kk: (kb, 0)))
        args.append(w_out)
        k0 += kk
    in_specs += [pl.BlockSpec((tm, d), lambda i: (i, 0)),
                 pl.BlockSpec((1, 1, d), bmap),
                 pl.BlockSpec((1, d), lambda i: (0, 0)),
                 pl.BlockSpec((1, 1, d), bmap), pl.BlockSpec((1, 1, d), bmap),
                 pl.BlockSpec((d, 2 * LANES), lambda i: (0, 0)),
                 pl.BlockSpec((1, LANES), lambda i: (0, 0))]
    args += [h, g_m, g.reshape(1, d), shift, scale, wr, br]
    return pl.pallas_call(
        functools.partial(_router_kernel, n_in=len(ys)),
        out_shape=[jax.ShapeDtypeStruct((m, d), F32),
                   jax.ShapeDtypeStruct((m, d // 2), I32),
                   jax.ShapeDtypeStruct((TOP_K, m), I32),
                   jax.ShapeDtypeStruct((TOP_K, m), F32),
                   jax.ShapeDtypeStruct((TOP_K, m), I32),
                   jax.ShapeDtypeStruct((N_EXPERTS, LANES), F32)],
        grid=(m // tm,),
        in_specs=in_specs,
        out_specs=[pl.BlockSpec((tm, d), lambda i: (i, 0)),
                   pl.BlockSpec((tm, d // 2), lambda i: (i, 0)), row4(), row4(), row4(),
                   pl.BlockSpec((N_EXPERTS, LANES), lambda i: (0, 0))],
        scratch_shapes=[pltpu.VMEM((N_EXPERTS, LANES), F32)],
        compiler_params=_cparams(("arbitrary",)),
    )(*args)


def _dest_kernel(ps_ref, topi_ref, rank_ref, o_ref):
    topi = topi_ref[...]
    acc = rank_ref[...]
    for e in range(N_EXPERTS):
        acc = acc + jnp.where(topi == e, ps_ref[e], 0)
    o_ref[...] = acc


def _dest_rows(pad_start, topi, rank, tw):
    k, m = topi.shape
    blk = lambda: pl.BlockSpec((k, tw), lambda i, ps: (0, i))
    return pl.pallas_call(
        _dest_kernel,
        out_shape=jax.ShapeDtypeStruct((k, m), I32),
        grid_spec=pltpu.PrefetchScalarGridSpec(
            num_scalar_prefetch=1, grid=(m // tw,), in_specs=[blk(), blk()], out_specs=blk()),
        compiler_params=_cparams(("parallel",)),
    )(pad_start, topi, rank)


SC_CORES = 2
SC_SUBCORES = 16
SC_ROWS = 128


def _sc_gather_rows(table, idx):
    b = idx.shape[0]
    w = table.shape[1]
    workers = SC_CORES * SC_SUBCORES
    per_w = b // workers
    assert per_w * workers == b and per_w % SC_ROWS == 0
    mesh = plsc.VectorSubcoreMesh(core_axis_name="c", subcore_axis_name="s")

    @functools.partial(
        pl.kernel, mesh=mesh, out_type=jax.ShapeDtypeStruct((b, w), I32),
        scratch_types=[pltpu.VMEM((SC_ROWS,), I32), pltpu.VMEM((SC_ROWS, w), I32),
                       pltpu.SemaphoreType.DMA])
    def gather(table_hbm, idx_hbm, out_hbm, idx_v, rows_v, sem):
        base = (lax.axis_index("s") * SC_CORES + lax.axis_index("c")) * per_w

        @pl.loop(0, per_w // SC_ROWS)
        def _(c):
            off = base + c * SC_ROWS
            pltpu.sync_copy(idx_hbm.at[pl.ds(off, SC_ROWS)], idx_v)
            pltpu.async_copy(table_hbm.at[idx_v], rows_v, sem).wait()
            pltpu.sync_copy(rows_v, out_hbm.at[pl.ds(off, SC_ROWS)])

    return gather(table, idx)


def _sc_scatter_rows(rows, dest, n_rows):
    m, w = rows.shape
    kk = dest.shape[0]
    workers = SC_CORES * SC_SUBCORES
    per_w = m // workers
    assert per_w * workers == m and per_w % SC_ROWS == 0
    mesh = plsc.VectorSubcoreMesh(core_axis_name="c", subcore_axis_name="s")

    @functools.partial(
        pl.kernel, mesh=mesh, out_type=jax.ShapeDtypeStruct((n_rows, w), I32),
        scratch_types=[pltpu.VMEM((SC_ROWS,), I32), pltpu.VMEM((SC_ROWS, w), I32),
                       pltpu.SemaphoreType.DMA])
    def scatter(rows_hbm, dest_hbm, out_hbm, idx_v, rows_v, sem):
        base = (lax.axis_index("s") * SC_CORES + lax.axis_index("c")) * per_w

        @pl.loop(0, per_w // SC_ROWS)
        def _(c):
            off = base + c * SC_ROWS
            pltpu.sync_copy(rows_hbm.at[pl.ds(off, SC_ROWS)], rows_v)
            for k in range(kk):
                pltpu.sync_copy(dest_hbm.at[pl.ds(k * m + off, SC_ROWS)], idx_v)
                pltpu.async_copy(rows_v, out_hbm.at[idx_v], sem).wait()

    return scatter(rows, dest.reshape(-1))


def _combine_dense_kernel(y_ref, gate_ref, h_ref, gf_ref, fn_ref, o_ref, *, final):
    acc = gate_ref[:, 0:1] * _unpack_pairs(y_ref[0])
    for k in range(1, TOP_K):
        acc = acc + gate_ref[:, k:k + 1] * _unpack_pairs(y_ref[k])
    hn = h_ref[...] + gf_ref[0] * acc
    if final:
        hn = hn * lax.rsqrt(jnp.mean(hn * hn, axis=-1, keepdims=True) + EPS) * fn_ref[...]
    o_ref[...] = hn


def _combine_dense(y4, gates_col, h, gf, fnorm, seq, tm, final):
    m, d = h.shape
    tiles_per_seq = seq // tm
    return pl.pallas_call(
        functools.partial(_combine_dense_kernel, final=final),
        out_shape=jax.ShapeDtypeStruct((m, d), F32),
        grid=(m // tm,),
        in_specs=[pl.BlockSpec((TOP_K, tm, d // 2), lambda i: (0, i, 0)),
                  pl.BlockSpec((tm, TOP_K), lambda i: (i, 0)),
                  pl.BlockSpec((tm, d), lambda i: (i, 0)),
                  pl.BlockSpec((1, 1, d), lambda i: (i // tiles_per_seq, 0, 0)),
                  pl.BlockSpec((1, d), lambda i: (0, 0))],
        out_specs=pl.BlockSpec((tm, d), lambda i: (i, 0)),
        compiler_params=_cparams(("parallel",)),
    )(y4, gates_col, h, gf, fnorm.reshape(1, d))


def _expert_kernel(be_ref, nb_ref, first_ref, x_ref, wgu_ref, bgu_ref, wd_ref, bd_ref, y_ref,
                   wgu_bf, wd_bf):
    i = pl.program_id(0)

    @pl.when(i < nb_ref[0])
    def _():
        dff = wd_bf.shape[0]

        @pl.when(first_ref[i] == 1)
        def _():
            rows = 64

            def cast(r, c):
                r0 = pl.multiple_of(r * rows, rows)
                wgu_bf[pl.ds(r0, rows), :] = wgu_ref[0, 0, pl.ds(r0, rows), :].astype(BF16)
                wd_bf[pl.ds(r0, rows), :] = wd_ref[0, 0, pl.ds(r0, rows), :].astype(BF16)
                return c

            lax.fori_loop(0, dff // rows, cast, 0)

        x = _unpack_pairs(x_ref[...]).astype(BF16)
        hb = _dot(x, wgu_bf[...]) + bgu_ref[0, 0]
        h_glu = jnp.minimum(hb[:, :dff], SWIGLU_LIMIT)
        h_lin = jnp.clip(hb[:, dff:], -SWIGLU_LIMIT, SWIGLU_LIMIT)
        half = 0.5 * h_glu
        act = (half + half * jnp.tanh(SWIGLU_ALPHA * half)) * (h_lin + 1.0)
        y_ref[...] = _pack_pairs(_dot(act.astype(BF16), wd_bf[...]) + bd_ref[0, 0])


def _experts(block_e, n_used, first, xs, wgu, bgu, wd, bd, layer):
    n_rows, wp = xs.shape
    _, ne, d, ff2 = wgu.shape
    assert d == ff2 // 2
    nblk = n_rows // EXPERT_BLOCK

    def xmap(i, be, nb, fi):
        return (jnp.minimum(i, nb[0] - 1), 0)

    emap = lambda i, be, nb, fi: (layer, be[i], 0, 0)
    grid_spec = pltpu.PrefetchScalarGridSpec(
        num_scalar_prefetch=3, grid=(nblk,),
        in_specs=[pl.BlockSpec((EXPERT_BLOCK, wp), xmap),
                  pl.BlockSpec((1, 1, d, ff2), emap), pl.BlockSpec((1, 1, 1, ff2), emap),
                  pl.BlockSpec((1, 1, ff2 // 2, d), emap), pl.BlockSpec((1, 1, 1, d), emap)],
        out_specs=pl.BlockSpec((EXPERT_BLOCK, wp), xmap),
        scratch_shapes=[pltpu.VMEM((d, ff2), BF16), pltpu.VMEM((ff2 // 2, d), BF16)])
    depth = wgu.shape[0]
    return pl.pallas_call(
        _expert_kernel,
        out_shape=jax.ShapeDtypeStruct((n_rows, wp), I32),
        grid_spec=grid_spec,
        compiler_params=_cparams(("arbitrary",)),
    )(block_e, n_used, first, xs, wgu, bgu.reshape(depth, ne, 1, ff2), wd, bd.reshape(depth, ne, 1, d))


def _moe(ys, w_out, g_m, h, g, shift, scale, gf, fnorm, wr, br, wgu, bgu, wd, bd, layer, seq, final):
    m, d = h.shape
    tm = MOE_TILE
    wr_p = jnp.zeros((d, LANES), F32).at[:, :N_EXPERTS].set(wr)
    br_p = jnp.zeros((1, LANES), F32).at[0, :N_EXPERTS].set(br)
    wr_hi = wr_p.astype(BF16)
    wr_split = jnp.concatenate([wr_hi, (wr_p - wr_hi.astype(F32)).astype(BF16)], axis=1)
    h, up, topi, gates, rank, cnt = _router(ys, w_out, h, g_m, g, shift, scale, wr_split, br_p, seq, tm)

    counts = cnt[:, 0].astype(I32)
    padded = (counts + EXPERT_BLOCK - 1) // EXPERT_BLOCK * EXPERT_BLOCK
    pad_end = jnp.cumsum(padded)
    pad_start = pad_end - padded
    nblk = m * TOP_K // EXPERT_BLOCK + N_EXPERTS
    n_rows = nblk * EXPERT_BLOCK
    n_used = pad_end[-1:] // EXPERT_BLOCK
    blk = jnp.arange(nblk, dtype=I32)
    blk_c = jnp.minimum(blk, n_used - 1)
    block_e = jnp.minimum(jnp.sum(blk_c[:, None] * EXPERT_BLOCK >= pad_end[None, :], axis=1),
                          N_EXPERTS - 1).astype(I32)
    first = jnp.concatenate([jnp.ones((1,), I32), (block_e[1:] != block_e[:-1]).astype(I32)])

    dest = _dest_rows(pad_start, topi, rank, min(m, DEST_TILE))
    xs = _sc_scatter_rows(up, dest, n_rows)
    y = _experts(block_e, n_used.astype(I32), first, xs, wgu, bgu, wd, bd, layer)
    y4 = _sc_gather_rows(y, dest.reshape(-1)).reshape(TOP_K, m, d // 2)
    return _combine_dense(y4, gates.T, h, gf, fnorm, seq, tm, final)


def _block_diag(w, group):
    nb, b, _ = w.shape
    per = group // b
    wg = w.reshape(nb // per, per, b, b)
    dense = jnp.einsum("gnde,nm->gndme", wg, jnp.eye(per, dtype=w.dtype))
    return dense.reshape(nb // per, group, group)


def kernel(x, c, mod_w, mod_b, norm_mix, norm_ffn, ev_w_in, ev_lru_conv_w, ev_lru_conv_b, ev_lru_w_r, ev_lru_b_r, ev_lru_w_i, ev_lru_b_i, ev_lru_lambda, ev_ml_conv_w, ev_ml_conv_b, ev_ml_w_q, ev_ml_w_k, ev_ml_w_v, ev_ml_w_ig, ev_ml_b_ig, ev_ml_w_fg, ev_ml_b_fg, ev_ml_norm, ev_ml_skip, ev_w_out, od_w_in, od_conv_w, od_conv_b, od_dt_bias, od_a_log, od_d, od_norm, od_w_out, moe_router_w, moe_router_b, moe_w_gu, moe_b_gu, moe_w_down, moe_b_down, final_norm):
    bsz, seq, d = x.shape
    depth = mod_w.shape[0]
    m = bsz * seq
    mod = _modulation(c, mod_w, mod_b)
    h = x.reshape(m, d).astype(F32)
    for layer in range(depth):
        sh_m, sc_m, g_m, sh_f, sc_f, g_f = (mod[layer, i] for i in range(6))
        j = layer // 2
        if layer % 2 == 0:
            w = ev_lru_lambda.shape[1]
            w_in = ev_w_in[j].astype(BF16)
            proj = _inproj(h, norm_mix[layer], sh_m, sc_m, w_in, None, [w_in.shape[1]], seq, PROJ_TILE)[0]
            lru_p = dict(conv_w=ev_lru_conv_w[j], conv_b=ev_lru_conv_b[j].reshape(1, w),
                         w_r=ev_lru_w_r[j].astype(BF16), b_r=ev_lru_b_r[j].reshape(1, w),
                         w_i=ev_lru_w_i[j].astype(BF16), b_i=ev_lru_b_i[j].reshape(1, w),
                         lam=ev_lru_lambda[j].reshape(1, w))
            ya = _lru(proj, lru_p, bsz, seq, LRU_TILE)
            wg = jnp.zeros((3 * w, LANES), F32)
            wg = wg.at[:, :ML_HEADS].set(ev_ml_w_ig[j]).at[:, ML_HEADS:2 * ML_HEADS].set(ev_ml_w_fg[j])
            bg = jnp.zeros((1, LANES), F32)
            bg = bg.at[0, :ML_HEADS].set(ev_ml_b_ig[j]).at[0, ML_HEADS:2 * ML_HEADS].set(ev_ml_b_fg[j])
            ml_p = dict(conv_w=ev_ml_conv_w[j], conv_b=ev_ml_conv_b[j].reshape(1, w),
                        w_q=_block_diag(ev_ml_w_q[j], LANES).astype(BF16),
                        w_k=_block_diag(ev_ml_w_k[j], LANES).astype(BF16),
                        w_v=_block_diag(ev_ml_w_v[j], LANES).astype(BF16),
                        w_g=wg.astype(BF16), b_g=bg,
                        norm=ev_ml_norm[j].reshape(1, w), skip=ev_ml_skip[j].reshape(1, w))
            yb = _mlstm(proj, ml_p, bsz, seq)
            ys, w_out = [ya, yb], ev_w_out[j].astype(BF16)
        else:
            inner = od_norm.shape[1]
            heads = od_dt_bias.shape[1]
            conv_ch = od_conv_w.shape[2]
            w_in = od_w_in[j]
            wdt = jnp.zeros((d, LANES), F32).at[:, :heads].set(w_in[:, inner + conv_ch:])
            z, xbc, dt_raw = _inproj(h, norm_mix[layer], sh_m, sc_m, w_in.astype(BF16),
                                     wdt.astype(BF16), [inner, conv_ch], seq, PROJ_TILE)
            pad = lambda v: jnp.zeros((1, LANES), F32).at[0, :heads].set(v)
            ssd_p = dict(conv_w=od_conv_w[j], conv_b=od_conv_b[j].reshape(1, conv_ch),
                         dt_bias=pad(od_dt_bias[j]), a_log=pad(od_a_log[j]),
                         d_skip=jnp.repeat(od_d[j], SSD_HEAD_DIM).reshape(1, inner),
                         norm=od_norm[j].reshape(1, inner))
            y = _ssd(z, xbc, dt_raw, ssd_p, bsz, seq)
            ys, w_out = [y], od_w_out[j].astype(BF16)
        h = _moe(ys, w_out, g_m, h, norm_ffn[layer], sh_f, sc_f, g_f, final_norm,
                 moe_router_w[layer], moe_router_b[layer],
                 moe_w_gu, moe_b_gu, moe_w_down, moe_b_down, layer, seq, final=(layer == depth - 1))
    return h.reshape(bsz, seq, d)
```

```python
import functools

import jax
import jax.numpy as jnp
from jax import lax
from jax.experimental import pallas as pl
from jax.experimental.pallas import tpu as pltpu
from jax.experimental.pallas import tpu_sc as plsc

F32 = jnp.float32
BF16 = jnp.bfloat16
I32 = jnp.int32
HIGHEST = lax.Precision.HIGHEST

EPS = 1e-6
CONV_WIDTH = 4
LANES = 128
SUBLANES = 8
LRU_HEADS = 8
LRU_C = 8.0
ML_HEADS = 8
CHUNK = 128
PROJ_TILE = 512
PROJ_TILE_EVEN = 1024
LRU_TILE = 512
MOE_TILE = 512
COMBINE_TILE = 1024
DEST_TILE = 8192
SSD_HEAD_DIM = 64
SSD_GROUPS = 8
SSD_STATE = 128
N_EXPERTS = 32
TOP_K = 4
SWIGLU_ALPHA = 1.702
SWIGLU_LIMIT = 7.0
EXPERT_BLOCK = 512
SEQ_PER_STEP = 2
CONV_COLS = 512
VMEM_LIMIT = 56 * 1024 * 1024


def _cparams(sem, **kw):
    return pltpu.CompilerParams(dimension_semantics=sem, vmem_limit_bytes=VMEM_LIMIT, **kw)


def _silu(x):
    half = 0.5 * x
    return half + half * jnp.tanh(half)


def _log_sigmoid(x):
    return jnp.minimum(x, 0.0) - jnp.log1p(jnp.exp(-jnp.abs(x)))


def _softplus(x):
    return jnp.maximum(x, 0.0) + jnp.log1p(jnp.exp(-jnp.abs(x)))


def _dot(a, b, **kw):
    return jnp.dot(a, b, preferred_element_type=F32, **kw)


def _dot_nt(a, b):
    return lax.dot_general(a, b, (((1,), (1,)), ((), ())), preferred_element_type=F32)


def _pack_pairs(x):
    w = x.shape[1] // 2
    lo = lax.bitcast_convert_type(x[:, :w].astype(BF16).astype(F32), I32)
    hi = lax.bitcast_convert_type(x[:, w:].astype(BF16).astype(F32), I32)
    return lax.shift_right_logical(lo, 16) | (hi & jnp.int32(-65536))


def _unpack_pairs(p):
    lo = lax.bitcast_convert_type(lax.shift_left(p, 16), F32)
    hi = lax.bitcast_convert_type(p & jnp.int32(-65536), F32)
    return jnp.concatenate([lo, hi], axis=1)


def _norm_mod(h, g, shift, scale):
    y = h * lax.rsqrt(jnp.mean(h * h, axis=-1, keepdims=True) + EPS)
    return (y * g) * (1.0 + scale) + shift


def _causal_conv(x, tail_ref, w_ref, b_ref, sl):
    t = x.shape[0]
    tail = tail_ref[:, sl]
    row8 = lax.broadcasted_iota(I32, tail.shape, 0)
    out = b_ref[:, sl] + x * w_ref[CONV_WIDTH - 1:CONV_WIDTH, sl]
    for k in range(1, CONV_WIDTH):
        xs = pltpu.roll(x, k, axis=0)
        first = jnp.where(row8 < k, pltpu.roll(tail, k, axis=0), xs[:SUBLANES])
        xs = jnp.concatenate([first, xs[SUBLANES:]], axis=0)
        out = out + xs * w_ref[CONV_WIDTH - 1 - k:CONV_WIDTH - k, sl]
    tail_ref[:, sl] = x[t - SUBLANES:]
    return out


def _shift_matrix(t):
    r = lax.broadcasted_iota(I32, ((CONV_WIDTH - 1) * t, t), 0)
    c = lax.broadcasted_iota(I32, ((CONV_WIDTH - 1) * t, t), 1)
    src = (r & (t - 1)) - lax.shift_right_logical(r, t.bit_length() - 1) - 1
    return (src == c).astype(BF16)


def _causal_conv_shifted(x, shifted, tail_ref, w_ref, b_ref, sl):
    t = x.shape[0]
    tail = tail_ref[:, sl]
    row8 = lax.broadcasted_iota(I32, tail.shape, 0)
    out = b_ref[:, sl] + x * w_ref[CONV_WIDTH - 1:CONV_WIDTH, sl]
    head = jnp.zeros_like(tail)
    for k in range(1, CONV_WIDTH):
        wk = w_ref[CONV_WIDTH - 1 - k:CONV_WIDTH - k, sl]
        out = out + shifted[(k - 1) * t:k * t] * wk
        head = head + jnp.where(row8 < k, pltpu.roll(tail, k, axis=0), 0.0) * wk
    tail_ref[:, sl] = x[t - SUBLANES:]
    return jnp.concatenate([out[:SUBLANES] + head, out[SUBLANES:]], axis=0)


def _mod_kernel(c_ref, w_ref, b_ref, o_ref):
    cond = _silu(c_ref[...])
    o_ref[0, 0] = _dot(cond, w_ref[0], precision=HIGHEST) + b_ref[0, 0]


def _modulation(c, mod_w, mod_b):
    depth, d, _ = mod_w.shape
    bsz = c.shape[0]
    out = pl.pallas_call(
        _mod_kernel,
        out_shape=jax.ShapeDtypeStruct((depth, 6, bsz, d), F32),
        grid=(depth, 6),
        in_specs=[pl.BlockSpec((bsz, d), lambda l, j: (0, 0)),
                  pl.BlockSpec((1, d, d), lambda l, j: (l, 0, j)),
                  pl.BlockSpec((1, 1, 1, d), lambda l, j: (l, j, 0, 0))],
        out_specs=pl.BlockSpec((1, 1, bsz, d), lambda l, j: (l, j, 0, 0)),
        compiler_params=_cparams(("parallel", "parallel")),
    )(c.astype(F32), mod_w, mod_b.reshape(depth, 6, 1, d))
    return out.reshape(depth, 6, bsz, 1, d)


def _inproj_kernel(h_ref, g_ref, sh_ref, sc_ref, w_ref, *rest, n_chunk, with_dt):
    if with_dt:
        wdt_ref, *o_refs, odt_ref = rest
    else:
        o_refs = rest
    u = _norm_mod(h_ref[...], g_ref[...], sh_ref[0], sc_ref[0]).astype(BF16)
    off = 0
    for o_ref in o_refs:
        for n0 in range(0, o_ref.shape[1], n_chunk):
            o_ref[:, n0:n0 + n_chunk] = _dot(u, w_ref[:, off + n0:off + n0 + n_chunk]).astype(o_ref.dtype)
        off += o_ref.shape[1]
    if with_dt:
        odt_ref[...] = _dot(u, wdt_ref[...])


def _inproj(h, g, shift, scale, w, wdt, splits, seq, tm):
    m, d = h.shape
    n = sum(splits)
    assert n <= w.shape[1] and n % LANES == 0
    tiles_per_seq = seq // tm
    bmap = lambda i: (i // tiles_per_seq, 0, 0)
    in_specs = [pl.BlockSpec((tm, d), lambda i: (i, 0)),
                pl.BlockSpec((1, d), lambda i: (0, 0)),
                pl.BlockSpec((1, 1, d), bmap),
                pl.BlockSpec((1, 1, d), bmap),
                pl.BlockSpec((d, n), lambda i: (0, 0), pipeline_mode=pl.Buffered(1))]
    out_shape = [jax.ShapeDtypeStruct((m, s), BF16) for s in splits]
    out_specs = [pl.BlockSpec((tm, s), lambda i: (i, 0)) for s in splits]
    args = [h, g.reshape(1, d), shift, scale, w]
    if wdt is not None:
        in_specs.append(pl.BlockSpec((d, LANES), lambda i: (0, 0)))
        out_shape.append(jax.ShapeDtypeStruct((m, LANES), F32))
        out_specs.append(pl.BlockSpec((tm, LANES), lambda i: (i, 0)))
        args.append(wdt)
    return pl.pallas_call(
        functools.partial(_inproj_kernel, n_chunk=1024, with_dt=wdt is not None),
        out_shape=out_shape, grid=(m // tm,), in_specs=in_specs, out_specs=out_specs,
        compiler_params=_cparams(("parallel",)),
    )(*args)


def _lru_kernel(xa_ref, ga_ref, cw_ref, cb_ref, wr_ref, br_ref, wi_ref, bi_ref, lam_ref,
                o_ref, tail_ref, hc_ref):
    @pl.when(pl.program_id(1) == 0)
    def _():
        tail_ref[...] = jnp.zeros_like(tail_ref)
        hc_ref[...] = jnp.zeros_like(hc_ref)

    t = xa_ref.shape[0]
    row_in_group = lax.broadcasted_iota(I32, (t, LANES), 0) % SUBLANES
    steps = [s for s in (1, 2, 4) if s < SUBLANES]
    masks = [row_in_group >= s for s in steps]
    for hh in range(LRU_HEADS):
        sl = slice(hh * LANES, (hh + 1) * LANES)
        xc = _causal_conv(xa_ref[:, sl].astype(F32), tail_ref, cw_ref, cb_ref, sl)
        xcb = xc.astype(BF16)
        r = jax.nn.sigmoid(_dot(xcb, wr_ref[hh]) + br_ref[:, sl])
        i = jax.nn.sigmoid(_dot(xcb, wi_ref[hh]) + bi_ref[:, sl])
        log_a = LRU_C * r * _log_sigmoid(lam_ref[:, sl])
        a = jnp.exp(log_a)
        th = jnp.tanh(log_a)
        n2 = -2.0 * th
        root = jnp.where(n2 > 0.0, n2 * lax.rsqrt(n2), 0.0)
        u = (root * lax.rsqrt(1.0 - th)) * (i * xc)
        def roll_in_groups(v, s):
            v3 = v.reshape(t // SUBLANES, SUBLANES, LANES)
            return pltpu.roll(v3, s, axis=1).reshape(t, LANES)

        for s, m in zip(steps, masks):
            u = jnp.where(m, u + a * roll_in_groups(u, s), u)
            a = jnp.where(m, a * roll_in_groups(a, s), a)
        carry = hc_ref[:, sl]
        groups = []
        for r0 in range(0, t, SUBLANES):
            hg = u[r0:r0 + SUBLANES] + a[r0:r0 + SUBLANES] * carry
            carry = hg[SUBLANES - 1:SUBLANES]
            groups.append(hg)
        hc_ref[:, sl] = carry
        h = jnp.concatenate(groups, axis=0)
        ga = ga_ref[:, sl].astype(F32)
        o_ref[:, sl] = (h * jax.nn.gelu(ga, approximate=True)).astype(o_ref.dtype)


def _lru(proj, p, bsz, seq, tm):
    m = proj.shape[0]
    w = LRU_HEADS * LANES
    nt = seq // tm
    vec = lambda: pl.BlockSpec((1, w), lambda b, j: (0, 0))
    return pl.pallas_call(
        _lru_kernel,
        out_shape=jax.ShapeDtypeStruct((m, w), BF16),
        grid=(bsz, nt),
        in_specs=[pl.BlockSpec((tm, w), lambda b, j: (b * nt + j, 0)),
                  pl.BlockSpec((tm, w), lambda b, j: (b * nt + j, 1)),
                  pl.BlockSpec((CONV_WIDTH, w), lambda b, j: (0, 0)), vec(),
                  pl.BlockSpec((LRU_HEADS, LANES, LANES), lambda b, j: (0, 0, 0)), vec(),
                  pl.BlockSpec((LRU_HEADS, LANES, LANES), lambda b, j: (0, 0, 0)), vec(), vec()],
        out_specs=pl.BlockSpec((tm, w), lambda b, j: (b * nt + j, 0)),
        scratch_shapes=[pltpu.VMEM((SUBLANES, w), F32), pltpu.VMEM((1, w), F32)],
        compiler_params=_cparams(("parallel", "arbitrary")),
    )(proj, proj, p["conv_w"], p["conv_b"], p["w_r"], p["b_r"], p["w_i"], p["b_i"], p["lam"])


def _mlstm_kernel(xb_ref, zb_ref, cw_ref, cb_ref, wq_ref, wk_ref, wv_ref, wg_ref, bg_ref,
                  nw_ref, sk_ref, o_ref, tail_ref, qkv_ref, xc_ref, caug_ref, m_ref):
    @pl.when(pl.program_id(1) == 0)
    def _():
        tail_ref[...] = jnp.zeros_like(tail_ref)
        caug_ref[...] = jnp.zeros_like(caug_ref)
        m_ref[...] = jnp.full(m_ref.shape, -jnp.inf, F32)

    for s in range(xb_ref.shape[0]):
        for c0 in range(0, xb_ref.shape[1], CHUNK):
            rows = pl.ds(c0, CHUNK)
            _mlstm_chunk(xb_ref.at[s, rows], zb_ref.at[s, rows], cw_ref, cb_ref, wq_ref, wk_ref, wv_ref,
                         wg_ref, bg_ref, nw_ref, sk_ref, o_ref.at[s, rows], tail_ref.at[s], qkv_ref.at[s],
                         xc_ref.at[s], caug_ref.at[s], m_ref.at[s])


def _mlstm_chunk(xb_ref, zb_ref, cw_ref, cb_ref, wq_ref, wk_ref, wv_ref, wg_ref, bg_ref,
                 nw_ref, sk_ref, o_ref, tail_ref, qkv_ref, xc_ref, caug_ref, m_ref):
    L = CHUNK
    width = ML_HEADS * LANES
    scale = LANES ** -0.5
    for hh in range(ML_HEADS):
        sl = slice(hh * LANES, (hh + 1) * LANES)
        xb = xb_ref[:, sl].astype(F32)
        xc = _silu(_causal_conv(xb, tail_ref, cw_ref, cb_ref, sl))
        xc_ref[:, sl] = xc
        xcb = xc.astype(BF16)
        qkv_ref[:, sl] = _dot(xcb, wq_ref[hh]).astype(BF16)
        qkv_ref[:, width + hh * LANES:width + (hh + 1) * LANES] = _dot(xcb, wk_ref[hh]).astype(BF16)
        qkv_ref[:, 2 * width + hh * LANES:2 * width + (hh + 1) * LANES] = (
            _dot(xb.astype(BF16), wv_ref[hh]).astype(BF16))

    gates = _dot(qkv_ref[...], wg_ref[...]) + bg_ref[...]
    rowi = lax.broadcasted_iota(I32, (L, L), 0)
    coli = lax.broadcasted_iota(I32, (L, L), 1)
    causal = rowi >= coli
    lf = jnp.where((coli >= ML_HEADS) & (coli < 2 * ML_HEADS), _log_sigmoid(gates), 0.0)
    tri = causal.astype(BF16)
    lf_hi = lf.astype(BF16)
    lf_mid = (lf - lf_hi.astype(F32)).astype(BF16)
    lf_lo = (lf - lf_hi.astype(F32) - lf_mid.astype(F32)).astype(BF16)
    gcum = _dot(tri, lf_hi) + (_dot(tri, lf_mid) + _dot(tri, lf_lo))
    x_col = jnp.where(coli < ML_HEADS, gates, gcum)
    x_row = x_col.T
    ones = jnp.ones((L, LANES), BF16)
    heads = range(ML_HEADS)
    hsl = [slice(hh * LANES, (hh + 1) * LANES) for hh in heads]

    qs = [qkv_ref[:, hsl[hh]] for hh in heads]
    ks = [qkv_ref[:, width + hh * LANES:width + (hh + 1) * LANES] for hh in heads]
    vaugs = [jnp.concatenate([qkv_ref[:, 2 * width + hh * LANES:2 * width + (hh + 1) * LANES], ones], axis=1)
             for hh in heads]
    scores = [_dot_nt(qs[hh], ks[hh]) * scale for hh in heads]
    ics = [jnp.broadcast_to(x_col[:, hh:hh + 1], (L, LANES)) for hh in heads]
    gcs = [jnp.broadcast_to(x_col[:, ML_HEADS + hh:ML_HEADS + hh + 1], (L, LANES)) for hh in heads]
    irs = [x_row[hh:hh + 1, :] for hh in heads]
    grs = [x_row[ML_HEADS + hh:ML_HEADS + hh + 1, :] for hh in heads]
    mps = [m_ref[hh] for hh in heads]
    dmats = [jnp.where(causal, gcs[hh] - grs[hh] + irs[hh], -jnp.inf) for hh in heads]
    m_inters = [mps[hh] + gcs[hh] for hh in heads]
    m_ts = [jnp.maximum(m_inters[hh], jnp.max(dmats[hh], axis=1, keepdims=True)) for hh in heads]
    qks = [(scores[hh] * jnp.exp(dmats[hh] - m_ts[hh])).astype(BF16) for hh in heads]
    caugs = [caug_ref[hh] for hh in heads]
    w_inters = [jnp.exp(m_inters[hh] - m_ts[hh]) for hh in heads]
    nds = [_dot(qks[hh], vaugs[hh])
           + jnp.concatenate([w_inters[hh], w_inters[hh]], axis=1) * _dot(qs[hh], caugs[hh].astype(BF16))
           for hh in heads]

    g_lasts = [gcs[hh][L - 1:L, :] for hh in heads]
    m_news = [jnp.maximum(mps[hh] + g_lasts[hh],
                          jnp.max(g_lasts[hh] - grs[hh] + irs[hh], axis=1, keepdims=True)) for hh in heads]
    for hh in heads:
        ws = jnp.exp(g_lasts[hh] - gcs[hh] + ics[hh] - m_news[hh])
        wc = jnp.exp(mps[hh] + g_lasts[hh] - m_news[hh])
        kw_t = (ks[hh].astype(F32) * (ws * scale)).T.astype(BF16)
        caug_ref[hh] = jnp.concatenate([wc, wc], axis=1) * caugs[hh] + _dot(kw_t, vaugs[hh])
        m_ref[hh] = m_news[hh]

    hvals = [nds[hh][:, :LANES] / jnp.maximum(jnp.abs(nds[hh][:, LANES:]), jnp.exp(-m_ts[hh]))
             for hh in heads]
    mus = [jnp.mean(hvals[hh], axis=1, keepdims=True) for hh in heads]
    dvs = [hvals[hh] - mus[hh] for hh in heads]
    variances = [jnp.mean(dvs[hh] * dvs[hh], axis=1, keepdims=True) for hh in heads]
    for hh in heads:
        sl = hsl[hh]
        hn = dvs[hh] * lax.rsqrt(variances[hh] + EPS) * nw_ref[:, sl]
        zb = zb_ref[:, sl].astype(F32)
        o_ref[:, sl] = ((hn + sk_ref[:, sl] * xc_ref[:, sl]) * _silu(zb)).astype(o_ref.dtype)


def _mlstm(proj, p, bsz, seq):
    m = proj.shape[0]
    w = ML_HEADS * LANES
    sps = 1
    cps = next(c for c in (4, 2, 1) if seq % (c * CHUNK) == 0)
    nt = seq // (cps * CHUNK)
    vec = lambda: pl.BlockSpec((1, w), lambda b, j: (0, 0))
    blk = lambda: pl.BlockSpec((ML_HEADS, LANES, LANES), lambda b, j: (0, 0, 0))
    tile = lambda col: pl.BlockSpec((sps, cps * CHUNK, w), lambda b, j: (b, j, col))
    proj3 = proj.reshape(bsz, seq, proj.shape[1])
    out = pl.pallas_call(
        _mlstm_kernel,
        out_shape=jax.ShapeDtypeStruct((bsz, seq, w), BF16),
        grid=(bsz // sps, nt),
        in_specs=[tile(2), tile(3),
                  pl.BlockSpec((CONV_WIDTH, w), lambda b, j: (0, 0)), vec(),
                  blk(), blk(), blk(),
                  pl.BlockSpec((3 * w, LANES), lambda b, j: (0, 0)),
                  pl.BlockSpec((1, LANES), lambda b, j: (0, 0)),
                  vec(), vec()],
        out_specs=tile(0),
        scratch_shapes=[pltpu.VMEM((sps, SUBLANES, w), F32),
                        pltpu.VMEM((sps, CHUNK, 3 * w), BF16),
                        pltpu.VMEM((sps, CHUNK, w), F32),
                        pltpu.VMEM((sps, ML_HEADS, LANES, 2 * LANES), F32),
                        pltpu.VMEM((sps, ML_HEADS, 1, LANES), F32)],
        compiler_params=_cparams(("parallel", "arbitrary")),
    )(proj3, proj3, p["conv_w"], p["conv_b"], p["w_q"], p["w_k"], p["w_v"], p["w_g"], p["b_g"],
      p["norm"], p["skip"])
    return out.reshape(m, w)


def _ssd_kernel(z_ref, xbc_ref, dt_ref, cw_ref, cb_ref, dtb_ref, alog_ref, dsk_ref, nw_ref,
                o_ref, tail_ref, act_ref, st_ref):
    @pl.when(pl.program_id(1) == 0)
    def _():
        tail_ref[...] = jnp.zeros_like(tail_ref)
        st_ref[...] = jnp.zeros_like(st_ref)

    for c0 in range(0, z_ref.shape[1], CHUNK):
        rows = pl.ds(c0, CHUNK)
        for s in range(z_ref.shape[0]):
            _ssd_chunk(z_ref.at[s, rows], xbc_ref.at[s, rows], dt_ref.at[s, rows], cw_ref, cb_ref, dtb_ref,
                       alog_ref, dsk_ref, nw_ref, o_ref.at[s, rows], tail_ref.at[s], act_ref.at[s],
                       st_ref.at[s])


def _ssd_chunk(z_ref, xbc_ref, dt_ref, cw_ref, cb_ref, dtb_ref, alog_ref, dsk_ref, nw_ref,
               o_ref, tail_ref, act_ref, st_ref):
    L = CHUNK
    inner = o_ref.shape[1]
    gw = inner // SSD_GROUPS
    hpg = gw // SSD_HEAD_DIM
    b_off = inner
    c_off = inner + SSD_GROUPS * SSD_STATE
    shift = _shift_matrix(L)
    for c0 in range(0, xbc_ref.shape[1], CONV_COLS):
        shifted = _dot(shift, xbc_ref[:, c0:c0 + CONV_COLS])
        for l0 in range(0, CONV_COLS, LANES):
            sl = slice(c0 + l0, c0 + l0 + LANES)
            act_ref[:, sl] = _silu(_causal_conv_shifted(
                xbc_ref[:, sl].astype(F32), shifted[:, l0:l0 + LANES], tail_ref, cw_ref, cb_ref, sl))

    rowi = lax.broadcasted_iota(I32, (L, L), 0)
    coli = lax.broadcasted_iota(I32, (L, L), 1)
    causal = rowi >= coli
    dt = _softplus(dt_ref[...] + dtb_ref[...])
    da = dt * (-jnp.exp(alog_ref[...]))
    tri = causal.astype(BF16)
    da_hi = da.astype(BF16)
    da_mid = (da - da_hi.astype(F32)).astype(BF16)
    da_lo = (da - da_hi.astype(F32) - da_mid.astype(F32)).astype(BF16)
    a = _dot(tri, da_hi) + (_dot(tri, da_mid) + _dot(tri, da_lo))
    a_t = a.T
    hpt = LANES // SSD_HEAD_DIM
    lane = lax.broadcasted_iota(I32, (L, LANES), 1)

    def over_heads(tiles):
        cols = []
        for c0 in range(0, hpg, hpt):
            out = tiles[c0 + hpt - 1]
            for j in range(hpt - 2, -1, -1):
                out = jnp.where(lane < (j + 1) * SSD_HEAD_DIM, tiles[c0 + j], out)
            cols.append(out)
        return jnp.concatenate(cols, axis=1)

    for g in range(SSD_GROUPS):
        gsl = slice(g * gw, (g + 1) * gw)
        xg = act_ref[:, gsl]
        bg = act_ref[:, b_off + g * SSD_STATE:b_off + (g + 1) * SSD_STATE]
        cg_ = act_ref[:, c_off + g * SSD_STATE:c_off + (g + 1) * SSD_STATE].astype(BF16)
        cb = _dot_nt(cg_, bg.astype(BF16))
        state = st_ref[g]
        hds = [g * hpg + jj for jj in range(hpg)]
        a_bs = [jnp.broadcast_to(a[:, hd:hd + 1], (L, LANES)) for hd in hds]
        dt_bs = [jnp.broadcast_to(dt[:, hd:hd + 1], (L, LANES)) for hd in hds]
        ea_x = over_heads([jnp.exp(a_b) for a_b in a_bs])
        to_end_x = over_heads([jnp.exp(a_b[L - 1:L, :] - a_b) for a_b in a_bs])
        xdt = xg * over_heads(dt_bs)
        inter = _dot(cg_, state.astype(BF16)) * ea_x
        acc = [inter[:, c0:c0 + LANES] for c0 in range(0, gw, LANES)]
        for jj in range(hpg):
            seg = jnp.where(causal, a_bs[jj] - a_t[hds[jj]:hds[jj] + 1, :], -jnp.inf)
            w = (cb * jnp.exp(seg)).astype(BF16)
            c, j = divmod(jj, hpt)
            in_head = (lane >= j * SSD_HEAD_DIM) & (lane < (j + 1) * SSD_HEAD_DIM)
            x_head = jnp.where(in_head, xdt[:, c * LANES:(c + 1) * LANES], 0.0).astype(BF16)
            acc[c] = acc[c] + _dot(w, x_head)
        acc = jnp.concatenate(acc, axis=1)
        y = (acc + dsk_ref[:, gsl] * xg) * _silu(z_ref[:, gsl].astype(F32))
        y = y * lax.rsqrt(jnp.mean(y * y, axis=1, keepdims=True) + EPS) * nw_ref[:, gsl]
        o_ref[:, gsl] = y.astype(o_ref.dtype)
        xw = (xdt * to_end_x).astype(BF16)
        st_ref[g] = ea_x[L - 1:L, :] * state + _dot(bg.T.astype(BF16), xw)


def _ssd(z, xbc, dt_raw, p, bsz, seq):
    m, inner = z.shape
    conv_ch = xbc.shape[1]
    sps = SEQ_PER_STEP if bsz % SEQ_PER_STEP == 0 else 1
    cps = 2 if seq % (2 * CHUNK) == 0 else 1
    nt = seq // (cps * CHUNK)
    vec = lambda n: pl.BlockSpec((1, n), lambda b, j: (0, 0))
    tile = lambda n: pl.BlockSpec((sps, cps * CHUNK, n), lambda b, j: (b, j, 0))
    out = pl.pallas_call(
        _ssd_kernel,
        out_shape=jax.ShapeDtypeStruct((bsz, seq, inner), BF16),
        grid=(bsz // sps, nt),
        in_specs=[tile(inner), tile(conv_ch), tile(LANES),
                  pl.BlockSpec((CONV_WIDTH, conv_ch), lambda b, j: (0, 0)), vec(conv_ch),
                  vec(LANES), vec(LANES), vec(inner), vec(inner)],
        out_specs=tile(inner),
        scratch_shapes=[pltpu.VMEM((sps, SUBLANES, conv_ch), F32),
                        pltpu.VMEM((sps, CHUNK, conv_ch), F32),
                        pltpu.VMEM((sps, SSD_GROUPS, SSD_STATE, inner // SSD_GROUPS), F32)],
        compiler_params=_cparams(("parallel", "arbitrary")),
    )(z.reshape(bsz, seq, inner), xbc.reshape(bsz, seq, conv_ch), dt_raw.reshape(bsz, seq, LANES),
      p["conv_w"], p["conv_b"], p["dt_bias"], p["a_log"], p["d_skip"], p["norm"])
    return out.reshape(m, inner)


def _router_kernel(*refs, n_in):
    y_refs, w_refs = refs[:n_in], refs[n_in:2 * n_in]
    (h_ref, gm_ref, g_ref, sh_ref, sc_ref, wr_ref, br_ref,
     hmid_ref, up_ref, topi_ref, gate_ref, rank_ref, cnt_ref, carry_ref) = refs[2 * n_in:]

    @pl.when(pl.program_id(0) == 0)
    def _():
        carry_ref[...] = jnp.zeros_like(carry_ref)

    tm = h_ref.shape[0]
    acc = _dot(y_refs[0][...], w_refs[0][...])
    for y_ref, w_ref in zip(y_refs[1:], w_refs[1:]):
        acc = acc + _dot(y_ref[...], w_ref[...])
    hmid = h_ref[...] + gm_ref[0] * acc
    hmid_ref[...] = hmid
    u = _norm_mod(hmid, g_ref[...], sh_ref[0], sc_ref[0])
    up_ref[...] = _pack_pairs(u)
    u_hi = u.astype(BF16)
    u_lo = (u - u_hi.astype(F32)).astype(BF16)
    hi_both = _dot(u_hi, wr_ref[...])
    logits = (hi_both[:, :LANES] + (_dot(u_lo, wr_ref[:, :LANES]) + hi_both[:, LANES:])
              + br_ref[...])
    lt = jnp.concatenate([logits[r0:r0 + LANES].T for r0 in range(0, tm, LANES)], axis=1)
    l = lt[:N_EXPERTS]
    e_iota = lax.broadcasted_iota(I32, (N_EXPERTS, tm), 0).astype(F32)
    vals, idxs, hots = [], [], []
    for _ in range(TOP_K):
        mx = jnp.max(l, axis=0, keepdims=True)
        idx = jnp.min(jnp.where(l == mx, e_iota, float(N_EXPERTS)), axis=0, keepdims=True)
        hot = e_iota == idx
        l = jnp.where(hot, -jnp.inf, l)
        vals.append(mx)
        idxs.append(idx)
        hots.append(hot)
    exps = [jnp.exp(v - vals[0]) for v in vals]
    den = exps[0] + exps[1] + exps[2] + exps[3]
    gate_ref[...] = jnp.concatenate([e / den for e in exps], axis=0)
    topi_ref[...] = jnp.concatenate(idxs, axis=0).astype(I32)

    sel = jnp.zeros((N_EXPERTS, tm), F32)
    for hot in hots:
        sel = jnp.where(hot, 1.0, sel)
    r_i = lax.broadcasted_iota(I32, (tm, tm), 0)
    c_i = lax.broadcasted_iota(I32, (tm, tm), 1)
    before = (r_i < c_i).astype(BF16)
    carry = carry_ref[:, 0:1]
    cum = _dot(sel.astype(BF16), before) + carry
    rank_ref[...] = jnp.concatenate(
        [jnp.sum(jnp.where(hot, cum, 0.0), axis=0, keepdims=True) for hot in hots], axis=0).astype(I32)
    total = carry + jnp.sum(sel, axis=1, keepdims=True)
    carry_ref[...] = jnp.broadcast_to(total, carry_ref.shape)
    cnt_ref[...] = jnp.broadcast_to(total, cnt_ref.shape)


def _router(ys, w_out, h, g_m, g, shift, scale, wr, br, seq, tm):
    m, d = h.shape
    tiles_per_seq = seq // tm
    bmap = lambda i: (i // tiles_per_seq, 0, 0)
    row4 = lambda: pl.BlockSpec((TOP_K, tm), lambda i: (0, i))
    in_specs, args, k0 = [], [], 0
    for y in ys:
        in_specs.append(pl.BlockSpec((tm, y.shape[1]), lambda i: (i, 0)))
        args.append(y)
    for y in ys:
        kk = y.shape[1]
        in_specs.append(pl.BlockSpec((kk, d), lambda i, kb=k0 // kk: (kb, 0)))
        args.append(w_out)
        k0 += kk
    in_specs += [pl.BlockSpec((tm, d), lambda i: (i, 0)),
                 pl.BlockSpec((1, 1, d), bmap),
                 pl.BlockSpec((1, d), lambda i: (0, 0)),
                 pl.BlockSpec((1, 1, d), bmap), pl.BlockSpec((1, 1, d), bmap),
                 pl.BlockSpec((d, 2 * LANES), lambda i: (0, 0)),
                 pl.BlockSpec((1, LANES), lambda i: (0, 0))]
    args += [h, g_m, g.reshape(1, d), shift, scale, wr, br]
    return pl.pallas_call(
        functools.partial(_router_kernel, n_in=len(ys)),
        out_shape=[jax.ShapeDtypeStruct((m, d), F32),
                   jax.ShapeDtypeStruct((m, d // 2), I32),
                   jax.ShapeDtypeStruct((TOP_K, m), I32),
                   jax.ShapeDtypeStruct((TOP_K, m), F32),
                   jax.ShapeDtypeStruct((TOP_K, m), I32),
                   jax.ShapeDtypeStruct((N_EXPERTS, LANES), F32)],
        grid=(m // tm,),
        in_specs=in_specs,
        out_specs=[pl.BlockSpec((tm, d), lambda i: (i, 0)),
                   pl.BlockSpec((tm, d // 2), lambda i: (i, 0)), row4(), row4(), row4(),
                   pl.BlockSpec((N_EXPERTS, LANES), lambda i: (0, 0))],
        scratch_shapes=[pltpu.VMEM((N_EXPERTS, LANES), F32)],
        compiler_params=_cparams(("arbitrary",)),
    )(*args)


def _dest_kernel(ps_ref, topi_ref, rank_ref, o_ref):
    topi = topi_ref[...]
    acc = rank_ref[...]
    for e in range(N_EXPERTS):
        acc = acc + jnp.where(topi == e, ps_ref[e], 0)
    o_ref[...] = acc


def _dest_rows(pad_start, topi, rank, tw):
    k, m = topi.shape
    blk = lambda: pl.BlockSpec((k, tw), lambda i, ps: (0, i))
    return pl.pallas_call(
        _dest_kernel,
        out_shape=jax.ShapeDtypeStruct((k, m), I32),
        grid_spec=pltpu.PrefetchScalarGridSpec(
            num_scalar_prefetch=1, grid=(m // tw,), in_specs=[blk(), blk()], out_specs=blk()),
        compiler_params=_cparams(("parallel",)),
    )(pad_start, topi, rank)


SC_CORES = 2
SC_SUBCORES = 16
SC_ROWS = 128


def _sc_gather_rows(table, idx):
    b = idx.shape[0]
    w = table.shape[1]
    workers = SC_CORES * SC_SUBCORES
    per_w = b // workers
    assert per_w * workers == b and per_w % SC_ROWS == 0
    mesh = plsc.VectorSubcoreMesh(core_axis_name="c", subcore_axis_name="s")

    @functools.partial(
        pl.kernel, mesh=mesh, out_type=jax.ShapeDtypeStruct((b, w), I32),
        scratch_types=[pltpu.VMEM((SC_ROWS,), I32), pltpu.VMEM((SC_ROWS, w), I32),
                       pltpu.SemaphoreType.DMA])
    def gather(table_hbm, idx_hbm, out_hbm, idx_v, rows_v, sem):
        base = (lax.axis_index("s") * SC_CORES + lax.axis_index("c")) * per_w

        @pl.loop(0, per_w // SC_ROWS)
        def _(c):
            off = base + c * SC_ROWS
            pltpu.sync_copy(idx_hbm.at[pl.ds(off, SC_ROWS)], idx_v)
            pltpu.async_copy(table_hbm.at[idx_v], rows_v, sem).wait()
            pltpu.sync_copy(rows_v, out_hbm.at[pl.ds(off, SC_ROWS)])

    return gather(table, idx)


def _sc_scatter_rows(rows, dest, n_rows):
    m, w = rows.shape
    kk = dest.shape[0]
    workers = SC_CORES * SC_SUBCORES
    per_w = m // workers
    assert per_w * workers == m and per_w % SC_ROWS == 0
    mesh = plsc.VectorSubcoreMesh(core_axis_name="c", subcore_axis_name="s")

    @functools.partial(
        pl.kernel, mesh=mesh, out_type=jax.ShapeDtypeStruct((n_rows, w), I32),
        scratch_types=[pltpu.VMEM((SC_ROWS,), I32), pltpu.VMEM((SC_ROWS, w), I32),
                       pltpu.SemaphoreType.DMA])
    def scatter(rows_hbm, dest_hbm, out_hbm, idx_v, rows_v, sem):
        base = (lax.axis_index("s") * SC_CORES + lax.axis_index("c")) * per_w

        @pl.loop(0, per_w // SC_ROWS)
        def _(c):
            off = base + c * SC_ROWS
            pltpu.sync_copy(rows_hbm.at[pl.ds(off, SC_ROWS)], rows_v)
            for k in range(kk):
                pltpu.sync_copy(dest_hbm.at[pl.ds(k * m + off, SC_ROWS)], idx_v)
                pltpu.async_copy(rows_v, out_hbm.at[idx_v], sem).wait()

    return scatter(rows, dest.reshape(-1))


def _combine_dense_kernel(y_ref, gate_ref, h_ref, gf_ref, fn_ref, o_ref, *, final):
    acc = gate_ref[:, 0:1] * _unpack_pairs(y_ref[0])
    for k in range(1, TOP_K):
        acc = acc + gate_ref[:, k:k + 1] * _unpack_pairs(y_ref[k])
    hn = h_ref[...] + gf_ref[0] * acc
    if final:
        hn = hn * lax.rsqrt(jnp.mean(hn * hn, axis=-1, keepdims=True) + EPS) * fn_ref[...]
    o_ref[...] = hn


def _combine_dense(y4, gates_col, h, gf, fnorm, seq, tm, final):
    m, d = h.shape
    tiles_per_seq = seq // tm
    return pl.pallas_call(
        functools.partial(_combine_dense_kernel, final=final),
        out_shape=jax.ShapeDtypeStruct((m, d), F32),
        grid=(m // tm,),
        in_specs=[pl.BlockSpec((TOP_K, tm, d // 2), lambda i: (0, i, 0)),
                  pl.BlockSpec((tm, TOP_K), lambda i: (i, 0)),
                  pl.BlockSpec((tm, d), lambda i: (i, 0)),
                  pl.BlockSpec((1, 1, d), lambda i: (i // tiles_per_seq, 0, 0)),
                  pl.BlockSpec((1, d), lambda i: (0, 0))],
        out_specs=pl.BlockSpec((tm, d), lambda i: (i, 0)),
        compiler_params=_cparams(("parallel",)),
    )(y4, gates_col, h, gf, fnorm.reshape(1, d))


def _expert_kernel(be_ref, nb_ref, first_ref, x_ref, wgu_ref, bgu_ref, wd_ref, bd_ref, y_ref,
                   wgu_bf, wd_bf):
    i = pl.program_id(0)

    @pl.when(i < nb_ref[0])
    def _():
        dff = wd_bf.shape[0]

        @pl.when(first_ref[i] == 1)
        def _():
            rows = 64

            def cast(r, c):
                r0 = pl.multiple_of(r * rows, rows)
                wgu_bf[pl.ds(r0, rows), :] = wgu_ref[0, 0, pl.ds(r0, rows), :].astype(BF16)
                wd_bf[pl.ds(r0, rows), :] = wd_ref[0, 0, pl.ds(r0, rows), :].astype(BF16)
                return c

            lax.fori_loop(0, dff // rows, cast, 0)

        x = _unpack_pairs(x_ref[...]).astype(BF16)
        hb = _dot(x, wgu_bf[...]) + bgu_ref[0, 0]
        h_glu = jnp.minimum(hb[:, :dff], SWIGLU_LIMIT)
        h_lin = jnp.clip(hb[:, dff:], -SWIGLU_LIMIT, SWIGLU_LIMIT)
        half = 0.5 * h_glu
        act = (half + half * jnp.tanh(SWIGLU_ALPHA * half)) * (h_lin + 1.0)
        y_ref[...] = _pack_pairs(_dot(act.astype(BF16), wd_bf[...]) + bd_ref[0, 0])


def _experts(block_e, n_used, first, xs, wgu, bgu, wd, bd, layer):
    n_rows, wp = xs.shape
    _, ne, d, ff2 = wgu.shape
    assert d == ff2 // 2
    nblk = n_rows // EXPERT_BLOCK

    def xmap(i, be, nb, fi):
        return (jnp.minimum(i, nb[0] - 1), 0)

    emap = lambda i, be, nb, fi: (layer, be[i], 0, 0)
    grid_spec = pltpu.PrefetchScalarGridSpec(
        num_scalar_prefetch=3, grid=(nblk,),
        in_specs=[pl.BlockSpec((EXPERT_BLOCK, wp), xmap),
                  pl.BlockSpec((1, 1, d, ff2), emap), pl.BlockSpec((1, 1, 1, ff2), emap),
                  pl.BlockSpec((1, 1, ff2 // 2, d), emap), pl.BlockSpec((1, 1, 1, d), emap)],
        out_specs=pl.BlockSpec((EXPERT_BLOCK, wp), xmap),
        scratch_shapes=[pltpu.VMEM((d, ff2), BF16), pltpu.VMEM((ff2 // 2, d), BF16)])
    depth = wgu.shape[0]
    return pl.pallas_call(
        _expert_kernel,
        out_shape=jax.ShapeDtypeStruct((n_rows, wp), I32),
        grid_spec=grid_spec,
        compiler_params=_cparams(("arbitrary",)),
    )(block_e, n_used, first, xs, wgu, bgu.reshape(depth, ne, 1, ff2), wd, bd.reshape(depth, ne, 1, d))


def _moe(ys, w_out, g_m, h, g, shift, scale, gf, fnorm, wr, br, wgu, bgu, wd, bd, layer, seq, final):
    m, d = h.shape
    tm = MOE_TILE
    wr_p = jnp.zeros((d, LANES), F32).at[:, :N_EXPERTS].set(wr)
    br_p = jnp.zeros((1, LANES), F32).at[0, :N_EXPERTS].set(br)
    wr_hi = wr_p.astype(BF16)
    wr_split = jnp.concatenate([wr_hi, (wr_p - wr_hi.astype(F32)).astype(BF16)], axis=1)
    h, up, topi, gates, rank, cnt = _router(ys, w_out, h, g_m, g, shift, scale, wr_split, br_p, seq, tm)

    counts = cnt[:, 0].astype(I32)
    padded = (counts + EXPERT_BLOCK - 1) // EXPERT_BLOCK * EXPERT_BLOCK
    pad_end = jnp.cumsum(padded)
    pad_start = pad_end - padded
    nblk = m * TOP_K // EXPERT_BLOCK + N_EXPERTS
    n_rows = nblk * EXPERT_BLOCK
    n_used = pad_end[-1:] // EXPERT_BLOCK
    blk = jnp.arange(nblk, dtype=I32)
    blk_c = jnp.minimum(blk, n_used - 1)
    block_e = jnp.minimum(jnp.sum(blk_c[:, None] * EXPERT_BLOCK >= pad_end[None, :], axis=1),
                          N_EXPERTS - 1).astype(I32)
    first = jnp.concatenate([jnp.ones((1,), I32), (block_e[1:] != block_e[:-1]).astype(I32)])

    dest = _dest_rows(pad_start, topi, rank, min(m, DEST_TILE))
    xs = _sc_scatter_rows(up, dest, n_rows)
    y = _experts(block_e, n_used.astype(I32), first, xs, wgu, bgu, wd, bd, layer)
    y4 = _sc_gather_rows(y, dest.reshape(-1)).reshape(TOP_K, m, d // 2)
    return _combine_dense(y4, gates.T, h, gf, fnorm, seq, min(seq, COMBINE_TILE), final)


def _block_diag(w, group):
    nb, b, _ = w.shape
    per = group // b
    wg = w.reshape(nb // per, per, b, b)
    dense = jnp.einsum("gnde,nm->gndme", wg, jnp.eye(per, dtype=w.dtype))
    return dense.reshape(nb // per, group, group)


def kernel(x, c, mod_w, mod_b, norm_mix, norm_ffn, ev_w_in, ev_lru_conv_w, ev_lru_conv_b, ev_lru_w_r, ev_lru_b_r, ev_lru_w_i, ev_lru_b_i, ev_lru_lambda, ev_ml_conv_w, ev_ml_conv_b, ev_ml_w_q, ev_ml_w_k, ev_ml_w_v, ev_ml_w_ig, ev_ml_b_ig, ev_ml_w_fg, ev_ml_b_fg, ev_ml_norm, ev_ml_skip, ev_w_out, od_w_in, od_conv_w, od_conv_b, od_dt_bias, od_a_log, od_d, od_norm, od_w_out, moe_router_w, moe_router_b, moe_w_gu, moe_b_gu, moe_w_down, moe_b_down, final_norm):
    bsz, seq, d = x.shape
    depth = mod_w.shape[0]
    m = bsz * seq
    mod = _modulation(c, mod_w, mod_b)
    h = x.reshape(m, d).astype(F32)
    for layer in range(depth):
        sh_m, sc_m, g_m, sh_f, sc_f, g_f = (mod[layer, i] for i in range(6))
        j = layer // 2
        if layer % 2 == 0:
            w = ev_lru_lambda.shape[1]
            w_in = ev_w_in[j].astype(BF16)
            proj = _inproj(h, norm_mix[layer], sh_m, sc_m, w_in, None, [w_in.shape[1]], seq,
                           min(seq, PROJ_TILE_EVEN))[0]
            lru_p = dict(conv_w=ev_lru_conv_w[j], conv_b=ev_lru_conv_b[j].reshape(1, w),
                         w_r=ev_lru_w_r[j].astype(BF16), b_r=ev_lru_b_r[j].reshape(1, w),
                         w_i=ev_lru_w_i[j].astype(BF16), b_i=ev_lru_b_i[j].reshape(1, w),
                         lam=ev_lru_lambda[j].reshape(1, w))
            ya = _lru(proj, lru_p, bsz, seq, LRU_TILE)
            wg = jnp.zeros((3 * w, LANES), F32)
            wg = wg.at[:, :ML_HEADS].set(ev_ml_w_ig[j]).at[:, ML_HEADS:2 * ML_HEADS].set(ev_ml_w_fg[j])
            bg = jnp.zeros((1, LANES), F32)
            bg = bg.at[0, :ML_HEADS].set(ev_ml_b_ig[j]).at[0, ML_HEADS:2 * ML_HEADS].set(ev_ml_b_fg[j])
            ml_p = dict(conv_w=ev_ml_conv_w[j], conv_b=ev_ml_conv_b[j].reshape(1, w),
                        w_q=_block_diag(ev_ml_w_q[j], LANES).astype(BF16),
                        w_k=_block_diag(ev_ml_w_k[j], LANES).astype(BF16),
                        w_v=_block_diag(ev_ml_w_v[j], LANES).astype(BF16),
                        w_g=wg.astype(BF16), b_g=bg,
                        norm=ev_ml_norm[j].reshape(1, w), skip=ev_ml_skip[j].reshape(1, w))
            yb = _mlstm(proj, ml_p, bsz, seq)
            ys, w_out = [ya, yb], ev_w_out[j].astype(BF16)
        else:
            inner = od_norm.shape[1]
            heads = od_dt_bias.shape[1]
            conv_ch = od_conv_w.shape[2]
            w_in = od_w_in[j]
            wdt = jnp.zeros((d, LANES), F32).at[:, :heads].set(w_in[:, inner + conv_ch:])
            z, xbc, dt_raw = _inproj(h, norm_mix[layer], sh_m, sc_m, w_in.astype(BF16),
                                     wdt.astype(BF16), [inner, conv_ch], seq, PROJ_TILE)
            pad = lambda v: jnp.zeros((1, LANES), F32).at[0, :heads].set(v)
            ssd_p = dict(conv_w=od_conv_w[j], conv_b=od_conv_b[j].reshape(1, conv_ch),
                         dt_bias=pad(od_dt_bias[j]), a_log=pad(od_a_log[j]),
                         d_skip=jnp.repeat(od_d[j], SSD_HEAD_DIM).reshape(1, inner),
                         norm=od_norm[j].reshape(1, inner))
            y = _ssd(z, xbc, dt_raw, ssd_p, bsz, seq)
            ys, w_out = [y], od_w_out[j].astype(BF16)
        h = _moe(ys, w_out, g_m, h, norm_ffn[layer], sh_f, sc_f, g_f, final_norm,
                 moe_router_w[layer], moe_router_b[layer],
                 moe_w_gu, moe_b_gu, moe_w_down, moe_b_down, layer, seq, final=(layer == depth - 1))
    return h.reshape(bsz, seq, d)
```

```python
import functools

import jax
import jax.numpy as jnp
from jax import lax
from jax.experimental import pallas as pl
from jax.experimental.pallas import tpu as pltpu
from jax.experimental.pallas import tpu_sc as plsc

F32 = jnp.float32
BF16 = jnp.bfloat16
I32 = jnp.int32
HIGHEST = lax.Precision.HIGHEST

EPS = 1e-6
CONV_WIDTH = 4
LANES = 128
SUBLANES = 8
LRU_HEADS = 8
LRU_C = 8.0
ML_HEADS = 8
CHUNK = 128
PROJ_TILE = 512
PROJ_TILE_EVEN = 1024
LRU_TILE = 512
MOE_TILE = 512
COMBINE_TILE = 1024
DEST_TILE = 8192
SSD_HEAD_DIM = 64
SSD_GROUPS = 8
SSD_STATE = 128
N_EXPERTS = 32
TOP_K = 4
SWIGLU_ALPHA = 1.702
SWIGLU_LIMIT = 7.0
EXPERT_BLOCK = 1024
EXPERT_ROWS = 512
SEQ_PER_STEP = 2
CONV_COLS = 512
VMEM_LIMIT = 56 * 1024 * 1024


def _cparams(sem, **kw):
    return pltpu.CompilerParams(dimension_semantics=sem, vmem_limit_bytes=VMEM_LIMIT, **kw)


def _silu(x):
    half = 0.5 * x
    return half + half * jnp.tanh(half)


def _log_sigmoid(x):
    return jnp.minimum(x, 0.0) - jnp.log1p(jnp.exp(-jnp.abs(x)))


def _softplus(x):
    return jnp.maximum(x, 0.0) + jnp.log1p(jnp.exp(-jnp.abs(x)))


def _dot(a, b, **kw):
    return jnp.dot(a, b, preferred_element_type=F32, **kw)


def _dot_nt(a, b):
    return lax.dot_general(a, b, (((1,), (1,)), ((), ())), preferred_element_type=F32)


def _pack_pairs(x):
    w = x.shape[1] // 2
    lo = lax.bitcast_convert_type(x[:, :w].astype(BF16).astype(F32), I32)
    hi = lax.bitcast_convert_type(x[:, w:].astype(BF16).astype(F32), I32)
    return lax.shift_right_logical(lo, 16) | (hi & jnp.int32(-65536))


def _unpack_pairs(p):
    lo = lax.bitcast_convert_type(lax.shift_left(p, 16), F32)
    hi = lax.bitcast_convert_type(p & jnp.int32(-65536), F32)
    return jnp.concatenate([lo, hi], axis=1)


def _norm_mod(h, g, shift, scale):
    y = h * lax.rsqrt(jnp.mean(h * h, axis=-1, keepdims=True) + EPS)
    return (y * g) * (1.0 + scale) + shift


def _causal_conv(x, tail_ref, w_ref, b_ref, sl):
    t = x.shape[0]
    tail = tail_ref[:, sl]
    row8 = lax.broadcasted_iota(I32, tail.shape, 0)
    out = b_ref[:, sl] + x * w_ref[CONV_WIDTH - 1:CONV_WIDTH, sl]
    for k in range(1, CONV_WIDTH):
        xs = pltpu.roll(x, k, axis=0)
        first = jnp.where(row8 < k, pltpu.roll(tail, k, axis=0), xs[:SUBLANES])
        xs = jnp.concatenate([first, xs[SUBLANES:]], axis=0)
        out = out + xs * w_ref[CONV_WIDTH - 1 - k:CONV_WIDTH - k, sl]
    tail_ref[:, sl] = x[t - SUBLANES:]
    return out


def _shift_matrix(t):
    r = lax.broadcasted_iota(I32, ((CONV_WIDTH - 1) * t, t), 0)
    c = lax.broadcasted_iota(I32, ((CONV_WIDTH - 1) * t, t), 1)
    src = (r & (t - 1)) - lax.shift_right_logical(r, t.bit_length() - 1) - 1
    return (src == c).astype(BF16)


def _causal_conv_shifted(x, shifted, tail_ref, w_ref, b_ref, sl):
    t = x.shape[0]
    tail = tail_ref[:, sl]
    row8 = lax.broadcasted_iota(I32, tail.shape, 0)
    out = b_ref[:, sl] + x * w_ref[CONV_WIDTH - 1:CONV_WIDTH, sl]
    head = jnp.zeros_like(tail)
    for k in range(1, CONV_WIDTH):
        wk = w_ref[CONV_WIDTH - 1 - k:CONV_WIDTH - k, sl]
        out = out + shifted[(k - 1) * t:k * t] * wk
        head = head + jnp.where(row8 < k, pltpu.roll(tail, k, axis=0), 0.0) * wk
    tail_ref[:, sl] = x[t - SUBLANES:]
    return jnp.concatenate([out[:SUBLANES] + head, out[SUBLANES:]], axis=0)


def _mod_kernel(c_ref, w_ref, b_ref, o_ref):
    cond = _silu(c_ref[...])
    o_ref[0, 0] = _dot(cond, w_ref[0], precision=HIGHEST) + b_ref[0, 0]


def _modulation(c, mod_w, mod_b):
    depth, d, _ = mod_w.shape
    bsz = c.shape[0]
    out = pl.pallas_call(
        _mod_kernel,
        out_shape=jax.ShapeDtypeStruct((depth, 6, bsz, d), F32),
        grid=(depth, 6),
        in_specs=[pl.BlockSpec((bsz, d), lambda l, j: (0, 0)),
                  pl.BlockSpec((1, d, d), lambda l, j: (l, 0, j)),
                  pl.BlockSpec((1, 1, 1, d), lambda l, j: (l, j, 0, 0))],
        out_specs=pl.BlockSpec((1, 1, bsz, d), lambda l, j: (l, j, 0, 0)),
        compiler_params=_cparams(("parallel", "parallel")),
    )(c.astype(F32), mod_w, mod_b.reshape(depth, 6, 1, d))
    return out.reshape(depth, 6, bsz, 1, d)


def _inproj_kernel(h_ref, g_ref, sh_ref, sc_ref, w_ref, *rest, n_chunk, with_dt):
    if with_dt:
        wdt_ref, *o_refs, odt_ref = rest
    else:
        o_refs = rest
    u = _norm_mod(h_ref[...], g_ref[...], sh_ref[0], sc_ref[0]).astype(BF16)
    off = 0
    for o_ref in o_refs:
        for n0 in range(0, o_ref.shape[1], n_chunk):
            o_ref[:, n0:n0 + n_chunk] = _dot(u, w_ref[:, off + n0:off + n0 + n_chunk]).astype(o_ref.dtype)
        off += o_ref.shape[1]
    if with_dt:
        odt_ref[...] = _dot(u, wdt_ref[...])


def _inproj(h, g, shift, scale, w, wdt, splits, seq, tm):
    m, d = h.shape
    n = sum(splits)
    assert n <= w.shape[1] and n % LANES == 0
    tiles_per_seq = seq // tm
    bmap = lambda i: (i // tiles_per_seq, 0, 0)
    in_specs = [pl.BlockSpec((tm, d), lambda i: (i, 0)),
                pl.BlockSpec((1, d), lambda i: (0, 0)),
                pl.BlockSpec((1, 1, d), bmap),
                pl.BlockSpec((1, 1, d), bmap),
                pl.BlockSpec((d, n), lambda i: (0, 0), pipeline_mode=pl.Buffered(1))]
    out_shape = [jax.ShapeDtypeStruct((m, s), BF16) for s in splits]
    out_specs = [pl.BlockSpec((tm, s), lambda i: (i, 0)) for s in splits]
    args = [h, g.reshape(1, d), shift, scale, w]
    if wdt is not None:
        in_specs.append(pl.BlockSpec((d, LANES), lambda i: (0, 0)))
        out_shape.append(jax.ShapeDtypeStruct((m, LANES), F32))
        out_specs.append(pl.BlockSpec((tm, LANES), lambda i: (i, 0)))
        args.append(wdt)
    return pl.pallas_call(
        functools.partial(_inproj_kernel, n_chunk=1024, with_dt=wdt is not None),
        out_shape=out_shape, grid=(m // tm,), in_specs=in_specs, out_specs=out_specs,
        compiler_params=_cparams(("parallel",)),
    )(*args)


def _lru_kernel(xa_ref, ga_ref, cw_ref, cb_ref, wr_ref, br_ref, wi_ref, bi_ref, lam_ref,
                o_ref, tail_ref, hc_ref):
    @pl.when(pl.program_id(1) == 0)
    def _():
        tail_ref[...] = jnp.zeros_like(tail_ref)
        hc_ref[...] = jnp.zeros_like(hc_ref)

    t = xa_ref.shape[0]
    row_in_group = lax.broadcasted_iota(I32, (t, LANES), 0) % SUBLANES
    steps = [s for s in (1, 2, 4) if s < SUBLANES]
    masks = [row_in_group >= s for s in steps]
    for hh in range(LRU_HEADS):
        sl = slice(hh * LANES, (hh + 1) * LANES)
        xc = _causal_conv(xa_ref[:, sl].astype(F32), tail_ref, cw_ref, cb_ref, sl)
        xcb = xc.astype(BF16)
        r = jax.nn.sigmoid(_dot(xcb, wr_ref[hh]) + br_ref[:, sl])
        i = jax.nn.sigmoid(_dot(xcb, wi_ref[hh]) + bi_ref[:, sl])
        log_a = LRU_C * r * _log_sigmoid(lam_ref[:, sl])
        a = jnp.exp(log_a)
        th = jnp.tanh(log_a)
        n2 = -2.0 * th
        root = jnp.where(n2 > 0.0, n2 * lax.rsqrt(n2), 0.0)
        u = (root * lax.rsqrt(1.0 - th)) * (i * xc)
        def roll_in_groups(v, s):
            v3 = v.reshape(t // SUBLANES, SUBLANES, LANES)
            return pltpu.roll(v3, s, axis=1).reshape(t, LANES)

        for s, m in zip(steps, masks):
            u = jnp.where(m, u + a * roll_in_groups(u, s), u)
            a = jnp.where(m, a * roll_in_groups(a, s), a)
        carry = hc_ref[:, sl]
        groups = []
        for r0 in range(0, t, SUBLANES):
            hg = u[r0:r0 + SUBLANES] + a[r0:r0 + SUBLANES] * carry
            carry = hg[SUBLANES - 1:SUBLANES]
            groups.append(hg)
        hc_ref[:, sl] = carry
        h = jnp.concatenate(groups, axis=0)
        ga = ga_ref[:, sl].astype(F32)
        o_ref[:, sl] = (h * jax.nn.gelu(ga, approximate=True)).astype(o_ref.dtype)


def _lru(proj, p, bsz, seq, tm):
    m = proj.shape[0]
    w = LRU_HEADS * LANES
    nt = seq // tm
    vec = lambda: pl.BlockSpec((1, w), lambda b, j: (0, 0))
    return pl.pallas_call(
        _lru_kernel,
        out_shape=jax.ShapeDtypeStruct((m, w), BF16),
        grid=(bsz, nt),
        in_specs=[pl.BlockSpec((tm, w), lambda b, j: (b * nt + j, 0)),
                  pl.BlockSpec((tm, w), lambda b, j: (b * nt + j, 1)),
                  pl.BlockSpec((CONV_WIDTH, w), lambda b, j: (0, 0)), vec(),
                  pl.BlockSpec((LRU_HEADS, LANES, LANES), lambda b, j: (0, 0, 0)), vec(),
                  pl.BlockSpec((LRU_HEADS, LANES, LANES), lambda b, j: (0, 0, 0)), vec(), vec()],
        out_specs=pl.BlockSpec((tm, w), lambda b, j: (b * nt + j, 0)),
        scratch_shapes=[pltpu.VMEM((SUBLANES, w), F32), pltpu.VMEM((1, w), F32)],
        compiler_params=_cparams(("parallel", "arbitrary")),
    )(proj, proj, p["conv_w"], p["conv_b"], p["w_r"], p["b_r"], p["w_i"], p["b_i"], p["lam"])


def _mlstm_kernel(xb_ref, zb_ref, cw_ref, cb_ref, wq_ref, wk_ref, wv_ref, wg_ref, bg_ref,
                  nw_ref, sk_ref, o_ref, tail_ref, qkv_ref, xc_ref, caug_ref, m_ref):
    @pl.when(pl.program_id(1) == 0)
    def _():
        tail_ref[...] = jnp.zeros_like(tail_ref)
        caug_ref[...] = jnp.zeros_like(caug_ref)
        m_ref[...] = jnp.full(m_ref.shape, -jnp.inf, F32)

    for s in range(xb_ref.shape[0]):
        for c0 in range(0, xb_ref.shape[1], CHUNK):
            rows = pl.ds(c0, CHUNK)
            _mlstm_chunk(xb_ref.at[s, rows], zb_ref.at[s, rows], cw_ref, cb_ref, wq_ref, wk_ref, wv_ref,
                         wg_ref, bg_ref, nw_ref, sk_ref, o_ref.at[s, rows], tail_ref.at[s], qkv_ref.at[s],
                         xc_ref.at[s], caug_ref.at[s], m_ref.at[s])


def _mlstm_chunk(xb_ref, zb_ref, cw_ref, cb_ref, wq_ref, wk_ref, wv_ref, wg_ref, bg_ref,
                 nw_ref, sk_ref, o_ref, tail_ref, qkv_ref, xc_ref, caug_ref, m_ref):
    L = CHUNK
    width = ML_HEADS * LANES
    scale = LANES ** -0.5
    for hh in range(ML_HEADS):
        sl = slice(hh * LANES, (hh + 1) * LANES)
        xb = xb_ref[:, sl].astype(F32)
        xc = _silu(_causal_conv(xb, tail_ref, cw_ref, cb_ref, sl))
        xc_ref[:, sl] = xc
        xcb = xc.astype(BF16)
        qkv_ref[:, sl] = _dot(xcb, wq_ref[hh]).astype(BF16)
        qkv_ref[:, width + hh * LANES:width + (hh + 1) * LANES] = _dot(xcb, wk_ref[hh]).astype(BF16)
        qkv_ref[:, 2 * width + hh * LANES:2 * width + (hh + 1) * LANES] = (
            _dot(xb.astype(BF16), wv_ref[hh]).astype(BF16))

    gates = _dot(qkv_ref[...], wg_ref[...]) + bg_ref[...]
    rowi = lax.broadcasted_iota(I32, (L, L), 0)
    coli = lax.broadcasted_iota(I32, (L, L), 1)
    causal = rowi >= coli
    lf = jnp.where((coli >= ML_HEADS) & (coli < 2 * ML_HEADS), _log_sigmoid(gates), 0.0)
    tri = causal.astype(BF16)
    lf_hi = lf.astype(BF16)
    lf_mid = (lf - lf_hi.astype(F32)).astype(BF16)
    lf_lo = (lf - lf_hi.astype(F32) - lf_mid.astype(F32)).astype(BF16)
    gcum = _dot(tri, lf_hi) + (_dot(tri, lf_mid) + _dot(tri, lf_lo))
    x_col = jnp.where(coli < ML_HEADS, gates, gcum)
    x_row = x_col.T
    ones = jnp.ones((L, LANES), BF16)
    heads = range(ML_HEADS)
    hsl = [slice(hh * LANES, (hh + 1) * LANES) for hh in heads]

    qs = [qkv_ref[:, hsl[hh]] for hh in heads]
    ks = [qkv_ref[:, width + hh * LANES:width + (hh + 1) * LANES] for hh in heads]
    vaugs = [jnp.concatenate([qkv_ref[:, 2 * width + hh * LANES:2 * width + (hh + 1) * LANES], ones], axis=1)
             for hh in heads]
    scores = [_dot_nt(qs[hh], ks[hh]) * scale for hh in heads]
    ics = [jnp.broadcast_to(x_col[:, hh:hh + 1], (L, LANES)) for hh in heads]
    gcs = [jnp.broadcast_to(x_col[:, ML_HEADS + hh:ML_HEADS + hh + 1], (L, LANES)) for hh in heads]
    irs = [x_row[hh:hh + 1, :] for hh in heads]
    grs = [x_row[ML_HEADS + hh:ML_HEADS + hh + 1, :] for hh in heads]
    mps = [m_ref[hh] for hh in heads]
    dmats = [jnp.where(causal, gcs[hh] - grs[hh] + irs[hh], -jnp.inf) for hh in heads]
    m_inters = [mps[hh] + gcs[hh] for hh in heads]
    m_ts = [jnp.maximum(m_inters[hh], jnp.max(dmats[hh], axis=1, keepdims=True)) for hh in heads]
    qks = [(scores[hh] * jnp.exp(dmats[hh] - m_ts[hh])).astype(BF16) for hh in heads]
    caugs = [caug_ref[hh] for hh in heads]
    w_inters = [jnp.exp(m_inters[hh] - m_ts[hh]) for hh in heads]
    nds = [_dot(qks[hh], vaugs[hh])
           + jnp.concatenate([w_inters[hh], w_inters[hh]], axis=1) * _dot(qs[hh], caugs[hh].astype(BF16))
           for hh in heads]

    g_lasts = [gcs[hh][L - 1:L, :] for hh in heads]
    m_news = [jnp.maximum(mps[hh] + g_lasts[hh],
                          jnp.max(g_lasts[hh] - grs[hh] + irs[hh], axis=1, keepdims=True)) for hh in heads]
    for hh in heads:
        ws = jnp.exp(g_lasts[hh] - gcs[hh] + ics[hh] - m_news[hh])
        wc = jnp.exp(mps[hh] + g_lasts[hh] - m_news[hh])
        kw_t = (ks[hh].astype(F32) * (ws * scale)).T.astype(BF16)
        caug_ref[hh] = jnp.concatenate([wc, wc], axis=1) * caugs[hh] + _dot(kw_t, vaugs[hh])
        m_ref[hh] = m_news[hh]

    hvals = [nds[hh][:, :LANES] / jnp.maximum(jnp.abs(nds[hh][:, LANES:]), jnp.exp(-m_ts[hh]))
             for hh in heads]
    mus = [jnp.mean(hvals[hh], axis=1, keepdims=True) for hh in heads]
    dvs = [hvals[hh] - mus[hh] for hh in heads]
    variances = [jnp.mean(dvs[hh] * dvs[hh], axis=1, keepdims=True) for hh in heads]
    for hh in heads:
        sl = hsl[hh]
        hn = dvs[hh] * lax.rsqrt(variances[hh] + EPS) * nw_ref[:, sl]
        zb = zb_ref[:, sl].astype(F32)
        o_ref[:, sl] = ((hn + sk_ref[:, sl] * xc_ref[:, sl]) * _silu(zb)).astype(o_ref.dtype)


def _mlstm(proj, p, bsz, seq):
    m = proj.shape[0]
    w = ML_HEADS * LANES
    sps = 1
    cps = next(c for c in (4, 2, 1) if seq % (c * CHUNK) == 0)
    nt = seq // (cps * CHUNK)
    vec = lambda: pl.BlockSpec((1, w), lambda b, j: (0, 0))
    blk = lambda: pl.BlockSpec((ML_HEADS, LANES, LANES), lambda b, j: (0, 0, 0))
    tile = lambda col: pl.BlockSpec((sps, cps * CHUNK, w), lambda b, j: (b, j, col))
    proj3 = proj.reshape(bsz, seq, proj.shape[1])
    out = pl.pallas_call(
        _mlstm_kernel,
        out_shape=jax.ShapeDtypeStruct((bsz, seq, w), BF16),
        grid=(bsz // sps, nt),
        in_specs=[tile(2), tile(3),
                  pl.BlockSpec((CONV_WIDTH, w), lambda b, j: (0, 0)), vec(),
                  blk(), blk(), blk(),
                  pl.BlockSpec((3 * w, LANES), lambda b, j: (0, 0)),
                  pl.BlockSpec((1, LANES), lambda b, j: (0, 0)),
                  vec(), vec()],
        out_specs=tile(0),
        scratch_shapes=[pltpu.VMEM((sps, SUBLANES, w), F32),
                        pltpu.VMEM((sps, CHUNK, 3 * w), BF16),
                        pltpu.VMEM((sps, CHUNK, w), F32),
                        pltpu.VMEM((sps, ML_HEADS, LANES, 2 * LANES), F32),
                        pltpu.VMEM((sps, ML_HEADS, 1, LANES), F32)],
        compiler_params=_cparams(("parallel", "arbitrary")),
    )(proj3, proj3, p["conv_w"], p["conv_b"], p["w_q"], p["w_k"], p["w_v"], p["w_g"], p["b_g"],
      p["norm"], p["skip"])
    return out.reshape(m, w)


def _ssd_kernel(z_ref, xbc_ref, dt_ref, cw_ref, cb_ref, dtb_ref, alog_ref, dsk_ref, nw_ref,
                o_ref, tail_ref, act_ref, st_ref):
    @pl.when(pl.program_id(1) == 0)
    def _():
        tail_ref[...] = jnp.zeros_like(tail_ref)
        st_ref[...] = jnp.zeros_like(st_ref)

    for c0 in range(0, z_ref.shape[1], CHUNK):
        rows = pl.ds(c0, CHUNK)
        for s in range(z_ref.shape[0]):
            _ssd_chunk(z_ref.at[s, rows], xbc_ref.at[s, rows], dt_ref.at[s, rows], cw_ref, cb_ref, dtb_ref,
                       alog_ref, dsk_ref, nw_ref, o_ref.at[s, rows], tail_ref.at[s], act_ref.at[s],
                       st_ref.at[s])


def _ssd_chunk(z_ref, xbc_ref, dt_ref, cw_ref, cb_ref, dtb_ref, alog_ref, dsk_ref, nw_ref,
               o_ref, tail_ref, act_ref, st_ref):
    L = CHUNK
    inner = o_ref.shape[1]
    gw = inner // SSD_GROUPS
    hpg = gw // SSD_HEAD_DIM
    b_off = inner
    c_off = inner + SSD_GROUPS * SSD_STATE
    shift = _shift_matrix(L)
    for c0 in range(0, xbc_ref.shape[1], CONV_COLS):
        shifted = _dot(shift, xbc_ref[:, c0:c0 + CONV_COLS])
        for l0 in range(0, CONV_COLS, LANES):
            sl = slice(c0 + l0, c0 + l0 + LANES)
            act_ref[:, sl] = _silu(_causal_conv_shifted(
                xbc_ref[:, sl].astype(F32), shifted[:, l0:l0 + LANES], tail_ref, cw_ref, cb_ref, sl))

    rowi = lax.broadcasted_iota(I32, (L, L), 0)
    coli = lax.broadcasted_iota(I32, (L, L), 1)
    causal = rowi >= coli
    dt = _softplus(dt_ref[...] + dtb_ref[...])
    da = dt * (-jnp.exp(alog_ref[...]))
    tri = causal.astype(BF16)
    da_hi = da.astype(BF16)
    da_mid = (da - da_hi.astype(F32)).astype(BF16)
    da_lo = (da - da_hi.astype(F32) - da_mid.astype(F32)).astype(BF16)
    a = _dot(tri, da_hi) + (_dot(tri, da_mid) + _dot(tri, da_lo))
    a_t = a.T
    hpt = LANES // SSD_HEAD_DIM
    lane = lax.broadcasted_iota(I32, (L, LANES), 1)

    def over_heads(tiles):
        cols = []
        for c0 in range(0, hpg, hpt):
            out = tiles[c0 + hpt - 1]
            for j in range(hpt - 2, -1, -1):
                out = jnp.where(lane < (j + 1) * SSD_HEAD_DIM, tiles[c0 + j], out)
            cols.append(out)
        return jnp.concatenate(cols, axis=1)

    for g in range(SSD_GROUPS):
        gsl = slice(g * gw, (g + 1) * gw)
        xg = act_ref[:, gsl]
        bg = act_ref[:, b_off + g * SSD_STATE:b_off + (g + 1) * SSD_STATE]
        cg_ = act_ref[:, c_off + g * SSD_STATE:c_off + (g + 1) * SSD_STATE].astype(BF16)
        cb = _dot_nt(cg_, bg.astype(BF16))
        state = st_ref[g]
        hds = [g * hpg + jj for jj in range(hpg)]
        a_bs = [jnp.broadcast_to(a[:, hd:hd + 1], (L, LANES)) for hd in hds]
        dt_bs = [jnp.broadcast_to(dt[:, hd:hd + 1], (L, LANES)) for hd in hds]
        ea_x = over_heads([jnp.exp(a_b) for a_b in a_bs])
        to_end_x = over_heads([jnp.exp(a_b[L - 1:L, :] - a_b) for a_b in a_bs])
        xdt = xg * over_heads(dt_bs)
        inter = _dot(cg_, state.astype(BF16)) * ea_x
        acc = [inter[:, c0:c0 + LANES] for c0 in range(0, gw, LANES)]
        for jj in range(hpg):
            seg = jnp.where(causal, a_bs[jj] - a_t[hds[jj]:hds[jj] + 1, :], -jnp.inf)
            w = (cb * jnp.exp(seg)).astype(BF16)
            c, j = divmod(jj, hpt)
            in_head = (lane >= j * SSD_HEAD_DIM) & (lane < (j + 1) * SSD_HEAD_DIM)
            x_head = jnp.where(in_head, xdt[:, c * LANES:(c + 1) * LANES], 0.0).astype(BF16)
            acc[c] = acc[c] + _dot(w, x_head)
        acc = jnp.concatenate(acc, axis=1)
        y = (acc + dsk_ref[:, gsl] * xg) * _silu(z_ref[:, gsl].astype(F32))
        y = y * lax.rsqrt(jnp.mean(y * y, axis=1, keepdims=True) + EPS) * nw_ref[:, gsl]
        o_ref[:, gsl] = y.astype(o_ref.dtype)
        xw = (xdt * to_end_x).astype(BF16)
        st_ref[g] = ea_x[L - 1:L, :] * state + _dot(bg.T.astype(BF16), xw)


def _ssd(z, xbc, dt_raw, p, bsz, seq):
    m, inner = z.shape
    conv_ch = xbc.shape[1]
    sps = SEQ_PER_STEP if bsz % SEQ_PER_STEP == 0 else 1
    cps = 2 if seq % (2 * CHUNK) == 0 else 1
    nt = seq // (cps * CHUNK)
    vec = lambda n: pl.BlockSpec((1, n), lambda b, j: (0, 0))
    tile = lambda n: pl.BlockSpec((sps, cps * CHUNK, n), lambda b, j: (b, j, 0))
    out = pl.pallas_call(
        _ssd_kernel,
        out_shape=jax.ShapeDtypeStruct((bsz, seq, inner), BF16),
        grid=(bsz // sps, nt),
        in_specs=[tile(inner), tile(conv_ch), tile(LANES),
                  pl.BlockSpec((CONV_WIDTH, conv_ch), lambda b, j: (0, 0)), vec(conv_ch),
                  vec(LANES), vec(LANES), vec(inner), vec(inner)],
        out_specs=tile(inner),
        scratch_shapes=[pltpu.VMEM((sps, SUBLANES, conv_ch), F32),
                        pltpu.VMEM((sps, CHUNK, conv_ch), F32),
                        pltpu.VMEM((sps, SSD_GROUPS, SSD_STATE, inner // SSD_GROUPS), F32)],
        compiler_params=_cparams(("parallel", "arbitrary")),
    )(z.reshape(bsz, seq, inner), xbc.reshape(bsz, seq, conv_ch), dt_raw.reshape(bsz, seq, LANES),
      p["conv_w"], p["conv_b"], p["dt_bias"], p["a_log"], p["d_skip"], p["norm"])
    return out.reshape(m, inner)


def _router_kernel(*refs, n_in):
    y_refs, w_refs = refs[:n_in], refs[n_in:2 * n_in]
    (h_ref, gm_ref, g_ref, sh_ref, sc_ref, wr_ref, br_ref,
     hmid_ref, up_ref, topi_ref, gate_ref, rank_ref, cnt_ref, carry_ref) = refs[2 * n_in:]

    @pl.when(pl.program_id(0) == 0)
    def _():
        carry_ref[...] = jnp.zeros_like(carry_ref)

    tm = h_ref.shape[0]
    acc = _dot(y_refs[0][...], w_refs[0][...])
    for y_ref, w_ref in zip(y_refs[1:], w_refs[1:]):
        acc = acc + _dot(y_ref[...], w_ref[...])
    hmid = h_ref[...] + gm_ref[0] * acc
    hmid_ref[...] = hmid
    u = _norm_mod(hmid, g_ref[...], sh_ref[0], sc_ref[0])
    up_ref[...] = _pack_pairs(u)
    u_hi = u.astype(BF16)
    u_lo = (u - u_hi.astype(F32)).astype(BF16)
    hi_both = _dot(u_hi, wr_ref[...])
    logits = (hi_both[:, :LANES] + (_dot(u_lo, wr_ref[:, :LANES]) + hi_both[:, LANES:])
              + br_ref[...])
    lt = jnp.concatenate([logits[r0:r0 + LANES].T for r0 in range(0, tm, LANES)], axis=1)
    l = lt[:N_EXPERTS]
    e_iota = lax.broadcasted_iota(I32, (N_EXPERTS, tm), 0).astype(F32)
    vals, idxs, hots = [], [], []
    for _ in range(TOP_K):
        mx = jnp.max(l, axis=0, keepdims=True)
        idx = jnp.min(jnp.where(l == mx, e_iota, float(N_EXPERTS)), axis=0, keepdims=True)
        hot = e_iota == idx
        l = jnp.where(hot, -jnp.inf, l)
        vals.append(mx)
        idxs.append(idx)
        hots.append(hot)
    exps = [jnp.exp(v - vals[0]) for v in vals]
    den = exps[0] + exps[1] + exps[2] + exps[3]
    gate_ref[...] = jnp.concatenate([e / den for e in exps], axis=0)
    topi_ref[...] = jnp.concatenate(idxs, axis=0).astype(I32)

    sel = jnp.zeros((N_EXPERTS, tm), F32)
    for hot in hots:
        sel = jnp.where(hot, 1.0, sel)
    r_i = lax.broadcasted_iota(I32, (tm, tm), 0)
    c_i = lax.broadcasted_iota(I32, (tm, tm), 1)
    before = (r_i < c_i).astype(BF16)
    carry = carry_ref[:, 0:1]
    cum = _dot(sel.astype(BF16), before) + carry
    rank_ref[...] = jnp.concatenate(
        [jnp.sum(jnp.where(hot, cum, 0.0), axis=0, keepdims=True) for hot in hots], axis=0).astype(I32)
    total = carry + jnp.sum(sel, axis=1, keepdims=True)
    carry_ref[...] = jnp.broadcast_to(total, carry_ref.shape)
    cnt_ref[...] = jnp.broadcast_to(total, cnt_ref.shape)


def _router(ys, w_out, h, g_m, g, shift, scale, wr, br, seq, tm):
    m, d = h.shape
    tiles_per_seq = seq // tm
    bmap = lambda i: (i // tiles_per_seq, 0, 0)
    row4 = lambda: pl.BlockSpec((TOP_K, tm), lambda i: (0, i))
    in_specs, args, k0 = [], [], 0
    for y in ys:
        in_specs.append(pl.BlockSpec((tm, y.shape[1]), lambda i: (i, 0)))
        args.append(y)
    for y in ys:
        kk = y.shape[1]
        in_specs.append(pl.BlockSpec((kk, d), lambda i, kb=k0 // kk: (kb, 0)))
        args.append(w_out)
        k0 += kk
    in_specs += [pl.BlockSpec((tm, d), lambda i: (i, 0)),
                 pl.BlockSpec((1, 1, d), bmap),
                 pl.BlockSpec((1, d), lambda i: (0, 0)),
                 pl.BlockSpec((1, 1, d), bmap), pl.BlockSpec((1, 1, d), bmap),
                 pl.BlockSpec((d, 2 * LANES), lambda i: (0, 0)),
                 pl.BlockSpec((1, LANES), lambda i: (0, 0))]
    args += [h, g_m, g.reshape(1, d), shift, scale, wr, br]
    return pl.pallas_call(
        functools.partial(_router_kernel, n_in=len(ys)),
        out_shape=[jax.ShapeDtypeStruct((m, d), F32),
                   jax.ShapeDtypeStruct((m, d // 2), I32),
                   jax.ShapeDtypeStruct((TOP_K, m), I32),
                   jax.ShapeDtypeStruct((TOP_K, m), F32),
                   jax.ShapeDtypeStruct((TOP_K, m), I32),
                   jax.ShapeDtypeStruct((N_EXPERTS, LANES), F32)],
        grid=(m // tm,),
        in_specs=in_specs,
        out_specs=[pl.BlockSpec((tm, d), lambda i: (i, 0)),
                   pl.BlockSpec((tm, d // 2), lambda i: (i, 0)), row4(), row4(), row4(),
                   pl.BlockSpec((N_EXPERTS, LANES), lambda i: (0, 0))],
        scratch_shapes=[pltpu.VMEM((N_EXPERTS, LANES), F32)],
        compiler_params=_cparams(("arbitrary",)),
    )(*args)


def _dest_kernel(ps_ref, topi_ref, rank_ref, o_ref):
    topi = topi_ref[...]
    acc = rank_ref[...]
    for e in range(N_EXPERTS):
        acc = acc + jnp.where(topi == e, ps_ref[e], 0)
    o_ref[...] = acc


def _dest_rows(pad_start, topi, rank, tw):
    k, m = topi.shape
    blk = lambda: pl.BlockSpec((k, tw), lambda i, ps: (0, i))
    return pl.pallas_call(
        _dest_kernel,
        out_shape=jax.ShapeDtypeStruct((k, m), I32),
        grid_spec=pltpu.PrefetchScalarGridSpec(
            num_scalar_prefetch=1, grid=(m // tw,), in_specs=[blk(), blk()], out_specs=blk()),
        compiler_params=_cparams(("parallel",)),
    )(pad_start, topi, rank)


SC_CORES = 2
SC_SUBCORES = 16
SC_ROWS = 128


def _sc_gather_rows(table, idx):
    b = idx.shape[0]
    w = table.shape[1]
    workers = SC_CORES * SC_SUBCORES
    per_w = b // workers
    assert per_w * workers == b and per_w % SC_ROWS == 0
    mesh = plsc.VectorSubcoreMesh(core_axis_name="c", subcore_axis_name="s")

    @functools.partial(
        pl.kernel, mesh=mesh, out_type=jax.ShapeDtypeStruct((b, w), I32),
        scratch_types=[pltpu.VMEM((SC_ROWS,), I32), pltpu.VMEM((SC_ROWS, w), I32),
                       pltpu.SemaphoreType.DMA])
    def gather(table_hbm, idx_hbm, out_hbm, idx_v, rows_v, sem):
        base = (lax.axis_index("s") * SC_CORES + lax.axis_index("c")) * per_w

        @pl.loop(0, per_w // SC_ROWS)
        def _(c):
            off = base + c * SC_ROWS
            pltpu.sync_copy(idx_hbm.at[pl.ds(off, SC_ROWS)], idx_v)
            pltpu.async_copy(table_hbm.at[idx_v], rows_v, sem).wait()
            pltpu.sync_copy(rows_v, out_hbm.at[pl.ds(off, SC_ROWS)])

    return gather(table, idx)


def _sc_scatter_rows(rows, dest, n_rows):
    m, w = rows.shape
    kk = dest.shape[0]
    workers = SC_CORES * SC_SUBCORES
    per_w = m // workers
    assert per_w * workers == m and per_w % SC_ROWS == 0
    mesh = plsc.VectorSubcoreMesh(core_axis_name="c", subcore_axis_name="s")

    @functools.partial(
        pl.kernel, mesh=mesh, out_type=jax.ShapeDtypeStruct((n_rows, w), I32),
        scratch_types=[pltpu.VMEM((SC_ROWS,), I32), pltpu.VMEM((SC_ROWS, w), I32),
                       pltpu.SemaphoreType.DMA])
    def scatter(rows_hbm, dest_hbm, out_hbm, idx_v, rows_v, sem):
        base = (lax.axis_index("s") * SC_CORES + lax.axis_index("c")) * per_w

        @pl.loop(0, per_w // SC_ROWS)
        def _(c):
            off = base + c * SC_ROWS
            pltpu.sync_copy(rows_hbm.at[pl.ds(off, SC_ROWS)], rows_v)
            for k in range(kk):
                pltpu.sync_copy(dest_hbm.at[pl.ds(k * m + off, SC_ROWS)], idx_v)
                pltpu.async_copy(rows_v, out_hbm.at[idx_v], sem).wait()

    return scatter(rows, dest.reshape(-1))


def _combine_dense_kernel(y_ref, gate_ref, h_ref, gf_ref, fn_ref, o_ref, *, final):
    acc = gate_ref[:, 0:1] * _unpack_pairs(y_ref[0])
    for k in range(1, TOP_K):
        acc = acc + gate_ref[:, k:k + 1] * _unpack_pairs(y_ref[k])
    hn = h_ref[...] + gf_ref[0] * acc
    if final:
        hn = hn * lax.rsqrt(jnp.mean(hn * hn, axis=-1, keepdims=True) + EPS) * fn_ref[...]
    o_ref[...] = hn


def _combine_dense(y4, gates_col, h, gf, fnorm, seq, tm, final):
    m, d = h.shape
    tiles_per_seq = seq // tm
    return pl.pallas_call(
        functools.partial(_combine_dense_kernel, final=final),
        out_shape=jax.ShapeDtypeStruct((m, d), F32),
        grid=(m // tm,),
        in_specs=[pl.BlockSpec((TOP_K, tm, d // 2), lambda i: (0, i, 0)),
                  pl.BlockSpec((tm, TOP_K), lambda i: (i, 0)),
                  pl.BlockSpec((tm, d), lambda i: (i, 0)),
                  pl.BlockSpec((1, 1, d), lambda i: (i // tiles_per_seq, 0, 0)),
                  pl.BlockSpec((1, d), lambda i: (0, 0))],
        out_specs=pl.BlockSpec((tm, d), lambda i: (i, 0)),
        compiler_params=_cparams(("parallel",)),
    )(y4, gates_col, h, gf, fnorm.reshape(1, d))


def _expert_kernel(be_ref, nb_ref, first_ref, nv_ref, x_ref, wgu_ref, bgu_ref, wd_ref, bd_ref, y_ref,
                   wgu_bf, wd_bf):
    i = pl.program_id(0)

    @pl.when(i < nb_ref[0])
    def _():
        dff = wd_bf.shape[0]

        @pl.when(first_ref[i] == 1)
        def _():
            rows = 64

            def cast(r, c):
                r0 = pl.multiple_of(r * rows, rows)
                wgu_bf[pl.ds(r0, rows), :] = wgu_ref[0, 0, pl.ds(r0, rows), :].astype(BF16)
                wd_bf[pl.ds(r0, rows), :] = wd_ref[0, 0, pl.ds(r0, rows), :].astype(BF16)
                return c

            lax.fori_loop(0, dff // rows, cast, 0)

        def ffn(r0):
            x = _unpack_pairs(x_ref[r0:r0 + EXPERT_ROWS, :]).astype(BF16)
            hb = _dot(x, wgu_bf[...]) + bgu_ref[0, 0]
            h_glu = jnp.minimum(hb[:, :dff], SWIGLU_LIMIT)
            h_lin = jnp.clip(hb[:, dff:], -SWIGLU_LIMIT, SWIGLU_LIMIT)
            half = 0.5 * h_glu
            act = (half + half * jnp.tanh(SWIGLU_ALPHA * half)) * (h_lin + 1.0)
            y_ref[r0:r0 + EXPERT_ROWS, :] = _pack_pairs(_dot(act.astype(BF16), wd_bf[...]) + bd_ref[0, 0])

        ffn(0)
        for r0 in range(EXPERT_ROWS, x_ref.shape[0], EXPERT_ROWS):
            pl.when(nv_ref[i] > r0)(functools.partial(ffn, r0))


def _experts(block_e, n_used, first, n_valid, xs, wgu, bgu, wd, bd, layer):
    n_rows, wp = xs.shape
    _, ne, d, ff2 = wgu.shape
    assert d == ff2 // 2
    nblk = n_rows // EXPERT_BLOCK

    def xmap(i, be, nb, fi, nv):
        return (jnp.minimum(i, nb[0] - 1), 0)

    emap = lambda i, be, nb, fi, nv: (layer, be[i], 0, 0)
    grid_spec = pltpu.PrefetchScalarGridSpec(
        num_scalar_prefetch=4, grid=(nblk,),
        in_specs=[pl.BlockSpec((EXPERT_BLOCK, wp), xmap),
                  pl.BlockSpec((1, 1, d, ff2), emap), pl.BlockSpec((1, 1, 1, ff2), emap),
                  pl.BlockSpec((1, 1, ff2 // 2, d), emap), pl.BlockSpec((1, 1, 1, d), emap)],
        out_specs=pl.BlockSpec((EXPERT_BLOCK, wp), xmap),
        scratch_shapes=[pltpu.VMEM((d, ff2), BF16), pltpu.VMEM((ff2 // 2, d), BF16)])
    depth = wgu.shape[0]
    return pl.pallas_call(
        _expert_kernel,
        out_shape=jax.ShapeDtypeStruct((n_rows, wp), I32),
        grid_spec=grid_spec,
        compiler_params=_cparams(("arbitrary",)),
    )(block_e, n_used, first, n_valid, xs, wgu, bgu.reshape(depth, ne, 1, ff2), wd,
      bd.reshape(depth, ne, 1, d))


def _moe(ys, w_out, g_m, h, g, shift, scale, gf, fnorm, wr, br, wgu, bgu, wd, bd, layer, seq, final):
    m, d = h.shape
    tm = MOE_TILE
    wr_p = jnp.zeros((d, LANES), F32).at[:, :N_EXPERTS].set(wr)
    br_p = jnp.zeros((1, LANES), F32).at[0, :N_EXPERTS].set(br)
    wr_hi = wr_p.astype(BF16)
    wr_split = jnp.concatenate([wr_hi, (wr_p - wr_hi.astype(F32)).astype(BF16)], axis=1)
    h, up, topi, gates, rank, cnt = _router(ys, w_out, h, g_m, g, shift, scale, wr_split, br_p, seq, tm)

    counts = cnt[:, 0].astype(I32)
    padded = (counts + EXPERT_BLOCK - 1) // EXPERT_BLOCK * EXPERT_BLOCK
    pad_end = jnp.cumsum(padded)
    pad_start = pad_end - padded
    nblk = m * TOP_K // EXPERT_BLOCK + N_EXPERTS
    n_rows = nblk * EXPERT_BLOCK
    n_used = pad_end[-1:] // EXPERT_BLOCK
    blk = jnp.arange(nblk, dtype=I32)
    blk_c = jnp.minimum(blk, n_used - 1)
    block_e = jnp.minimum(jnp.sum(blk_c[:, None] * EXPERT_BLOCK >= pad_end[None, :], axis=1),
                          N_EXPERTS - 1).astype(I32)
    first = jnp.concatenate([jnp.ones((1,), I32), (block_e[1:] != block_e[:-1]).astype(I32)])
    real_end = jnp.sum(jnp.where(block_e[:, None] == jnp.arange(N_EXPERTS, dtype=I32)[None, :],
                                 (pad_start + counts)[None, :], 0), axis=1)
    n_valid = jnp.clip(real_end - blk_c * EXPERT_BLOCK, 0, EXPERT_BLOCK).astype(I32)

    dest = _dest_rows(pad_start, topi, rank, min(m, DEST_TILE))
    xs = _sc_scatter_rows(up, dest, n_rows)
    y = _experts(block_e, n_used.astype(I32), first, n_valid, xs, wgu, bgu, wd, bd, layer)
    y4 = _sc_gather_rows(y, dest.reshape(-1)).reshape(TOP_K, m, d // 2)
    return _combine_dense(y4, gates.T, h, gf, fnorm, seq, min(seq, COMBINE_TILE), final)


def _block_diag(w, group):
    nb, b, _ = w.shape
    per = group // b
    wg = w.reshape(nb // per, per, b, b)
    dense = jnp.einsum("gnde,nm->gndme", wg, jnp.eye(per, dtype=w.dtype))
    return dense.reshape(nb // per, group, group)


def kernel(x, c, mod_w, mod_b, norm_mix, norm_ffn, ev_w_in, ev_lru_conv_w, ev_lru_conv_b, ev_lru_w_r, ev_lru_b_r, ev_lru_w_i, ev_lru_b_i, ev_lru_lambda, ev_ml_conv_w, ev_ml_conv_b, ev_ml_w_q, ev_ml_w_k, ev_ml_w_v, ev_ml_w_ig, ev_ml_b_ig, ev_ml_w_fg, ev_ml_b_fg, ev_ml_norm, ev_ml_skip, ev_w_out, od_w_in, od_conv_w, od_conv_b, od_dt_bias, od_a_log, od_d, od_norm, od_w_out, moe_router_w, moe_router_b, moe_w_gu, moe_b_gu, moe_w_down, moe_b_down, final_norm):
    bsz, seq, d = x.shape
    depth = mod_w.shape[0]
    m = bsz * seq
    mod = _modulation(c, mod_w, mod_b)
    h = x.reshape(m, d).astype(F32)
    for layer in range(depth):
        sh_m, sc_m, g_m, sh_f, sc_f, g_f = (mod[layer, i] for i in range(6))
        j = layer // 2
        if layer % 2 == 0:
            w = ev_lru_lambda.shape[1]
            w_in = ev_w_in[j].astype(BF16)
            proj = _inproj(h, norm_mix[layer], sh_m, sc_m, w_in, None, [w_in.shape[1]], seq,
                           min(seq, PROJ_TILE_EVEN))[0]
            lru_p = dict(conv_w=ev_lru_conv_w[j], conv_b=ev_lru_conv_b[j].reshape(1, w),
                         w_r=ev_lru_w_r[j].astype(BF16), b_r=ev_lru_b_r[j].reshape(1, w),
                         w_i=ev_lru_w_i[j].astype(BF16), b_i=ev_lru_b_i[j].reshape(1, w),
                         lam=ev_lru_lambda[j].reshape(1, w))
            ya = _lru(proj, lru_p, bsz, seq, LRU_TILE)
            wg = jnp.zeros((3 * w, LANES), F32)
            wg = wg.at[:, :ML_HEADS].set(ev_ml_w_ig[j]).at[:, ML_HEADS:2 * ML_HEADS].set(ev_ml_w_fg[j])
            bg = jnp.zeros((1, LANES), F32)
            bg = bg.at[0, :ML_HEADS].set(ev_ml_b_ig[j]).at[0, ML_HEADS:2 * ML_HEADS].set(ev_ml_b_fg[j])
            ml_p = dict(conv_w=ev_ml_conv_w[j], conv_b=ev_ml_conv_b[j].reshape(1, w),
                        w_q=_block_diag(ev_ml_w_q[j], LANES).astype(BF16),
                        w_k=_block_diag(ev_ml_w_k[j], LANES).astype(BF16),
                        w_v=_block_diag(ev_ml_w_v[j], LANES).astype(BF16),
                        w_g=wg.astype(BF16), b_g=bg,
                        norm=ev_ml_norm[j].reshape(1, w), skip=ev_ml_skip[j].reshape(1, w))
            yb = _mlstm(proj, ml_p, bsz, seq)
            ys, w_out = [ya, yb], ev_w_out[j].astype(BF16)
        else:
            inner = od_norm.shape[1]
            heads = od_dt_bias.shape[1]
            conv_ch = od_conv_w.shape[2]
            w_in = od_w_in[j]
            wdt = jnp.zeros((d, LANES), F32).at[:, :heads].set(w_in[:, inner + conv_ch:])
            z, xbc, dt_raw = _inproj(h, norm_mix[layer], sh_m, sc_m, w_in.astype(BF16),
                                     wdt.astype(BF16), [inner, conv_ch], seq, PROJ_TILE)
            pad = lambda v: jnp.zeros((1, LANES), F32).at[0, :heads].set(v)
            ssd_p = dict(conv_w=od_conv_w[j], conv_b=od_conv_b[j].reshape(1, conv_ch),
                         dt_bias=pad(od_dt_bias[j]), a_log=pad(od_a_log[j]),
                         d_skip=jnp.repeat(od_d[j], SSD_HEAD_DIM).reshape(1, inner),
                         norm=od_norm[j].reshape(1, inner))
            y = _ssd(z, xbc, dt_raw, ssd_p, bsz, seq)
            ys, w_out = [y], od_w_out[j].astype(BF16)
        h = _moe(ys, w_out, g_m, h, norm_ffn[layer], sh_f, sc_f, g_f, final_norm,
                 moe_router_w[layer], moe_router_b[layer],
                 moe_w_gu, moe_b_gu, moe_w_down, moe_b_down, layer, seq, final=(layer == depth - 1))
    return h.reshape(bsz, seq, d)
```

```python
import functools

import jax
import jax.numpy as jnp
from jax import lax
from jax.experimental import pallas as pl
from jax.experimental.pallas import tpu as pltpu
from jax.experimental.pallas import tpu_sc as plsc

F32 = jnp.float32
BF16 = jnp.bfloat16
I32 = jnp.int32
HIGHEST = lax.Precision.HIGHEST

EPS = 1e-6
CONV_WIDTH = 4
LANES = 128
SUBLANES = 8
LRU_HEADS = 8
LRU_C = 8.0
ML_HEADS = 8
CHUNK = 128
PROJ_TILE = 512
PROJ_TILE_EVEN = 1024
LRU_TILE = 512
MOE_TILE = 512
COMBINE_TILE = 1024
DEST_TILE = 8192
SSD_HEAD_DIM = 64
SSD_GROUPS = 8
SSD_STATE = 128
N_EXPERTS = 32
TOP_K = 4
SWIGLU_ALPHA = 1.702
SWIGLU_LIMIT = 7.0
EXPERT_BLOCK = 2048
EXPERT_ROWS = 512
SEQ_PER_STEP = 2
CONV_COLS = 512
VMEM_LIMIT = 56 * 1024 * 1024


def _cparams(sem, **kw):
    return pltpu.CompilerParams(dimension_semantics=sem, vmem_limit_bytes=VMEM_LIMIT, **kw)


def _silu(x):
    half = 0.5 * x
    return half + half * jnp.tanh(half)


def _log_sigmoid(x):
    return jnp.minimum(x, 0.0) - jnp.log1p(jnp.exp(-jnp.abs(x)))


def _softplus(x):
    return jnp.maximum(x, 0.0) + jnp.log1p(jnp.exp(-jnp.abs(x)))


def _dot(a, b, **kw):
    return jnp.dot(a, b, preferred_element_type=F32, **kw)


def _dot_nt(a, b):
    return lax.dot_general(a, b, (((1,), (1,)), ((), ())), preferred_element_type=F32)


def _pack_pairs(x):
    w = x.shape[1] // 2
    lo = lax.bitcast_convert_type(x[:, :w].astype(BF16).astype(F32), I32)
    hi = lax.bitcast_convert_type(x[:, w:].astype(BF16).astype(F32), I32)
    return lax.shift_right_logical(lo, 16) | (hi & jnp.int32(-65536))


def _unpack_pairs(p):
    lo = lax.bitcast_convert_type(lax.shift_left(p, 16), F32)
    hi = lax.bitcast_convert_type(p & jnp.int32(-65536), F32)
    return jnp.concatenate([lo, hi], axis=1)


def _norm_mod(h, g, shift, scale):
    y = h * lax.rsqrt(jnp.mean(h * h, axis=-1, keepdims=True) + EPS)
    return (y * g) * (1.0 + scale) + shift


def _causal_conv(x, tail_ref, w_ref, b_ref, sl):
    t = x.shape[0]
    tail = tail_ref[:, sl]
    row8 = lax.broadcasted_iota(I32, tail.shape, 0)
    out = b_ref[:, sl] + x * w_ref[CONV_WIDTH - 1:CONV_WIDTH, sl]
    for k in range(1, CONV_WIDTH):
        xs = pltpu.roll(x, k, axis=0)
        first = jnp.where(row8 < k, pltpu.roll(tail, k, axis=0), xs[:SUBLANES])
        xs = jnp.concatenate([first, xs[SUBLANES:]], axis=0)
        out = out + xs * w_ref[CONV_WIDTH - 1 - k:CONV_WIDTH - k, sl]
    tail_ref[:, sl] = x[t - SUBLANES:]
    return out


def _shift_matrix(t):
    r = lax.broadcasted_iota(I32, ((CONV_WIDTH - 1) * t, t), 0)
    c = lax.broadcasted_iota(I32, ((CONV_WIDTH - 1) * t, t), 1)
    src = (r & (t - 1)) - lax.shift_right_logical(r, t.bit_length() - 1) - 1
    return (src == c).astype(BF16)


def _causal_conv_shifted(x, shifted, tail_ref, w_ref, b_ref, sl):
    t = x.shape[0]
    tail = tail_ref[:, sl]
    row8 = lax.broadcasted_iota(I32, tail.shape, 0)
    out = b_ref[:, sl] + x * w_ref[CONV_WIDTH - 1:CONV_WIDTH, sl]
    head = jnp.zeros_like(tail)
    for k in range(1, CONV_WIDTH):
        wk = w_ref[CONV_WIDTH - 1 - k:CONV_WIDTH - k, sl]
        out = out + shifted[(k - 1) * t:k * t] * wk
        head = head + jnp.where(row8 < k, pltpu.roll(tail, k, axis=0), 0.0) * wk
    tail_ref[:, sl] = x[t - SUBLANES:]
    return jnp.concatenate([out[:SUBLANES] + head, out[SUBLANES:]], axis=0)


def _mod_kernel(c_ref, w_ref, b_ref, o_ref):
    cond = _silu(c_ref[...])
    o_ref[0, 0] = _dot(cond, w_ref[0], precision=HIGHEST) + b_ref[0, 0]


def _modulation(c, mod_w, mod_b):
    depth, d, _ = mod_w.shape
    bsz = c.shape[0]
    out = pl.pallas_call(
        _mod_kernel,
        out_shape=jax.ShapeDtypeStruct((depth, 6, bsz, d), F32),
        grid=(depth, 6),
        in_specs=[pl.BlockSpec((bsz, d), lambda l, j: (0, 0)),
                  pl.BlockSpec((1, d, d), lambda l, j: (l, 0, j)),
                  pl.BlockSpec((1, 1, 1, d), lambda l, j: (l, j, 0, 0))],
        out_specs=pl.BlockSpec((1, 1, bsz, d), lambda l, j: (l, j, 0, 0)),
        compiler_params=_cparams(("parallel", "parallel")),
    )(c.astype(F32), mod_w, mod_b.reshape(depth, 6, 1, d))
    return out.reshape(depth, 6, bsz, 1, d)


def _inproj_kernel(h_ref, g_ref, sh_ref, sc_ref, w_ref, *rest, n_chunk, with_dt):
    if with_dt:
        wdt_ref, *o_refs, odt_ref = rest
    else:
        o_refs = rest
    u = _norm_mod(h_ref[...], g_ref[...], sh_ref[0], sc_ref[0]).astype(BF16)
    off = 0
    for o_ref in o_refs:
        for n0 in range(0, o_ref.shape[1], n_chunk):
            o_ref[:, n0:n0 + n_chunk] = _dot(u, w_ref[:, off + n0:off + n0 + n_chunk]).astype(o_ref.dtype)
        off += o_ref.shape[1]
    if with_dt:
        odt_ref[...] = _dot(u, wdt_ref[...])


def _inproj(h, g, shift, scale, w, wdt, splits, seq, tm):
    m, d = h.shape
    n = sum(splits)
    assert n <= w.shape[1] and n % LANES == 0
    tiles_per_seq = seq // tm
    bmap = lambda i: (i // tiles_per_seq, 0, 0)
    in_specs = [pl.BlockSpec((tm, d), lambda i: (i, 0)),
                pl.BlockSpec((1, d), lambda i: (0, 0)),
                pl.BlockSpec((1, 1, d), bmap),
                pl.BlockSpec((1, 1, d), bmap),
                pl.BlockSpec((d, n), lambda i: (0, 0), pipeline_mode=pl.Buffered(1))]
    out_shape = [jax.ShapeDtypeStruct((m, s), BF16) for s in splits]
    out_specs = [pl.BlockSpec((tm, s), lambda i: (i, 0)) for s in splits]
    args = [h, g.reshape(1, d), shift, scale, w]
    if wdt is not None:
        in_specs.append(pl.BlockSpec((d, LANES), lambda i: (0, 0)))
        out_shape.append(jax.ShapeDtypeStruct((m, LANES), F32))
        out_specs.append(pl.BlockSpec((tm, LANES), lambda i: (i, 0)))
        args.append(wdt)
    return pl.pallas_call(
        functools.partial(_inproj_kernel, n_chunk=1024, with_dt=wdt is not None),
        out_shape=out_shape, grid=(m // tm,), in_specs=in_specs, out_specs=out_specs,
        compiler_params=_cparams(("parallel",)),
    )(*args)


def _lru_kernel(xa_ref, ga_ref, cw_ref, cb_ref, wr_ref, br_ref, wi_ref, bi_ref, lam_ref,
                o_ref, tail_ref, hc_ref):
    @pl.when(pl.program_id(1) == 0)
    def _():
        tail_ref[...] = jnp.zeros_like(tail_ref)
        hc_ref[...] = jnp.zeros_like(hc_ref)

    t = xa_ref.shape[0]
    row_in_group = lax.broadcasted_iota(I32, (t, LANES), 0) % SUBLANES
    steps = [s for s in (1, 2, 4) if s < SUBLANES]
    masks = [row_in_group >= s for s in steps]
    for hh in range(LRU_HEADS):
        sl = slice(hh * LANES, (hh + 1) * LANES)
        xc = _causal_conv(xa_ref[:, sl].astype(F32), tail_ref, cw_ref, cb_ref, sl)
        xcb = xc.astype(BF16)
        r = jax.nn.sigmoid(_dot(xcb, wr_ref[hh]) + br_ref[:, sl])
        i = jax.nn.sigmoid(_dot(xcb, wi_ref[hh]) + bi_ref[:, sl])
        log_a = LRU_C * r * _log_sigmoid(lam_ref[:, sl])
        a = jnp.exp(log_a)
        th = jnp.tanh(log_a)
        n2 = -2.0 * th
        root = jnp.where(n2 > 0.0, n2 * lax.rsqrt(n2), 0.0)
        u = (root * lax.rsqrt(1.0 - th)) * (i * xc)
        def roll_in_groups(v, s):
            v3 = v.reshape(t // SUBLANES, SUBLANES, LANES)
            return pltpu.roll(v3, s, axis=1).reshape(t, LANES)

        for s, m in zip(steps, masks):
            u = jnp.where(m, u + a * roll_in_groups(u, s), u)
            a = jnp.where(m, a * roll_in_groups(a, s), a)
        carry = hc_ref[:, sl]
        groups = []
        for r0 in range(0, t, SUBLANES):
            hg = u[r0:r0 + SUBLANES] + a[r0:r0 + SUBLANES] * carry
            carry = hg[SUBLANES - 1:SUBLANES]
            groups.append(hg)
        hc_ref[:, sl] = carry
        h = jnp.concatenate(groups, axis=0)
        ga = ga_ref[:, sl].astype(F32)
        o_ref[:, sl] = (h * jax.nn.gelu(ga, approximate=True)).astype(o_ref.dtype)


def _lru(proj, p, bsz, seq, tm):
    m = proj.shape[0]
    w = LRU_HEADS * LANES
    nt = seq // tm
    vec = lambda: pl.BlockSpec((1, w), lambda b, j: (0, 0))
    return pl.pallas_call(
        _lru_kernel,
        out_shape=jax.ShapeDtypeStruct((m, w), BF16),
        grid=(bsz, nt),
        in_specs=[pl.BlockSpec((tm, w), lambda b, j: (b * nt + j, 0)),
                  pl.BlockSpec((tm, w), lambda b, j: (b * nt + j, 1)),
                  pl.BlockSpec((CONV_WIDTH, w), lambda b, j: (0, 0)), vec(),
                  pl.BlockSpec((LRU_HEADS, LANES, LANES), lambda b, j: (0, 0, 0)), vec(),
                  pl.BlockSpec((LRU_HEADS, LANES, LANES), lambda b, j: (0, 0, 0)), vec(), vec()],
        out_specs=pl.BlockSpec((tm, w), lambda b, j: (b * nt + j, 0)),
        scratch_shapes=[pltpu.VMEM((SUBLANES, w), F32), pltpu.VMEM((1, w), F32)],
        compiler_params=_cparams(("parallel", "arbitrary")),
    )(proj, proj, p["conv_w"], p["conv_b"], p["w_r"], p["b_r"], p["w_i"], p["b_i"], p["lam"])


def _mlstm_kernel(xb_ref, zb_ref, cw_ref, cb_ref, wq_ref, wk_ref, wv_ref, wg_ref, bg_ref,
                  nw_ref, sk_ref, o_ref, tail_ref, qkv_ref, xc_ref, caug_ref, m_ref):
    @pl.when(pl.program_id(1) == 0)
    def _():
        tail_ref[...] = jnp.zeros_like(tail_ref)
        caug_ref[...] = jnp.zeros_like(caug_ref)
        m_ref[...] = jnp.full(m_ref.shape, -jnp.inf, F32)

    for s in range(xb_ref.shape[0]):
        for c0 in range(0, xb_ref.shape[1], CHUNK):
            rows = pl.ds(c0, CHUNK)
            _mlstm_chunk(xb_ref.at[s, rows], zb_ref.at[s, rows], cw_ref, cb_ref, wq_ref, wk_ref, wv_ref,
                         wg_ref, bg_ref, nw_ref, sk_ref, o_ref.at[s, rows], tail_ref.at[s], qkv_ref.at[s],
                         xc_ref.at[s], caug_ref.at[s], m_ref.at[s])


def _mlstm_chunk(xb_ref, zb_ref, cw_ref, cb_ref, wq_ref, wk_ref, wv_ref, wg_ref, bg_ref,
                 nw_ref, sk_ref, o_ref, tail_ref, qkv_ref, xc_ref, caug_ref, m_ref):
    L = CHUNK
    width = ML_HEADS * LANES
    scale = LANES ** -0.5
    for hh in range(ML_HEADS):
        sl = slice(hh * LANES, (hh + 1) * LANES)
        xb = xb_ref[:, sl].astype(F32)
        xc = _silu(_causal_conv(xb, tail_ref, cw_ref, cb_ref, sl))
        xc_ref[:, sl] = xc
        xcb = xc.astype(BF16)
        qkv_ref[:, sl] = _dot(xcb, wq_ref[hh]).astype(BF16)
        qkv_ref[:, width + hh * LANES:width + (hh + 1) * LANES] = _dot(xcb, wk_ref[hh]).astype(BF16)
        qkv_ref[:, 2 * width + hh * LANES:2 * width + (hh + 1) * LANES] = (
            _dot(xb.astype(BF16), wv_ref[hh]).astype(BF16))

    gates = _dot(qkv_ref[...], wg_ref[...]) + bg_ref[...]
    rowi = lax.broadcasted_iota(I32, (L, L), 0)
    coli = lax.broadcasted_iota(I32, (L, L), 1)
    causal = rowi >= coli
    lf = jnp.where((coli >= ML_HEADS) & (coli < 2 * ML_HEADS), _log_sigmoid(gates), 0.0)
    tri = causal.astype(BF16)
    lf_hi = lf.astype(BF16)
    lf_mid = (lf - lf_hi.astype(F32)).astype(BF16)
    lf_lo = (lf - lf_hi.astype(F32) - lf_mid.astype(F32)).astype(BF16)
    gcum = _dot(tri, lf_hi) + (_dot(tri, lf_mid) + _dot(tri, lf_lo))
    x_col = jnp.where(coli < ML_HEADS, gates, gcum)
    x_row = x_col.T
    ones = jnp.ones((L, LANES), BF16)
    heads = range(ML_HEADS)
    hsl = [slice(hh * LANES, (hh + 1) * LANES) for hh in heads]

    qs = [qkv_ref[:, hsl[hh]] for hh in heads]
    ks = [qkv_ref[:, width + hh * LANES:width + (hh + 1) * LANES] for hh in heads]
    vaugs = [jnp.concatenate([qkv_ref[:, 2 * width + hh * LANES:2 * width + (hh + 1) * LANES], ones], axis=1)
             for hh in heads]
    scores = [_dot_nt(qs[hh], ks[hh]) * scale for hh in heads]
    ics = [jnp.broadcast_to(x_col[:, hh:hh + 1], (L, LANES)) for hh in heads]
    gcs = [jnp.broadcast_to(x_col[:, ML_HEADS + hh:ML_HEADS + hh + 1], (L, LANES)) for hh in heads]
    irs = [x_row[hh:hh + 1, :] for hh in heads]
    grs = [x_row[ML_HEADS + hh:ML_HEADS + hh + 1, :] for hh in heads]
    mps = [m_ref[hh] for hh in heads]
    dmats = [jnp.where(causal, gcs[hh] - grs[hh] + irs[hh], -jnp.inf) for hh in heads]
    m_inters = [mps[hh] + gcs[hh] for hh in heads]
    m_ts = [jnp.maximum(m_inters[hh], jnp.max(dmats[hh], axis=1, keepdims=True)) for hh in heads]
    qks = [(scores[hh] * jnp.exp(dmats[hh] - m_ts[hh])).astype(BF16) for hh in heads]
    caugs = [caug_ref[hh] for hh in heads]
    w_inters = [jnp.exp(m_inters[hh] - m_ts[hh]) for hh in heads]
    nds = [_dot(qks[hh], vaugs[hh])
           + jnp.concatenate([w_inters[hh], w_inters[hh]], axis=1) * _dot(qs[hh], caugs[hh].astype(BF16))
           for hh in heads]

    g_lasts = [gcs[hh][L - 1:L, :] for hh in heads]
    m_news = [jnp.maximum(mps[hh] + g_lasts[hh],
                          jnp.max(g_lasts[hh] - grs[hh] + irs[hh], axis=1, keepdims=True)) for hh in heads]
    for hh in heads:
        ws = jnp.exp(g_lasts[hh] - gcs[hh] + ics[hh] - m_news[hh])
        wc = jnp.exp(mps[hh] + g_lasts[hh] - m_news[hh])
        kw_t = (ks[hh].astype(F32) * (ws * scale)).T.astype(BF16)
        caug_ref[hh] = jnp.concatenate([wc, wc], axis=1) * caugs[hh] + _dot(kw_t, vaugs[hh])
        m_ref[hh] = m_news[hh]

    hvals = [nds[hh][:, :LANES] / jnp.maximum(jnp.abs(nds[hh][:, LANES:]), jnp.exp(-m_ts[hh]))
             for hh in heads]
    mus = [jnp.mean(hvals[hh], axis=1, keepdims=True) for hh in heads]
    dvs = [hvals[hh] - mus[hh] for hh in heads]
    variances = [jnp.mean(dvs[hh] * dvs[hh], axis=1, keepdims=True) for hh in heads]
    for hh in heads:
        sl = hsl[hh]
        hn = dvs[hh] * lax.rsqrt(variances[hh] + EPS) * nw_ref[:, sl]
        zb = zb_ref[:, sl].astype(F32)
        o_ref[:, sl] = ((hn + sk_ref[:, sl] * xc_ref[:, sl]) * _silu(zb)).astype(o_ref.dtype)


def _mlstm(proj, p, bsz, seq):
    m = proj.shape[0]
    w = ML_HEADS * LANES
    sps = 1
    cps = next(c for c in (4, 2, 1) if seq % (c * CHUNK) == 0)
    nt = seq // (cps * CHUNK)
    vec = lambda: pl.BlockSpec((1, w), lambda b, j: (0, 0))
    blk = lambda: pl.BlockSpec((ML_HEADS, LANES, LANES), lambda b, j: (0, 0, 0))
    tile = lambda col: pl.BlockSpec((sps, cps * CHUNK, w), lambda b, j: (b, j, col))
    proj3 = proj.reshape(bsz, seq, proj.shape[1])
    out = pl.pallas_call(
        _mlstm_kernel,
        out_shape=jax.ShapeDtypeStruct((bsz, seq, w), BF16),
        grid=(bsz // sps, nt),
        in_specs=[tile(2), tile(3),
                  pl.BlockSpec((CONV_WIDTH, w), lambda b, j: (0, 0)), vec(),
                  blk(), blk(), blk(),
                  pl.BlockSpec((3 * w, LANES), lambda b, j: (0, 0)),
                  pl.BlockSpec((1, LANES), lambda b, j: (0, 0)),
                  vec(), vec()],
        out_specs=tile(0),
        scratch_shapes=[pltpu.VMEM((sps, SUBLANES, w), F32),
                        pltpu.VMEM((sps, CHUNK, 3 * w), BF16),
                        pltpu.VMEM((sps, CHUNK, w), F32),
                        pltpu.VMEM((sps, ML_HEADS, LANES, 2 * LANES), F32),
                        pltpu.VMEM((sps, ML_HEADS, 1, LANES), F32)],
        compiler_params=_cparams(("parallel", "arbitrary")),
    )(proj3, proj3, p["conv_w"], p["conv_b"], p["w_q"], p["w_k"], p["w_v"], p["w_g"], p["b_g"],
      p["norm"], p["skip"])
    return out.reshape(m, w)


def _ssd_kernel(z_ref, xbc_ref, dt_ref, cw_ref, cb_ref, dtb_ref, alog_ref, dsk_ref, nw_ref,
                o_ref, tail_ref, act_ref, st_ref):
    @pl.when(pl.program_id(1) == 0)
    def _():
        tail_ref[...] = jnp.zeros_like(tail_ref)
        st_ref[...] = jnp.zeros_like(st_ref)

    for c0 in range(0, z_ref.shape[1], CHUNK):
        rows = pl.ds(c0, CHUNK)
        for s in range(z_ref.shape[0]):
            _ssd_chunk(z_ref.at[s, rows], xbc_ref.at[s, rows], dt_ref.at[s, rows], cw_ref, cb_ref, dtb_ref,
                       alog_ref, dsk_ref, nw_ref, o_ref.at[s, rows], tail_ref.at[s], act_ref.at[s],
                       st_ref.at[s])


def _ssd_chunk(z_ref, xbc_ref, dt_ref, cw_ref, cb_ref, dtb_ref, alog_ref, dsk_ref, nw_ref,
               o_ref, tail_ref, act_ref, st_ref):
    L = CHUNK
    inner = o_ref.shape[1]
    gw = inner // SSD_GROUPS
    hpg = gw // SSD_HEAD_DIM
    b_off = inner
    c_off = inner + SSD_GROUPS * SSD_STATE
    shift = _shift_matrix(L)
    for c0 in range(0, xbc_ref.shape[1], CONV_COLS):
        shifted = _dot(shift, xbc_ref[:, c0:c0 + CONV_COLS])
        for l0 in range(0, CONV_COLS, LANES):
            sl = slice(c0 + l0, c0 + l0 + LANES)
            act_ref[:, sl] = _silu(_causal_conv_shifted(
                xbc_ref[:, sl].astype(F32), shifted[:, l0:l0 + LANES], tail_ref, cw_ref, cb_ref, sl))

    rowi = lax.broadcasted_iota(I32, (L, L), 0)
    coli = lax.broadcasted_iota(I32, (L, L), 1)
    causal = rowi >= coli
    dt = _softplus(dt_ref[...] + dtb_ref[...])
    da = dt * (-jnp.exp(alog_ref[...]))
    tri = causal.astype(BF16)
    da_hi = da.astype(BF16)
    da_mid = (da - da_hi.astype(F32)).astype(BF16)
    da_lo = (da - da_hi.astype(F32) - da_mid.astype(F32)).astype(BF16)
    a = _dot(tri, da_hi) + (_dot(tri, da_mid) + _dot(tri, da_lo))
    a_t = a.T
    hpt = LANES // SSD_HEAD_DIM
    lane = lax.broadcasted_iota(I32, (L, LANES), 1)

    def over_heads(tiles):
        cols = []
        for c0 in range(0, hpg, hpt):
            out = tiles[c0 + hpt - 1]
            for j in range(hpt - 2, -1, -1):
                out = jnp.where(lane < (j + 1) * SSD_HEAD_DIM, tiles[c0 + j], out)
            cols.append(out)
        return jnp.concatenate(cols, axis=1)

    for g in range(SSD_GROUPS):
        gsl = slice(g * gw, (g + 1) * gw)
        xg = act_ref[:, gsl]
        bg = act_ref[:, b_off + g * SSD_STATE:b_off + (g + 1) * SSD_STATE]
        cg_ = act_ref[:, c_off + g * SSD_STATE:c_off + (g + 1) * SSD_STATE].astype(BF16)
        cb = _dot_nt(cg_, bg.astype(BF16))
        state = st_ref[g]
        hds = [g * hpg + jj for jj in range(hpg)]
        a_bs = [jnp.broadcast_to(a[:, hd:hd + 1], (L, LANES)) for hd in hds]
        dt_bs = [jnp.broadcast_to(dt[:, hd:hd + 1], (L, LANES)) for hd in hds]
        ea_x = over_heads([jnp.exp(a_b) for a_b in a_bs])
        to_end_x = over_heads([jnp.exp(a_b[L - 1:L, :] - a_b) for a_b in a_bs])
        xdt = xg * over_heads(dt_bs)
        inter = _dot(cg_, state.astype(BF16)) * ea_x
        acc = [inter[:, c0:c0 + LANES] for c0 in range(0, gw, LANES)]
        for jj in range(hpg):
            seg = jnp.where(causal, a_bs[jj] - a_t[hds[jj]:hds[jj] + 1, :], -jnp.inf)
            w = (cb * jnp.exp(seg)).astype(BF16)
            c, j = divmod(jj, hpt)
            in_head = (lane >= j * SSD_HEAD_DIM) & (lane < (j + 1) * SSD_HEAD_DIM)
            x_head = jnp.where(in_head, xdt[:, c * LANES:(c + 1) * LANES], 0.0).astype(BF16)
            acc[c] = acc[c] + _dot(w, x_head)
        acc = jnp.concatenate(acc, axis=1)
        y = (acc + dsk_ref[:, gsl] * xg) * _silu(z_ref[:, gsl].astype(F32))
        y = y * lax.rsqrt(jnp.mean(y * y, axis=1, keepdims=True) + EPS) * nw_ref[:, gsl]
        o_ref[:, gsl] = y.astype(o_ref.dtype)
        xw = (xdt * to_end_x).astype(BF16)
        st_ref[g] = ea_x[L - 1:L, :] * state + _dot(bg.T.astype(BF16), xw)


def _ssd(z, xbc, dt_raw, p, bsz, seq):
    m, inner = z.shape
    conv_ch = xbc.shape[1]
    sps = SEQ_PER_STEP if bsz % SEQ_PER_STEP == 0 else 1
    cps = 2 if seq % (2 * CHUNK) == 0 else 1
    nt = seq // (cps * CHUNK)
    vec = lambda n: pl.BlockSpec((1, n), lambda b, j: (0, 0))
    tile = lambda n: pl.BlockSpec((sps, cps * CHUNK, n), lambda b, j: (b, j, 0))
    out = pl.pallas_call(
        _ssd_kernel,
        out_shape=jax.ShapeDtypeStruct((bsz, seq, inner), BF16),
        grid=(bsz // sps, nt),
        in_specs=[tile(inner), tile(conv_ch), tile(LANES),
                  pl.BlockSpec((CONV_WIDTH, conv_ch), lambda b, j: (0, 0)), vec(conv_ch),
                  vec(LANES), vec(LANES), vec(inner), vec(inner)],
        out_specs=tile(inner),
        scratch_shapes=[pltpu.VMEM((sps, SUBLANES, conv_ch), F32),
                        pltpu.VMEM((sps, CHUNK, conv_ch), F32),
                        pltpu.VMEM((sps, SSD_GROUPS, SSD_STATE, inner // SSD_GROUPS), F32)],
        compiler_params=_cparams(("parallel", "arbitrary")),
    )(z.reshape(bsz, seq, inner), xbc.reshape(bsz, seq, conv_ch), dt_raw.reshape(bsz, seq, LANES),
      p["conv_w"], p["conv_b"], p["dt_bias"], p["a_log"], p["d_skip"], p["norm"])
    return out.reshape(m, inner)


def _router_kernel(*refs, n_in):
    y_refs, w_refs = refs[:n_in], refs[n_in:2 * n_in]
    (h_ref, gm_ref, g_ref, sh_ref, sc_ref, wr_ref, br_ref,
     hmid_ref, up_ref, topi_ref, gate_ref, rank_ref, cnt_ref, carry_ref) = refs[2 * n_in:]

    @pl.when(pl.program_id(0) == 0)
    def _():
        carry_ref[...] = jnp.zeros_like(carry_ref)

    tm = h_ref.shape[0]
    acc = _dot(y_refs[0][...], w_refs[0][...])
    for y_ref, w_ref in zip(y_refs[1:], w_refs[1:]):
        acc = acc + _dot(y_ref[...], w_ref[...])
    hmid = h_ref[...] + gm_ref[0] * acc
    hmid_ref[...] = hmid
    u = _norm_mod(hmid, g_ref[...], sh_ref[0], sc_ref[0])
    up_ref[...] = _pack_pairs(u)
    u_hi = u.astype(BF16)
    u_lo = (u - u_hi.astype(F32)).astype(BF16)
    hi_both = _dot(u_hi, wr_ref[...])
    logits = (hi_both[:, :LANES] + (_dot(u_lo, wr_ref[:, :LANES]) + hi_both[:, LANES:])
              + br_ref[...])
    lt = jnp.concatenate([logits[r0:r0 + LANES].T for r0 in range(0, tm, LANES)], axis=1)
    l = lt[:N_EXPERTS]
    e_iota = lax.broadcasted_iota(I32, (N_EXPERTS, tm), 0).astype(F32)
    vals, idxs, hots = [], [], []
    for _ in range(TOP_K):
        mx = jnp.max(l, axis=0, keepdims=True)
        idx = jnp.min(jnp.where(l == mx, e_iota, float(N_EXPERTS)), axis=0, keepdims=True)
        hot = e_iota == idx
        l = jnp.where(hot, -jnp.inf, l)
        vals.append(mx)
        idxs.append(idx)
        hots.append(hot)
    exps = [jnp.exp(v - vals[0]) for v in vals]
    den = exps[0] + exps[1] + exps[2] + exps[3]
    gate_ref[...] = jnp.concatenate([e / den for e in exps], axis=0)
    topi_ref[...] = jnp.concatenate(idxs, axis=0).astype(I32)

    sel = jnp.zeros((N_EXPERTS, tm), F32)
    for hot in hots:
        sel = jnp.where(hot, 1.0, sel)
    r_i = lax.broadcasted_iota(I32, (tm, tm), 0)
    c_i = lax.broadcasted_iota(I32, (tm, tm), 1)
    before = (r_i < c_i).astype(BF16)
    carry = carry_ref[:, 0:1]
    cum = _dot(sel.astype(BF16), before) + carry
    rank_ref[...] = jnp.concatenate(
        [jnp.sum(jnp.where(hot, cum, 0.0), axis=0, keepdims=True) for hot in hots], axis=0).astype(I32)
    total = carry + jnp.sum(sel, axis=1, keepdims=True)
    carry_ref[...] = jnp.broadcast_to(total, carry_ref.shape)
    cnt_ref[...] = jnp.broadcast_to(total, cnt_ref.shape)


def _router(ys, w_out, h, g_m, g, shift, scale, wr, br, seq, tm):
    m, d = h.shape
    tiles_per_seq = seq // tm
    bmap = lambda i: (i // tiles_per_seq, 0, 0)
    row4 = lambda: pl.BlockSpec((TOP_K, tm), lambda i: (0, i))
    in_specs, args, k0 = [], [], 0
    for y in ys:
        in_specs.append(pl.BlockSpec((tm, y.shape[1]), lambda i: (i, 0)))
        args.append(y)
    for y in ys:
        kk = y.shape[1]
        in_specs.append(pl.BlockSpec((kk, d), lambda i, kb=k0 // kk: (kb, 0)))
        args.append(w_out)
        k0 += kk
    in_specs += [pl.BlockSpec((tm, d), lambda i: (i, 0)),
                 pl.BlockSpec((1, 1, d), bmap),
                 pl.BlockSpec((1, d), lambda i: (0, 0)),
                 pl.BlockSpec((1, 1, d), bmap), pl.BlockSpec((1, 1, d), bmap),
                 pl.BlockSpec((d, 2 * LANES), lambda i: (0, 0)),
                 pl.BlockSpec((1, LANES), lambda i: (0, 0))]
    args += [h, g_m, g.reshape(1, d), shift, scale, wr, br]
    return pl.pallas_call(
        functools.partial(_router_kernel, n_in=len(ys)),
        out_shape=[jax.ShapeDtypeStruct((m, d), F32),
                   jax.ShapeDtypeStruct((m, d // 2), I32),
                   jax.ShapeDtypeStruct((TOP_K, m), I32),
                   jax.ShapeDtypeStruct((TOP_K, m), F32),
                   jax.ShapeDtypeStruct((TOP_K, m), I32),
                   jax.ShapeDtypeStruct((N_EXPERTS, LANES), F32)],
        grid=(m // tm,),
        in_specs=in_specs,
        out_specs=[pl.BlockSpec((tm, d), lambda i: (i, 0)),
                   pl.BlockSpec((tm, d // 2), lambda i: (i, 0)), row4(), row4(), row4(),
                   pl.BlockSpec((N_EXPERTS, LANES), lambda i: (0, 0))],
        scratch_shapes=[pltpu.VMEM((N_EXPERTS, LANES), F32)],
        compiler_params=_cparams(("arbitrary",)),
    )(*args)


def _dest_kernel(ps_ref, topi_ref, rank_ref, o_ref):
    topi = topi_ref[...]
    acc = rank_ref[...]
    for e in range(N_EXPERTS):
        acc = acc + jnp.where(topi == e, ps_ref[e], 0)
    o_ref[...] = acc


def _dest_rows(pad_start, topi, rank, tw):
    k, m = topi.shape
    blk = lambda: pl.BlockSpec((k, tw), lambda i, ps: (0, i))
    return pl.pallas_call(
        _dest_kernel,
        out_shape=jax.ShapeDtypeStruct((k, m), I32),
        grid_spec=pltpu.PrefetchScalarGridSpec(
            num_scalar_prefetch=1, grid=(m // tw,), in_specs=[blk(), blk()], out_specs=blk()),
        compiler_params=_cparams(("parallel",)),
    )(pad_start, topi, rank)


SC_CORES = 2
SC_SUBCORES = 16
SC_ROWS = 128


def _sc_gather_rows(table, idx):
    b = idx.shape[0]
    w = table.shape[1]
    workers = SC_CORES * SC_SUBCORES
    per_w = b // workers
    assert per_w * workers == b and per_w % SC_ROWS == 0
    mesh = plsc.VectorSubcoreMesh(core_axis_name="c", subcore_axis_name="s")

    @functools.partial(
        pl.kernel, mesh=mesh, out_type=jax.ShapeDtypeStruct((b, w), I32),
        scratch_types=[pltpu.VMEM((SC_ROWS,), I32), pltpu.VMEM((SC_ROWS, w), I32),
                       pltpu.SemaphoreType.DMA])
    def gather(table_hbm, idx_hbm, out_hbm, idx_v, rows_v, sem):
        base = (lax.axis_index("s") * SC_CORES + lax.axis_index("c")) * per_w

        @pl.loop(0, per_w // SC_ROWS)
        def _(c):
            off = base + c * SC_ROWS
            pltpu.sync_copy(idx_hbm.at[pl.ds(off, SC_ROWS)], idx_v)
            pltpu.async_copy(table_hbm.at[idx_v], rows_v, sem).wait()
            pltpu.sync_copy(rows_v, out_hbm.at[pl.ds(off, SC_ROWS)])

    return gather(table, idx)


def _sc_scatter_rows(rows, dest, n_rows):
    m, w = rows.shape
    kk = dest.shape[0]
    workers = SC_CORES * SC_SUBCORES
    per_w = m // workers
    assert per_w * workers == m and per_w % SC_ROWS == 0
    mesh = plsc.VectorSubcoreMesh(core_axis_name="c", subcore_axis_name="s")

    @functools.partial(
        pl.kernel, mesh=mesh, out_type=jax.ShapeDtypeStruct((n_rows, w), I32),
        scratch_types=[pltpu.VMEM((SC_ROWS,), I32), pltpu.VMEM((SC_ROWS, w), I32),
                       pltpu.SemaphoreType.DMA])
    def scatter(rows_hbm, dest_hbm, out_hbm, idx_v, rows_v, sem):
        base = (lax.axis_index("s") * SC_CORES + lax.axis_index("c")) * per_w

        @pl.loop(0, per_w // SC_ROWS)
        def _(c):
            off = base + c * SC_ROWS
            pltpu.sync_copy(rows_hbm.at[pl.ds(off, SC_ROWS)], rows_v)
            for k in range(kk):
                pltpu.sync_copy(dest_hbm.at[pl.ds(k * m + off, SC_ROWS)], idx_v)
                pltpu.async_copy(rows_v, out_hbm.at[idx_v], sem).wait()

    return scatter(rows, dest.reshape(-1))


def _combine_dense_kernel(y_ref, gate_ref, h_ref, gf_ref, fn_ref, o_ref, *, final):
    acc = gate_ref[:, 0:1] * _unpack_pairs(y_ref[0])
    for k in range(1, TOP_K):
        acc = acc + gate_ref[:, k:k + 1] * _unpack_pairs(y_ref[k])
    hn = h_ref[...] + gf_ref[0] * acc
    if final:
        hn = hn * lax.rsqrt(jnp.mean(hn * hn, axis=-1, keepdims=True) + EPS) * fn_ref[...]
    o_ref[...] = hn


def _combine_dense(y4, gates_col, h, gf, fnorm, seq, tm, final):
    m, d = h.shape
    tiles_per_seq = seq // tm
    return pl.pallas_call(
        functools.partial(_combine_dense_kernel, final=final),
        out_shape=jax.ShapeDtypeStruct((m, d), F32),
        grid=(m // tm,),
        in_specs=[pl.BlockSpec((TOP_K, tm, d // 2), lambda i: (0, i, 0)),
                  pl.BlockSpec((tm, TOP_K), lambda i: (i, 0)),
                  pl.BlockSpec((tm, d), lambda i: (i, 0)),
                  pl.BlockSpec((1, 1, d), lambda i: (i // tiles_per_seq, 0, 0)),
                  pl.BlockSpec((1, d), lambda i: (0, 0))],
        out_specs=pl.BlockSpec((tm, d), lambda i: (i, 0)),
        compiler_params=_cparams(("parallel",)),
    )(y4, gates_col, h, gf, fnorm.reshape(1, d))


def _expert_kernel(be_ref, nb_ref, first_ref, nv_ref, x_ref, wgu_ref, bgu_ref, wd_ref, bd_ref, y_ref,
                   wgu_bf, wd_bf):
    i = pl.program_id(0)

    @pl.when(i < nb_ref[0])
    def _():
        dff = wd_bf.shape[0]

        @pl.when(first_ref[i] == 1)
        def _():
            rows = 64

            def cast(r, c):
                r0 = pl.multiple_of(r * rows, rows)
                wgu_bf[pl.ds(r0, rows), :] = wgu_ref[0, 0, pl.ds(r0, rows), :].astype(BF16)
                wd_bf[pl.ds(r0, rows), :] = wd_ref[0, 0, pl.ds(r0, rows), :].astype(BF16)
                return c

            lax.fori_loop(0, dff // rows, cast, 0)

        def ffn(r0):
            x = _unpack_pairs(x_ref[r0:r0 + EXPERT_ROWS, :]).astype(BF16)
            hb = _dot(x, wgu_bf[...]) + bgu_ref[0, 0]
            h_glu = jnp.minimum(hb[:, :dff], SWIGLU_LIMIT)
            h_lin = jnp.clip(hb[:, dff:], -SWIGLU_LIMIT, SWIGLU_LIMIT)
            half = 0.5 * h_glu
            act = (half + half * jnp.tanh(SWIGLU_ALPHA * half)) * (h_lin + 1.0)
            y_ref[r0:r0 + EXPERT_ROWS, :] = _pack_pairs(_dot(act.astype(BF16), wd_bf[...]) + bd_ref[0, 0])

        ffn(0)
        for r0 in range(EXPERT_ROWS, x_ref.shape[0], EXPERT_ROWS):
            pl.when(nv_ref[i] > r0)(functools.partial(ffn, r0))


def _experts(block_e, n_used, first, n_valid, xs, wgu, bgu, wd, bd, layer):
    n_rows, wp = xs.shape
    _, ne, d, ff2 = wgu.shape
    assert d == ff2 // 2
    nblk = n_rows // EXPERT_BLOCK

    def xmap(i, be, nb, fi, nv):
        return (jnp.minimum(i, nb[0] - 1), 0)

    emap = lambda i, be, nb, fi, nv: (layer, be[i], 0, 0)
    grid_spec = pltpu.PrefetchScalarGridSpec(
        num_scalar_prefetch=4, grid=(nblk,),
        in_specs=[pl.BlockSpec((EXPERT_BLOCK, wp), xmap),
                  pl.BlockSpec((1, 1, d, ff2), emap), pl.BlockSpec((1, 1, 1, ff2), emap),
                  pl.BlockSpec((1, 1, ff2 // 2, d), emap), pl.BlockSpec((1, 1, 1, d), emap)],
        out_specs=pl.BlockSpec((EXPERT_BLOCK, wp), xmap),
        scratch_shapes=[pltpu.VMEM((d, ff2), BF16), pltpu.VMEM((ff2 // 2, d), BF16)])
    depth = wgu.shape[0]
    return pl.pallas_call(
        _expert_kernel,
        out_shape=jax.ShapeDtypeStruct((n_rows, wp), I32),
        grid_spec=grid_spec,
        compiler_params=_cparams(("arbitrary",)),
    )(block_e, n_used, first, n_valid, xs, wgu, bgu.reshape(depth, ne, 1, ff2), wd,
      bd.reshape(depth, ne, 1, d))


def _moe(ys, w_out, g_m, h, g, shift, scale, gf, fnorm, wr, br, wgu, bgu, wd, bd, layer, seq, final):
    m, d = h.shape
    tm = MOE_TILE
    wr_p = jnp.zeros((d, LANES), F32).at[:, :N_EXPERTS].set(wr)
    br_p = jnp.zeros((1, LANES), F32).at[0, :N_EXPERTS].set(br)
    wr_hi = wr_p.astype(BF16)
    wr_split = jnp.concatenate([wr_hi, (wr_p - wr_hi.astype(F32)).astype(BF16)], axis=1)
    h, up, topi, gates, rank, cnt = _router(ys, w_out, h, g_m, g, shift, scale, wr_split, br_p, seq, tm)

    counts = cnt[:, 0].astype(I32)
    padded = (counts + EXPERT_BLOCK - 1) // EXPERT_BLOCK * EXPERT_BLOCK
    pad_end = jnp.cumsum(padded)
    pad_start = pad_end - padded
    nblk = m * TOP_K // EXPERT_BLOCK + N_EXPERTS
    n_rows = nblk * EXPERT_BLOCK
    n_used = pad_end[-1:] // EXPERT_BLOCK
    blk = jnp.arange(nblk, dtype=I32)
    blk_c = jnp.minimum(blk, n_used - 1)
    block_e = jnp.minimum(jnp.sum(blk_c[:, None] * EXPERT_BLOCK >= pad_end[None, :], axis=1),
                          N_EXPERTS - 1).astype(I32)
    first = jnp.concatenate([jnp.ones((1,), I32), (block_e[1:] != block_e[:-1]).astype(I32)])
    real_end = jnp.sum(jnp.where(block_e[:, None] == jnp.arange(N_EXPERTS, dtype=I32)[None, :],
                                 (pad_start + counts)[None, :], 0), axis=1)
    n_valid = jnp.clip(real_end - blk_c * EXPERT_BLOCK, 0, EXPERT_BLOCK).astype(I32)

    dest = _dest_rows(pad_start, topi, rank, min(m, DEST_TILE))
    xs = _sc_scatter_rows(up, dest, n_rows)
    y = _experts(block_e, n_used.astype(I32), first, n_valid, xs, wgu, bgu, wd, bd, layer)
    y4 = _sc_gather_rows(y, dest.reshape(-1)).reshape(TOP_K, m, d // 2)
    return _combine_dense(y4, gates.T, h, gf, fnorm, seq, min(seq, COMBINE_TILE), final)


def _block_diag(w, group):
    nb, b, _ = w.shape
    per = group // b
    wg = w.reshape(nb // per, per, b, b)
    dense = jnp.einsum("gnde,nm->gndme", wg, jnp.eye(per, dtype=w.dtype))
    return dense.reshape(nb // per, group, group)


def kernel(x, c, mod_w, mod_b, norm_mix, norm_ffn, ev_w_in, ev_lru_conv_w, ev_lru_conv_b, ev_lru_w_r, ev_lru_b_r, ev_lru_w_i, ev_lru_b_i, ev_lru_lambda, ev_ml_conv_w, ev_ml_conv_b, ev_ml_w_q, ev_ml_w_k, ev_ml_w_v, ev_ml_w_ig, ev_ml_b_ig, ev_ml_w_fg, ev_ml_b_fg, ev_ml_norm, ev_ml_skip, ev_w_out, od_w_in, od_conv_w, od_conv_b, od_dt_bias, od_a_log, od_d, od_norm, od_w_out, moe_router_w, moe_router_b, moe_w_gu, moe_b_gu, moe_w_down, moe_b_down, final_norm):
    bsz, seq, d = x.shape
    depth = mod_w.shape[0]
    m = bsz * seq
    mod = _modulation(c, mod_w, mod_b)
    h = x.reshape(m, d).astype(F32)
    for layer in range(depth):
        sh_m, sc_m, g_m, sh_f, sc_f, g_f = (mod[layer, i] for i in range(6))
        j = layer // 2
        if layer % 2 == 0:
            w = ev_lru_lambda.shape[1]
            w_in = ev_w_in[j].astype(BF16)
            proj = _inproj(h, norm_mix[layer], sh_m, sc_m, w_in, None, [w_in.shape[1]], seq,
                           min(seq, PROJ_TILE_EVEN))[0]
            lru_p = dict(conv_w=ev_lru_conv_w[j], conv_b=ev_lru_conv_b[j].reshape(1, w),
                         w_r=ev_lru_w_r[j].astype(BF16), b_r=ev_lru_b_r[j].reshape(1, w),
                         w_i=ev_lru_w_i[j].astype(BF16), b_i=ev_lru_b_i[j].reshape(1, w),
                         lam=ev_lru_lambda[j].reshape(1, w))
            ya = _lru(proj, lru_p, bsz, seq, LRU_TILE)
            wg = jnp.zeros((3 * w, LANES), F32)
            wg = wg.at[:, :ML_HEADS].set(ev_ml_w_ig[j]).at[:, ML_HEADS:2 * ML_HEADS].set(ev_ml_w_fg[j])
            bg = jnp.zeros((1, LANES), F32)
            bg = bg.at[0, :ML_HEADS].set(ev_ml_b_ig[j]).at[0, ML_HEADS:2 * ML_HEADS].set(ev_ml_b_fg[j])
            ml_p = dict(conv_w=ev_ml_conv_w[j], conv_b=ev_ml_conv_b[j].reshape(1, w),
                        w_q=_block_diag(ev_ml_w_q[j], LANES).astype(BF16),
                        w_k=_block_diag(ev_ml_w_k[j], LANES).astype(BF16),
                        w_v=_block_diag(ev_ml_w_v[j], LANES).astype(BF16),
                        w_g=wg.astype(BF16), b_g=bg,
                        norm=ev_ml_norm[j].reshape(1, w), skip=ev_ml_skip[j].reshape(1, w))
            yb = _mlstm(proj, ml_p, bsz, seq)
            ys, w_out = [ya, yb], ev_w_out[j].astype(BF16)
        else:
            inner = od_norm.shape[1]
            heads = od_dt_bias.shape[1]
            conv_ch = od_conv_w.shape[2]
            w_in = od_w_in[j]
            wdt = jnp.zeros((d, LANES), F32).at[:, :heads].set(w_in[:, inner + conv_ch:])
            z, xbc, dt_raw = _inproj(h, norm_mix[layer], sh_m, sc_m, w_in.astype(BF16),
                                     wdt.astype(BF16), [inner, conv_ch], seq, PROJ_TILE)
            pad = lambda v: jnp.zeros((1, LANES), F32).at[0, :heads].set(v)
            ssd_p = dict(conv_w=od_conv_w[j], conv_b=od_conv_b[j].reshape(1, conv_ch),
                         dt_bias=pad(od_dt_bias[j]), a_log=pad(od_a_log[j]),
                         d_skip=jnp.repeat(od_d[j], SSD_HEAD_DIM).reshape(1, inner),
                         norm=od_norm[j].reshape(1, inner))
            y = _ssd(z, xbc, dt_raw, ssd_p, bsz, seq)
            ys, w_out = [y], od_w_out[j].astype(BF16)
        h = _moe(ys, w_out, g_m, h, norm_ffn[layer], sh_f, sc_f, g_f, final_norm,
                 moe_router_w[layer], moe_router_b[layer],
                 moe_w_gu, moe_b_gu, moe_w_down, moe_b_down, layer, seq, final=(layer == depth - 1))
    return h.reshape(bsz, seq, d)
```

```python
import functools

import jax
import jax.numpy as jnp
from jax import lax
from jax.experimental import pallas as pl
from jax.experimental.pallas import tpu as pltpu
from jax.experimental.pallas import tpu_sc as plsc

F32 = jnp.float32
BF16 = jnp.bfloat16
I32 = jnp.int32
HIGHEST = lax.Precision.HIGHEST

EPS = 1e-6
CONV_WIDTH = 4
LANES = 128
SUBLANES = 8
LRU_HEADS = 8
LRU_C = 8.0
ML_HEADS = 8
CHUNK = 128
PROJ_TILE = 512
PROJ_TILE_EVEN = 1024
LRU_TILE = 512
MOE_TILE = 1024
COMBINE_TILE = 1024
DEST_TILE = 8192
SSD_HEAD_DIM = 64
SSD_GROUPS = 8
SSD_STATE = 128
N_EXPERTS = 32
TOP_K = 4
SWIGLU_ALPHA = 1.702
SWIGLU_LIMIT = 7.0
EXPERT_BLOCK = 2048
EXPERT_ROWS = 512
SEQ_PER_STEP = 2
CONV_COLS = 512
VMEM_LIMIT = 56 * 1024 * 1024


def _cparams(sem, **kw):
    return pltpu.CompilerParams(dimension_semantics=sem, vmem_limit_bytes=VMEM_LIMIT, **kw)


def _silu(x):
    half = 0.5 * x
    return half + half * jnp.tanh(half)


def _log_sigmoid(x):
    return jnp.minimum(x, 0.0) - jnp.log1p(jnp.exp(-jnp.abs(x)))


def _softplus(x):
    return jnp.maximum(x, 0.0) + jnp.log1p(jnp.exp(-jnp.abs(x)))


def _dot(a, b, **kw):
    return jnp.dot(a, b, preferred_element_type=F32, **kw)


def _dot_nt(a, b):
    return lax.dot_general(a, b, (((1,), (1,)), ((), ())), preferred_element_type=F32)


def _pack_pairs(x):
    w = x.shape[1] // 2
    lo = lax.bitcast_convert_type(x[:, :w].astype(BF16).astype(F32), I32)
    hi = lax.bitcast_convert_type(x[:, w:].astype(BF16).astype(F32), I32)
    return lax.shift_right_logical(lo, 16) | (hi & jnp.int32(-65536))


def _unpack_pairs(p):
    lo = lax.bitcast_convert_type(lax.shift_left(p, 16), F32)
    hi = lax.bitcast_convert_type(p & jnp.int32(-65536), F32)
    return jnp.concatenate([lo, hi], axis=1)


def _norm_mod(h, g, shift, scale):
    y = h * lax.rsqrt(jnp.mean(h * h, axis=-1, keepdims=True) + EPS)
    return (y * g) * (1.0 + scale) + shift


def _causal_conv(x, tail_ref, w_ref, b_ref, sl):
    t = x.shape[0]
    tail = tail_ref[:, sl]
    row8 = lax.broadcasted_iota(I32, tail.shape, 0)
    out = b_ref[:, sl] + x * w_ref[CONV_WIDTH - 1:CONV_WIDTH, sl]
    for k in range(1, CONV_WIDTH):
        xs = pltpu.roll(x, k, axis=0)
        first = jnp.where(row8 < k, pltpu.roll(tail, k, axis=0), xs[:SUBLANES])
        xs = jnp.concatenate([first, xs[SUBLANES:]], axis=0)
        out = out + xs * w_ref[CONV_WIDTH - 1 - k:CONV_WIDTH - k, sl]
    tail_ref[:, sl] = x[t - SUBLANES:]
    return out


def _shift_matrix(t):
    r = lax.broadcasted_iota(I32, ((CONV_WIDTH - 1) * t, t), 0)
    c = lax.broadcasted_iota(I32, ((CONV_WIDTH - 1) * t, t), 1)
    src = (r & (t - 1)) - lax.shift_right_logical(r, t.bit_length() - 1) - 1
    return (src == c).astype(BF16)


def _causal_conv_shifted(x, shifted, tail_ref, w_ref, b_ref, sl):
    t = x.shape[0]
    tail = tail_ref[:, sl]
    row8 = lax.broadcasted_iota(I32, tail.shape, 0)
    out = b_ref[:, sl] + x * w_ref[CONV_WIDTH - 1:CONV_WIDTH, sl]
    head = jnp.zeros_like(tail)
    for k in range(1, CONV_WIDTH):
        wk = w_ref[CONV_WIDTH - 1 - k:CONV_WIDTH - k, sl]
        out = out + shifted[(k - 1) * t:k * t] * wk
        head = head + jnp.where(row8 < k, pltpu.roll(tail, k, axis=0), 0.0) * wk
    tail_ref[:, sl] = x[t - SUBLANES:]
    return jnp.concatenate([out[:SUBLANES] + head, out[SUBLANES:]], axis=0)


def _mod_kernel(c_ref, w_ref, b_ref, o_ref):
    cond = _silu(c_ref[...])
    o_ref[0, 0] = _dot(cond, w_ref[0], precision=HIGHEST) + b_ref[0, 0]


def _modulation(c, mod_w, mod_b):
    depth, d, _ = mod_w.shape
    bsz = c.shape[0]
    out = pl.pallas_call(
        _mod_kernel,
        out_shape=jax.ShapeDtypeStruct((depth, 6, bsz, d), F32),
        grid=(depth, 6),
        in_specs=[pl.BlockSpec((bsz, d), lambda l, j: (0, 0)),
                  pl.BlockSpec((1, d, d), lambda l, j: (l, 0, j)),
                  pl.BlockSpec((1, 1, 1, d), lambda l, j: (l, j, 0, 0))],
        out_specs=pl.BlockSpec((1, 1, bsz, d), lambda l, j: (l, j, 0, 0)),
        compiler_params=_cparams(("parallel", "parallel")),
    )(c.astype(F32), mod_w, mod_b.reshape(depth, 6, 1, d))
    return out.reshape(depth, 6, bsz, 1, d)


def _inproj_kernel(h_ref, g_ref, sh_ref, sc_ref, w_ref, *rest, n_chunk, with_dt):
    if with_dt:
        wdt_ref, *o_refs, odt_ref = rest
    else:
        o_refs = rest
    u = _norm_mod(h_ref[...], g_ref[...], sh_ref[0], sc_ref[0]).astype(BF16)
    off = 0
    for o_ref in o_refs:
        for n0 in range(0, o_ref.shape[1], n_chunk):
            o_ref[:, n0:n0 + n_chunk] = _dot(u, w_ref[:, off + n0:off + n0 + n_chunk]).astype(o_ref.dtype)
        off += o_ref.shape[1]
    if with_dt:
        odt_ref[...] = _dot(u, wdt_ref[...])


def _inproj(h, g, shift, scale, w, wdt, splits, seq, tm):
    m, d = h.shape
    n = sum(splits)
    assert n <= w.shape[1] and n % LANES == 0
    tiles_per_seq = seq // tm
    bmap = lambda i: (i // tiles_per_seq, 0, 0)
    in_specs = [pl.BlockSpec((tm, d), lambda i: (i, 0)),
                pl.BlockSpec((1, d), lambda i: (0, 0)),
                pl.BlockSpec((1, 1, d), bmap),
                pl.BlockSpec((1, 1, d), bmap),
                pl.BlockSpec((d, n), lambda i: (0, 0), pipeline_mode=pl.Buffered(1))]
    out_shape = [jax.ShapeDtypeStruct((m, s), BF16) for s in splits]
    out_specs = [pl.BlockSpec((tm, s), lambda i: (i, 0)) for s in splits]
    args = [h, g.reshape(1, d), shift, scale, w]
    if wdt is not None:
        in_specs.append(pl.BlockSpec((d, LANES), lambda i: (0, 0)))
        out_shape.append(jax.ShapeDtypeStruct((m, LANES), F32))
        out_specs.append(pl.BlockSpec((tm, LANES), lambda i: (i, 0)))
        args.append(wdt)
    return pl.pallas_call(
        functools.partial(_inproj_kernel, n_chunk=1024, with_dt=wdt is not None),
        out_shape=out_shape, grid=(m // tm,), in_specs=in_specs, out_specs=out_specs,
        compiler_params=_cparams(("parallel",)),
    )(*args)


def _lru_kernel(xa_ref, ga_ref, cw_ref, cb_ref, wr_ref, br_ref, wi_ref, bi_ref, lam_ref,
                o_ref, tail_ref, hc_ref):
    @pl.when(pl.program_id(1) == 0)
    def _():
        tail_ref[...] = jnp.zeros_like(tail_ref)
        hc_ref[...] = jnp.zeros_like(hc_ref)

    t = xa_ref.shape[0]
    row_in_group = lax.broadcasted_iota(I32, (t, LANES), 0) % SUBLANES
    steps = [s for s in (1, 2, 4) if s < SUBLANES]
    masks = [row_in_group >= s for s in steps]
    for hh in range(LRU_HEADS):
        sl = slice(hh * LANES, (hh + 1) * LANES)
        xc = _causal_conv(xa_ref[:, sl].astype(F32), tail_ref, cw_ref, cb_ref, sl)
        xcb = xc.astype(BF16)
        r = jax.nn.sigmoid(_dot(xcb, wr_ref[hh]) + br_ref[:, sl])
        i = jax.nn.sigmoid(_dot(xcb, wi_ref[hh]) + bi_ref[:, sl])
        log_a = LRU_C * r * _log_sigmoid(lam_ref[:, sl])
        a = jnp.exp(log_a)
        th = jnp.tanh(log_a)
        n2 = -2.0 * th
        root = jnp.where(n2 > 0.0, n2 * lax.rsqrt(n2), 0.0)
        u = (root * lax.rsqrt(1.0 - th)) * (i * xc)
        def roll_in_groups(v, s):
            v3 = v.reshape(t // SUBLANES, SUBLANES, LANES)
            return pltpu.roll(v3, s, axis=1).reshape(t, LANES)

        for s, m in zip(steps, masks):
            u = jnp.where(m, u + a * roll_in_groups(u, s), u)
            a = jnp.where(m, a * roll_in_groups(a, s), a)
        carry = hc_ref[:, sl]
        groups = []
        for r0 in range(0, t, SUBLANES):
            hg = u[r0:r0 + SUBLANES] + a[r0:r0 + SUBLANES] * carry
            carry = hg[SUBLANES - 1:SUBLANES]
            groups.append(hg)
        hc_ref[:, sl] = carry
        h = jnp.concatenate(groups, axis=0)
        ga = ga_ref[:, sl].astype(F32)
        o_ref[:, sl] = (h * jax.nn.gelu(ga, approximate=True)).astype(o_ref.dtype)


def _lru(proj, p, bsz, seq, tm):
    m = proj.shape[0]
    w = LRU_HEADS * LANES
    nt = seq // tm
    vec = lambda: pl.BlockSpec((1, w), lambda b, j: (0, 0))
    return pl.pallas_call(
        _lru_kernel,
        out_shape=jax.ShapeDtypeStruct((m, w), BF16),
        grid=(bsz, nt),
        in_specs=[pl.BlockSpec((tm, w), lambda b, j: (b * nt + j, 0)),
                  pl.BlockSpec((tm, w), lambda b, j: (b * nt + j, 1)),
                  pl.BlockSpec((CONV_WIDTH, w), lambda b, j: (0, 0)), vec(),
                  pl.BlockSpec((LRU_HEADS, LANES, LANES), lambda b, j: (0, 0, 0)), vec(),
                  pl.BlockSpec((LRU_HEADS, LANES, LANES), lambda b, j: (0, 0, 0)), vec(), vec()],
        out_specs=pl.BlockSpec((tm, w), lambda b, j: (b * nt + j, 0)),
        scratch_shapes=[pltpu.VMEM((SUBLANES, w), F32), pltpu.VMEM((1, w), F32)],
        compiler_params=_cparams(("parallel", "arbitrary")),
    )(proj, proj, p["conv_w"], p["conv_b"], p["w_r"], p["b_r"], p["w_i"], p["b_i"], p["lam"])


def _mlstm_kernel(xb_ref, zb_ref, cw_ref, cb_ref, wq_ref, wk_ref, wv_ref, wg_ref, bg_ref,
                  nw_ref, sk_ref, o_ref, tail_ref, qkv_ref, xc_ref, caug_ref, m_ref):
    @pl.when(pl.program_id(1) == 0)
    def _():
        tail_ref[...] = jnp.zeros_like(tail_ref)
        caug_ref[...] = jnp.zeros_like(caug_ref)
        m_ref[...] = jnp.full(m_ref.shape, -jnp.inf, F32)

    for s in range(xb_ref.shape[0]):
        for c0 in range(0, xb_ref.shape[1], CHUNK):
            rows = pl.ds(c0, CHUNK)
            _mlstm_chunk(xb_ref.at[s, rows], zb_ref.at[s, rows], cw_ref, cb_ref, wq_ref, wk_ref, wv_ref,
                         wg_ref, bg_ref, nw_ref, sk_ref, o_ref.at[s, rows], tail_ref.at[s], qkv_ref.at[s],
                         xc_ref.at[s], caug_ref.at[s], m_ref.at[s])


def _mlstm_chunk(xb_ref, zb_ref, cw_ref, cb_ref, wq_ref, wk_ref, wv_ref, wg_ref, bg_ref,
                 nw_ref, sk_ref, o_ref, tail_ref, qkv_ref, xc_ref, caug_ref, m_ref):
    L = CHUNK
    width = ML_HEADS * LANES
    scale = LANES ** -0.5
    for hh in range(ML_HEADS):
        sl = slice(hh * LANES, (hh + 1) * LANES)
        xb = xb_ref[:, sl].astype(F32)
        xc = _silu(_causal_conv(xb, tail_ref, cw_ref, cb_ref, sl))
        xc_ref[:, sl] = xc
        xcb = xc.astype(BF16)
        qkv_ref[:, sl] = _dot(xcb, wq_ref[hh]).astype(BF16)
        qkv_ref[:, width + hh * LANES:width + (hh + 1) * LANES] = _dot(xcb, wk_ref[hh]).astype(BF16)
        qkv_ref[:, 2 * width + hh * LANES:2 * width + (hh + 1) * LANES] = (
            _dot(xb.astype(BF16), wv_ref[hh]).astype(BF16))

    gates = _dot(qkv_ref[...], wg_ref[...]) + bg_ref[...]
    rowi = lax.broadcasted_iota(I32, (L, L), 0)
    coli = lax.broadcasted_iota(I32, (L, L), 1)
    causal = rowi >= coli
    lf = jnp.where((coli >= ML_HEADS) & (coli < 2 * ML_HEADS), _log_sigmoid(gates), 0.0)
    tri = causal.astype(BF16)
    lf_hi = lf.astype(BF16)
    lf_mid = (lf - lf_hi.astype(F32)).astype(BF16)
    lf_lo = (lf - lf_hi.astype(F32) - lf_mid.astype(F32)).astype(BF16)
    gcum = _dot(tri, lf_hi) + (_dot(tri, lf_mid) + _dot(tri, lf_lo))
    x_col = jnp.where(coli < ML_HEADS, gates, gcum)
    x_row = x_col.T
    ones = jnp.ones((L, LANES), BF16)
    heads = range(ML_HEADS)
    hsl = [slice(hh * LANES, (hh + 1) * LANES) for hh in heads]

    qs = [qkv_ref[:, hsl[hh]] for hh in heads]
    ks = [qkv_ref[:, width + hh * LANES:width + (hh + 1) * LANES] for hh in heads]
    vaugs = [jnp.concatenate([qkv_ref[:, 2 * width + hh * LANES:2 * width + (hh + 1) * LANES], ones], axis=1)
             for hh in heads]
    scores = [_dot_nt(qs[hh], ks[hh]) * scale for hh in heads]
    ics = [jnp.broadcast_to(x_col[:, hh:hh + 1], (L, LANES)) for hh in heads]
    gcs = [jnp.broadcast_to(x_col[:, ML_HEADS + hh:ML_HEADS + hh + 1], (L, LANES)) for hh in heads]
    irs = [x_row[hh:hh + 1, :] for hh in heads]
    grs = [x_row[ML_HEADS + hh:ML_HEADS + hh + 1, :] for hh in heads]
    mps = [m_ref[hh] for hh in heads]
    dmats = [jnp.where(causal, gcs[hh] - grs[hh] + irs[hh], -jnp.inf) for hh in heads]
    m_inters = [mps[hh] + gcs[hh] for hh in heads]
    m_ts = [jnp.maximum(m_inters[hh], jnp.max(dmats[hh], axis=1, keepdims=True)) for hh in heads]
    qks = [(scores[hh] * jnp.exp(dmats[hh] - m_ts[hh])).astype(BF16) for hh in heads]
    caugs = [caug_ref[hh] for hh in heads]
    w_inters = [jnp.exp(m_inters[hh] - m_ts[hh]) for hh in heads]
    nds = [_dot(qks[hh], vaugs[hh])
           + jnp.concatenate([w_inters[hh], w_inters[hh]], axis=1) * _dot(qs[hh], caugs[hh].astype(BF16))
           for hh in heads]

    g_lasts = [gcs[hh][L - 1:L, :] for hh in heads]
    m_news = [jnp.maximum(mps[hh] + g_lasts[hh],
                          jnp.max(g_lasts[hh] - grs[hh] + irs[hh], axis=1, keepdims=True)) for hh in heads]
    for hh in heads:
        ws = jnp.exp(g_lasts[hh] - gcs[hh] + ics[hh] - m_news[hh])
        wc = jnp.exp(mps[hh] + g_lasts[hh] - m_news[hh])
        kw_t = (ks[hh].astype(F32) * (ws * scale)).T.astype(BF16)
        caug_ref[hh] = jnp.concatenate([wc, wc], axis=1) * caugs[hh] + _dot(kw_t, vaugs[hh])
        m_ref[hh] = m_news[hh]

    hvals = [nds[hh][:, :LANES] / jnp.maximum(jnp.abs(nds[hh][:, LANES:]), jnp.exp(-m_ts[hh]))
             for hh in heads]
    mus = [jnp.mean(hvals[hh], axis=1, keepdims=True) for hh in heads]
    dvs = [hvals[hh] - mus[hh] for hh in heads]
    variances = [jnp.mean(dvs[hh] * dvs[hh], axis=1, keepdims=True) for hh in heads]
    for hh in heads:
        sl = hsl[hh]
        hn = dvs[hh] * lax.rsqrt(variances[hh] + EPS) * nw_ref[:, sl]
        zb = zb_ref[:, sl].astype(F32)
        o_ref[:, sl] = ((hn + sk_ref[:, sl] * xc_ref[:, sl]) * _silu(zb)).astype(o_ref.dtype)


def _mlstm(proj, p, bsz, seq):
    m = proj.shape[0]
    w = ML_HEADS * LANES
    sps = 1
    cps = next(c for c in (4, 2, 1) if seq % (c * CHUNK) == 0)
    nt = seq // (cps * CHUNK)
    vec = lambda: pl.BlockSpec((1, w), lambda b, j: (0, 0))
    blk = lambda: pl.BlockSpec((ML_HEADS, LANES, LANES), lambda b, j: (0, 0, 0))
    tile = lambda col: pl.BlockSpec((sps, cps * CHUNK, w), lambda b, j: (b, j, col))
    proj3 = proj.reshape(bsz, seq, proj.shape[1])
    out = pl.pallas_call(
        _mlstm_kernel,
        out_shape=jax.ShapeDtypeStruct((bsz, seq, w), BF16),
        grid=(bsz // sps, nt),
        in_specs=[tile(2), tile(3),
                  pl.BlockSpec((CONV_WIDTH, w), lambda b, j: (0, 0)), vec(),
                  blk(), blk(), blk(),
                  pl.BlockSpec((3 * w, LANES), lambda b, j: (0, 0)),
                  pl.BlockSpec((1, LANES), lambda b, j: (0, 0)),
                  vec(), vec()],
        out_specs=tile(0),
        scratch_shapes=[pltpu.VMEM((sps, SUBLANES, w), F32),
                        pltpu.VMEM((sps, CHUNK, 3 * w), BF16),
                        pltpu.VMEM((sps, CHUNK, w), F32),
                        pltpu.VMEM((sps, ML_HEADS, LANES, 2 * LANES), F32),
                        pltpu.VMEM((sps, ML_HEADS, 1, LANES), F32)],
        compiler_params=_cparams(("parallel", "arbitrary")),
    )(proj3, proj3, p["conv_w"], p["conv_b"], p["w_q"], p["w_k"], p["w_v"], p["w_g"], p["b_g"],
      p["norm"], p["skip"])
    return out.reshape(m, w)


def _ssd_kernel(z_ref, xbc_ref, dt_ref, cw_ref, cb_ref, dtb_ref, alog_ref, dsk_ref, nw_ref,
                o_ref, tail_ref, act_ref, st_ref):
    @pl.when(pl.program_id(1) == 0)
    def _():
        tail_ref[...] = jnp.zeros_like(tail_ref)
        st_ref[...] = jnp.zeros_like(st_ref)

    for c0 in range(0, z_ref.shape[1], CHUNK):
        rows = pl.ds(c0, CHUNK)
        for s in range(z_ref.shape[0]):
            _ssd_chunk(z_ref.at[s, rows], xbc_ref.at[s, rows], dt_ref.at[s, rows], cw_ref, cb_ref, dtb_ref,
                       alog_ref, dsk_ref, nw_ref, o_ref.at[s, rows], tail_ref.at[s], act_ref.at[s],
                       st_ref.at[s])


def _ssd_chunk(z_ref, xbc_ref, dt_ref, cw_ref, cb_ref, dtb_ref, alog_ref, dsk_ref, nw_ref,
               o_ref, tail_ref, act_ref, st_ref):
    L = CHUNK
    inner = o_ref.shape[1]
    gw = inner // SSD_GROUPS
    hpg = gw // SSD_HEAD_DIM
    b_off = inner
    c_off = inner + SSD_GROUPS * SSD_STATE
    shift = _shift_matrix(L)
    for c0 in range(0, xbc_ref.shape[1], CONV_COLS):
        shifted = _dot(shift, xbc_ref[:, c0:c0 + CONV_COLS])
        for l0 in range(0, CONV_COLS, LANES):
            sl = slice(c0 + l0, c0 + l0 + LANES)
            act_ref[:, sl] = _silu(_causal_conv_shifted(
                xbc_ref[:, sl].astype(F32), shifted[:, l0:l0 + LANES], tail_ref, cw_ref, cb_ref, sl))

    rowi = lax.broadcasted_iota(I32, (L, L), 0)
    coli = lax.broadcasted_iota(I32, (L, L), 1)
    causal = rowi >= coli
    dt = _softplus(dt_ref[...] + dtb_ref[...])
    da = dt * (-jnp.exp(alog_ref[...]))
    tri = causal.astype(BF16)
    da_hi = da.astype(BF16)
    da_mid = (da - da_hi.astype(F32)).astype(BF16)
    da_lo = (da - da_hi.astype(F32) - da_mid.astype(F32)).astype(BF16)
    a = _dot(tri, da_hi) + (_dot(tri, da_mid) + _dot(tri, da_lo))
    a_t = a.T
    hpt = LANES // SSD_HEAD_DIM
    lane = lax.broadcasted_iota(I32, (L, LANES), 1)

    def over_heads(tiles):
        cols = []
        for c0 in range(0, hpg, hpt):
            out = tiles[c0 + hpt - 1]
            for j in range(hpt - 2, -1, -1):
                out = jnp.where(lane < (j + 1) * SSD_HEAD_DIM, tiles[c0 + j], out)
            cols.append(out)
        return jnp.concatenate(cols, axis=1)

    for g in range(SSD_GROUPS):
        gsl = slice(g * gw, (g + 1) * gw)
        xg = act_ref[:, gsl]
        bg = act_ref[:, b_off + g * SSD_STATE:b_off + (g + 1) * SSD_STATE]
        cg_ = act_ref[:, c_off + g * SSD_STATE:c_off + (g + 1) * SSD_STATE].astype(BF16)
        cb = _dot_nt(cg_, bg.astype(BF16))
        state = st_ref[g]
        hds = [g * hpg + jj for jj in range(hpg)]
        a_bs = [jnp.broadcast_to(a[:, hd:hd + 1], (L, LANES)) for hd in hds]
        dt_bs = [jnp.broadcast_to(dt[:, hd:hd + 1], (L, LANES)) for hd in hds]
        ea_x = over_heads([jnp.exp(a_b) for a_b in a_bs])
        to_end_x = over_heads([jnp.exp(a_b[L - 1:L, :] - a_b) for a_b in a_bs])
        xdt = xg * over_heads(dt_bs)
        inter = _dot(cg_, state.astype(BF16)) * ea_x
        acc = [inter[:, c0:c0 + LANES] for c0 in range(0, gw, LANES)]
        for jj in range(hpg):
            seg = jnp.where(causal, a_bs[jj] - a_t[hds[jj]:hds[jj] + 1, :], -jnp.inf)
            w = (cb * jnp.exp(seg)).astype(BF16)
            c, j = divmod(jj, hpt)
            in_head = (lane >= j * SSD_HEAD_DIM) & (lane < (j + 1) * SSD_HEAD_DIM)
            x_head = jnp.where(in_head, xdt[:, c * LANES:(c + 1) * LANES], 0.0).astype(BF16)
            acc[c] = acc[c] + _dot(w, x_head)
        acc = jnp.concatenate(acc, axis=1)
        y = (acc + dsk_ref[:, gsl] * xg) * _silu(z_ref[:, gsl].astype(F32))
        y = y * lax.rsqrt(jnp.mean(y * y, axis=1, keepdims=True) + EPS) * nw_ref[:, gsl]
        o_ref[:, gsl] = y.astype(o_ref.dtype)
        xw = (xdt * to_end_x).astype(BF16)
        st_ref[g] = ea_x[L - 1:L, :] * state + _dot(bg.T.astype(BF16), xw)


def _ssd(z, xbc, dt_raw, p, bsz, seq):
    m, inner = z.shape
    conv_ch = xbc.shape[1]
    sps = SEQ_PER_STEP if bsz % SEQ_PER_STEP == 0 else 1
    cps = 2 if seq % (2 * CHUNK) == 0 else 1
    nt = seq // (cps * CHUNK)
    vec = lambda n: pl.BlockSpec((1, n), lambda b, j: (0, 0))
    tile = lambda n: pl.BlockSpec((sps, cps * CHUNK, n), lambda b, j: (b, j, 0))
    out = pl.pallas_call(
        _ssd_kernel,
        out_shape=jax.ShapeDtypeStruct((bsz, seq, inner), BF16),
        grid=(bsz // sps, nt),
        in_specs=[tile(inner), tile(conv_ch), tile(LANES),
                  pl.BlockSpec((CONV_WIDTH, conv_ch), lambda b, j: (0, 0)), vec(conv_ch),
                  vec(LANES), vec(LANES), vec(inner), vec(inner)],
        out_specs=tile(inner),
        scratch_shapes=[pltpu.VMEM((sps, SUBLANES, conv_ch), F32),
                        pltpu.VMEM((sps, CHUNK, conv_ch), F32),
                        pltpu.VMEM((sps, SSD_GROUPS, SSD_STATE, inner // SSD_GROUPS), F32)],
        compiler_params=_cparams(("parallel", "arbitrary")),
    )(z.reshape(bsz, seq, inner), xbc.reshape(bsz, seq, conv_ch), dt_raw.reshape(bsz, seq, LANES),
      p["conv_w"], p["conv_b"], p["dt_bias"], p["a_log"], p["d_skip"], p["norm"])
    return out.reshape(m, inner)


def _router_kernel(*refs, n_in):
    y_refs, w_refs = refs[:n_in], refs[n_in:2 * n_in]
    (h_ref, gm_ref, g_ref, sh_ref, sc_ref, wr_ref, br_ref,
     hmid_ref, up_ref, topi_ref, gate_ref, rank_ref, cnt_ref, carry_ref) = refs[2 * n_in:]

    @pl.when(pl.program_id(0) == 0)
    def _():
        carry_ref[...] = jnp.zeros_like(carry_ref)

    tm = h_ref.shape[0]
    acc = _dot(y_refs[0][...], w_refs[0][...])
    for y_ref, w_ref in zip(y_refs[1:], w_refs[1:]):
        acc = acc + _dot(y_ref[...], w_ref[...])
    hmid = h_ref[...] + gm_ref[0] * acc
    hmid_ref[...] = hmid
    u = _norm_mod(hmid, g_ref[...], sh_ref[0], sc_ref[0])
    up_ref[...] = _pack_pairs(u)
    u_hi = u.astype(BF16)
    u_lo = (u - u_hi.astype(F32)).astype(BF16)
    hi_both = _dot(u_hi, wr_ref[...])
    logits = (hi_both[:, :LANES] + (_dot(u_lo, wr_ref[:, :LANES]) + hi_both[:, LANES:])
              + br_ref[...])
    lt = jnp.concatenate([logits[r0:r0 + LANES].T for r0 in range(0, tm, LANES)], axis=1)
    l = lt[:N_EXPERTS]
    e_iota = lax.broadcasted_iota(I32, (N_EXPERTS, tm), 0).astype(F32)
    vals, idxs, hots = [], [], []
    for _ in range(TOP_K):
        mx = jnp.max(l, axis=0, keepdims=True)
        idx = jnp.min(jnp.where(l == mx, e_iota, float(N_EXPERTS)), axis=0, keepdims=True)
        hot = e_iota == idx
        l = jnp.where(hot, -jnp.inf, l)
        vals.append(mx)
        idxs.append(idx)
        hots.append(hot)
    exps = [jnp.exp(v - vals[0]) for v in vals]
    den = exps[0] + exps[1] + exps[2] + exps[3]
    gate_ref[...] = jnp.concatenate([e / den for e in exps], axis=0)
    topi_ref[...] = jnp.concatenate(idxs, axis=0).astype(I32)

    sel = jnp.zeros((N_EXPERTS, tm), F32)
    for hot in hots:
        sel = jnp.where(hot, 1.0, sel)
    r_i = lax.broadcasted_iota(I32, (tm, tm), 0)
    c_i = lax.broadcasted_iota(I32, (tm, tm), 1)
    before = (r_i < c_i).astype(BF16)
    carry = carry_ref[:, 0:1]
    cum = _dot(sel.astype(BF16), before) + carry
    rank_ref[...] = jnp.concatenate(
        [jnp.sum(jnp.where(hot, cum, 0.0), axis=0, keepdims=True) for hot in hots], axis=0).astype(I32)
    total = carry + jnp.sum(sel, axis=1, keepdims=True)
    carry_ref[...] = jnp.broadcast_to(total, carry_ref.shape)
    cnt_ref[...] = jnp.broadcast_to(total, cnt_ref.shape)


def _router(ys, w_out, h, g_m, g, shift, scale, wr, br, seq, tm):
    m, d = h.shape
    tiles_per_seq = seq // tm
    bmap = lambda i: (i // tiles_per_seq, 0, 0)
    row4 = lambda: pl.BlockSpec((TOP_K, tm), lambda i: (0, i))
    in_specs, args, k0 = [], [], 0
    for y in ys:
        in_specs.append(pl.BlockSpec((tm, y.shape[1]), lambda i: (i, 0)))
        args.append(y)
    for y in ys:
        kk = y.shape[1]
        in_specs.append(pl.BlockSpec((kk, d), lambda i, kb=k0 // kk: (kb, 0)))
        args.append(w_out)
        k0 += kk
    in_specs += [pl.BlockSpec((tm, d), lambda i: (i, 0)),
                 pl.BlockSpec((1, 1, d), bmap),
                 pl.BlockSpec((1, d), lambda i: (0, 0)),
                 pl.BlockSpec((1, 1, d), bmap), pl.BlockSpec((1, 1, d), bmap),
                 pl.BlockSpec((d, 2 * LANES), lambda i: (0, 0)),
                 pl.BlockSpec((1, LANES), lambda i: (0, 0))]
    args += [h, g_m, g.reshape(1, d), shift, scale, wr, br]
    return pl.pallas_call(
        functools.partial(_router_kernel, n_in=len(ys)),
        out_shape=[jax.ShapeDtypeStruct((m, d), F32),
                   jax.ShapeDtypeStruct((m, d // 2), I32),
                   jax.ShapeDtypeStruct((TOP_K, m), I32),
                   jax.ShapeDtypeStruct((TOP_K, m), F32),
                   jax.ShapeDtypeStruct((TOP_K, m), I32),
                   jax.ShapeDtypeStruct((N_EXPERTS, LANES), F32)],
        grid=(m // tm,),
        in_specs=in_specs,
        out_specs=[pl.BlockSpec((tm, d), lambda i: (i, 0)),
                   pl.BlockSpec((tm, d // 2), lambda i: (i, 0)), row4(), row4(), row4(),
                   pl.BlockSpec((N_EXPERTS, LANES), lambda i: (0, 0))],
        scratch_shapes=[pltpu.VMEM((N_EXPERTS, LANES), F32)],
        compiler_params=_cparams(("arbitrary",)),
    )(*args)


def _dest_kernel(ps_ref, topi_ref, rank_ref, o_ref):
    topi = topi_ref[...]
    acc = rank_ref[...]
    for e in range(N_EXPERTS):
        acc = acc + jnp.where(topi == e, ps_ref[e], 0)
    o_ref[...] = acc


def _dest_rows(pad_start, topi, rank, tw):
    k, m = topi.shape
    blk = lambda: pl.BlockSpec((k, tw), lambda i, ps: (0, i))
    return pl.pallas_call(
        _dest_kernel,
        out_shape=jax.ShapeDtypeStruct((k, m), I32),
        grid_spec=pltpu.PrefetchScalarGridSpec(
            num_scalar_prefetch=1, grid=(m // tw,), in_specs=[blk(), blk()], out_specs=blk()),
        compiler_params=_cparams(("parallel",)),
    )(pad_start, topi, rank)


SC_CORES = 2
SC_SUBCORES = 16
SC_ROWS = 128


def _sc_gather_rows(table, idx):
    b = idx.shape[0]
    w = table.shape[1]
    workers = SC_CORES * SC_SUBCORES
    per_w = b // workers
    assert per_w * workers == b and per_w % SC_ROWS == 0
    mesh = plsc.VectorSubcoreMesh(core_axis_name="c", subcore_axis_name="s")

    @functools.partial(
        pl.kernel, mesh=mesh, out_type=jax.ShapeDtypeStruct((b, w), I32),
        scratch_types=[pltpu.VMEM((SC_ROWS,), I32), pltpu.VMEM((SC_ROWS, w), I32),
                       pltpu.SemaphoreType.DMA])
    def gather(table_hbm, idx_hbm, out_hbm, idx_v, rows_v, sem):
        base = (lax.axis_index("s") * SC_CORES + lax.axis_index("c")) * per_w

        @pl.loop(0, per_w // SC_ROWS)
        def _(c):
            off = base + c * SC_ROWS
            pltpu.sync_copy(idx_hbm.at[pl.ds(off, SC_ROWS)], idx_v)
            pltpu.async_copy(table_hbm.at[idx_v], rows_v, sem).wait()
            pltpu.sync_copy(rows_v, out_hbm.at[pl.ds(off, SC_ROWS)])

    return gather(table, idx)


def _sc_scatter_rows(rows, dest, n_rows):
    m, w = rows.shape
    kk = dest.shape[0]
    workers = SC_CORES * SC_SUBCORES
    per_w = m // workers
    assert per_w * workers == m and per_w % SC_ROWS == 0
    mesh = plsc.VectorSubcoreMesh(core_axis_name="c", subcore_axis_name="s")

    @functools.partial(
        pl.kernel, mesh=mesh, out_type=jax.ShapeDtypeStruct((n_rows, w), I32),
        scratch_types=[pltpu.VMEM((SC_ROWS,), I32), pltpu.VMEM((SC_ROWS, w), I32),
                       pltpu.SemaphoreType.DMA])
    def scatter(rows_hbm, dest_hbm, out_hbm, idx_v, rows_v, sem):
        base = (lax.axis_index("s") * SC_CORES + lax.axis_index("c")) * per_w

        @pl.loop(0, per_w // SC_ROWS)
        def _(c):
            off = base + c * SC_ROWS
            pltpu.sync_copy(rows_hbm.at[pl.ds(off, SC_ROWS)], rows_v)
            for k in range(kk):
                pltpu.sync_copy(dest_hbm.at[pl.ds(k * m + off, SC_ROWS)], idx_v)
                pltpu.async_copy(rows_v, out_hbm.at[idx_v], sem).wait()

    return scatter(rows, dest.reshape(-1))


def _combine_dense_kernel(y_ref, gate_ref, h_ref, gf_ref, fn_ref, o_ref, *, final):
    acc = gate_ref[:, 0:1] * _unpack_pairs(y_ref[0])
    for k in range(1, TOP_K):
        acc = acc + gate_ref[:, k:k + 1] * _unpack_pairs(y_ref[k])
    hn = h_ref[...] + gf_ref[0] * acc
    if final:
        hn = hn * lax.rsqrt(jnp.mean(hn * hn, axis=-1, keepdims=True) + EPS) * fn_ref[...]
    o_ref[...] = hn


def _combine_dense(y4, gates_col, h, gf, fnorm, seq, tm, final):
    m, d = h.shape
    tiles_per_seq = seq // tm
    return pl.pallas_call(
        functools.partial(_combine_dense_kernel, final=final),
        out_shape=jax.ShapeDtypeStruct((m, d), F32),
        grid=(m // tm,),
        in_specs=[pl.BlockSpec((TOP_K, tm, d // 2), lambda i: (0, i, 0)),
                  pl.BlockSpec((tm, TOP_K), lambda i: (i, 0)),
                  pl.BlockSpec((tm, d), lambda i: (i, 0)),
                  pl.BlockSpec((1, 1, d), lambda i: (i // tiles_per_seq, 0, 0)),
                  pl.BlockSpec((1, d), lambda i: (0, 0))],
        out_specs=pl.BlockSpec((tm, d), lambda i: (i, 0)),
        compiler_params=_cparams(("parallel",)),
    )(y4, gates_col, h, gf, fnorm.reshape(1, d))


def _expert_kernel(be_ref, nb_ref, first_ref, nv_ref, x_ref, wgu_ref, bgu_ref, wd_ref, bd_ref, y_ref,
                   wgu_bf, wd_bf):
    i = pl.program_id(0)

    @pl.when(i < nb_ref[0])
    def _():
        dff = wd_bf.shape[0]

        @pl.when(first_ref[i] == 1)
        def _():
            rows = 64

            def cast(r, c):
                r0 = pl.multiple_of(r * rows, rows)
                wgu_bf[pl.ds(r0, rows), :] = wgu_ref[0, 0, pl.ds(r0, rows), :].astype(BF16)
                wd_bf[pl.ds(r0, rows), :] = wd_ref[0, 0, pl.ds(r0, rows), :].astype(BF16)
                return c

            lax.fori_loop(0, dff // rows, cast, 0)

        def ffn(r0):
            x = _unpack_pairs(x_ref[r0:r0 + EXPERT_ROWS, :]).astype(BF16)
            hb = _dot(x, wgu_bf[...]) + bgu_ref[0, 0]
            h_glu = jnp.minimum(hb[:, :dff], SWIGLU_LIMIT)
            h_lin = jnp.clip(hb[:, dff:], -SWIGLU_LIMIT, SWIGLU_LIMIT)
            half = 0.5 * h_glu
            act = (half + half * jnp.tanh(SWIGLU_ALPHA * half)) * (h_lin + 1.0)
            y_ref[r0:r0 + EXPERT_ROWS, :] = _pack_pairs(_dot(act.astype(BF16), wd_bf[...]) + bd_ref[0, 0])

        ffn(0)
        for r0 in range(EXPERT_ROWS, x_ref.shape[0], EXPERT_ROWS):
            pl.when(nv_ref[i] > r0)(functools.partial(ffn, r0))


def _experts(block_e, n_used, first, n_valid, xs, wgu, bgu, wd, bd, layer):
    n_rows, wp = xs.shape
    _, ne, d, ff2 = wgu.shape
    assert d == ff2 // 2
    nblk = n_rows // EXPERT_BLOCK

    def xmap(i, be, nb, fi, nv):
        return (jnp.minimum(i, nb[0] - 1), 0)

    emap = lambda i, be, nb, fi, nv: (layer, be[i], 0, 0)
    grid_spec = pltpu.PrefetchScalarGridSpec(
        num_scalar_prefetch=4, grid=(nblk,),
        in_specs=[pl.BlockSpec((EXPERT_BLOCK, wp), xmap),
                  pl.BlockSpec((1, 1, d, ff2), emap), pl.BlockSpec((1, 1, 1, ff2), emap),
                  pl.BlockSpec((1, 1, ff2 // 2, d), emap), pl.BlockSpec((1, 1, 1, d), emap)],
        out_specs=pl.BlockSpec((EXPERT_BLOCK, wp), xmap),
        scratch_shapes=[pltpu.VMEM((d, ff2), BF16), pltpu.VMEM((ff2 // 2, d), BF16)])
    depth = wgu.shape[0]
    return pl.pallas_call(
        _expert_kernel,
        out_shape=jax.ShapeDtypeStruct((n_rows, wp), I32),
        grid_spec=grid_spec,
        compiler_params=_cparams(("arbitrary",)),
    )(block_e, n_used, first, n_valid, xs, wgu, bgu.reshape(depth, ne, 1, ff2), wd,
      bd.reshape(depth, ne, 1, d))


def _moe(ys, w_out, g_m, h, g, shift, scale, gf, fnorm, wr, br, wgu, bgu, wd, bd, layer, seq, final):
    m, d = h.shape
    tm = MOE_TILE
    wr_p = jnp.zeros((d, LANES), F32).at[:, :N_EXPERTS].set(wr)
    br_p = jnp.zeros((1, LANES), F32).at[0, :N_EXPERTS].set(br)
    wr_hi = wr_p.astype(BF16)
    wr_split = jnp.concatenate([wr_hi, (wr_p - wr_hi.astype(F32)).astype(BF16)], axis=1)
    h, up, topi, gates, rank, cnt = _router(ys, w_out, h, g_m, g, shift, scale, wr_split, br_p, seq, tm)

    counts = cnt[:, 0].astype(I32)
    padded = (counts + EXPERT_BLOCK - 1) // EXPERT_BLOCK * EXPERT_BLOCK
    pad_end = jnp.cumsum(padded)
    pad_start = pad_end - padded
    nblk = m * TOP_K // EXPERT_BLOCK + N_EXPERTS
    n_rows = nblk * EXPERT_BLOCK
    n_used = pad_end[-1:] // EXPERT_BLOCK
    blk = jnp.arange(nblk, dtype=I32)
    blk_c = jnp.minimum(blk, n_used - 1)
    block_e = jnp.minimum(jnp.sum(blk_c[:, None] * EXPERT_BLOCK >= pad_end[None, :], axis=1),
                          N_EXPERTS - 1).astype(I32)
    first = jnp.concatenate([jnp.ones((1,), I32), (block_e[1:] != block_e[:-1]).astype(I32)])
    real_end = jnp.sum(jnp.where(block_e[:, None] == jnp.arange(N_EXPERTS, dtype=I32)[None, :],
                                 (pad_start + counts)[None, :], 0), axis=1)
    n_valid = jnp.clip(real_end - blk_c * EXPERT_BLOCK, 0, EXPERT_BLOCK).astype(I32)

    dest = _dest_rows(pad_start, topi, rank, min(m, DEST_TILE))
    xs = _sc_scatter_rows(up, dest, n_rows)
    y = _experts(block_e, n_used.astype(I32), first, n_valid, xs, wgu, bgu, wd, bd, layer)
    y4 = _sc_gather_rows(y, dest.reshape(-1)).reshape(TOP_K, m, d // 2)
    return _combine_dense(y4, gates.T, h, gf, fnorm, seq, min(seq, COMBINE_TILE), final)


def _block_diag(w, group):
    nb, b, _ = w.shape
    per = group // b
    wg = w.reshape(nb // per, per, b, b)
    dense = jnp.einsum("gnde,nm->gndme", wg, jnp.eye(per, dtype=w.dtype))
    return dense.reshape(nb // per, group, group)


def kernel(x, c, mod_w, mod_b, norm_mix, norm_ffn, ev_w_in, ev_lru_conv_w, ev_lru_conv_b, ev_lru_w_r, ev_lru_b_r, ev_lru_w_i, ev_lru_b_i, ev_lru_lambda, ev_ml_conv_w, ev_ml_conv_b, ev_ml_w_q, ev_ml_w_k, ev_ml_w_v, ev_ml_w_ig, ev_ml_b_ig, ev_ml_w_fg, ev_ml_b_fg, ev_ml_norm, ev_ml_skip, ev_w_out, od_w_in, od_conv_w, od_conv_b, od_dt_bias, od_a_log, od_d, od_norm, od_w_out, moe_router_w, moe_router_b, moe_w_gu, moe_b_gu, moe_w_down, moe_b_down, final_norm):
    bsz, seq, d = x.shape
    depth = mod_w.shape[0]
    m = bsz * seq
    mod = _modulation(c, mod_w, mod_b)
    h = x.reshape(m, d).astype(F32)
    for layer in range(depth):
        sh_m, sc_m, g_m, sh_f, sc_f, g_f = (mod[layer, i] for i in range(6))
        j = layer // 2
        if layer % 2 == 0:
            w = ev_lru_lambda.shape[1]
            w_in = ev_w_in[j].astype(BF16)
            proj = _inproj(h, norm_mix[layer], sh_m, sc_m, w_in, None, [w_in.shape[1]], seq,
                           min(seq, PROJ_TILE_EVEN))[0]
            lru_p = dict(conv_w=ev_lru_conv_w[j], conv_b=ev_lru_conv_b[j].reshape(1, w),
                         w_r=ev_lru_w_r[j].astype(BF16), b_r=ev_lru_b_r[j].reshape(1, w),
                         w_i=ev_lru_w_i[j].astype(BF16), b_i=ev_lru_b_i[j].reshape(1, w),
                         lam=ev_lru_lambda[j].reshape(1, w))
            ya = _lru(proj, lru_p, bsz, seq, LRU_TILE)
            wg = jnp.zeros((3 * w, LANES), F32)
            wg = wg.at[:, :ML_HEADS].set(ev_ml_w_ig[j]).at[:, ML_HEADS:2 * ML_HEADS].set(ev_ml_w_fg[j])
            bg = jnp.zeros((1, LANES), F32)
            bg = bg.at[0, :ML_HEADS].set(ev_ml_b_ig[j]).at[0, ML_HEADS:2 * ML_HEADS].set(ev_ml_b_fg[j])
            ml_p = dict(conv_w=ev_ml_conv_w[j], conv_b=ev_ml_conv_b[j].reshape(1, w),
                        w_q=_block_diag(ev_ml_w_q[j], LANES).astype(BF16),
                        w_k=_block_diag(ev_ml_w_k[j], LANES).astype(BF16),
                        w_v=_block_diag(ev_ml_w_v[j], LANES).astype(BF16),
                        w_g=wg.astype(BF16), b_g=bg,
                        norm=ev_ml_norm[j].reshape(1, w), skip=ev_ml_skip[j].reshape(1, w))
            yb = _mlstm(proj, ml_p, bsz, seq)
            ys, w_out = [ya, yb], ev_w_out[j].astype(BF16)
        else:
            inner = od_norm.shape[1]
            heads = od_dt_bias.shape[1]
            conv_ch = od_conv_w.shape[2]
            w_in = od_w_in[j]
            wdt = jnp.zeros((d, LANES), F32).at[:, :heads].set(w_in[:, inner + conv_ch:])
            z, xbc, dt_raw = _inproj(h, norm_mix[layer], sh_m, sc_m, w_in.astype(BF16),
                                     wdt.astype(BF16), [inner, conv_ch], seq, PROJ_TILE)
            pad = lambda v: jnp.zeros((1, LANES), F32).at[0, :heads].set(v)
            ssd_p = dict(conv_w=od_conv_w[j], conv_b=od_conv_b[j].reshape(1, conv_ch),
                         dt_bias=pad(od_dt_bias[j]), a_log=pad(od_a_log[j]),
                         d_skip=jnp.repeat(od_d[j], SSD_HEAD_DIM).reshape(1, inner),
                         norm=od_norm[j].reshape(1, inner))
            y = _ssd(z, xbc, dt_raw, ssd_p, bsz, seq)
            ys, w_out = [y], od_w_out[j].astype(BF16)
        h = _moe(ys, w_out, g_m, h, norm_ffn[layer], sh_f, sc_f, g_f, final_norm,
                 moe_router_w[layer], moe_router_b[layer],
                 moe_w_gu, moe_b_gu, moe_w_down, moe_b_down, layer, seq, final=(layer == depth - 1))
    return h.reshape(bsz, seq, d)
```
